```python
import math
import jax, jax.numpy as jnp
from jax import lax
import numpy as np

D_MODEL = 1024
BATCH = 8
SEQ = 8192
DEPTH = 4

CHUNK = 64
EPS = 1e-6
NEG_INF = -1e30
N_BRANCH = 3
BR_WIDTH = 512

SGU_BLOCK = 128
SGU_GROUPS = 8
SGU_GROUP_DIM = BR_WIDTH // SGU_GROUPS

MLA_HEADS = 8
MLA_NOPE = 64
MLA_ROPE = 32
MLA_V = 64
MLA_QK = MLA_NOPE + MLA_ROPE
MLA_Q_RANK = 256
MLA_KV_RANK = 128
ROPE_BASE = 10000.0
Q_BLOCK = 128

CA_HEADS = 8
CA_HEAD_DIM = BR_WIDTH // CA_HEADS
LEFT_CHUNKS = 8
BAND = (LEFT_CHUNKS + 1) * CHUNK
REL_CLIP = 128

IN_WIDTHS = (BR_WIDTH, BR_WIDTH, BR_WIDTH,
             MLA_Q_RANK, MLA_KV_RANK, MLA_ROPE, BR_WIDTH,
             BR_WIDTH, BR_WIDTH, BR_WIDTH, BR_WIDTH,
             N_BRANCH * D_MODEL)
D_IN = sum(IN_WIDTHS)

kernel_name = "hybrid_sgu_mla_chunkattn_streaming"


def rmsnorm(x, g):
    xf = x.astype(jnp.float32)
    y = xf * lax.rsqrt(jnp.mean(xf * xf, axis=-1, keepdims=True) + EPS)
    return (y * g.astype(jnp.float32)).astype(x.dtype)


def layernorm(x, g, b):
    xf = x.astype(jnp.float32)
    mu = jnp.mean(xf, axis=-1, keepdims=True)
    xc = xf - mu
    y = xc * lax.rsqrt(jnp.mean(xc * xc, axis=-1, keepdims=True) + EPS)
    return (y * g.astype(jnp.float32) + b.astype(jnp.float32)).astype(x.dtype)


def apply_rope(x, pos):
    half = x.shape[-1] // 2
    inv = ROPE_BASE ** (-jnp.arange(half, dtype=jnp.float32) / half)
    ang = pos.astype(jnp.float32)[:, None] * inv[None, :]
    cos = jnp.cos(ang)[:, None, :]
    sin = jnp.sin(ang)[:, None, :]
    xf = x.astype(jnp.float32)
    x1, x2 = xf[..., :half], xf[..., half:]
    return jnp.concatenate([x1 * cos - x2 * sin, x1 * sin + x2 * cos], axis=-1).astype(x.dtype)


def sgu_mixer(u, v, ln_g, ln_b, w_s, b_s):
    B, S, _ = u.shape
    nb = S // SGU_BLOCK
    v = layernorm(v, ln_g, ln_b)
    vb = v.reshape(B, nb, SGU_BLOCK, SGU_GROUPS, SGU_GROUP_DIM)
    tri = jnp.tril(jnp.ones((SGU_BLOCK, SGU_BLOCK), dtype=bool))
    ws = jnp.where(tri[None], w_s, 0.0).astype(v.dtype)
    mixed = jnp.einsum('gts,bnsgc->bntgc', ws, vb) + b_s.T.astype(v.dtype)[None, None, :, :, None]
    return u * mixed.reshape(B, S, BR_WIDTH)


def mla_mixer(q_down, kv_down, k_rope_in, q_norm_g, kv_norm_g, w_uq, w_ukv, pos):
    B, S, _ = q_down.shape
    cq = rmsnorm(q_down, q_norm_g)
    q = (cq @ w_uq).reshape(B, S, MLA_HEADS, MLA_QK)
    q = jnp.concatenate([q[..., :MLA_NOPE], apply_rope(q[..., MLA_NOPE:], pos)], axis=-1)
    ckv = rmsnorm(kv_down, kv_norm_g)
    kv = (ckv @ w_ukv).reshape(B, S, MLA_HEADS, MLA_NOPE + MLA_V)
    k_nope, v = kv[..., :MLA_NOPE], kv[..., MLA_NOPE:]
    k_r = apply_rope(k_rope_in[:, :, None, :], pos)
    k = jnp.concatenate([k_nope, jnp.broadcast_to(k_r, (B, S, MLA_HEADS, MLA_ROPE))], axis=-1)
    scale = MLA_QK ** -0.5
    nqb = S // Q_BLOCK
    qb = q.reshape(B, nqb, Q_BLOCK, MLA_HEADS, MLA_QK).transpose(1, 0, 2, 3, 4)
    key_chunk = jnp.arange(S) // CHUNK

    def block(args):
        qi, bi = args
        s = jnp.einsum('bqhd,bkhd->bhqk', qi, k).astype(jnp.float32) * scale
        q_chunk = (bi * Q_BLOCK + jnp.arange(Q_BLOCK)) // CHUNK
        mask = key_chunk[None, :] <= q_chunk[:, None]
        s = jnp.where(mask[None, None], s, NEG_INF)
        p = jax.nn.softmax(s, axis=-1).astype(v.dtype)
        return jnp.einsum('bhqk,bkhd->bqhd', p, v)

    o = lax.map(block, (qb, jnp.arange(nqb)))
    return o.transpose(1, 0, 2, 3, 4).reshape(B, S, MLA_HEADS * MLA_V)


def chunk_band_mixer(q, k, v, rel_table):
    B, S, _ = q.shape
    nc = S // CHUNK
    pad = LEFT_CHUNKS * CHUNK
    q = q.reshape(B, S, CA_HEADS, CA_HEAD_DIM)
    k = k.reshape(B, S, CA_HEADS, CA_HEAD_DIM)
    v = v.reshape(B, S, CA_HEADS, CA_HEAD_DIM)
    kp = jnp.pad(k, ((0, 0), (pad, 0), (0, 0), (0, 0)))
    vp = jnp.pad(v, ((0, 0), (pad, 0), (0, 0), (0, 0)))
    qc = q.reshape(B, nc, CHUNK, CA_HEADS, CA_HEAD_DIM).transpose(1, 0, 2, 3, 4)
    i = jnp.arange(CHUNK)
    j = jnp.arange(BAND)
    dist = i[:, None] + pad - j[None, :]
    idx = jnp.clip(dist, -REL_CLIP, REL_CLIP) + REL_CLIP
    bias = rel_table[:, idx].astype(jnp.float32)
    scale = CA_HEAD_DIM ** -0.5

    def chunk(args):
        qi, ci = args
        kb = lax.dynamic_slice_in_dim(kp, ci * CHUNK, BAND, axis=1)
        vb = lax.dynamic_slice_in_dim(vp, ci * CHUNK, BAND, axis=1)
        s = jnp.einsum('bqhd,bkhd->bhqk', qi, kb).astype(jnp.float32) * scale + bias[None]
        valid = j >= (LEFT_CHUNKS - ci) * CHUNK
        s = jnp.where(valid[None, None, None, :], s, NEG_INF)
        p = jax.nn.softmax(s, axis=-1).astype(vb.dtype)
        return jnp.einsum('bhqk,bkhd->bqhd', p, vb)

    o = lax.map(chunk, (qc, jnp.arange(nc)))
    return o.transpose(1, 0, 2, 3, 4).reshape(B, S, BR_WIDTH)


def _fwd_setup_inputs(seed: int = 0) -> dict:
    key = jax.random.key(seed)
    ks = jax.random.split(key, 16)
    f32 = jnp.float32
    L, D = DEPTH, D_MODEL
    nrm = lambda k, shape, s: jax.random.normal(k, shape, f32) * s
    return {
        "x": jax.random.normal(ks[0], (BATCH, SEQ, D), f32),
        "w_in": nrm(ks[1], (L, D, D_IN), D ** -0.5),
        "pre_g": 1.0 + nrm(ks[2], (L, D), 0.1),
        "post_g": 1.0 + nrm(ks[3], (L, D), 0.1),
        "sgu_ln_g": 1.0 + nrm(ks[4], (L, BR_WIDTH), 0.1),
        "sgu_ln_b": nrm(ks[5], (L, BR_WIDTH), 0.02),
        "sgu_w": nrm(ks[6], (L, SGU_GROUPS, SGU_BLOCK, SGU_BLOCK), SGU_BLOCK ** -0.5),
        "sgu_b": 1.0 + nrm(ks[7], (L, SGU_GROUPS, SGU_BLOCK), 0.1),
        "mla_q_norm_g": 1.0 + nrm(ks[8], (L, MLA_Q_RANK), 0.1),
        "mla_kv_norm_g": 1.0 + nrm(ks[9], (L, MLA_KV_RANK), 0.1),
        "mla_w_uq": nrm(ks[10], (L, MLA_Q_RANK, MLA_HEADS * MLA_QK), MLA_Q_RANK ** -0.5),
        "mla_w_ukv": nrm(ks[11], (L, MLA_KV_RANK, MLA_HEADS * (MLA_NOPE + MLA_V)), MLA_KV_RANK ** -0.5),
        "ca_rel_bias": nrm(ks[12], (L, CA_HEADS, 2 * REL_CLIP + 1), 0.5),
        "w_branch": nrm(ks[13], (L, N_BRANCH, BR_WIDTH, D), BR_WIDTH ** -0.5),
        "gate_b": nrm(ks[14], (L, N_BRANCH, D), 0.1),
        "w_out": nrm(ks[15], (L, D, D), D ** -0.5),
    }


def _fwd_reference(x, w_in, pre_g, post_g, sgu_ln_g, sgu_ln_b, sgu_w, sgu_b,
              mla_q_norm_g, mla_kv_norm_g, mla_w_uq, mla_w_ukv, ca_rel_bias,
              w_branch, gate_b, w_out):
    B, S, D = x.shape
    pos = jnp.arange(S)
    offsets = [0]
    for w in IN_WIDTHS:
        offsets.append(offsets[-1] + w)
    for l in range(DEPTH):
        xn = rmsnorm(x, pre_g[l])
        proj = xn @ w_in[l]
        (u_a, v_a, z_a, qd_b, kvd_b, kr_b, z_b,
         q_c, k_c, v_c, z_c, g_logits) = [proj[..., offsets[n]:offsets[n + 1]] for n in range(len(IN_WIDTHS))]
        y_a = sgu_mixer(u_a, v_a, sgu_ln_g[l], sgu_ln_b[l], sgu_w[l], sgu_b[l]) * jax.nn.silu(z_a)
        y_b = mla_mixer(qd_b, kvd_b, kr_b, mla_q_norm_g[l], mla_kv_norm_g[l],
                        mla_w_uq[l], mla_w_ukv[l], pos) * jax.nn.silu(z_b)
        y_c = chunk_band_mixer(q_c, k_c, v_c, ca_rel_bias[l]) * jax.nn.silu(z_c)
        ys = jnp.stack([y_a, y_b, y_c], axis=2)
        br = jnp.einsum('bsnc,ncd->bsnd', ys, w_branch[l])
        gates = jax.nn.sigmoid(g_logits.reshape(B, S, N_BRANCH, D) + gate_b[l])
        merged = jnp.sum(gates * br, axis=2)
        x = x + rmsnorm(merged @ w_out[l], post_g[l])
    return x


import jax as _jax
import jax.numpy as _jnp

TWIN_FORMAT = 'train_step'
FWD_PARAMS = ['x', 'w_in', 'pre_g', 'post_g', 'sgu_ln_g', 'sgu_ln_b', 'sgu_w', 'sgu_b', 'mla_q_norm_g', 'mla_kv_norm_g', 'mla_w_uq', 'mla_w_ukv', 'ca_rel_bias', 'w_branch', 'gate_b', 'w_out']
TWIN_WEIGHTS = ['w_in', 'pre_g', 'post_g', 'sgu_ln_g', 'sgu_ln_b', 'sgu_w', 'sgu_b', 'mla_q_norm_g', 'mla_kv_norm_g', 'mla_w_uq', 'mla_w_ukv', 'ca_rel_bias', 'w_branch', 'gate_b', 'w_out']
TWIN_DIFF_INPUT = 'x'
TWIN_INPUTS = ['x', 'w_in', 'pre_g', 'post_g', 'sgu_ln_g', 'sgu_ln_b', 'sgu_w', 'sgu_b', 'mla_q_norm_g', 'mla_kv_norm_g', 'mla_w_uq', 'mla_w_ukv', 'ca_rel_bias', 'w_branch', 'gate_b', 'w_out', 'loss_target', 'm_w_in', 'm_pre_g', 'm_post_g', 'm_sgu_ln_g', 'm_sgu_ln_b', 'm_sgu_w', 'm_sgu_b', 'm_mla_q_norm_g', 'm_mla_kv_norm_g', 'm_mla_w_uq', 'm_mla_w_ukv', 'm_ca_rel_bias', 'm_w_branch', 'm_gate_b', 'm_w_out', 'v_w_in', 'v_pre_g', 'v_post_g', 'v_sgu_ln_g', 'v_sgu_ln_b', 'v_sgu_w', 'v_sgu_b', 'v_mla_q_norm_g', 'v_mla_kv_norm_g', 'v_mla_w_uq', 'v_mla_w_ukv', 'v_ca_rel_bias', 'v_w_branch', 'v_gate_b', 'v_w_out']
TWIN_OUTPUTS = ['loss', 'grad_x', 'grad_w_in', 'grad_pre_g', 'grad_post_g', 'grad_sgu_ln_g', 'grad_sgu_ln_b', 'grad_sgu_w', 'grad_sgu_b', 'grad_mla_q_norm_g', 'grad_mla_kv_norm_g', 'grad_mla_w_uq', 'grad_mla_w_ukv', 'grad_ca_rel_bias', 'grad_w_branch', 'grad_gate_b', 'grad_w_out', 'delta_w_in', 'delta_pre_g', 'delta_post_g', 'delta_sgu_ln_g', 'delta_sgu_ln_b', 'delta_sgu_w', 'delta_sgu_b', 'delta_mla_q_norm_g', 'delta_mla_kv_norm_g', 'delta_mla_w_uq', 'delta_mla_w_ukv', 'delta_ca_rel_bias', 'delta_w_branch', 'delta_gate_b', 'delta_w_out', 'new_m_w_in', 'new_m_pre_g', 'new_m_post_g', 'new_m_sgu_ln_g', 'new_m_sgu_ln_b', 'new_m_sgu_w', 'new_m_sgu_b', 'new_m_mla_q_norm_g', 'new_m_mla_kv_norm_g', 'new_m_mla_w_uq', 'new_m_mla_w_ukv', 'new_m_ca_rel_bias', 'new_m_w_branch', 'new_m_gate_b', 'new_m_w_out', 'new_v_w_in', 'new_v_pre_g', 'new_v_post_g', 'new_v_sgu_ln_g', 'new_v_sgu_ln_b', 'new_v_sgu_w', 'new_v_sgu_b', 'new_v_mla_q_norm_g', 'new_v_mla_kv_norm_g', 'new_v_mla_w_uq', 'new_v_mla_w_ukv', 'new_v_ca_rel_bias', 'new_v_w_branch', 'new_v_gate_b', 'new_v_w_out']
TWIN_LEAF_KINDS = {'loss': 'loss', 'grad_x': 'grad_x', 'grad_w_in': 'grad_w', 'grad_pre_g': 'grad_w', 'grad_post_g': 'grad_w', 'grad_sgu_ln_g': 'grad_w', 'grad_sgu_ln_b': 'grad_w', 'grad_sgu_w': 'grad_w', 'grad_sgu_b': 'grad_w', 'grad_mla_q_norm_g': 'grad_w', 'grad_mla_kv_norm_g': 'grad_w', 'grad_mla_w_uq': 'grad_w', 'grad_mla_w_ukv': 'grad_w', 'grad_ca_rel_bias': 'grad_w', 'grad_w_branch': 'grad_w', 'grad_gate_b': 'grad_w', 'grad_w_out': 'grad_w', 'delta_w_in': 'delta_w', 'delta_pre_g': 'delta_w', 'delta_post_g': 'delta_w', 'delta_sgu_ln_g': 'delta_w', 'delta_sgu_ln_b': 'delta_w', 'delta_sgu_w': 'delta_w', 'delta_sgu_b': 'delta_w', 'delta_mla_q_norm_g': 'delta_w', 'delta_mla_kv_norm_g': 'delta_w', 'delta_mla_w_uq': 'delta_w', 'delta_mla_w_ukv': 'delta_w', 'delta_ca_rel_bias': 'delta_w', 'delta_w_branch': 'delta_w', 'delta_gate_b': 'delta_w', 'delta_w_out': 'delta_w', 'new_m_w_in': 'new_m', 'new_m_pre_g': 'new_m', 'new_m_post_g': 'new_m', 'new_m_sgu_ln_g': 'new_m', 'new_m_sgu_ln_b': 'new_m', 'new_m_sgu_w': 'new_m', 'new_m_sgu_b': 'new_m', 'new_m_mla_q_norm_g': 'new_m', 'new_m_mla_kv_norm_g': 'new_m', 'new_m_mla_w_uq': 'new_m', 'new_m_mla_w_ukv': 'new_m', 'new_m_ca_rel_bias': 'new_m', 'new_m_w_branch': 'new_m', 'new_m_gate_b': 'new_m', 'new_m_w_out': 'new_m', 'new_v_w_in': 'new_v', 'new_v_pre_g': 'new_v', 'new_v_post_g': 'new_v', 'new_v_sgu_ln_g': 'new_v', 'new_v_sgu_ln_b': 'new_v', 'new_v_sgu_w': 'new_v', 'new_v_sgu_b': 'new_v', 'new_v_mla_q_norm_g': 'new_v', 'new_v_mla_kv_norm_g': 'new_v', 'new_v_mla_w_uq': 'new_v', 'new_v_mla_w_ukv': 'new_v', 'new_v_ca_rel_bias': 'new_v', 'new_v_w_branch': 'new_v', 'new_v_gate_b': 'new_v', 'new_v_w_out': 'new_v'}


def _forward(args):
    return _fwd_reference(*[args[k] for k in FWD_PARAMS])


def _output_shape():
    def fwd():
        inp = _fwd_setup_inputs(0)
        return _fwd_reference(*[inp[k] for k in FWD_PARAMS])
    out = _jax.eval_shape(fwd)
    return out.shape, out.dtype

N_MICROBATCH = 1
ADAM_LR = 0.001
ADAM_B1 = 0.9
ADAM_B2 = 0.999
ADAM_EPS = 1e-08
ADAM_WD = 0.01
ADAM_STEP = 10
PER_EXAMPLE_BATCH_AXIS = {'x': 0, 'loss_target': 0}
SHARED_INPUTS = []
_WEIGHT_DTYPES = {'w_in': _jnp.float32, 'pre_g': _jnp.float32, 'post_g': _jnp.float32, 'sgu_ln_g': _jnp.float32, 'sgu_ln_b': _jnp.float32, 'sgu_w': _jnp.float32, 'sgu_b': _jnp.float32, 'mla_q_norm_g': _jnp.float32, 'mla_kv_norm_g': _jnp.float32, 'mla_w_uq': _jnp.float32, 'mla_w_ukv': _jnp.float32, 'ca_rel_bias': _jnp.float32, 'w_branch': _jnp.float32, 'gate_b': _jnp.float32, 'w_out': _jnp.float32}
MOMENT_SCALE = {'w_in': 6.273877e-01, 'pre_g': 1.793383e+00, 'post_g': 6.434799e+01, 'sgu_ln_g': 7.947918e-01, 'sgu_ln_b': 8.034658e-01, 'sgu_w': 5.548910e-01, 'sgu_b': 8.115724e-01, 'mla_q_norm_g': 2.649043e-01, 'mla_kv_norm_g': 5.822929e-01, 'mla_w_uq': 1.610886e-01, 'mla_w_ukv': 1.914338e-01, 'ca_rel_bias': 9.463004e-02, 'w_branch': 7.205441e-01, 'gate_b': 3.082918e-01, 'w_out': 1.306398e+00}


def _to_microbatches(a, axis):
    t = _jnp.moveaxis(a, axis, 0)
    t = t.reshape((N_MICROBATCH, t.shape[0] // N_MICROBATCH) + t.shape[1:])
    return _jnp.moveaxis(t, 1, axis + 1)


def setup_inputs(seed: int = 0) -> dict:
    inp = _fwd_setup_inputs(seed)
    key = _jax.random.fold_in(_jax.random.key(seed), 7919)
    shape, _ = _output_shape()
    out = dict(inp)
    out["loss_target"] = _jax.random.normal(_jax.random.fold_in(key, 0), shape, _jnp.float32)
    for i, name in enumerate(TWIN_WEIGHTS):
        w = inp[name].astype(_jnp.float32)
        if MOMENT_SCALE is None:
            s = _jnp.sqrt(_jnp.mean(_jnp.square(w)) + 1e-30)
        else:
            s = MOMENT_SCALE[name]
        km, kv = _jax.random.split(_jax.random.fold_in(key, i + 1))
        out[name] = w
        out["m_" + name] = s * _jax.random.normal(km, w.shape, _jnp.float32)
        out["v_" + name] = (s * s) * _jax.random.uniform(kv, w.shape, _jnp.float32, 0.5, 1.5)
    if N_MICROBATCH > 1:
        for name, axis in PER_EXAMPLE_BATCH_AXIS.items():
            out[name] = _to_microbatches(out[name], axis)
    return {'x': out['x'], 'w_in': out['w_in'], 'pre_g': out['pre_g'], 'post_g': out['post_g'], 'sgu_ln_g': out['sgu_ln_g'], 'sgu_ln_b': out['sgu_ln_b'], 'sgu_w': out['sgu_w'], 'sgu_b': out['sgu_b'], 'mla_q_norm_g': out['mla_q_norm_g'], 'mla_kv_norm_g': out['mla_kv_norm_g'], 'mla_w_uq': out['mla_w_uq'], 'mla_w_ukv': out['mla_w_ukv'], 'ca_rel_bias': out['ca_rel_bias'], 'w_branch': out['w_branch'], 'gate_b': out['gate_b'], 'w_out': out['w_out'], 'loss_target': out['loss_target'], 'm_w_in': out['m_w_in'], 'm_pre_g': out['m_pre_g'], 'm_post_g': out['m_post_g'], 'm_sgu_ln_g': out['m_sgu_ln_g'], 'm_sgu_ln_b': out['m_sgu_ln_b'], 'm_sgu_w': out['m_sgu_w'], 'm_sgu_b': out['m_sgu_b'], 'm_mla_q_norm_g': out['m_mla_q_norm_g'], 'm_mla_kv_norm_g': out['m_mla_kv_norm_g'], 'm_mla_w_uq': out['m_mla_w_uq'], 'm_mla_w_ukv': out['m_mla_w_ukv'], 'm_ca_rel_bias': out['m_ca_rel_bias'], 'm_w_branch': out['m_w_branch'], 'm_gate_b': out['m_gate_b'], 'm_w_out': out['m_w_out'], 'v_w_in': out['v_w_in'], 'v_pre_g': out['v_pre_g'], 'v_post_g': out['v_post_g'], 'v_sgu_ln_g': out['v_sgu_ln_g'], 'v_sgu_ln_b': out['v_sgu_ln_b'], 'v_sgu_w': out['v_sgu_w'], 'v_sgu_b': out['v_sgu_b'], 'v_mla_q_norm_g': out['v_mla_q_norm_g'], 'v_mla_kv_norm_g': out['v_mla_kv_norm_g'], 'v_mla_w_uq': out['v_mla_w_uq'], 'v_mla_w_ukv': out['v_mla_w_ukv'], 'v_ca_rel_bias': out['v_ca_rel_bias'], 'v_w_branch': out['v_w_branch'], 'v_gate_b': out['v_gate_b'], 'v_w_out': out['v_w_out']}


def _loss(weights, diff, rest, loss_target):
    with _jax.named_scope("forward"):
        args = {**rest, TWIN_DIFF_INPUT: diff, **{k: w.astype(_WEIGHT_DTYPES[k]) for k, w in weights.items()}}
        y = _forward(args)
    with _jax.named_scope("loss_head"):
        err = _jnp.square(y.astype(_jnp.float32) - loss_target)
        return 0.5 * _jnp.sum(_jnp.mean(err, axis=-1)) if err.ndim else 0.5 * err


def _adamw(w, g, m, v):
    m = ADAM_B1 * m + (1.0 - ADAM_B1) * g
    v = ADAM_B2 * v + (1.0 - ADAM_B2) * _jnp.square(g)
    m_hat = m / (1.0 - ADAM_B1 ** ADAM_STEP)
    v_hat = v / (1.0 - ADAM_B2 ** ADAM_STEP)
    delta = -ADAM_LR * (m_hat / (_jnp.sqrt(v_hat) + ADAM_EPS) + ADAM_WD * w)
    return delta, m, v


def reference(x, w_in, pre_g, post_g, sgu_ln_g, sgu_ln_b, sgu_w, sgu_b, mla_q_norm_g, mla_kv_norm_g, mla_w_uq, mla_w_ukv, ca_rel_bias, w_branch, gate_b, w_out, loss_target, m_w_in, m_pre_g, m_post_g, m_sgu_ln_g, m_sgu_ln_b, m_sgu_w, m_sgu_b, m_mla_q_norm_g, m_mla_kv_norm_g, m_mla_w_uq, m_mla_w_ukv, m_ca_rel_bias, m_w_branch, m_gate_b, m_w_out, v_w_in, v_pre_g, v_post_g, v_sgu_ln_g, v_sgu_ln_b, v_sgu_w, v_sgu_b, v_mla_q_norm_g, v_mla_kv_norm_g, v_mla_w_uq, v_mla_w_ukv, v_ca_rel_bias, v_w_branch, v_gate_b, v_w_out):
    given = dict(x=x, w_in=w_in, pre_g=pre_g, post_g=post_g, sgu_ln_g=sgu_ln_g, sgu_ln_b=sgu_ln_b, sgu_w=sgu_w, sgu_b=sgu_b, mla_q_norm_g=mla_q_norm_g, mla_kv_norm_g=mla_kv_norm_g, mla_w_uq=mla_w_uq, mla_w_ukv=mla_w_ukv, ca_rel_bias=ca_rel_bias, w_branch=w_branch, gate_b=gate_b, w_out=w_out, loss_target=loss_target, m_w_in=m_w_in, m_pre_g=m_pre_g, m_post_g=m_post_g, m_sgu_ln_g=m_sgu_ln_g, m_sgu_ln_b=m_sgu_ln_b, m_sgu_w=m_sgu_w, m_sgu_b=m_sgu_b, m_mla_q_norm_g=m_mla_q_norm_g, m_mla_kv_norm_g=m_mla_kv_norm_g, m_mla_w_uq=m_mla_w_uq, m_mla_w_ukv=m_mla_w_ukv, m_ca_rel_bias=m_ca_rel_bias, m_w_branch=m_w_branch, m_gate_b=m_gate_b, m_w_out=m_w_out, v_w_in=v_w_in, v_pre_g=v_pre_g, v_post_g=v_post_g, v_sgu_ln_g=v_sgu_ln_g, v_sgu_ln_b=v_sgu_ln_b, v_sgu_w=v_sgu_w, v_sgu_b=v_sgu_b, v_mla_q_norm_g=v_mla_q_norm_g, v_mla_kv_norm_g=v_mla_kv_norm_g, v_mla_w_uq=v_mla_w_uq, v_mla_w_ukv=v_mla_w_ukv, v_ca_rel_bias=v_ca_rel_bias, v_w_branch=v_w_branch, v_gate_b=v_gate_b, v_w_out=v_w_out)
    weights = {n: given[n] for n in TWIN_WEIGHTS}
    shared = {n: given[n] for n in SHARED_INPUTS}
    per_example = {n: given[n] for n in ['x']}
    grad_fn = _jax.value_and_grad(_loss, argnums=(0, 1))

    def one_microbatch(ex, loss_target):
        ex = dict(ex)
        diff = ex.pop(TWIN_DIFF_INPUT)
        return grad_fn(weights, diff, {**shared, **ex}, loss_target)

    if N_MICROBATCH == 1:
        loss, (grad_w, grad_x) = one_microbatch(per_example, given["loss_target"])
    else:
        def body(carry, xs):
            loss_sum, grad_sum = carry
            l_k, (gw_k, gx_k) = one_microbatch(xs[0], xs[1])
            with _jax.named_scope("update"):
                return (loss_sum + l_k, _jax.tree.map(_jnp.add, grad_sum, gw_k)), gx_k

        init = (_jnp.zeros((), _jnp.float32), _jax.tree.map(_jnp.zeros_like, weights))
        (loss, grad_w), grad_x = _jax.lax.scan(body, init, (per_example, given["loss_target"]))
    with _jax.named_scope("update"):
        delta_w, new_m, new_v = {}, {}, {}
        for n in TWIN_WEIGHTS:
            delta_w[n], new_m[n], new_v[n] = _adamw(weights[n], grad_w[n], given["m_" + n], given["v_" + n])
    return (loss, grad_x, *[grad_w[n] for n in TWIN_WEIGHTS], *[delta_w[n] for n in TWIN_WEIGHTS],
            *[new_m[n] for n in TWIN_WEIGHTS], *[new_v[n] for n in TWIN_WEIGHTS])
```

```python
import functools

import jax
import jax.numpy as jnp
from jax import lax
from jax.experimental import pallas as pl
from jax.experimental.pallas import tpu as pltpu

F32 = jnp.float32
MXU_DTYPE = jnp.bfloat16
EPS = 1e-6
NEG = -1e30
MESH_AXES = ("x", "y", "c")
N_DEV = 8

D = 1024
BW = 512
N_BRANCH = 3
CHUNK = 64
SGU_T = 128
SGU_G = 8
MLA_H = 8
MLA_NOPE = 64
MLA_ROPE = 32
MLA_QK = MLA_NOPE + MLA_ROPE
Q_RANK = 256
KV_RANK = 128
CA_H = 8
CA_DH = 64
LEFT_CHUNKS = 8
REL_CLIP = 128
D_IN = 7584

G_OFF, UA, VA, ZA, QD, ZB, QC, KC, VC, ZC, DP = 0, 3072, 3584, 4096, 4608, 5120, 5632, 6144, 6656, 7168, 7680
BAND_TQ = 256
BAND_W = 3 * BAND_TQ

ADAM_LR, ADAM_B1, ADAM_B2, ADAM_EPS, ADAM_WD, ADAM_STEP = 0.001, 0.9, 0.999, 1e-08, 0.01, 10


def _params(sem, mib):
    return pltpu.CompilerParams(dimension_semantics=sem, vmem_limit_bytes=mib << 20)


def _mm(a, b):
    return jnp.dot(a.astype(MXU_DTYPE), b.astype(MXU_DTYPE), preferred_element_type=F32)


def _mm_nt(a, b):
    return lax.dot_general(a.astype(MXU_DTYPE), b.astype(MXU_DTYPE), (((1,), (1,)), ((), ())),
                           preferred_element_type=F32)


def _mm_tn(a, b):
    return lax.dot_general(a.astype(MXU_DTYPE), b.astype(MXU_DTYPE), (((0,), (0,)), ((), ())),
                           preferred_element_type=F32)


def _sigmoid(z):
    return 1.0 / (1.0 + jnp.exp(-z))


def _rms(x):
    r = lax.rsqrt(jnp.mean(x * x, axis=-1, keepdims=True) + EPS)
    return x * r, r


def _rms_bwd(dn, n, r):
    return r * (dn - n * jnp.mean(dn * n, axis=-1, keepdims=True))


def _rope(b, c, sa, sb):
    return b * c + pltpu.roll(b, 112, 1) * sa + pltpu.roll(b, 16, 1) * sb


def _rope_t(d, c, sa, sb):
    return d * c + pltpu.roll(d * sa, 16, 1) + pltpu.roll(d * sb, 112, 1)


def _all_gather(blk):
    R = blk.shape[0]

    def body(x_ref, out_ref, send_sems, recv_sems, local_sem):
        x, y, c = lax.axis_index("x"), lax.axis_index("y"), lax.axis_index("c")
        me, sibling = (x, y, c), (x, y, 1 - c)
        chips = [(1 - x, y), (x, 1 - y), (1 - x, 1 - y)]

        def slot(px, py, pc):
            return out_ref.at[4 * px + 2 * py + pc]

        def copy(k, block, to, src=None):
            return pltpu.make_async_remote_copy(
                src_ref=slot(*block) if src is None else src, dst_ref=slot(*block),
                send_sem=send_sems.at[k], recv_sem=recv_sems.at[k],
                device_id=to, device_id_type=pl.DeviceIdType.MESH)

        mine = pltpu.make_async_copy(x_ref, slot(*me), local_sem)
        mine.start()
        first = [copy(0, me, sibling, src=x_ref)]
        first += [copy(1 + j, me, (*chip, c), src=x_ref) for j, chip in enumerate(chips)]
        for cp in first:
            cp.start()
        passed = [copy(4 + j, (*chip, c), sibling) for j, chip in enumerate(chips)]
        for j, chip in enumerate(chips):
            copy(1 + j, (*chip, c), me).wait_recv()
            passed[j].start()
        copy(0, sibling, me).wait_recv()
        for j, chip in enumerate(chips):
            copy(4 + j, (*chip, 1 - c), me).wait_recv()
        for cp in first + passed:
            cp.wait_send()
        mine.wait()

    return pl.pallas_call(
        body, name="all_gather",
        out_shape=jax.ShapeDtypeStruct((N_DEV, R, 128), blk.dtype),
        in_specs=[pl.BlockSpec(memory_space=pl.ANY)],
        out_specs=pl.BlockSpec(memory_space=pl.ANY),
        scratch_shapes=[pltpu.SemaphoreType.DMA((7,)), pltpu.SemaphoreType.DMA((7,)), pltpu.SemaphoreType.DMA(())],
    )(blk)


def _exchange(send):
    R = send.shape[1]

    def body(s_ref, r_ref, send_sems, recv_sems, local_sem):
        x, y, c = lax.axis_index("x"), lax.axis_index("y"), lax.axis_index("c")
        me = 4 * x + 2 * y + c
        local = pltpu.make_async_copy(s_ref.at[me], r_ref.at[me], local_sem)
        local.start()

        def peer(k):
            px = 1 - x if (k >> 2) & 1 else x
            py = 1 - y if (k >> 1) & 1 else y
            pc = 1 - c if k & 1 else c
            return (px, py, pc), 4 * px + 2 * py + pc

        copies = []
        for k in range(1, N_DEV):
            pos, pid = peer(k)
            cp = pltpu.make_async_remote_copy(
                src_ref=s_ref.at[pid], dst_ref=r_ref.at[me],
                send_sem=send_sems.at[k], recv_sem=recv_sems.at[k],
                device_id=pos, device_id_type=pl.DeviceIdType.MESH)
            cp.start()
            copies.append(cp)
        for k in range(1, N_DEV):
            pos, pid = peer(k)
            pltpu.make_async_remote_copy(
                src_ref=s_ref.at[pid], dst_ref=r_ref.at[pid],
                send_sem=send_sems.at[k], recv_sem=recv_sems.at[k],
                device_id=pos, device_id_type=pl.DeviceIdType.MESH).wait_recv()
        for cp in copies:
            cp.wait_send()
        local.wait()

    return pl.pallas_call(
        body, name="grad_exchange",
        out_shape=jax.ShapeDtypeStruct((N_DEV, R, 128), send.dtype),
        in_specs=[pl.BlockSpec(memory_space=pl.ANY)],
        out_specs=pl.BlockSpec(memory_space=pl.ANY),
        scratch_shapes=[pltpu.SemaphoreType.DMA((N_DEV,)), pltpu.SemaphoreType.DMA((N_DEV,)),
                        pltpu.SemaphoreType.DMA(())],
    )(send)


def _reduce_adamw(recv, w, m, v):
    R = w.shape[0]
    tr = 128 if R % 128 == 0 else 8
    c1 = 1.0 - ADAM_B1 ** ADAM_STEP
    c2 = 1.0 - ADAM_B2 ** ADAM_STEP

    def body(r_ref, w_ref, m_ref, v_ref, g_ref, d_ref, nm_ref, nv_ref):
        g = r_ref[0]
        for s in range(1, N_DEV):
            g = g + r_ref[s]
        m2 = ADAM_B1 * m_ref[...] + (1.0 - ADAM_B1) * g
        v2 = ADAM_B2 * v_ref[...] + (1.0 - ADAM_B2) * (g * g)
        m_hat = m2 / c1
        v_hat = v2 / c2
        g_ref[...] = g
        d_ref[...] = -ADAM_LR * (m_hat / (jnp.sqrt(v_hat) + ADAM_EPS) + ADAM_WD * w_ref[...])
        nm_ref[...] = m2
        nv_ref[...] = v2

    row = pl.BlockSpec((tr, 1024), lambda i: (i, 0))
    return pl.pallas_call(
        body, name="reduce_adamw", grid=(R // tr,),
        in_specs=[pl.BlockSpec((N_DEV, tr, 1024), lambda i: (0, i, 0)), row, row, row],
        out_specs=[row, row, row, row],
        out_shape=[jax.ShapeDtypeStruct((R, 1024), F32)] * 4,
        compiler_params=_params(("parallel",), 40),
    )(recv, w, m, v)


def _inproj_fwd(x, pre_g, w_pad):
    S = x.shape[0]
    tm, tn = min(512, S), 1536

    def body(x_ref, g_ref, w_ref, proj_ref, xn_ref):
        @pl.when(pl.program_id(1) == 0)
        def _():
            n, _ = _rms(x_ref[...])
            xn_ref[...] = (n * g_ref[...]).astype(xn_ref.dtype)
        proj_ref[...] = jnp.dot(xn_ref[...], w_ref[...], preferred_element_type=F32)

    return pl.pallas_call(
        body, name="inproj_fwd", grid=(S // tm, DP // tn),
        in_specs=[pl.BlockSpec((tm, D), lambda i, j: (i, 0)), pl.BlockSpec((1, D), lambda i, j: (0, 0)),
                  pl.BlockSpec((D, tn), lambda i, j: (0, j))],
        out_specs=[pl.BlockSpec((tm, tn), lambda i, j: (i, j)), pl.BlockSpec((tm, D), lambda i, j: (i, 0))],
        out_shape=[jax.ShapeDtypeStruct((S, DP), F32), jax.ShapeDtypeStruct((S, D), MXU_DTYPE)],
        compiler_params=_params(("parallel", "arbitrary"), 48),
    )(x, pre_g, w_pad)


def _sgu_tri():
    return lax.broadcasted_iota(jnp.int32, (SGU_T, SGU_T), 0) >= lax.broadcasted_iota(jnp.int32, (SGU_T, SGU_T), 1)


def _sgu_mix(ws_m, vb, bfull, grp):
    mixed = bfull
    for g in range(SGU_G):
        mixed = mixed + jnp.where(grp == g, _mm(ws_m[g], vb), 0.0)
    return mixed


def _layernorm(v, g, b):
    xc = v - jnp.mean(v, axis=-1, keepdims=True)
    rstd = lax.rsqrt(jnp.mean(xc * xc, axis=-1, keepdims=True) + EPS)
    vhat = xc * rstd
    return vhat * g + b, vhat, rstd


def _sgu_fwd(proj, ln_g, ln_b, ws, bfull):
    S = proj.shape[0]
    tm = min(512, S)

    def body(u_ref, v_ref, z_ref, g_ref, b_ref, ws_ref, bf_ref, ya_ref):
        tri = _sgu_tri()
        ws_m = [jnp.where(tri, ws_ref[g], 0.0).astype(MXU_DTYPE) for g in range(SGU_G)]
        grp = lax.broadcasted_iota(jnp.int32, (1, BW), 1) // (BW // SGU_G)
        for b in range(tm // SGU_T):
            r = slice(b * SGU_T, (b + 1) * SGU_T)
            vln, _, _ = _layernorm(v_ref[r, :], g_ref[...], b_ref[...])
            mixed = _sgu_mix(ws_m, vln.astype(MXU_DTYPE), bf_ref[...], grp)
            z = z_ref[r, :]
            ya_ref[r, :] = (u_ref[r, :] * mixed * (z * _sigmoid(z))).astype(ya_ref.dtype)

    blk = lambda cb: pl.BlockSpec((tm, BW), lambda i, cb=cb: (i, cb))
    vec = pl.BlockSpec((1, BW), lambda i: (0, 0))
    return pl.pallas_call(
        body, name="sgu_fwd", grid=(S // tm,),
        in_specs=[blk(UA // BW), blk(VA // BW), blk(ZA // BW), vec, vec,
                  pl.BlockSpec((SGU_G, SGU_T, SGU_T), lambda i: (0, 0, 0)),
                  pl.BlockSpec((SGU_T, BW), lambda i: (0, 0))],
        out_specs=pl.BlockSpec((tm, BW), lambda i: (i, 0)),
        out_shape=jax.ShapeDtypeStruct((S, BW), MXU_DTYPE),
        compiler_params=_params(("parallel",), 32),
    )(proj, proj, proj, ln_g, ln_b, ws, bfull)


def _mla_prep_fwd(proj, gq, gkv, wuq, wuk, wuv, cos_t, sin_a, sin_b):
    S = proj.shape[0]
    tm = min(512, S)

    def body(p_ref, gq_ref, gkv_ref, wuq_ref, wuk_ref, wuv_ref, c_ref, sa_ref, sb_ref, q_ref, k_ref, v_ref):
        nq, _ = _rms(p_ref[:, 0:Q_RANK])
        nkv, _ = _rms(p_ref[:, Q_RANK:Q_RANK + KV_RANK])
        cq = (nq * gq_ref[...]).astype(MXU_DTYPE)
        ckv = (nkv * gkv_ref[...]).astype(MXU_DTYPE)
        qf = _mm(cq, wuq_ref[...])
        kf = _mm(ckv, wuk_ref[...])
        v_ref[...] = _mm(ckv, wuv_ref[...]).astype(v_ref.dtype)
        c, sa, sb = c_ref[...], sa_ref[...], sb_ref[...]
        krr = _rope(p_ref[:, Q_RANK + KV_RANK:BW], c, sa, sb)
        for h in range(MLA_H):
            sl = slice(128 * h, 128 * (h + 1))
            q_ref[:, sl] = _rope(qf[:, sl], c, sa, sb).astype(q_ref.dtype)
            k_ref[:, sl] = (kf[:, sl] + krr).astype(k_ref.dtype)

    full = lambda a: pl.BlockSpec(a.shape, lambda i: (0,) * a.ndim)
    tab = pl.BlockSpec((tm, 128), lambda i: (i, 0))
    return pl.pallas_call(
        body, name="mla_prep_fwd", grid=(S // tm,),
        in_specs=[pl.BlockSpec((tm, BW), lambda i: (i, QD // BW)), full(gq), full(gkv), full(wuq), full(wuk),
                  full(wuv), tab, tab, tab],
        out_specs=[pl.BlockSpec((tm, 1024), lambda i: (i, 0)), pl.BlockSpec((tm, 1024), lambda i: (i, 0)),
                   pl.BlockSpec((tm, BW), lambda i: (i, 0))],
        out_shape=[jax.ShapeDtypeStruct((S, 1024), MXU_DTYPE), jax.ShapeDtypeStruct((S, 1024), MXU_DTYPE),
                   jax.ShapeDtypeStruct((S, BW), MXU_DTYPE)],
        compiler_params=_params(("parallel",), 40),
    )(proj, gq, gkv, wuq, wuk, wuv, cos_t, sin_a, sin_b)


def _chunk_mask(T):
    rc = lax.broadcasted_iota(jnp.int32, (T, T), 0) >> 6
    cc = lax.broadcasted_iota(jnp.int32, (T, T), 1) >> 6
    return cc <= rc


def _flash_fwd(qh, kh, vh):
    S = qh.shape[0]
    T = min(512, S)
    n = S // T
    scale = MLA_QK ** -0.5

    def body(q_ref, k_ref, v_ref, o_ref, lse_ref, m_scr, l_scr, acc_scr):
        qi, ki = pl.program_id(1), pl.program_id(2)

        @pl.when(ki == 0)
        def _():
            m_scr[...] = jnp.full(m_scr.shape, NEG, F32)
            l_scr[...] = jnp.zeros(l_scr.shape, F32)
            acc_scr[...] = jnp.zeros(acc_scr.shape, F32)

        def step(masked):
            head = lax.broadcasted_iota(jnp.int32, (1, 128), 1) // 64
            for j in range(2):
                sl = slice(128 * j, 128 * (j + 1))
                s = _mm_nt(q_ref[:, sl], k_ref[:, sl]) * scale
                if masked:
                    s = jnp.where(_chunk_mask(T), s, NEG)
                m_old = m_scr[j]
                m_new = jnp.maximum(m_old, jnp.max(s, axis=1, keepdims=True))
                alpha = jnp.exp(m_old - m_new)
                p = jnp.exp(s - m_new)
                l_scr[j] = alpha * l_scr[j] + jnp.sum(p, axis=1, keepdims=True)
                vj = jnp.where(head == j, v_ref[...], 0)
                acc_scr[j] = alpha * acc_scr[j] + _mm(p, vj)
                m_scr[j] = m_new

        @pl.when(ki < qi)
        def _():
            step(False)

        @pl.when(ki == qi)
        def _():
            step(True)
            o_ref[...] = acc_scr[0] / l_scr[0] + acc_scr[1] / l_scr[1]
            for j in range(2):
                lse_ref[0, :, j:j + 1] = m_scr[j] + jnp.log(l_scr[j])

    kmap = lambda hp, qi, ki: (jnp.minimum(ki, qi), hp)
    return pl.pallas_call(
        body, name="flash_fwd", grid=(MLA_H // 2, n, n),
        in_specs=[pl.BlockSpec((T, 256), lambda hp, qi, ki: (qi, hp)), pl.BlockSpec((T, 256), kmap),
                  pl.BlockSpec((T, 128), kmap)],
        out_specs=[pl.BlockSpec((T, 128), lambda hp, qi, ki: (qi, hp)),
                   pl.BlockSpec((1, T, 2), lambda hp, qi, ki: (hp, qi, 0))],
        out_shape=[jax.ShapeDtypeStruct((S, BW), F32), jax.ShapeDtypeStruct((MLA_H // 2, S, 2), F32)],
        scratch_shapes=[pltpu.VMEM((2, T, 1), F32), pltpu.VMEM((2, T, 1), F32), pltpu.VMEM((2, T, 128), F32)],
        compiler_params=_params(("parallel", "parallel", "arbitrary"), 40),
    )(qh, kh, vh)


def _band_specs(S):
    q = pl.BlockSpec((BAND_TQ, 128), lambda hp, qi: (qi, QC // 128 + hp))
    ks = [pl.BlockSpec((BAND_TQ, 128), lambda hp, qi, t=t: (jnp.maximum(qi - 2 + t, 0), KC // 128 + hp))
          for t in range(3)]
    vs = [pl.BlockSpec((BAND_TQ, 128), lambda hp, qi, t=t: (jnp.maximum(qi - 2 + t, 0), VC // 128 + hp))
          for t in range(3)]
    bias = pl.BlockSpec((2, BAND_TQ, BAND_W), lambda hp, qi: (hp, 0, 0))
    return q, ks, vs, bias


def _band_probs(qj, kcat, bias_j, valid, scale):
    s = _mm_nt(qj, kcat) * scale + bias_j
    s = jnp.where(valid, s, NEG)
    p = jnp.exp(s - jnp.max(s, axis=1, keepdims=True))
    return p / jnp.sum(p, axis=1, keepdims=True)


def _band_valid(qi):
    tile = lax.broadcasted_iota(jnp.int32, (1, BAND_W), 1) // BAND_TQ
    return tile + qi >= 2


def _band_fwd(proj, bias):
    S = proj.shape[0]
    scale = CA_DH ** -0.5

    def body(q_ref, k0, k1, k2, v0, v1, v2, b_ref, o_ref):
        qi = pl.program_id(1)
        head = lax.broadcasted_iota(jnp.int32, (1, 128), 1) // 64
        kcat = jnp.concatenate([k0[...], k1[...], k2[...]], axis=0).astype(MXU_DTYPE)
        vcat = jnp.concatenate([v0[...], v1[...], v2[...]], axis=0).astype(MXU_DTYPE)
        valid = _band_valid(qi)
        q = q_ref[...]
        o = jnp.zeros((BAND_TQ, 128), F32)
        for j in range(2):
            pn = _band_probs(jnp.where(head == j, q, 0.0), kcat, b_ref[j], valid, scale)
            o = o + _mm(pn, jnp.where(head == j, vcat, 0))
        o_ref[...] = o

    q, ks, vs, bspec = _band_specs(S)
    return pl.pallas_call(
        body, name="band_fwd", grid=(CA_H // 2, S // BAND_TQ),
        in_specs=[q, *ks, *vs, bspec],
        out_specs=pl.BlockSpec((BAND_TQ, 128), lambda hp, qi: (qi, hp)),
        out_shape=jax.ShapeDtypeStruct((S, BW), F32),
        compiler_params=_params(("parallel", "arbitrary"), 40),
    )(proj, proj, proj, proj, proj, proj, proj, bias)


def _merge_fwd(x, ya, ob, oc, proj, gate_b, wbr, wout, post_g):
    S = x.shape[0]
    tm = min(256, S)

    def body(x_ref, ya_ref, ob_ref, oc_ref, zb_ref, zc_ref, g0, g1, g2, gb_ref, wbr_ref, wo_ref, pg_ref,
             xo_ref, mg_ref, h_ref):
        zb, zc = zb_ref[...], zc_ref[...]
        ys = [ya_ref[...], ob_ref[...] * (zb * _sigmoid(zb)), oc_ref[...] * (zc * _sigmoid(zc))]
        merged = jnp.zeros((tm, D), F32)
        for i, g_ref in enumerate((g0, g1, g2)):
            merged = merged + _sigmoid(g_ref[...] + gb_ref[i:i + 1, :]) * _mm(ys[i], wbr_ref[i])
        mg_ref[...] = merged.astype(mg_ref.dtype)
        h = _mm(merged, wo_ref[...])
        h_ref[...] = h
        n, _ = _rms(h)
        xo_ref[...] = x_ref[...] + n * pg_ref[...]

    row = lambda w, cb=0: pl.BlockSpec((tm, w), lambda i, cb=cb: (i, cb))
    full = lambda a: pl.BlockSpec(a.shape, lambda i: (0,) * a.ndim)
    return pl.pallas_call(
        body, name="merge_fwd", grid=(S // tm,),
        in_specs=[row(D), row(BW), row(BW), row(BW), row(BW, ZB // BW), row(BW, ZC // BW),
                  row(D, 0), row(D, 1), row(D, 2), full(gate_b), full(wbr), full(wout), full(post_g)],
        out_specs=[row(D), row(D), row(D)],
        out_shape=[jax.ShapeDtypeStruct((S, D), F32), jax.ShapeDtypeStruct((S, D), MXU_DTYPE),
                   jax.ShapeDtypeStruct((S, D), F32)],
        compiler_params=_params(("parallel",), 56),
    )(x, ya, ob, oc, proj, proj, proj, proj, proj, gate_b, wbr, wout, post_g)


def _loss_fwd_bwd(y, target):
    S = y.shape[0]
    tm = min(512, S)

    def body(y_ref, t_ref, loss_ref, dy_ref):
        @pl.when(pl.program_id(0) == 0)
        def _():
            loss_ref[...] = jnp.zeros((1, 1), F32)
        err = y_ref[...] - t_ref[...]
        loss_ref[...] += 0.5 * jnp.sum(jnp.mean(err * err, axis=-1, keepdims=True), axis=0, keepdims=True)
        dy_ref[...] = err * (1.0 / D)

    row = pl.BlockSpec((tm, D), lambda i: (i, 0))
    return pl.pallas_call(
        body, name="loss", grid=(S // tm,),
        in_specs=[row, row],
        out_specs=[pl.BlockSpec((1, 1), lambda i: (0, 0)), row],
        out_shape=[jax.ShapeDtypeStruct((1, 1), F32), jax.ShapeDtypeStruct((S, D), F32)],
        compiler_params=_params(("arbitrary",), 32),
    )(y, target)


def _first_step_zero(refs, first):
    @pl.when(first)
    def _():
        for r in refs:
            r[...] = jnp.zeros(r.shape, r.dtype)


def _out_bwd(dxo, h, merged, post_g, wout_t):
    S = dxo.shape[0]
    tm = min(256, S)

    def body(d_ref, h_ref, mg_ref, pg_ref, wt_ref, dm_ref, dw_ref, dg_ref):
        _first_step_zero((dw_ref, dg_ref), pl.program_id(0) == 0)
        d = d_ref[...]
        hn, r = _rms(h_ref[...])
        dg_ref[...] += jnp.sum(d * hn, axis=0, keepdims=True)
        dh = _rms_bwd(d * pg_ref[...], hn, r)
        dm_ref[...] = _mm(dh, wt_ref[...])
        dw_ref[...] += _mm_tn(mg_ref[...], dh)

    row = pl.BlockSpec((tm, D), lambda i: (i, 0))
    full = lambda shape: pl.BlockSpec(shape, lambda i: (0,) * len(shape))
    return pl.pallas_call(
        body, name="out_bwd", grid=(S // tm,),
        in_specs=[row, row, row, full((1, D)), full((D, D))],
        out_specs=[row, full((D, D)), full((1, D))],
        out_shape=[jax.ShapeDtypeStruct((S, D), F32), jax.ShapeDtypeStruct((D, D), F32),
                   jax.ShapeDtypeStruct((1, D), F32)],
        compiler_params=_params(("arbitrary",), 40),
    )(dxo, h, merged, post_g, wout_t)


def _gate_bwd(dm, ya, ob, oc, proj, gate_b, wbr, wbr_t):
    S = dm.shape[0]
    tm = min(128, S)

    def body(dm_ref, ya_ref, ob_ref, oc_ref, zb_ref, zc_ref, g0, g1, g2, gb_ref, wbr_ref, wbt_ref,
             dg_ref, dzb_ref, dzc_ref, dya_ref, dob_ref, doc_ref, dwbr_ref, dgb_ref):
        _first_step_zero((dwbr_ref, dgb_ref), pl.program_id(0) == 0)
        dmv = dm_ref[...]
        zb, zc = zb_ref[...], zc_ref[...]
        sgb, sgc = _sigmoid(zb), _sigmoid(zc)
        ob, oc = ob_ref[...], oc_ref[...]
        ys = [ya_ref[...], (ob * (zb * sgb)).astype(MXU_DTYPE), (oc * (zc * sgc)).astype(MXU_DTYPE)]
        dys = []
        for i, g_ref in enumerate((g0, g1, g2)):
            br = _mm(ys[i], wbr_ref[i])
            gate = _sigmoid(g_ref[...] + gb_ref[i:i + 1, :])
            dgl = dmv * br * (gate * (1.0 - gate))
            dg_ref[:, D * i:D * (i + 1)] = dgl
            dgb_ref[i:i + 1, :] += jnp.sum(dgl, axis=0, keepdims=True)
            dbr = dmv * gate
            dys.append(_mm(dbr, wbt_ref[i]))
            dwbr_ref[i] += _mm_tn(ys[i], dbr)
        dya_ref[...] = dys[0]
        dob_ref[...] = dys[1] * (zb * sgb)
        dzb_ref[...] = dys[1] * ob * (sgb * (1.0 + zb * (1.0 - sgb)))
        doc_ref[...] = dys[2] * (zc * sgc)
        dzc_ref[...] = dys[2] * oc * (sgc * (1.0 + zc * (1.0 - sgc)))

    row = lambda w, cb=0: pl.BlockSpec((tm, w), lambda i, cb=cb: (i, cb))
    full = lambda a: pl.BlockSpec(a.shape, lambda i: (0,) * a.ndim)
    sds = lambda w: jax.ShapeDtypeStruct((S, w), F32)
    return pl.pallas_call(
        body, name="gate_bwd", grid=(S // tm,),
        in_specs=[row(D), row(BW), row(BW), row(BW), row(BW, ZB // BW), row(BW, ZC // BW),
                  row(D, 0), row(D, 1), row(D, 2), full(gate_b), full(wbr), full(wbr_t)],
        out_specs=[row(3 * D), row(BW), row(BW), row(BW), row(BW), row(BW),
                   pl.BlockSpec((N_BRANCH, BW, D), lambda i: (0, 0, 0)), pl.BlockSpec((N_BRANCH, D), lambda i: (0, 0))],
        out_shape=[sds(3 * D), sds(BW), sds(BW), sds(BW), sds(BW), sds(BW),
                   jax.ShapeDtypeStruct((N_BRANCH, BW, D), F32), jax.ShapeDtypeStruct((N_BRANCH, D), F32)],
        compiler_params=_params(("arbitrary",), 56),
    )(dm, ya, ob, oc, proj, proj, proj, proj, proj, gate_b, wbr, wbr_t)


def _band_bwd(proj, bias, do):
    S = proj.shape[0]
    scale = CA_DH ** -0.5

    def body(q_ref, k0, k1, k2, v0, v1, v2, b_ref, do_ref, dq_ref, dk_ref, dv_ref, db_ref):
        qi = pl.program_id(1)
        _first_step_zero((dk_ref, dv_ref, db_ref), qi == 0)
        head = lax.broadcasted_iota(jnp.int32, (1, 128), 1) // 64
        kcat = jnp.concatenate([k0[...], k1[...], k2[...]], axis=0).astype(MXU_DTYPE)
        vcat = jnp.concatenate([v0[...], v1[...], v2[...]], axis=0).astype(MXU_DTYPE)
        valid = _band_valid(qi)
        q, dov = q_ref[...], do_ref[...]
        dq = jnp.zeros((BAND_TQ, 128), F32)
        dk = jnp.zeros((BAND_W, 128), F32)
        dv = jnp.zeros((BAND_W, 128), F32)
        for j in range(2):
            qj = jnp.where(head == j, q, 0.0).astype(MXU_DTYPE)
            doj = jnp.where(head == j, dov, 0.0).astype(MXU_DTYPE)
            pn = _band_probs(qj, kcat, b_ref[j], valid, scale)
            dv = dv + _mm_tn(pn, doj)
            dp = _mm_nt(doj, vcat)
            ds = pn * (dp - jnp.sum(pn * dp, axis=1, keepdims=True))
            db_ref[j] += ds
            dsb = (ds * scale).astype(MXU_DTYPE)
            dq = dq + _mm(dsb, jnp.where(head == j, kcat, 0))
            dk = dk + _mm_tn(dsb, qj)
        dq_ref[...] = dq
        for t in range(3):
            @pl.when(qi - 2 + t >= 0)
            def _(t=t):
                rows = pl.ds(pl.multiple_of((qi - 2 + t) * BAND_TQ, BAND_TQ), BAND_TQ)
                dk_ref[rows, :] += dk[t * BAND_TQ:(t + 1) * BAND_TQ]
                dv_ref[rows, :] += dv[t * BAND_TQ:(t + 1) * BAND_TQ]

    q, ks, vs, bspec = _band_specs(S)
    col = pl.BlockSpec((S, 128), lambda hp, qi: (0, hp))
    return pl.pallas_call(
        body, name="band_bwd", grid=(CA_H // 2, S // BAND_TQ),
        in_specs=[q, *ks, *vs, bspec, pl.BlockSpec((BAND_TQ, 128), lambda hp, qi: (qi, hp))],
        out_specs=[pl.BlockSpec((BAND_TQ, 128), lambda hp, qi: (qi, hp)), col, col, bspec],
        out_shape=[jax.ShapeDtypeStruct((S, BW), F32)] * 3 + [jax.ShapeDtypeStruct((CA_H, BAND_TQ, BAND_W), F32)],
        compiler_params=_params(("parallel", "arbitrary"), 56),
    )(proj, proj, proj, proj, proj, proj, proj, bias, do)


def _bias_fold(db):
    n_chunk = BAND_TQ // CHUNK
    band = (LEFT_CHUNKS + 1) * CHUNK
    lo = REL_CLIP - (CHUNK - 1)

    def body(db_ref, out_ref):
        i = lax.broadcasted_iota(jnp.int32, (CHUNK, band), 0)
        j = lax.broadcasted_iota(jnp.int32, (CHUNK, band), 1)
        idx = jnp.clip(i + LEFT_CHUNKS * CHUNK - j, -REL_CLIP, REL_CLIP) + REL_CLIP
        lane = lax.broadcasted_iota(jnp.int32, (1, 384), 1)
        for h in range(CA_H):
            dc = db_ref[h, 0:CHUNK, 0:band]
            for c in range(1, n_chunk):
                dc = dc + db_ref[h, c * CHUNK:(c + 1) * CHUNK, c * CHUNK:c * CHUNK + band]

            def fold(r, acc):
                val = jnp.sum(jnp.where(idx == r, dc, 0.0), keepdims=True)
                return acc + jnp.where(lane == r, val, 0.0)

            out_ref[h:h + 1, :] = lax.fori_loop(lo, 2 * REL_CLIP + 1, fold, jnp.zeros((1, 384), F32))

    return pl.pallas_call(
        body, name="bias_fold",
        out_shape=jax.ShapeDtypeStruct((CA_H, 384), F32),
        in_specs=[pl.BlockSpec(memory_space=pltpu.VMEM)],
        out_specs=pl.BlockSpec(memory_space=pltpu.VMEM),
        compiler_params=pltpu.CompilerParams(vmem_limit_bytes=40 << 20),
    )(db)


def _flash_bwd(qh, kh, vh, o, lse, do):
    S = qh.shape[0]
    T = min(512, S)
    n = S // T
    scale = MLA_QK ** -0.5

    def body(q_ref, k_ref, v_ref, o_ref, lse_ref, do_ref, dq_ref, dk_ref, dv_ref, dk_scr, dv_scr):
        ki, qi = pl.program_id(1), pl.program_id(2)

        @pl.when(qi == ki)
        def _():
            dk_scr[...] = jnp.zeros(dk_scr.shape, F32)
            dv_scr[...] = jnp.zeros(dv_scr.shape, F32)

        def step(masked):
            head = lax.broadcasted_iota(jnp.int32, (1, 128), 1) // 64
            dov, ov = do_ref[...], o_ref[...]
            rows = pl.ds(pl.multiple_of(qi * T, T), T)
            dv = jnp.zeros((T, 128), F32)
            for j in range(2):
                sl = slice(128 * j, 128 * (j + 1))
                qj, kj = q_ref[:, sl], k_ref[:, sl]
                s = _mm_nt(qj, kj) * scale
                if masked:
                    s = jnp.where(_chunk_mask(T), s, NEG)
                p = jnp.exp(s - lse_ref[0, :, j:j + 1])
                doj = jnp.where(head == j, dov, 0.0)
                delta = jnp.sum(doj * ov, axis=1, keepdims=True)
                dv = dv + _mm_tn(p, doj)
                dp = _mm_nt(doj, v_ref[...])
                dsb = (p * (dp - delta) * scale).astype(MXU_DTYPE)
                dk_scr[:, sl] += _mm_tn(dsb, qj)
                dqj = _mm(dsb, kj)

                @pl.when(ki == 0)
                def _():
                    dq_ref[rows, sl] = dqj

                @pl.when(ki > 0)
                def _():
                    dq_ref[rows, sl] += dqj
            dv_scr[...] += dv

        @pl.when(qi > ki)
        def _():
            step(False)

        @pl.when(qi == ki)
        def _():
            step(True)

        @pl.when(qi == n - 1)
        def _():
            dk_ref[...] = dk_scr[...]
            dv_ref[...] = dv_scr[...]

    qmap = lambda hp, ki, qi: (jnp.maximum(qi, ki), hp)
    return pl.pallas_call(
        body, name="flash_bwd", grid=(MLA_H // 2, n, n),
        in_specs=[pl.BlockSpec((T, 256), qmap), pl.BlockSpec((T, 256), lambda hp, ki, qi: (ki, hp)),
                  pl.BlockSpec((T, 128), lambda hp, ki, qi: (ki, hp)), pl.BlockSpec((T, 128), qmap),
                  pl.BlockSpec((1, T, 2), lambda hp, ki, qi: (hp, jnp.maximum(qi, ki), 0)),
                  pl.BlockSpec((T, 128), qmap)],
        out_specs=[pl.BlockSpec((S, 256), lambda hp, ki, qi: (0, hp)),
                   pl.BlockSpec((T, 256), lambda hp, ki, qi: (ki, hp)),
                   pl.BlockSpec((T, 128), lambda hp, ki, qi: (ki, hp))],
        out_shape=[jax.ShapeDtypeStruct((S, 1024), F32), jax.ShapeDtypeStruct((S, 1024), F32),
                   jax.ShapeDtypeStruct((S, BW), F32)],
        scratch_shapes=[pltpu.VMEM((T, 256), F32), pltpu.VMEM((T, 128), F32)],
        compiler_params=_params(("parallel", "arbitrary", "arbitrary"), 56),
    )(qh, kh, vh, o, lse, do)


def _mla_prep_bwd(proj, dqf, dkf, dvf, gq, gkv, wuq_t, wuk_t, wuv_t, cos_t, sin_a, sin_b):
    S = proj.shape[0]
    tm = min(512, S)

    def body(p_ref, dq_ref, dk_ref, dv_ref, gq_ref, gkv_ref, wq_ref, wk_ref, wv_ref, c_ref, sa_ref, sb_ref,
             db_ref, dwq_ref, dwk_ref, dwv_ref, dgq_ref, dgkv_ref):
        _first_step_zero((dwq_ref, dwk_ref, dwv_ref, dgq_ref, dgkv_ref), pl.program_id(0) == 0)
        c, sa, sb = c_ref[...], sa_ref[...], sb_ref[...]
        nq, rq = _rms(p_ref[:, 0:Q_RANK])
        nkv, rkv = _rms(p_ref[:, Q_RANK:Q_RANK + KV_RANK])
        cq = (nq * gq_ref[...]).astype(MXU_DTYPE)
        ckv = (nkv * gkv_ref[...]).astype(MXU_DTYPE)
        dkr = jnp.zeros((tm, 128), F32)
        dq_pre = []
        for h in range(MLA_H):
            sl = slice(128 * h, 128 * (h + 1))
            dq_pre.append(_rope_t(dq_ref[:, sl], c, sa, sb).astype(MXU_DTYPE))
            dkr = dkr + dk_ref[:, sl]
        dq_pre = jnp.concatenate(dq_pre, axis=1)
        dcq = _mm(dq_pre, wq_ref[...])
        dwq_ref[...] += _mm_tn(cq, dq_pre)
        dgq_ref[...] += jnp.sum(dcq * nq, axis=0, keepdims=True)
        db_ref[:, 0:Q_RANK] = _rms_bwd(dcq * gq_ref[...], nq, rq)
        dk = dk_ref[...].astype(MXU_DTYPE)
        dv = dv_ref[...].astype(MXU_DTYPE)
        dckv = _mm(dk, wk_ref[...]) + _mm(dv, wv_ref[...])
        dwk_ref[...] += _mm_tn(ckv, dk)
        dwv_ref[...] += _mm_tn(ckv, dv)
        dgkv_ref[...] += jnp.sum(dckv * nkv, axis=0, keepdims=True)
        db_ref[:, Q_RANK:Q_RANK + KV_RANK] = _rms_bwd(dckv * gkv_ref[...], nkv, rkv)
        lane = lax.broadcasted_iota(jnp.int32, (1, 128), 1)
        rope_lanes = (lane >= MLA_NOPE) & (lane < MLA_QK)
        db_ref[:, Q_RANK + KV_RANK:BW] = _rope_t(jnp.where(rope_lanes, dkr, 0.0), c, sa, sb)

    full = lambda a: pl.BlockSpec(a.shape, lambda i: (0,) * a.ndim)
    fulls = lambda shape: pl.BlockSpec(shape, lambda i: (0,) * len(shape))
    tab = pl.BlockSpec((tm, 128), lambda i: (i, 0))
    row = lambda w, cb=0: pl.BlockSpec((tm, w), lambda i, cb=cb: (i, cb))
    return pl.pallas_call(
        body, name="mla_prep_bwd", grid=(S // tm,),
        in_specs=[row(BW, QD // BW), row(1024), row(1024), row(BW), full(gq), full(gkv), full(wuq_t),
                  full(wuk_t), full(wuv_t), tab, tab, tab],
        out_specs=[row(BW), fulls((Q_RANK, 1024)), fulls((KV_RANK, 1024)), fulls((KV_RANK, BW)),
                   fulls((1, Q_RANK)), fulls((1, KV_RANK))],
        out_shape=[jax.ShapeDtypeStruct((S, BW), F32), jax.ShapeDtypeStruct((Q_RANK, 1024), F32),
                   jax.ShapeDtypeStruct((KV_RANK, 1024), F32), jax.ShapeDtypeStruct((KV_RANK, BW), F32),
                   jax.ShapeDtypeStruct((1, Q_RANK), F32), jax.ShapeDtypeStruct((1, KV_RANK), F32)],
        compiler_params=_params(("arbitrary",), 56),
    )(proj, dqf, dkf, dvf, gq, gkv, wuq_t, wuk_t, wuv_t, cos_t, sin_a, sin_b)


def _sgu_bwd(proj, dya, ln_g, ln_b, ws, bfull):
    S = proj.shape[0]
    tm = min(512, S)

    def body(u_ref, v_ref, z_ref, dy_ref, g_ref, b_ref, ws_ref, bf_ref,
             da_ref, dws_ref, dbf_ref, dlg_ref, dlb_ref, dbs_ref):
        i = pl.program_id(0)
        _first_step_zero((dws_ref, dbf_ref, dlg_ref, dlb_ref, dbs_ref), i == 0)
        tri = _sgu_tri()
        ws_m = [jnp.where(tri, ws_ref[g], 0.0).astype(MXU_DTYPE) for g in range(SGU_G)]
        grp = lax.broadcasted_iota(jnp.int32, (1, BW), 1) // (BW // SGU_G)
        for b in range(tm // SGU_T):
            r = slice(b * SGU_T, (b + 1) * SGU_T)
            vln, vhat, rstd = _layernorm(v_ref[r, :], g_ref[...], b_ref[...])
            vb = vln.astype(MXU_DTYPE)
            mixed = _sgu_mix(ws_m, vb, bf_ref[...], grp)
            u, z, dy = u_ref[r, :], z_ref[r, :], dy_ref[r, :]
            sg = _sigmoid(z)
            sz = z * sg
            da_ref[r, 0:BW] = dy * mixed * sz
            da_ref[r, 2 * BW:3 * BW] = dy * u * mixed * (sg * (1.0 + z * (1.0 - sg)))
            dmix = dy * u * sz
            dbf_ref[...] += dmix
            dvln = jnp.zeros((SGU_T, BW), F32)
            for g in range(SGU_G):
                dmg = jnp.where(grp == g, dmix, 0.0).astype(MXU_DTYPE)
                dvln = dvln + _mm_tn(ws_m[g], dmg)
                dws_ref[g] += jnp.where(tri, _mm_nt(dmg, vb), 0.0)
            dlg_ref[...] += jnp.sum(dvln * vhat, axis=0, keepdims=True)
            dlb_ref[...] += jnp.sum(dvln, axis=0, keepdims=True)
            dvh = dvln * g_ref[...]
            da_ref[r, BW:2 * BW] = rstd * (dvh - jnp.mean(dvh, axis=-1, keepdims=True)
                                           - vhat * jnp.mean(dvh * vhat, axis=-1, keepdims=True))

        @pl.when(i == pl.num_programs(0) - 1)
        def _():
            dbf = dbf_ref[...]
            for g in range(SGU_G):
                dbs_ref[:, g:g + 1] = jnp.sum(jnp.where(grp == g, dbf, 0.0), axis=1, keepdims=True)

    blk = lambda cb: pl.BlockSpec((tm, BW), lambda i, cb=cb: (i, cb))
    fulls = lambda shape: pl.BlockSpec(shape, lambda i: (0,) * len(shape))
    return pl.pallas_call(
        body, name="sgu_bwd", grid=(S // tm,),
        in_specs=[blk(UA // BW), blk(VA // BW), blk(ZA // BW), blk(0), fulls((1, BW)), fulls((1, BW)),
                  fulls((SGU_G, SGU_T, SGU_T)), fulls((SGU_T, BW))],
        out_specs=[pl.BlockSpec((tm, 3 * BW), lambda i: (i, 0)), fulls((SGU_G, SGU_T, SGU_T)),
                   fulls((SGU_T, BW)), fulls((1, BW)), fulls((1, BW)), fulls((SGU_T, SGU_G))],
        out_shape=[jax.ShapeDtypeStruct((S, 3 * BW), F32), jax.ShapeDtypeStruct((SGU_G, SGU_T, SGU_T), F32),
                   jax.ShapeDtypeStruct((SGU_T, BW), F32), jax.ShapeDtypeStruct((1, BW), F32),
                   jax.ShapeDtypeStruct((1, BW), F32), jax.ShapeDtypeStruct((SGU_T, SGU_G), F32)],
        compiler_params=_params(("arbitrary",), 40),
    )(proj, proj, proj, dya, ln_g, ln_b, ws, bfull)


def _inproj_bwd_dx(segs, w_t, x, pre_g, dxo):
    S = x.shape[0]
    tm = min(512, S)
    nk = DP // BW
    bounds = [(k0, k0 + a.shape[1] // BW) for a, k0 in segs]

    def body(*refs):
        seg_refs = refs[:len(segs)]
        w_ref, x_ref, g_ref, dxo_ref, dx_ref, dg_ref, acc = refs[len(segs):]
        i, k = pl.program_id(0), pl.program_id(1)
        _first_step_zero((dg_ref,), (i == 0) & (k == 0))

        @pl.when(k == 0)
        def _():
            acc[...] = jnp.zeros(acc.shape, F32)

        for ref, (lo, hi) in zip(seg_refs, bounds):
            @pl.when((k >= lo) & (k < hi))
            def _(ref=ref):
                acc[...] += _mm(ref[...], w_ref[...])

        @pl.when(k == nk - 1)
        def _():
            n, r = _rms(x_ref[...])
            dxn = acc[...]
            dg_ref[...] += jnp.sum(dxn * n, axis=0, keepdims=True)
            dx_ref[...] = dxo_ref[...] + _rms_bwd(dxn * g_ref[...], n, r)

    seg_specs = [pl.BlockSpec((tm, BW), lambda i, k, lo=lo, hi=hi: (i, jnp.clip(k - lo, 0, hi - lo - 1)))
                 for lo, hi in bounds]
    row = pl.BlockSpec((tm, D), lambda i, k: (i, 0))
    vec = pl.BlockSpec((1, D), lambda i, k: (0, 0))
    return pl.pallas_call(
        body, name="inproj_bwd_dx", grid=(S // tm, nk),
        in_specs=seg_specs + [pl.BlockSpec((BW, D), lambda i, k: (k, 0)), row, vec, row],
        out_specs=[row, vec],
        out_shape=[jax.ShapeDtypeStruct((S, D), F32), jax.ShapeDtypeStruct((1, D), F32)],
        scratch_shapes=[pltpu.VMEM((tm, D), F32)],
        compiler_params=_params(("arbitrary", "arbitrary"), 56),
    )(*[a for a, _ in segs], w_t, x, pre_g, dxo)


def _matmul_tn(a, b):
    S, K = a.shape
    W = b.shape[1]
    tk, tn = min(2048, S), BW

    def body(a_ref, b_ref, o_ref):
        _first_step_zero((o_ref,), pl.program_id(1) == 0)
        o_ref[...] += _mm_tn(a_ref[...], b_ref[...])

    return pl.pallas_call(
        body, name="matmul_tn", grid=(W // tn, S // tk),
        in_specs=[pl.BlockSpec((tk, K), lambda j, k: (k, 0)), pl.BlockSpec((tk, tn), lambda j, k: (k, j))],
        out_specs=pl.BlockSpec((K, tn), lambda j, k: (0, j)),
        out_shape=jax.ShapeDtypeStruct((K, W), F32),
        compiler_params=_params(("parallel", "arbitrary"), 48),
    )(a, b)


def _pad_w_in(w):
    z = lambda n: jnp.zeros((w.shape[0], n), w.dtype)
    return jnp.concatenate([w[:, 4512:], w[:, :1920], z(64), w[:, 1920:1952], z(32), w[:, 1952:4512]], axis=1)


def _unpad_dw_in(dw):
    a = dw[:, G_OFF + 3 * D:]
    return jnp.concatenate([a[:, :1920], a[:, 1984:2016], a[:, 2048:], dw[:, :3 * D]], axis=1)


def _pad_heads(w, n_head, width):
    k = w.shape[0]
    return jnp.pad(w.reshape(k, n_head, width), ((0, 0), (0, 0), (0, 128 - width))).reshape(k, n_head * 128)


def _unpad_heads(w, n_head, width):
    k = w.shape[0]
    return w.reshape(k, n_head, 128)[:, :, :width].reshape(k, n_head * width)


def _rope_tables(S):
    half = MLA_ROPE // 2
    inv = 10000.0 ** (-jnp.arange(half, dtype=F32) / half)
    ang = jnp.arange(S, dtype=F32)[:, None] * inv[None, :]
    cos, sin = jnp.cos(ang), jnp.sin(ang)
    one = lambda n: jnp.ones((S, n), F32)
    zero = lambda n: jnp.zeros((S, n), F32)
    cos_t = jnp.concatenate([one(MLA_NOPE), cos, cos, one(128 - MLA_QK)], axis=1)
    sin_a = jnp.concatenate([zero(MLA_NOPE), -sin, zero(128 - MLA_NOPE - half)], axis=1)
    sin_b = jnp.concatenate([zero(MLA_NOPE + half), sin, zero(128 - MLA_QK)], axis=1)
    return cos_t, sin_a, sin_b


def _band_bias_tile(rel_table):
    i = jnp.arange(BAND_TQ)[:, None]
    j = jnp.arange(BAND_W)[None, :]
    idx = jnp.clip(i + 2 * BAND_TQ - j, -REL_CLIP, REL_CLIP) + REL_CLIP
    ci, cj = i // CHUNK, j // CHUNK
    valid = (ci <= cj) & (cj <= ci + LEFT_CHUNKS)
    return jnp.where(valid[None], rel_table[:, idx], NEG)


SHARDED = ("w_in", "mla_w_uq", "mla_w_ukv", "w_branch", "w_out")
REPLICATED = ("pre_g", "post_g", "sgu_ln_g", "sgu_ln_b", "sgu_w", "sgu_b", "mla_q_norm_g", "mla_kv_norm_g",
              "ca_rel_bias")
PACK_ORDER = SHARDED + ("gate_b",) + REPLICATED
OUT_ORDER = ("w_in", "pre_g", "post_g", "sgu_ln_g", "sgu_ln_b", "sgu_w", "sgu_b", "mla_q_norm_g", "mla_kv_norm_g",
             "mla_w_uq", "mla_w_ukv", "ca_rel_bias", "w_branch", "gate_b", "w_out")


def _to_rows(flat, width, mult):
    n = flat.shape[-1]
    pad = (-n) % (width * mult)
    flat = jnp.pad(flat, [(0, 0)] * (flat.ndim - 1) + [(0, pad)])
    return flat.reshape(flat.shape[:-1] + (-1, width))


def _gather_shards(shards, dtype):
    names = list(shards)
    flat = jnp.concatenate([shards[n].astype(dtype).reshape(-1) for n in names])
    got = _all_gather(_to_rows(flat, 128, 16)).reshape(N_DEV, -1)
    out, off = {}, 0
    for n in names:
        size = shards[n].size
        out[n] = got[:, off:off + size].reshape((N_DEV,) + shards[n].shape)
        off += size
    return out


def _unshard(g, axis):
    g = jnp.moveaxis(g, 0, axis)
    return g.reshape(g.shape[:axis] + (g.shape[axis] * g.shape[axis + 1],) + g.shape[axis + 2:])


def _shard_rows(full, axis):
    s = full.shape
    g = full.reshape(s[:axis] + (N_DEV, s[axis] // N_DEV) + s[axis + 1:])
    return jnp.moveaxis(g, axis, 0).reshape(N_DEV, -1)


SHARD_AXIS = {"w_in": 2, "mla_w_uq": 2, "mla_w_ukv": 2, "w_branch": 3, "gate_b": 2, "w_out": 1}


def kernel(x, w_in, pre_g, post_g, sgu_ln_g, sgu_ln_b, sgu_w, sgu_b, mla_q_norm_g, mla_kv_norm_g, mla_w_uq, mla_w_ukv, ca_rel_bias, w_branch, gate_b, w_out, loss_target, m_w_in, m_pre_g, m_post_g, m_sgu_ln_g, m_sgu_ln_b, m_sgu_w, m_sgu_b, m_mla_q_norm_g, m_mla_kv_norm_g, m_mla_w_uq, m_mla_w_ukv, m_ca_rel_bias, m_w_branch, m_gate_b, m_w_out, v_w_in, v_pre_g, v_post_g, v_sgu_ln_g, v_sgu_ln_b, v_sgu_w, v_sgu_b, v_mla_q_norm_g, v_mla_kv_norm_g, v_mla_w_uq, v_mla_w_ukv, v_ca_rel_bias, v_w_branch, v_gate_b, v_w_out):
    weights = dict(w_in=w_in, pre_g=pre_g, post_g=post_g, sgu_ln_g=sgu_ln_g, sgu_ln_b=sgu_ln_b, sgu_w=sgu_w,
                   sgu_b=sgu_b, mla_q_norm_g=mla_q_norm_g, mla_kv_norm_g=mla_kv_norm_g, mla_w_uq=mla_w_uq,
                   mla_w_ukv=mla_w_ukv, ca_rel_bias=ca_rel_bias, w_branch=w_branch, gate_b=gate_b, w_out=w_out)
    mom_m = dict(w_in=m_w_in, pre_g=m_pre_g, post_g=m_post_g, sgu_ln_g=m_sgu_ln_g, sgu_ln_b=m_sgu_ln_b,
                 sgu_w=m_sgu_w, sgu_b=m_sgu_b, mla_q_norm_g=m_mla_q_norm_g, mla_kv_norm_g=m_mla_kv_norm_g,
                 mla_w_uq=m_mla_w_uq, mla_w_ukv=m_mla_w_ukv, ca_rel_bias=m_ca_rel_bias, w_branch=m_w_branch,
                 gate_b=m_gate_b, w_out=m_w_out)
    mom_v = dict(w_in=v_w_in, pre_g=v_pre_g, post_g=v_post_g, sgu_ln_g=v_sgu_ln_g, sgu_ln_b=v_sgu_ln_b,
                 sgu_w=v_sgu_w, sgu_b=v_sgu_b, mla_q_norm_g=v_mla_q_norm_g, mla_kv_norm_g=v_mla_kv_norm_g,
                 mla_w_uq=v_mla_w_uq, mla_w_ukv=v_mla_w_ukv, ca_rel_bias=v_ca_rel_bias, w_branch=v_w_branch,
                 gate_b=v_gate_b, w_out=v_w_out)
    depth = w_in.shape[0]
    S = x.shape[1]
    xs = x.reshape(S, D)

    gw = _gather_shards({n: weights[n] for n in SHARDED}, MXU_DTYPE)
    gw.update(_gather_shards({"gate_b": gate_b}, F32))
    full = {n: _unshard(gw[n], SHARD_AXIS[n]) for n in gw}
    cos_t, sin_a, sin_b = _rope_tables(S)

    layers = []
    for l in range(depth):
        kv = full["mla_w_ukv"][l].reshape(KV_RANK, MLA_H, MLA_NOPE + 64)
        lw = dict(
            w_pad=_pad_w_in(full["w_in"][l]),
            wuq=_pad_heads(full["mla_w_uq"][l], MLA_H, MLA_QK),
            wuk=_pad_heads(kv[:, :, :MLA_NOPE].reshape(KV_RANK, MLA_H * MLA_NOPE), MLA_H, MLA_NOPE),
            wuv=kv[:, :, MLA_NOPE:].reshape(KV_RANK, BW),
            wbr=full["w_branch"][l], wout=full["w_out"][l], gate_b=full["gate_b"][l],
            pre_g=pre_g[l][None], post_g=post_g[l][None], ln_g=sgu_ln_g[l][None], ln_b=sgu_ln_b[l][None],
            ws=sgu_w[l], bfull=jnp.repeat(sgu_b[l].T, BW // SGU_G, axis=1),
            gq=mla_q_norm_g[l][None], gkv=mla_kv_norm_g[l][None], bias=_band_bias_tile(ca_rel_bias[l]))
        layers.append(lw)

    saved = []
    h_x = xs
    for lw in layers:
        proj, xn = _inproj_fwd(h_x, lw["pre_g"], lw["w_pad"])
        ya = _sgu_fwd(proj, lw["ln_g"], lw["ln_b"], lw["ws"], lw["bfull"])
        qh, kh, vh = _mla_prep_fwd(proj, lw["gq"], lw["gkv"], lw["wuq"], lw["wuk"], lw["wuv"], cos_t, sin_a, sin_b)
        ob, lse = _flash_fwd(qh, kh, vh)
        oc = _band_fwd(proj, lw["bias"])
        x_new, merged, hh = _merge_fwd(h_x, ya, ob, oc, proj, lw["gate_b"], lw["wbr"], lw["wout"], lw["post_g"])
        saved.append(dict(x=h_x, proj=proj, xn=xn, ya=ya, qh=qh, kh=kh, vh=vh, ob=ob, lse=lse, oc=oc,
                          merged=merged, h=hh))
        h_x = x_new

    loss_part, dx = _loss_fwd_bwd(h_x, loss_target.reshape(S, D))
    loss = lax.psum(loss_part[0, 0], MESH_AXES)

    grads = {n: [None] * depth for n in OUT_ORDER}
    for l in reversed(range(depth)):
        lw, sv = layers[l], saved[l]
        proj = sv["proj"]
        dmerged, dw_out, dg_post = _out_bwd(dx, sv["h"], sv["merged"], lw["post_g"], lw["wout"].T)
        dgl, dzb, dzc, dya, dob, doc, dwbr, dgb = _gate_bwd(
            dmerged, sv["ya"], sv["ob"], sv["oc"], proj, lw["gate_b"], lw["wbr"], jnp.swapaxes(lw["wbr"], 1, 2))
        dqc, dkc, dvc, dbias = _band_bwd(proj, lw["bias"], doc)
        drel = _bias_fold(dbias)[:, :2 * REL_CLIP + 1]
        dqf, dkf, dvf = _flash_bwd(sv["qh"], sv["kh"], sv["vh"], sv["ob"], sv["lse"], dob)
        db, dwuq, dwuk, dwuv, dgq, dgkv = _mla_prep_bwd(
            proj, dqf, dkf, dvf, lw["gq"], lw["gkv"], lw["wuq"].T, lw["wuk"].T, lw["wuv"].T, cos_t, sin_a, sin_b)
        da, dws, _, dlg, dlb, dbs = _sgu_bwd(proj, dya, lw["ln_g"], lw["ln_b"], lw["ws"], lw["bfull"])
        segs = [(dgl, G_OFF // BW), (da, UA // BW), (db, QD // BW), (dzb, ZB // BW), (dqc, QC // BW),
                (dkc, KC // BW), (dvc, VC // BW), (dzc, ZC // BW)]
        dx, dg_pre = _inproj_bwd_dx(segs, lw["w_pad"].T, sv["x"], lw["pre_g"], dx)
        dw_pad = jnp.concatenate([_matmul_tn(sv["xn"], a) for a, _ in segs], axis=1)

        grads["w_in"][l] = _unpad_dw_in(dw_pad)
        grads["pre_g"][l] = dg_pre[0]
        grads["post_g"][l] = dg_post[0]
        grads["sgu_ln_g"][l] = dlg[0]
        grads["sgu_ln_b"][l] = dlb[0]
        grads["sgu_w"][l] = dws
        grads["sgu_b"][l] = dbs.T
        grads["mla_q_norm_g"][l] = dgq[0]
        grads["mla_kv_norm_g"][l] = dgkv[0]
        grads["mla_w_uq"][l] = _unpad_heads(dwuq, MLA_H, MLA_QK)
        dk3 = _unpad_heads(dwuk, MLA_H, MLA_NOPE).reshape(KV_RANK, MLA_H, MLA_NOPE)
        dv3 = dwuv.reshape(KV_RANK, MLA_H, 64)
        grads["mla_w_ukv"][l] = jnp.concatenate([dk3, dv3], axis=2).reshape(KV_RANK, MLA_H * (MLA_NOPE + 64))
        grads["ca_rel_bias"][l] = drel
        grads["w_branch"][l] = dwbr
        grads["gate_b"][l] = dgb
        grads["w_out"][l] = dw_out
    grad_x = dx.reshape(x.shape)
    gfull = {n: jnp.stack(grads[n]) for n in OUT_ORDER}

    small = jnp.concatenate([gfull[n].reshape(-1) for n in REPLICATED])
    rows = jnp.concatenate([_shard_rows(gfull[n], SHARD_AXIS[n]) for n in SHARDED + ("gate_b",)]
                           + [jnp.broadcast_to(small[None], (N_DEV, small.size))], axis=1)
    send = _to_rows(rows, 1024, 128)
    R = send.shape[1]
    recv = _exchange(send.reshape(N_DEV, R * 8, 128)).reshape(N_DEV, R, 1024)

    pack = lambda d: _to_rows(jnp.concatenate([d[n].reshape(-1) for n in PACK_ORDER]), 1024, 128)
    g_p, d_p, nm_p, nv_p = _reduce_adamw(recv, pack(weights), pack(mom_m), pack(mom_v))

    def unpack(p):
        flat, out, off = p.reshape(-1), {}, 0
        for n in PACK_ORDER:
            out[n] = flat[off:off + weights[n].size].reshape(weights[n].shape)
            off += weights[n].size
        return out

    g_o, d_o, nm_o, nv_o = unpack(g_p), unpack(d_p), unpack(nm_p), unpack(nv_p)
    return (loss, grad_x, *[g_o[n] for n in OUT_ORDER], *[d_o[n] for n in OUT_ORDER],
            *[nm_o[n] for n in OUT_ORDER], *[nv_o[n] for n in OUT_ORDER])
```

```python
import functools

import jax
import jax.numpy as jnp
from jax import lax
from jax.experimental import pallas as pl
from jax.experimental.pallas import tpu as pltpu

F32 = jnp.float32
MXU_DTYPE = jnp.bfloat16
EPS = 1e-6
NEG = -1e30
MESH_AXES = ("x", "y", "c")
N_DEV = 8

D = 1024
BW = 512
N_BRANCH = 3
CHUNK = 64
SGU_T = 128
SGU_G = 8
MLA_H = 8
MLA_NOPE = 64
MLA_ROPE = 32
MLA_QK = MLA_NOPE + MLA_ROPE
Q_RANK = 256
KV_RANK = 128
CA_H = 8
CA_DH = 64
LEFT_CHUNKS = 8
REL_CLIP = 128
D_IN = 7584

G_OFF, UA, VA, ZA, QD, ZB, QC, KC, VC, ZC, DP = 0, 3072, 3584, 4096, 4608, 5120, 5632, 6144, 6656, 7168, 7680
BAND_TQ = 256
BAND_W = 3 * BAND_TQ

ADAM_LR, ADAM_B1, ADAM_B2, ADAM_EPS, ADAM_WD, ADAM_STEP = 0.001, 0.9, 0.999, 1e-08, 0.01, 10


def _params(sem, mib):
    return pltpu.CompilerParams(dimension_semantics=sem, vmem_limit_bytes=mib << 20)


def _mm(a, b):
    return jnp.dot(a.astype(MXU_DTYPE), b.astype(MXU_DTYPE), preferred_element_type=F32)


def _mm_nt(a, b):
    return lax.dot_general(a.astype(MXU_DTYPE), b.astype(MXU_DTYPE), (((1,), (1,)), ((), ())),
                           preferred_element_type=F32)


def _mm_tn(a, b):
    return lax.dot_general(a.astype(MXU_DTYPE), b.astype(MXU_DTYPE), (((0,), (0,)), ((), ())),
                           preferred_element_type=F32)


def _sigmoid(z):
    return 1.0 / (1.0 + jnp.exp(-z))


def _rms(x):
    r = lax.rsqrt(jnp.mean(x * x, axis=-1, keepdims=True) + EPS)
    return x * r, r


def _rms_bwd(dn, n, r):
    return r * (dn - n * jnp.mean(dn * n, axis=-1, keepdims=True))


def _rope(b, c, sa, sb):
    return b * c + pltpu.roll(b, 112, 1) * sa + pltpu.roll(b, 16, 1) * sb


def _rope_t(d, c, sa, sb):
    return d * c + pltpu.roll(d * sa, 16, 1) + pltpu.roll(d * sb, 112, 1)


def _all_gather(blk):
    R = blk.shape[0]

    def body(x_ref, out_ref, send_sems, recv_sems, local_sem):
        x, y, c = lax.axis_index("x"), lax.axis_index("y"), lax.axis_index("c")
        me, sibling = (x, y, c), (x, y, 1 - c)
        chips = [(1 - x, y), (x, 1 - y), (1 - x, 1 - y)]

        def slot(px, py, pc):
            return out_ref.at[4 * px + 2 * py + pc]

        def copy(k, block, to, src=None):
            return pltpu.make_async_remote_copy(
                src_ref=slot(*block) if src is None else src, dst_ref=slot(*block),
                send_sem=send_sems.at[k], recv_sem=recv_sems.at[k],
                device_id=to, device_id_type=pl.DeviceIdType.MESH)

        mine = pltpu.make_async_copy(x_ref, slot(*me), local_sem)
        mine.start()
        first = [copy(0, me, sibling, src=x_ref)]
        first += [copy(1 + j, me, (*chip, c), src=x_ref) for j, chip in enumerate(chips)]
        for cp in first:
            cp.start()
        passed = [copy(4 + j, (*chip, c), sibling) for j, chip in enumerate(chips)]
        for j, chip in enumerate(chips):
            copy(1 + j, (*chip, c), me).wait_recv()
            passed[j].start()
        copy(0, sibling, me).wait_recv()
        for j, chip in enumerate(chips):
            copy(4 + j, (*chip, 1 - c), me).wait_recv()
        for cp in first + passed:
            cp.wait_send()
        mine.wait()

    return pl.pallas_call(
        body, name="all_gather",
        out_shape=jax.ShapeDtypeStruct((N_DEV, R, 128), blk.dtype),
        in_specs=[pl.BlockSpec(memory_space=pl.ANY)],
        out_specs=pl.BlockSpec(memory_space=pl.ANY),
        scratch_shapes=[pltpu.SemaphoreType.DMA((7,)), pltpu.SemaphoreType.DMA((7,)), pltpu.SemaphoreType.DMA(())],
    )(blk)


def _exchange(send):
    R = send.shape[1]

    def body(s_ref, r_ref, send_sems, recv_sems, local_sem):
        x, y, c = lax.axis_index("x"), lax.axis_index("y"), lax.axis_index("c")
        me = 4 * x + 2 * y + c
        local = pltpu.make_async_copy(s_ref.at[me], r_ref.at[me], local_sem)
        local.start()

        def peer(k):
            px = 1 - x if (k >> 2) & 1 else x
            py = 1 - y if (k >> 1) & 1 else y
            pc = 1 - c if k & 1 else c
            return (px, py, pc), 4 * px + 2 * py + pc

        copies = []
        for k in range(1, N_DEV):
            pos, pid = peer(k)
            cp = pltpu.make_async_remote_copy(
                src_ref=s_ref.at[pid], dst_ref=r_ref.at[me],
                send_sem=send_sems.at[k], recv_sem=recv_sems.at[k],
                device_id=pos, device_id_type=pl.DeviceIdType.MESH)
            cp.start()
            copies.append(cp)
        for k in range(1, N_DEV):
            pos, pid = peer(k)
            pltpu.make_async_remote_copy(
                src_ref=s_ref.at[pid], dst_ref=r_ref.at[pid],
                send_sem=send_sems.at[k], recv_sem=recv_sems.at[k],
                device_id=pos, device_id_type=pl.DeviceIdType.MESH).wait_recv()
        for cp in copies:
            cp.wait_send()
        local.wait()

    return pl.pallas_call(
        body, name="grad_exchange",
        out_shape=jax.ShapeDtypeStruct((N_DEV, R, 128), send.dtype),
        in_specs=[pl.BlockSpec(memory_space=pl.ANY)],
        out_specs=pl.BlockSpec(memory_space=pl.ANY),
        scratch_shapes=[pltpu.SemaphoreType.DMA((N_DEV,)), pltpu.SemaphoreType.DMA((N_DEV,)),
                        pltpu.SemaphoreType.DMA(())],
    )(send)


def _reduce_adamw(recv, w, m, v):
    R = w.shape[0]
    tr = 128 if R % 128 == 0 else 8
    c1 = 1.0 - ADAM_B1 ** ADAM_STEP
    c2 = 1.0 - ADAM_B2 ** ADAM_STEP

    def body(r_ref, w_ref, m_ref, v_ref, g_ref, d_ref, nm_ref, nv_ref):
        g = r_ref[0].astype(F32)
        for s in range(1, N_DEV):
            g = g + r_ref[s].astype(F32)
        m2 = ADAM_B1 * m_ref[...] + (1.0 - ADAM_B1) * g
        v2 = ADAM_B2 * v_ref[...] + (1.0 - ADAM_B2) * (g * g)
        m_hat = m2 / c1
        v_hat = v2 / c2
        g_ref[...] = g
        d_ref[...] = -ADAM_LR * (m_hat / (jnp.sqrt(v_hat) + ADAM_EPS) + ADAM_WD * w_ref[...])
        nm_ref[...] = m2
        nv_ref[...] = v2

    row = pl.BlockSpec((tr, 1024), lambda i: (i, 0))
    return pl.pallas_call(
        body, name="reduce_adamw", grid=(R // tr,),
        in_specs=[pl.BlockSpec((N_DEV, tr, 1024), lambda i: (0, i, 0)), row, row, row],
        out_specs=[row, row, row, row],
        out_shape=[jax.ShapeDtypeStruct((R, 1024), F32)] * 4,
        compiler_params=_params(("parallel",), 40),
    )(recv, w, m, v)


def _inproj_fwd(x, pre_g, w_pad):
    S = x.shape[0]
    tm, tn = min(512, S), 1536

    def body(x_ref, g_ref, w_ref, proj_ref, xn_ref):
        @pl.when(pl.program_id(1) == 0)
        def _():
            n, _ = _rms(x_ref[...])
            xn_ref[...] = (n * g_ref[...]).astype(xn_ref.dtype)
        proj_ref[...] = jnp.dot(xn_ref[...], w_ref[...], preferred_element_type=F32)

    return pl.pallas_call(
        body, name="inproj_fwd", grid=(S // tm, DP // tn),
        in_specs=[pl.BlockSpec((tm, D), lambda i, j: (i, 0)), pl.BlockSpec((1, D), lambda i, j: (0, 0)),
                  pl.BlockSpec((D, tn), lambda i, j: (0, j))],
        out_specs=[pl.BlockSpec((tm, tn), lambda i, j: (i, j)), pl.BlockSpec((tm, D), lambda i, j: (i, 0))],
        out_shape=[jax.ShapeDtypeStruct((S, DP), F32), jax.ShapeDtypeStruct((S, D), MXU_DTYPE)],
        compiler_params=_params(("parallel", "arbitrary"), 48),
    )(x, pre_g, w_pad)


def _sgu_tri():
    return lax.broadcasted_iota(jnp.int32, (SGU_T, SGU_T), 0) >= lax.broadcasted_iota(jnp.int32, (SGU_T, SGU_T), 1)


def _sgu_mix(ws_m, vb, bfull, grp):
    mixed = bfull
    for g in range(SGU_G):
        mixed = mixed + jnp.where(grp == g, _mm(ws_m[g], vb), 0.0)
    return mixed


def _layernorm(v, g, b):
    xc = v - jnp.mean(v, axis=-1, keepdims=True)
    rstd = lax.rsqrt(jnp.mean(xc * xc, axis=-1, keepdims=True) + EPS)
    vhat = xc * rstd
    return vhat * g + b, vhat, rstd


def _sgu_fwd(proj, ln_g, ln_b, ws, bfull):
    S = proj.shape[0]
    tm = min(512, S)

    def body(u_ref, v_ref, z_ref, g_ref, b_ref, ws_ref, bf_ref, ya_ref):
        tri = _sgu_tri()
        ws_m = [jnp.where(tri, ws_ref[g], 0.0).astype(MXU_DTYPE) for g in range(SGU_G)]
        grp = lax.broadcasted_iota(jnp.int32, (1, BW), 1) // (BW // SGU_G)
        for b in range(tm // SGU_T):
            r = slice(b * SGU_T, (b + 1) * SGU_T)
            vln, _, _ = _layernorm(v_ref[r, :], g_ref[...], b_ref[...])
            mixed = _sgu_mix(ws_m, vln.astype(MXU_DTYPE), bf_ref[...], grp)
            z = z_ref[r, :]
            ya_ref[r, :] = (u_ref[r, :] * mixed * (z * _sigmoid(z))).astype(ya_ref.dtype)

    blk = lambda cb: pl.BlockSpec((tm, BW), lambda i, cb=cb: (i, cb))
    vec = pl.BlockSpec((1, BW), lambda i: (0, 0))
    return pl.pallas_call(
        body, name="sgu_fwd", grid=(S // tm,),
        in_specs=[blk(UA // BW), blk(VA // BW), blk(ZA // BW), vec, vec,
                  pl.BlockSpec((SGU_G, SGU_T, SGU_T), lambda i: (0, 0, 0)),
                  pl.BlockSpec((SGU_T, BW), lambda i: (0, 0))],
        out_specs=pl.BlockSpec((tm, BW), lambda i: (i, 0)),
        out_shape=jax.ShapeDtypeStruct((S, BW), MXU_DTYPE),
        compiler_params=_params(("parallel",), 32),
    )(proj, proj, proj, ln_g, ln_b, ws, bfull)


def _mla_prep_fwd(proj, gq, gkv, wuq, wuk, wuv, cos_t, sin_a, sin_b):
    S = proj.shape[0]
    tm = min(512, S)

    def body(p_ref, gq_ref, gkv_ref, wuq_ref, wuk_ref, wuv_ref, c_ref, sa_ref, sb_ref, q_ref, k_ref, v_ref):
        nq, _ = _rms(p_ref[:, 0:Q_RANK])
        nkv, _ = _rms(p_ref[:, Q_RANK:Q_RANK + KV_RANK])
        cq = (nq * gq_ref[...]).astype(MXU_DTYPE)
        ckv = (nkv * gkv_ref[...]).astype(MXU_DTYPE)
        qf = _mm(cq, wuq_ref[...])
        kf = _mm(ckv, wuk_ref[...])
        v_ref[...] = _mm(ckv, wuv_ref[...]).astype(v_ref.dtype)
        c, sa, sb = c_ref[...], sa_ref[...], sb_ref[...]
        krr = _rope(p_ref[:, Q_RANK + KV_RANK:BW], c, sa, sb)
        for h in range(MLA_H):
            sl = slice(128 * h, 128 * (h + 1))
            q_ref[:, sl] = _rope(qf[:, sl], c, sa, sb).astype(q_ref.dtype)
            k_ref[:, sl] = (kf[:, sl] + krr).astype(k_ref.dtype)

    full = lambda a: pl.BlockSpec(a.shape, lambda i: (0,) * a.ndim)
    tab = pl.BlockSpec((tm, 128), lambda i: (i, 0))
    return pl.pallas_call(
        body, name="mla_prep_fwd", grid=(S // tm,),
        in_specs=[pl.BlockSpec((tm, BW), lambda i: (i, QD // BW)), full(gq), full(gkv), full(wuq), full(wuk),
                  full(wuv), tab, tab, tab],
        out_specs=[pl.BlockSpec((tm, 1024), lambda i: (i, 0)), pl.BlockSpec((tm, 1024), lambda i: (i, 0)),
                   pl.BlockSpec((tm, BW), lambda i: (i, 0))],
        out_shape=[jax.ShapeDtypeStruct((S, 1024), MXU_DTYPE), jax.ShapeDtypeStruct((S, 1024), MXU_DTYPE),
                   jax.ShapeDtypeStruct((S, BW), MXU_DTYPE)],
        compiler_params=_params(("parallel",), 40),
    )(proj, gq, gkv, wuq, wuk, wuv, cos_t, sin_a, sin_b)


MLA_SCALE = MLA_QK ** -0.5
MLA_SCALE_LOG2 = MLA_SCALE * 1.4426950408889634


def _tri_tables(n, q_major):
    if q_major:
        pairs = [(qi, ki) for qi in range(n) for ki in range(qi + 1)]
    else:
        pairs = [(qi, ki) for ki in range(n) for qi in range(ki, n)]
    return (jnp.asarray([p[0] for p in pairs], jnp.int32), jnp.asarray([p[1] for p in pairs], jnp.int32))


def _chunk_mask(T):
    rc = lax.broadcasted_iota(jnp.int32, (T, T), 0) >> 6
    cc = lax.broadcasted_iota(jnp.int32, (T, T), 1) >> 6
    return cc <= rc


def _flash_fwd(qh, kh, vh):
    S = qh.shape[0]
    T = min(512, S)
    n = S // T
    qt, kt = _tri_tables(n, q_major=True)

    def body(qt_ref, kt_ref, q_ref, k_ref, v_ref, o_ref, lse_ref, m_scr, l_scr, acc_scr):
        t = pl.program_id(1)
        qi, ki = qt_ref[t], kt_ref[t]

        @pl.when(ki == 0)
        def _():
            m_scr[...] = jnp.full(m_scr.shape, NEG, F32)
            l_scr[...] = jnp.zeros(l_scr.shape, F32)
            acc_scr[...] = jnp.zeros(acc_scr.shape, F32)

        def step(masked):
            head = lax.broadcasted_iota(jnp.int32, (1, 128), 1) // 64
            for j in range(2):
                sl = slice(128 * j, 128 * (j + 1))
                s = _mm_nt(q_ref[:, sl], k_ref[:, sl]) * MLA_SCALE_LOG2
                if masked:
                    s = jnp.where(_chunk_mask(T), s, NEG)
                m_old = m_scr[j]
                m_new = jnp.maximum(m_old, jnp.max(s, axis=1, keepdims=True))
                alpha = jnp.exp2(m_old - m_new)
                p = jnp.exp2(s - m_new)
                l_scr[j] = alpha * l_scr[j] + jnp.sum(p, axis=1, keepdims=True)
                vj = jnp.where(head == j, v_ref[...], 0)
                acc_scr[j] = alpha * acc_scr[j] + _mm(p, vj)
                m_scr[j] = m_new

        @pl.when(ki < qi)
        def _():
            step(False)

        @pl.when(ki == qi)
        def _():
            step(True)
            o_ref[...] = acc_scr[0] / l_scr[0] + acc_scr[1] / l_scr[1]
            for j in range(2):
                lse_ref[0, :, j:j + 1] = m_scr[j] + jnp.log2(l_scr[j])

    qmap = lambda hp, t, qt, kt: (qt[t], hp)
    kmap = lambda hp, t, qt, kt: (kt[t], hp)
    return pl.pallas_call(
        body, name="flash_fwd",
        grid_spec=pltpu.PrefetchScalarGridSpec(
            num_scalar_prefetch=2, grid=(MLA_H // 2, qt.shape[0]),
            in_specs=[pl.BlockSpec((T, 256), qmap), pl.BlockSpec((T, 256), kmap), pl.BlockSpec((T, 128), kmap)],
            out_specs=[pl.BlockSpec((T, 128), qmap),
                       pl.BlockSpec((1, T, 2), lambda hp, t, qt, kt: (hp, qt[t], 0))],
            scratch_shapes=[pltpu.VMEM((2, T, 1), F32), pltpu.VMEM((2, T, 1), F32),
                            pltpu.VMEM((2, T, 128), F32)]),
        out_shape=[jax.ShapeDtypeStruct((S, BW), F32), jax.ShapeDtypeStruct((MLA_H // 2, S, 2), F32)],
        compiler_params=_params(("parallel", "arbitrary"), 40),
    )(qt, kt, qh, kh, vh)


def _band_specs(S):
    q = pl.BlockSpec((BAND_TQ, 128), lambda hp, qi: (qi, QC // 128 + hp))
    ks = [pl.BlockSpec((BAND_TQ, 128), lambda hp, qi, t=t: (jnp.maximum(qi - 2 + t, 0), KC // 128 + hp))
          for t in range(3)]
    vs = [pl.BlockSpec((BAND_TQ, 128), lambda hp, qi, t=t: (jnp.maximum(qi - 2 + t, 0), VC // 128 + hp))
          for t in range(3)]
    bias = pl.BlockSpec((2, BAND_TQ, BAND_W), lambda hp, qi: (hp, 0, 0))
    return q, ks, vs, bias


def _band_probs(qj, kcat, bias_j, valid, scale):
    s = _mm_nt(qj, kcat) * scale + bias_j
    s = jnp.where(valid, s, NEG)
    p = jnp.exp(s - jnp.max(s, axis=1, keepdims=True))
    return p / jnp.sum(p, axis=1, keepdims=True)


def _band_valid(qi):
    tile = lax.broadcasted_iota(jnp.int32, (1, BAND_W), 1) // BAND_TQ
    return tile + qi >= 2


def _band_fwd(proj, bias):
    S = proj.shape[0]
    scale = CA_DH ** -0.5

    def body(q_ref, k0, k1, k2, v0, v1, v2, b_ref, o_ref):
        qi = pl.program_id(1)
        head = lax.broadcasted_iota(jnp.int32, (1, 128), 1) // 64
        kcat = jnp.concatenate([k0[...], k1[...], k2[...]], axis=0).astype(MXU_DTYPE)
        vcat = jnp.concatenate([v0[...], v1[...], v2[...]], axis=0).astype(MXU_DTYPE)
        valid = _band_valid(qi)
        q = q_ref[...]
        o = jnp.zeros((BAND_TQ, 128), F32)
        for j in range(2):
            pn = _band_probs(jnp.where(head == j, q, 0.0), kcat, b_ref[j], valid, scale)
            o = o + _mm(pn, jnp.where(head == j, vcat, 0))
        o_ref[...] = o

    q, ks, vs, bspec = _band_specs(S)
    return pl.pallas_call(
        body, name="band_fwd", grid=(CA_H // 2, S // BAND_TQ),
        in_specs=[q, *ks, *vs, bspec],
        out_specs=pl.BlockSpec((BAND_TQ, 128), lambda hp, qi: (qi, hp)),
        out_shape=jax.ShapeDtypeStruct((S, BW), F32),
        compiler_params=_params(("parallel", "arbitrary"), 40),
    )(proj, proj, proj, proj, proj, proj, proj, bias)


def _merge_fwd(x, ya, ob, oc, proj, gate_b, wbr, wout, post_g):
    S = x.shape[0]
    tm = min(256, S)

    def body(x_ref, ya_ref, ob_ref, oc_ref, zb_ref, zc_ref, g0, g1, g2, gb_ref, wbr_ref, wo_ref, pg_ref,
             xo_ref, mg_ref, h_ref):
        zb, zc = zb_ref[...], zc_ref[...]
        ys = [ya_ref[...], ob_ref[...] * (zb * _sigmoid(zb)), oc_ref[...] * (zc * _sigmoid(zc))]
        merged = jnp.zeros((tm, D), F32)
        for i, g_ref in enumerate((g0, g1, g2)):
            merged = merged + _sigmoid(g_ref[...] + gb_ref[i:i + 1, :]) * _mm(ys[i], wbr_ref[i])
        mg_ref[...] = merged.astype(mg_ref.dtype)
        h = _mm(merged, wo_ref[...])
        h_ref[...] = h
        n, _ = _rms(h)
        xo_ref[...] = x_ref[...] + n * pg_ref[...]

    row = lambda w, cb=0: pl.BlockSpec((tm, w), lambda i, cb=cb: (i, cb))
    full = lambda a: pl.BlockSpec(a.shape, lambda i: (0,) * a.ndim)
    return pl.pallas_call(
        body, name="merge_fwd", grid=(S // tm,),
        in_specs=[row(D), row(BW), row(BW), row(BW), row(BW, ZB // BW), row(BW, ZC // BW),
                  row(D, 0), row(D, 1), row(D, 2), full(gate_b), full(wbr), full(wout), full(post_g)],
        out_specs=[row(D), row(D), row(D)],
        out_shape=[jax.ShapeDtypeStruct((S, D), F32), jax.ShapeDtypeStruct((S, D), MXU_DTYPE),
                   jax.ShapeDtypeStruct((S, D), F32)],
        compiler_params=_params(("parallel",), 56),
    )(x, ya, ob, oc, proj, proj, proj, proj, proj, gate_b, wbr, wout, post_g)


def _loss_fwd_bwd(y, target):
    S = y.shape[0]
    tm = min(512, S)

    def body(y_ref, t_ref, loss_ref, dy_ref):
        @pl.when(pl.program_id(0) == 0)
        def _():
            loss_ref[...] = jnp.zeros((1, 1), F32)
        err = y_ref[...] - t_ref[...]
        loss_ref[...] += 0.5 * jnp.sum(jnp.mean(err * err, axis=-1, keepdims=True), axis=0, keepdims=True)
        dy_ref[...] = err * (1.0 / D)

    row = pl.BlockSpec((tm, D), lambda i: (i, 0))
    return pl.pallas_call(
        body, name="loss", grid=(S // tm,),
        in_specs=[row, row],
        out_specs=[pl.BlockSpec((1, 1), lambda i: (0, 0)), row],
        out_shape=[jax.ShapeDtypeStruct((1, 1), F32), jax.ShapeDtypeStruct((S, D), F32)],
        compiler_params=_params(("arbitrary",), 32),
    )(y, target)


def _first_step_zero(refs, first):
    @pl.when(first)
    def _():
        for r in refs:
            r[...] = jnp.zeros(r.shape, r.dtype)


def _out_bwd(dxo, h, merged, post_g, wout_t):
    S = dxo.shape[0]
    tm = min(256, S)

    def body(d_ref, h_ref, mg_ref, pg_ref, wt_ref, dm_ref, dw_ref, dg_ref):
        _first_step_zero((dw_ref, dg_ref), pl.program_id(0) == 0)
        d = d_ref[...]
        hn, r = _rms(h_ref[...])
        dg_ref[...] += jnp.sum(d * hn, axis=0, keepdims=True)
        dh = _rms_bwd(d * pg_ref[...], hn, r)
        dm_ref[...] = _mm(dh, wt_ref[...])
        dw_ref[...] += _mm_tn(mg_ref[...], dh)

    row = pl.BlockSpec((tm, D), lambda i: (i, 0))
    full = lambda shape: pl.BlockSpec(shape, lambda i: (0,) * len(shape))
    return pl.pallas_call(
        body, name="out_bwd", grid=(S // tm,),
        in_specs=[row, row, row, full((1, D)), full((D, D))],
        out_specs=[row, full((D, D)), full((1, D))],
        out_shape=[jax.ShapeDtypeStruct((S, D), F32), jax.ShapeDtypeStruct((D, D), F32),
                   jax.ShapeDtypeStruct((1, D), F32)],
        compiler_params=_params(("arbitrary",), 40),
    )(dxo, h, merged, post_g, wout_t)


def _gate_bwd(dm, ya, ob, oc, proj, gate_b, wbr, wbr_t):
    S = dm.shape[0]
    tm = min(128, S)

    def body(dm_ref, ya_ref, ob_ref, oc_ref, zb_ref, zc_ref, g0, g1, g2, gb_ref, wbr_ref, wbt_ref,
             dg_ref, dzb_ref, dzc_ref, dya_ref, dob_ref, doc_ref, dwbr_ref, dgb_ref):
        _first_step_zero((dwbr_ref, dgb_ref), pl.program_id(0) == 0)
        dmv = dm_ref[...]
        zb, zc = zb_ref[...], zc_ref[...]
        sgb, sgc = _sigmoid(zb), _sigmoid(zc)
        ob, oc = ob_ref[...], oc_ref[...]
        ys = [ya_ref[...], (ob * (zb * sgb)).astype(MXU_DTYPE), (oc * (zc * sgc)).astype(MXU_DTYPE)]
        dys = []
        for i, g_ref in enumerate((g0, g1, g2)):
            br = _mm(ys[i], wbr_ref[i])
            gate = _sigmoid(g_ref[...] + gb_ref[i:i + 1, :])
            dgl = dmv * br * (gate * (1.0 - gate))
            dg_ref[:, D * i:D * (i + 1)] = dgl.astype(dg_ref.dtype)
            dgb_ref[i:i + 1, :] += jnp.sum(dgl, axis=0, keepdims=True)
            dbr = dmv * gate
            dys.append(_mm(dbr, wbt_ref[i]))
            dwbr_ref[i] += _mm_tn(dbr, ys[i])
        dya_ref[...] = dys[0]
        dob_ref[...] = dys[1] * (zb * sgb)
        dzb_ref[...] = (dys[1] * ob * (sgb * (1.0 + zb * (1.0 - sgb)))).astype(dzb_ref.dtype)
        doc_ref[...] = dys[2] * (zc * sgc)
        dzc_ref[...] = (dys[2] * oc * (sgc * (1.0 + zc * (1.0 - sgc)))).astype(dzc_ref.dtype)

    row = lambda w, cb=0: pl.BlockSpec((tm, w), lambda i, cb=cb: (i, cb))
    full = lambda a: pl.BlockSpec(a.shape, lambda i: (0,) * a.ndim)
    sds = lambda w, dt=F32: jax.ShapeDtypeStruct((S, w), dt)
    return pl.pallas_call(
        body, name="gate_bwd", grid=(S // tm,),
        in_specs=[row(D), row(BW), row(BW), row(BW), row(BW, ZB // BW), row(BW, ZC // BW),
                  row(D, 0), row(D, 1), row(D, 2), full(gate_b), full(wbr), full(wbr_t)],
        out_specs=[row(3 * D), row(BW), row(BW), row(BW), row(BW), row(BW),
                   pl.BlockSpec((N_BRANCH, D, BW), lambda i: (0, 0, 0)), pl.BlockSpec((N_BRANCH, D), lambda i: (0, 0))],
        out_shape=[sds(3 * D, MXU_DTYPE), sds(BW, MXU_DTYPE), sds(BW, MXU_DTYPE), sds(BW), sds(BW), sds(BW),
                   jax.ShapeDtypeStruct((N_BRANCH, D, BW), F32), jax.ShapeDtypeStruct((N_BRANCH, D), F32)],
        compiler_params=_params(("arbitrary",), 56),
    )(dm, ya, ob, oc, proj, proj, proj, proj, proj, gate_b, wbr, wbr_t)


def _band_bwd(proj, bias, do):
    S = proj.shape[0]
    scale = CA_DH ** -0.5

    def body(q_ref, k0, k1, k2, v0, v1, v2, b_ref, do_ref, dq_ref, dk_ref, dv_ref, db_ref):
        qi = pl.program_id(1)
        _first_step_zero((dk_ref, dv_ref, db_ref), qi == 0)
        head = lax.broadcasted_iota(jnp.int32, (1, 128), 1) // 64
        kcat = jnp.concatenate([k0[...], k1[...], k2[...]], axis=0).astype(MXU_DTYPE)
        vcat = jnp.concatenate([v0[...], v1[...], v2[...]], axis=0).astype(MXU_DTYPE)
        valid = _band_valid(qi)
        q, dov = q_ref[...], do_ref[...]
        dq = jnp.zeros((BAND_TQ, 128), F32)
        dk = jnp.zeros((BAND_W, 128), F32)
        dv = jnp.zeros((BAND_W, 128), F32)
        for j in range(2):
            qj = jnp.where(head == j, q, 0.0).astype(MXU_DTYPE)
            doj = jnp.where(head == j, dov, 0.0).astype(MXU_DTYPE)
            pn = _band_probs(qj, kcat, b_ref[j], valid, scale)
            dv = dv + _mm_tn(pn, doj)
            dp = _mm_nt(doj, vcat)
            ds = pn * (dp - jnp.sum(pn * dp, axis=1, keepdims=True))
            db_ref[j] += ds
            dsb = (ds * scale).astype(MXU_DTYPE)
            dq = dq + _mm(dsb, jnp.where(head == j, kcat, 0))
            dk = dk + _mm_tn(dsb, qj)
        dq_ref[...] = dq.astype(dq_ref.dtype)
        for t in range(3):
            @pl.when(qi - 2 + t >= 0)
            def _(t=t):
                rows = pl.ds(pl.multiple_of((qi - 2 + t) * BAND_TQ, BAND_TQ), BAND_TQ)
                dk_ref[rows, :] += dk[t * BAND_TQ:(t + 1) * BAND_TQ]
                dv_ref[rows, :] += dv[t * BAND_TQ:(t + 1) * BAND_TQ]

    q, ks, vs, bspec = _band_specs(S)
    col = pl.BlockSpec((S, 128), lambda hp, qi: (0, hp))
    return pl.pallas_call(
        body, name="band_bwd", grid=(CA_H // 2, S // BAND_TQ),
        in_specs=[q, *ks, *vs, bspec, pl.BlockSpec((BAND_TQ, 128), lambda hp, qi: (qi, hp))],
        out_specs=[pl.BlockSpec((BAND_TQ, 128), lambda hp, qi: (qi, hp)), col, col, bspec],
        out_shape=[jax.ShapeDtypeStruct((S, BW), MXU_DTYPE), jax.ShapeDtypeStruct((S, BW), F32),
                   jax.ShapeDtypeStruct((S, BW), F32), jax.ShapeDtypeStruct((CA_H, BAND_TQ, BAND_W), F32)],
        compiler_params=_params(("parallel", "arbitrary"), 56),
    )(proj, proj, proj, proj, proj, proj, proj, bias, do)


BIAS_LO = REL_CLIP - (CHUNK - 1)
BIAS_FAR = 2 * REL_CLIP
BIAS_NEAR0 = BAND_W // 2


def _band_index(col0, ncol):
    i = lax.broadcasted_iota(jnp.int32, (BAND_TQ, ncol), 0)
    j = lax.broadcasted_iota(jnp.int32, (BAND_TQ, ncol), 1) + col0
    idx = jnp.clip(i + 2 * BAND_TQ - j, -REL_CLIP, REL_CLIP) + REL_CLIP
    ci, cj = i // CHUNK, j // CHUNK
    return jnp.where((ci <= cj) & (cj <= ci + LEFT_CHUNKS), idx, -1)


def _bias_build(rel_table):
    near = BAND_W - BIAS_NEAR0

    def body(tab_ref, out_ref):
        idx_far = _band_index(0, BIAS_NEAR0)
        idx_near = _band_index(BIAS_NEAR0, near)
        for h in range(CA_H):
            far = tab_ref[h, BIAS_FAR]
            out_ref[h, :, 0:BIAS_NEAR0] = jnp.where(idx_far >= 0, far, NEG)

            def fill(r, tile):
                return jnp.where(idx_near == r, tab_ref[h, r], tile)

            out_ref[h, :, BIAS_NEAR0:BAND_W] = lax.fori_loop(
                BIAS_LO, BIAS_FAR + 1, fill, jnp.full((BAND_TQ, near), NEG, F32))

    return pl.pallas_call(
        body, name="bias_build",
        out_shape=jax.ShapeDtypeStruct((CA_H, BAND_TQ, BAND_W), F32),
        in_specs=[pl.BlockSpec(memory_space=pltpu.SMEM)],
        out_specs=pl.BlockSpec(memory_space=pltpu.VMEM),
        compiler_params=pltpu.CompilerParams(vmem_limit_bytes=40 << 20),
    )(rel_table)


def _bias_fold(db):
    near = BAND_W - BIAS_NEAR0

    def body(db_ref, out_ref):
        idx_near = _band_index(BIAS_NEAR0, near)
        lane = lax.broadcasted_iota(jnp.int32, (1, 384), 1)
        for h in range(CA_H):
            def fold(r, acc):
                val = jnp.sum(jnp.where(idx_near == r, db_ref[h, :, BIAS_NEAR0:BAND_W], 0.0), keepdims=True)
                return acc + jnp.where(lane == r, val, 0.0)

            far = jnp.sum(db_ref[h, :, 0:BIAS_NEAR0], keepdims=True)
            out_ref[h:h + 1, :] = lax.fori_loop(BIAS_LO, BIAS_FAR + 1, fold, jnp.where(lane == BIAS_FAR, far, 0.0))

    return pl.pallas_call(
        body, name="bias_fold",
        out_shape=jax.ShapeDtypeStruct((CA_H, 384), F32),
        in_specs=[pl.BlockSpec(memory_space=pltpu.VMEM)],
        out_specs=pl.BlockSpec(memory_space=pltpu.VMEM),
        compiler_params=pltpu.CompilerParams(vmem_limit_bytes=40 << 20),
    )(db)


def _flash_bwd(qh, kh, vh, o, lse, do):
    S = qh.shape[0]
    T = min(512, S)
    n = S // T
    qt, kt = _tri_tables(n, q_major=False)

    def body(qt_ref, kt_ref, q_ref, k_ref, v_ref, o_ref, lse_ref, do_ref, dq_ref, dk_ref, dv_ref, dk_scr, dv_scr):
        t = pl.program_id(1)
        qi, ki = qt_ref[t], kt_ref[t]

        @pl.when(qi == ki)
        def _():
            dk_scr[...] = jnp.zeros(dk_scr.shape, F32)
            dv_scr[...] = jnp.zeros(dv_scr.shape, F32)

        def step(masked):
            head = lax.broadcasted_iota(jnp.int32, (1, 128), 1) // 64
            dov, ov = do_ref[...], o_ref[...]
            rows = pl.ds(pl.multiple_of(qi * T, T), T)
            dv = jnp.zeros((T, 128), F32)
            for j in range(2):
                sl = slice(128 * j, 128 * (j + 1))
                qj, kj = q_ref[:, sl], k_ref[:, sl]
                s = _mm_nt(qj, kj) * MLA_SCALE_LOG2
                if masked:
                    s = jnp.where(_chunk_mask(T), s, NEG)
                p = jnp.exp2(s - lse_ref[0, :, j:j + 1])
                doj = jnp.where(head == j, dov, 0.0)
                delta = jnp.sum(doj * ov, axis=1, keepdims=True)
                dv = dv + _mm_tn(p, doj)
                dp = _mm_nt(doj, v_ref[...])
                dsb = (p * (dp - delta) * MLA_SCALE).astype(MXU_DTYPE)
                dk_scr[:, sl] += _mm_tn(dsb, qj)
                dqj = _mm(dsb, kj)

                @pl.when(ki == 0)
                def _():
                    dq_ref[rows, sl] = dqj

                @pl.when(ki > 0)
                def _():
                    dq_ref[rows, sl] += dqj
            dv_scr[...] += dv

        @pl.when(qi > ki)
        def _():
            step(False)

        @pl.when(qi == ki)
        def _():
            step(True)

        @pl.when(qi == n - 1)
        def _():
            dk_ref[...] = dk_scr[...]
            dv_ref[...] = dv_scr[...]

    qmap = lambda hp, t, qt, kt: (qt[t], hp)
    kmap = lambda hp, t, qt, kt: (kt[t], hp)
    return pl.pallas_call(
        body, name="flash_bwd",
        grid_spec=pltpu.PrefetchScalarGridSpec(
            num_scalar_prefetch=2, grid=(MLA_H // 2, qt.shape[0]),
            in_specs=[pl.BlockSpec((T, 256), qmap), pl.BlockSpec((T, 256), kmap), pl.BlockSpec((T, 128), kmap),
                      pl.BlockSpec((T, 128), qmap), pl.BlockSpec((1, T, 2), lambda hp, t, qt, kt: (hp, qt[t], 0)),
                      pl.BlockSpec((T, 128), qmap)],
            out_specs=[pl.BlockSpec((S, 256), lambda hp, t, qt, kt: (0, hp)), pl.BlockSpec((T, 256), kmap),
                       pl.BlockSpec((T, 128), kmap)],
            scratch_shapes=[pltpu.VMEM((T, 256), F32), pltpu.VMEM((T, 128), F32)]),
        out_shape=[jax.ShapeDtypeStruct((S, 1024), F32), jax.ShapeDtypeStruct((S, 1024), F32),
                   jax.ShapeDtypeStruct((S, BW), F32)],
        compiler_params=_params(("parallel", "arbitrary"), 56),
    )(qt, kt, qh, kh, vh, o, lse, do)


def _mla_prep_bwd(proj, dqf, dkf, dvf, gq, gkv, wuq_t, wuk_t, wuv_t, cos_t, sin_a, sin_b):
    S = proj.shape[0]
    tm = min(512, S)

    def body(p_ref, dq_ref, dk_ref, dv_ref, gq_ref, gkv_ref, wq_ref, wk_ref, wv_ref, c_ref, sa_ref, sb_ref,
             db_ref, dwq_ref, dwk_ref, dwv_ref, dgq_ref, dgkv_ref):
        _first_step_zero((dwq_ref, dwk_ref, dwv_ref, dgq_ref, dgkv_ref), pl.program_id(0) == 0)
        c, sa, sb = c_ref[...], sa_ref[...], sb_ref[...]
        nq, rq = _rms(p_ref[:, 0:Q_RANK])
        nkv, rkv = _rms(p_ref[:, Q_RANK:Q_RANK + KV_RANK])
        cq = (nq * gq_ref[...]).astype(MXU_DTYPE)
        ckv = (nkv * gkv_ref[...]).astype(MXU_DTYPE)
        dkr = jnp.zeros((tm, 128), F32)
        dq_pre = []
        for h in range(MLA_H):
            sl = slice(128 * h, 128 * (h + 1))
            dq_pre.append(_rope_t(dq_ref[:, sl], c, sa, sb).astype(MXU_DTYPE))
            dkr = dkr + dk_ref[:, sl]
        dq_pre = jnp.concatenate(dq_pre, axis=1)
        dcq = _mm(dq_pre, wq_ref[...])
        dwq_ref[...] += _mm_tn(dq_pre, cq)
        dgq_ref[...] += jnp.sum(dcq * nq, axis=0, keepdims=True)
        db_ref[:, 0:Q_RANK] = _rms_bwd(dcq * gq_ref[...], nq, rq).astype(db_ref.dtype)
        dk = dk_ref[...].astype(MXU_DTYPE)
        dv = dv_ref[...].astype(MXU_DTYPE)
        dckv = _mm(dk, wk_ref[...]) + _mm(dv, wv_ref[...])
        dwk_ref[...] += _mm_tn(dk, ckv)
        dwv_ref[...] += _mm_tn(dv, ckv)
        dgkv_ref[...] += jnp.sum(dckv * nkv, axis=0, keepdims=True)
        db_ref[:, Q_RANK:Q_RANK + KV_RANK] = _rms_bwd(dckv * gkv_ref[...], nkv, rkv).astype(db_ref.dtype)
        lane = lax.broadcasted_iota(jnp.int32, (1, 128), 1)
        rope_lanes = (lane >= MLA_NOPE) & (lane < MLA_QK)
        db_ref[:, Q_RANK + KV_RANK:BW] = _rope_t(jnp.where(rope_lanes, dkr, 0.0), c, sa, sb).astype(db_ref.dtype)

    full = lambda a: pl.BlockSpec(a.shape, lambda i: (0,) * a.ndim)
    fulls = lambda shape: pl.BlockSpec(shape, lambda i: (0,) * len(shape))
    tab = pl.BlockSpec((tm, 128), lambda i: (i, 0))
    row = lambda w, cb=0: pl.BlockSpec((tm, w), lambda i, cb=cb: (i, cb))
    return pl.pallas_call(
        body, name="mla_prep_bwd", grid=(S // tm,),
        in_specs=[row(BW, QD // BW), row(1024), row(1024), row(BW), full(gq), full(gkv), full(wuq_t),
                  full(wuk_t), full(wuv_t), tab, tab, tab],
        out_specs=[row(BW), fulls((1024, Q_RANK)), fulls((1024, KV_RANK)), fulls((BW, KV_RANK)),
                   fulls((1, Q_RANK)), fulls((1, KV_RANK))],
        out_shape=[jax.ShapeDtypeStruct((S, BW), MXU_DTYPE), jax.ShapeDtypeStruct((1024, Q_RANK), F32),
                   jax.ShapeDtypeStruct((1024, KV_RANK), F32), jax.ShapeDtypeStruct((BW, KV_RANK), F32),
                   jax.ShapeDtypeStruct((1, Q_RANK), F32), jax.ShapeDtypeStruct((1, KV_RANK), F32)],
        compiler_params=_params(("arbitrary",), 56),
    )(proj, dqf, dkf, dvf, gq, gkv, wuq_t, wuk_t, wuv_t, cos_t, sin_a, sin_b)


def _sgu_bwd(proj, dya, ln_g, ln_b, ws, bfull):
    S = proj.shape[0]
    tm = min(512, S)

    def body(u_ref, v_ref, z_ref, dy_ref, g_ref, b_ref, ws_ref, bf_ref,
             da_ref, dws_ref, dbf_ref, dlg_ref, dlb_ref, dbs_ref):
        i = pl.program_id(0)
        _first_step_zero((dws_ref, dbf_ref, dlg_ref, dlb_ref, dbs_ref), i == 0)
        tri = _sgu_tri()
        ws_m = [jnp.where(tri, ws_ref[g], 0.0).astype(MXU_DTYPE) for g in range(SGU_G)]
        grp = lax.broadcasted_iota(jnp.int32, (1, BW), 1) // (BW // SGU_G)
        for b in range(tm // SGU_T):
            r = slice(b * SGU_T, (b + 1) * SGU_T)
            vln, vhat, rstd = _layernorm(v_ref[r, :], g_ref[...], b_ref[...])
            vb = vln.astype(MXU_DTYPE)
            mixed = _sgu_mix(ws_m, vb, bf_ref[...], grp)
            u, z, dy = u_ref[r, :], z_ref[r, :], dy_ref[r, :]
            sg = _sigmoid(z)
            sz = z * sg
            da_ref[r, 0:BW] = (dy * mixed * sz).astype(da_ref.dtype)
            da_ref[r, 2 * BW:3 * BW] = (dy * u * mixed * (sg * (1.0 + z * (1.0 - sg)))).astype(da_ref.dtype)
            dmix = dy * u * sz
            dbf_ref[...] += dmix
            dvln = jnp.zeros((SGU_T, BW), F32)
            for g in range(SGU_G):
                dmg = jnp.where(grp == g, dmix, 0.0).astype(MXU_DTYPE)
                dvln = dvln + _mm_tn(ws_m[g], dmg)
                dws_ref[g] += jnp.where(tri, _mm_nt(dmg, vb), 0.0)
            dlg_ref[...] += jnp.sum(dvln * vhat, axis=0, keepdims=True)
            dlb_ref[...] += jnp.sum(dvln, axis=0, keepdims=True)
            dvh = dvln * g_ref[...]
            da_ref[r, BW:2 * BW] = (rstd * (dvh - jnp.mean(dvh, axis=-1, keepdims=True)
                                            - vhat * jnp.mean(dvh * vhat, axis=-1, keepdims=True))).astype(da_ref.dtype)

        @pl.when(i == pl.num_programs(0) - 1)
        def _():
            dbf = dbf_ref[...]
            for g in range(SGU_G):
                dbs_ref[:, g:g + 1] = jnp.sum(jnp.where(grp == g, dbf, 0.0), axis=1, keepdims=True)

    blk = lambda cb: pl.BlockSpec((tm, BW), lambda i, cb=cb: (i, cb))
    fulls = lambda shape: pl.BlockSpec(shape, lambda i: (0,) * len(shape))
    return pl.pallas_call(
        body, name="sgu_bwd", grid=(S // tm,),
        in_specs=[blk(UA // BW), blk(VA // BW), blk(ZA // BW), blk(0), fulls((1, BW)), fulls((1, BW)),
                  fulls((SGU_G, SGU_T, SGU_T)), fulls((SGU_T, BW))],
        out_specs=[pl.BlockSpec((tm, 3 * BW), lambda i: (i, 0)), fulls((SGU_G, SGU_T, SGU_T)),
                   fulls((SGU_T, BW)), fulls((1, BW)), fulls((1, BW)), fulls((SGU_T, SGU_G))],
        out_shape=[jax.ShapeDtypeStruct((S, 3 * BW), MXU_DTYPE), jax.ShapeDtypeStruct((SGU_G, SGU_T, SGU_T), F32),
                   jax.ShapeDtypeStruct((SGU_T, BW), F32), jax.ShapeDtypeStruct((1, BW), F32),
                   jax.ShapeDtypeStruct((1, BW), F32), jax.ShapeDtypeStruct((SGU_T, SGU_G), F32)],
        compiler_params=_params(("arbitrary",), 40),
    )(proj, proj, proj, dya, ln_g, ln_b, ws, bfull)


DX_TK = 3 * BW


def _inproj_bwd_dx(dgl, da, small, w_t, x, pre_g, dxo):
    S = x.shape[0]
    tm = min(512, S)
    nk = DP // DX_TK

    def body(g_ref, a_ref, s0, s1, s2, s3, s4, s5, w_ref, x_ref, pg_ref, dxo_ref, dx_ref, dg_ref, acc):
        i, k = pl.program_id(0), pl.program_id(1)
        _first_step_zero((dg_ref,), (i == 0) & (k == 0))

        @pl.when(k == 0)
        def _():
            acc[...] = jnp.zeros(acc.shape, F32)

        @pl.when(k < 2)
        def _():
            acc[...] += _mm(g_ref[...], w_ref[...])

        @pl.when(k == 2)
        def _():
            acc[...] += _mm(a_ref[...], w_ref[...])

        for kk, trio in ((3, (s0, s1, s2)), (4, (s3, s4, s5))):
            @pl.when(k == kk)
            def _(trio=trio):
                t = acc[...]
                for n, ref in enumerate(trio):
                    t = t + _mm(ref[...], w_ref[n * BW:(n + 1) * BW, :])
                acc[...] = t

        @pl.when(k == nk - 1)
        def _():
            n, r = _rms(x_ref[...])
            dxn = acc[...]
            dg_ref[...] += jnp.sum(dxn * n, axis=0, keepdims=True)
            dx_ref[...] = dxo_ref[...] + _rms_bwd(dxn * pg_ref[...], n, r)

    row = pl.BlockSpec((tm, D), lambda i, k: (i, 0))
    vec = pl.BlockSpec((1, D), lambda i, k: (0, 0))
    fixed = lambda w: pl.BlockSpec((tm, w), lambda i, k: (i, 0))
    return pl.pallas_call(
        body, name="inproj_bwd_dx", grid=(S // tm, nk),
        in_specs=[pl.BlockSpec((tm, DX_TK), lambda i, k: (i, jnp.minimum(k, 1))), fixed(DX_TK)]
                 + [fixed(BW)] * 6 + [pl.BlockSpec((DX_TK, D), lambda i, k: (k, 0)), row, vec, row],
        out_specs=[row, vec],
        out_shape=[jax.ShapeDtypeStruct((S, D), F32), jax.ShapeDtypeStruct((1, D), F32)],
        scratch_shapes=[pltpu.VMEM((tm, D), F32)],
        compiler_params=_params(("arbitrary", "arbitrary"), 56),
    )(dgl, da, *small, w_t, x, pre_g, dxo)


def _matmul_tn(a, b):
    S, W = a.shape
    K = b.shape[1]
    tk, tn = min(2048, S), BW

    def body(a_ref, b_ref, o_ref):
        _first_step_zero((o_ref,), pl.program_id(1) == 0)
        o_ref[...] += _mm_tn(a_ref[...], b_ref[...])

    return pl.pallas_call(
        body, name="matmul_tn", grid=(W // tn, S // tk),
        in_specs=[pl.BlockSpec((tk, tn), lambda j, k: (k, j)), pl.BlockSpec((tk, K), lambda j, k: (k, 0))],
        out_specs=pl.BlockSpec((tn, K), lambda j, k: (j, 0)),
        out_shape=jax.ShapeDtypeStruct((W, K), F32),
        compiler_params=_params(("parallel", "arbitrary"), 48),
    )(a, b)


def _pad_rows_w_in(wt):
    z = lambda n: jnp.zeros((n, wt.shape[1]), wt.dtype)
    return jnp.concatenate([wt[4512:], wt[:1920], z(64), wt[1920:1952], z(32), wt[1952:4512]], axis=0)


def _unpad_rows_w_in(dwt):
    a = dwt[3 * D:]
    return jnp.concatenate([a[:1920], a[1984:2016], a[2048:], dwt[:3 * D]], axis=0)


def _pad_head_rows(wt, width):
    k = wt.shape[1]
    return jnp.pad(wt.reshape(MLA_H, width, k), ((0, 0), (0, 128 - width), (0, 0))).reshape(MLA_H * 128, k)


def _unpad_head_rows(wt, width):
    k = wt.shape[1]
    return wt.reshape(MLA_H, 128, k)[:, :width].reshape(MLA_H * width, k)


def _rope_tables(S):
    half = MLA_ROPE // 2
    inv = 10000.0 ** (-jnp.arange(half, dtype=F32) / half)
    ang = jnp.arange(S, dtype=F32)[:, None] * inv[None, :]
    cos, sin = jnp.cos(ang), jnp.sin(ang)
    one = lambda n: jnp.ones((S, n), F32)
    zero = lambda n: jnp.zeros((S, n), F32)
    cos_t = jnp.concatenate([one(MLA_NOPE), cos, cos, one(128 - MLA_QK)], axis=1)
    sin_a = jnp.concatenate([zero(MLA_NOPE), -sin, zero(128 - MLA_NOPE - half)], axis=1)
    sin_b = jnp.concatenate([zero(MLA_NOPE + half), sin, zero(128 - MLA_QK)], axis=1)
    return cos_t, sin_a, sin_b


SHARDED = ("w_in", "mla_w_uq", "mla_w_ukv", "w_branch", "w_out")
SHARD_PACK = SHARDED + ("gate_b",)
REPLICATED = ("pre_g", "post_g", "sgu_ln_g", "sgu_ln_b", "sgu_w", "sgu_b", "mla_q_norm_g", "mla_kv_norm_g",
              "ca_rel_bias")
OUT_ORDER = ("w_in", "pre_g", "post_g", "sgu_ln_g", "sgu_ln_b", "sgu_w", "sgu_b", "mla_q_norm_g", "mla_kv_norm_g",
             "mla_w_uq", "mla_w_ukv", "ca_rel_bias", "w_branch", "gate_b", "w_out")
TRANSPOSED = {"w_in": (0, 2, 1), "mla_w_uq": (0, 2, 1), "mla_w_ukv": (0, 2, 1), "w_branch": (0, 1, 3, 2)}


def _canon(name, a):
    return jnp.transpose(a, TRANSPOSED[name]) if name in TRANSPOSED else a


def _to_rows(flat, width, mult):
    n = flat.shape[-1]
    pad = (-n) % (width * mult)
    flat = jnp.pad(flat, [(0, 0)] * (flat.ndim - 1) + [(0, pad)])
    return flat.reshape(flat.shape[:-1] + (-1, width))


def _gather_shards(shards, dtype):
    names = list(shards)
    flat = jnp.concatenate([shards[n].astype(dtype).reshape(-1) for n in names])
    got = _all_gather(_to_rows(flat, 128, 16)).reshape(N_DEV, -1)
    out, off = {}, 0
    for n in names:
        size = shards[n].size
        out[n] = got[:, off:off + size].reshape((N_DEV,) + shards[n].shape)
        off += size
    return out


def _pack(d, names):
    return _to_rows(jnp.concatenate([d[n].reshape(-1) for n in names]), 1024, 128)


def _unpack(p, like, names):
    flat, out, off = p.reshape(-1), {}, 0
    for n in names:
        out[n] = flat[off:off + like[n].size].reshape(like[n].shape)
        off += like[n].size
    return out


def kernel(x, w_in, pre_g, post_g, sgu_ln_g, sgu_ln_b, sgu_w, sgu_b, mla_q_norm_g, mla_kv_norm_g, mla_w_uq, mla_w_ukv, ca_rel_bias, w_branch, gate_b, w_out, loss_target, m_w_in, m_pre_g, m_post_g, m_sgu_ln_g, m_sgu_ln_b, m_sgu_w, m_sgu_b, m_mla_q_norm_g, m_mla_kv_norm_g, m_mla_w_uq, m_mla_w_ukv, m_ca_rel_bias, m_w_branch, m_gate_b, m_w_out, v_w_in, v_pre_g, v_post_g, v_sgu_ln_g, v_sgu_ln_b, v_sgu_w, v_sgu_b, v_mla_q_norm_g, v_mla_kv_norm_g, v_mla_w_uq, v_mla_w_ukv, v_ca_rel_bias, v_w_branch, v_gate_b, v_w_out):
    weights = dict(w_in=w_in, pre_g=pre_g, post_g=post_g, sgu_ln_g=sgu_ln_g, sgu_ln_b=sgu_ln_b, sgu_w=sgu_w,
                   sgu_b=sgu_b, mla_q_norm_g=mla_q_norm_g, mla_kv_norm_g=mla_kv_norm_g, mla_w_uq=mla_w_uq,
                   mla_w_ukv=mla_w_ukv, ca_rel_bias=ca_rel_bias, w_branch=w_branch, gate_b=gate_b, w_out=w_out)
    mom_m = dict(w_in=m_w_in, pre_g=m_pre_g, post_g=m_post_g, sgu_ln_g=m_sgu_ln_g, sgu_ln_b=m_sgu_ln_b,
                 sgu_w=m_sgu_w, sgu_b=m_sgu_b, mla_q_norm_g=m_mla_q_norm_g, mla_kv_norm_g=m_mla_kv_norm_g,
                 mla_w_uq=m_mla_w_uq, mla_w_ukv=m_mla_w_ukv, ca_rel_bias=m_ca_rel_bias, w_branch=m_w_branch,
                 gate_b=m_gate_b, w_out=m_w_out)
    mom_v = dict(w_in=v_w_in, pre_g=v_pre_g, post_g=v_post_g, sgu_ln_g=v_sgu_ln_g, sgu_ln_b=v_sgu_ln_b,
                 sgu_w=v_sgu_w, sgu_b=v_sgu_b, mla_q_norm_g=v_mla_q_norm_g, mla_kv_norm_g=v_mla_kv_norm_g,
                 mla_w_uq=v_mla_w_uq, mla_w_ukv=v_mla_w_ukv, ca_rel_bias=v_ca_rel_bias, w_branch=v_w_branch,
                 gate_b=v_gate_b, w_out=v_w_out)
    depth = w_in.shape[0]
    S = x.shape[1]
    xs = x.reshape(S, D)

    cw = {n: _canon(n, weights[n]) for n in SHARD_PACK}
    gw = _gather_shards({n: cw[n] for n in SHARDED}, MXU_DTYPE)
    gw.update(_gather_shards({"gate_b": gate_b}, F32))
    cos_t, sin_a, sin_b = _rope_tables(S)

    layers = []
    for l in range(depth):
        w_t = _pad_rows_w_in(gw["w_in"][:, l].reshape(D_IN, D))
        wuq_t = _pad_head_rows(gw["mla_w_uq"][:, l].reshape(MLA_H * MLA_QK, Q_RANK), MLA_QK)
        ukv_t = gw["mla_w_ukv"][:, l]
        wuk_t = jnp.pad(ukv_t[:, :MLA_NOPE], ((0, 0), (0, 128 - MLA_NOPE), (0, 0))).reshape(MLA_H * 128, KV_RANK)
        wuv_t = ukv_t[:, MLA_NOPE:].reshape(BW, KV_RANK)
        wbr_t = jnp.swapaxes(gw["w_branch"][:, l], 0, 1).reshape(N_BRANCH, D, BW)
        lw = dict(
            w_t=w_t, w_pad=w_t.T, wuq_t=wuq_t, wuq=wuq_t.T, wuk_t=wuk_t, wuk=wuk_t.T, wuv_t=wuv_t, wuv=wuv_t.T,
            wbr_t=wbr_t, wbr=jnp.swapaxes(wbr_t, 1, 2), wout=gw["w_out"][:, l].reshape(D, D),
            gate_b=jnp.swapaxes(gw["gate_b"][:, l], 0, 1).reshape(N_BRANCH, D),
            pre_g=pre_g[l][None], post_g=post_g[l][None], ln_g=sgu_ln_g[l][None], ln_b=sgu_ln_b[l][None],
            ws=sgu_w[l], bfull=jnp.repeat(sgu_b[l].T, BW // SGU_G, axis=1),
            gq=mla_q_norm_g[l][None], gkv=mla_kv_norm_g[l][None], bias=_bias_build(ca_rel_bias[l]))
        layers.append(lw)

    saved = []
    h_x = xs
    for lw in layers:
        proj, xn = _inproj_fwd(h_x, lw["pre_g"], lw["w_pad"])
        ya = _sgu_fwd(proj, lw["ln_g"], lw["ln_b"], lw["ws"], lw["bfull"])
        qh, kh, vh = _mla_prep_fwd(proj, lw["gq"], lw["gkv"], lw["wuq"], lw["wuk"], lw["wuv"], cos_t, sin_a, sin_b)
        ob, lse = _flash_fwd(qh, kh, vh)
        oc = _band_fwd(proj, lw["bias"])
        x_new, merged, hh = _merge_fwd(h_x, ya, ob, oc, proj, lw["gate_b"], lw["wbr"], lw["wout"], lw["post_g"])
        saved.append(dict(x=h_x, proj=proj, xn=xn, ya=ya, qh=qh, kh=kh, vh=vh, ob=ob, lse=lse, oc=oc,
                          merged=merged, h=hh))
        h_x = x_new

    loss_part, dx = _loss_fwd_bwd(h_x, loss_target.reshape(S, D))
    loss = lax.psum(loss_part[0, 0], MESH_AXES)

    rows = {n: [None] * depth for n in SHARD_PACK}
    rep = {n: [None] * depth for n in REPLICATED}
    for l in reversed(range(depth)):
        lw, sv = layers[l], saved[l]
        proj = sv["proj"]
        dmerged, dw_out, dg_post = _out_bwd(dx, sv["h"], sv["merged"], lw["post_g"], lw["wout"].T)
        dgl, dzb, dzc, dya, dob, doc, dwbr_t, dgb = _gate_bwd(
            dmerged, sv["ya"], sv["ob"], sv["oc"], proj, lw["gate_b"], lw["wbr"], lw["wbr_t"])
        dqc, dkc, dvc, dbias = _band_bwd(proj, lw["bias"], doc)
        drel = _bias_fold(dbias)[:, :2 * REL_CLIP + 1]
        dqf, dkf, dvf = _flash_bwd(sv["qh"], sv["kh"], sv["vh"], sv["ob"], sv["lse"], dob)
        db, dwuq_t, dwuk_t, dwuv_t, dgq, dgkv = _mla_prep_bwd(
            proj, dqf, dkf, dvf, lw["gq"], lw["gkv"], lw["wuq_t"], lw["wuk_t"], lw["wuv_t"], cos_t, sin_a, sin_b)
        da, dws, _, dlg, dlb, dbs = _sgu_bwd(proj, dya, lw["ln_g"], lw["ln_b"], lw["ws"], lw["bfull"])
        small = (db, dzb, dqc, dkc, dvc, dzc)
        dx, dg_pre = _inproj_bwd_dx(dgl, da, small, lw["w_t"], sv["x"], lw["pre_g"], dx)
        dw_t = jnp.concatenate([_matmul_tn(a, sv["xn"]) for a in (dgl, da) + small], axis=0)

        rows["w_in"][l] = _unpad_rows_w_in(dw_t).reshape(N_DEV, -1)
        rows["mla_w_uq"][l] = _unpad_head_rows(dwuq_t, MLA_QK).reshape(N_DEV, -1)
        dk3 = dwuk_t.reshape(MLA_H, 128, KV_RANK)[:, :MLA_NOPE]
        dv3 = dwuv_t.reshape(MLA_H, 64, KV_RANK)
        rows["mla_w_ukv"][l] = jnp.concatenate([dk3, dv3], axis=1).reshape(N_DEV, -1)
        rows["w_branch"][l] = jnp.swapaxes(dwbr_t.reshape(N_BRANCH, N_DEV, D // N_DEV, BW), 0, 1).reshape(N_DEV, -1)
        rows["w_out"][l] = dw_out.reshape(N_DEV, -1)
        rows["gate_b"][l] = jnp.swapaxes(dgb.reshape(N_BRANCH, N_DEV, D // N_DEV), 0, 1).reshape(N_DEV, -1)
        rep["pre_g"][l] = dg_pre[0]
        rep["post_g"][l] = dg_post[0]
        rep["sgu_ln_g"][l] = dlg[0]
        rep["sgu_ln_b"][l] = dlb[0]
        rep["sgu_w"][l] = dws
        rep["sgu_b"][l] = dbs.T
        rep["mla_q_norm_g"][l] = dgq[0]
        rep["mla_kv_norm_g"][l] = dgkv[0]
        rep["ca_rel_bias"][l] = drel
    grad_x = dx.reshape(x.shape)

    send = jnp.concatenate([jnp.stack(rows[n], axis=1).reshape(N_DEV, -1) for n in SHARD_PACK], axis=1)
    send = _to_rows(send.astype(MXU_DTYPE), 1024, 128)
    R = send.shape[1]
    recv = _exchange(send.reshape(N_DEV, R * 8, 128)).reshape(N_DEV, R, 1024)
    cm = {n: _canon(n, mom_m[n]) for n in SHARD_PACK}
    cv = {n: _canon(n, mom_v[n]) for n in SHARD_PACK}
    outs_sh = _reduce_adamw(recv, _pack(cw, SHARD_PACK), _pack(cm, SHARD_PACK), _pack(cv, SHARD_PACK))
    outs_sh = [{n: _canon(n, a) for n, a in _unpack(p, cw, SHARD_PACK).items()} for p in outs_sh]

    part = _pack({n: jnp.stack(rep[n]) for n in REPLICATED}, REPLICATED)
    Rs = part.shape[0]
    allp = _all_gather(part.reshape(Rs * 8, 128)).reshape(N_DEV, Rs, 1024)
    outs_rep = _reduce_adamw(allp, _pack(weights, REPLICATED), _pack(mom_m, REPLICATED), _pack(mom_v, REPLICATED))
    outs_rep = [_unpack(p, weights, REPLICATED) for p in outs_rep]

    outs = [{**a, **b} for a, b in zip(outs_sh, outs_rep)]
    return (loss, grad_x, *[o[n] for o in outs for n in OUT_ORDER])
```

```python
import functools

import jax
import jax.numpy as jnp
from jax import lax
from jax.experimental import pallas as pl
from jax.experimental.pallas import tpu as pltpu

F32 = jnp.float32
MXU_DTYPE = jnp.bfloat16
EPS = 1e-6
NEG = -1e30
MESH_AXES = ("x", "y", "c")
N_DEV = 8

D = 1024
BW = 512
N_BRANCH = 3
CHUNK = 64
SGU_T = 128
SGU_G = 8
MLA_H = 8
MLA_NOPE = 64
MLA_ROPE = 32
MLA_QK = MLA_NOPE + MLA_ROPE
Q_RANK = 256
KV_RANK = 128
CA_H = 8
CA_DH = 64
LEFT_CHUNKS = 8
REL_CLIP = 128
D_IN = 7584

G_OFF, UA, VA, ZA, QD, ZB, QC, KC, VC, ZC, DP = 0, 3072, 3584, 4096, 4608, 5120, 5632, 6144, 6656, 7168, 7680
BAND_TQ = 256
BAND_W = 3 * BAND_TQ

ADAM_LR, ADAM_B1, ADAM_B2, ADAM_EPS, ADAM_WD, ADAM_STEP = 0.001, 0.9, 0.999, 1e-08, 0.01, 10


def _params(sem, mib):
    return pltpu.CompilerParams(dimension_semantics=sem, vmem_limit_bytes=mib << 20)


def _mm(a, b):
    return jnp.dot(a.astype(MXU_DTYPE), b.astype(MXU_DTYPE), preferred_element_type=F32)


def _mm_nt(a, b):
    return lax.dot_general(a.astype(MXU_DTYPE), b.astype(MXU_DTYPE), (((1,), (1,)), ((), ())),
                           preferred_element_type=F32)


def _mm_tn(a, b):
    return lax.dot_general(a.astype(MXU_DTYPE), b.astype(MXU_DTYPE), (((0,), (0,)), ((), ())),
                           preferred_element_type=F32)


def _sigmoid(z):
    return 1.0 / (1.0 + jnp.exp(-z))


def _rms(x):
    r = lax.rsqrt(jnp.mean(x * x, axis=-1, keepdims=True) + EPS)
    return x * r, r


def _rms_bwd(dn, n, r):
    return r * (dn - n * jnp.mean(dn * n, axis=-1, keepdims=True))


def _rope(b, c, sa, sb):
    return b * c + pltpu.roll(b, 112, 1) * sa + pltpu.roll(b, 16, 1) * sb


def _rope_t(d, c, sa, sb):
    return d * c + pltpu.roll(d * sa, 16, 1) + pltpu.roll(d * sb, 112, 1)


def _all_gather(blk):
    R = blk.shape[0]

    def body(x_ref, out_ref, send_sems, recv_sems, local_sem):
        x, y, c = lax.axis_index("x"), lax.axis_index("y"), lax.axis_index("c")
        me, sibling = (x, y, c), (x, y, 1 - c)
        chips = [(1 - x, y), (x, 1 - y), (1 - x, 1 - y)]

        def slot(px, py, pc):
            return out_ref.at[4 * px + 2 * py + pc]

        def copy(k, block, to, src=None):
            return pltpu.make_async_remote_copy(
                src_ref=slot(*block) if src is None else src, dst_ref=slot(*block),
                send_sem=send_sems.at[k], recv_sem=recv_sems.at[k],
                device_id=to, device_id_type=pl.DeviceIdType.MESH)

        mine = pltpu.make_async_copy(x_ref, slot(*me), local_sem)
        mine.start()
        first = [copy(0, me, sibling, src=x_ref)]
        first += [copy(1 + j, me, (*chip, c), src=x_ref) for j, chip in enumerate(chips)]
        for cp in first:
            cp.start()
        passed = [copy(4 + j, (*chip, c), sibling) for j, chip in enumerate(chips)]
        for j, chip in enumerate(chips):
            copy(1 + j, (*chip, c), me).wait_recv()
            passed[j].start()
        copy(0, sibling, me).wait_recv()
        for j, chip in enumerate(chips):
            copy(4 + j, (*chip, 1 - c), me).wait_recv()
        for cp in first + passed:
            cp.wait_send()
        mine.wait()

    return pl.pallas_call(
        body, name="all_gather",
        out_shape=jax.ShapeDtypeStruct((N_DEV, R, 128), blk.dtype),
        in_specs=[pl.BlockSpec(memory_space=pl.ANY)],
        out_specs=pl.BlockSpec(memory_space=pl.ANY),
        scratch_shapes=[pltpu.SemaphoreType.DMA((7,)), pltpu.SemaphoreType.DMA((7,)), pltpu.SemaphoreType.DMA(())],
    )(blk)


def _exchange(send):
    R = send.shape[1]

    def body(s_ref, r_ref, send_sems, recv_sems, local_sem):
        x, y, c = lax.axis_index("x"), lax.axis_index("y"), lax.axis_index("c")
        me = 4 * x + 2 * y + c
        local = pltpu.make_async_copy(s_ref.at[me], r_ref.at[me], local_sem)
        local.start()

        def peer(k):
            px = 1 - x if (k >> 2) & 1 else x
            py = 1 - y if (k >> 1) & 1 else y
            pc = 1 - c if k & 1 else c
            return (px, py, pc), 4 * px + 2 * py + pc

        copies = []
        for k in range(1, N_DEV):
            pos, pid = peer(k)
            cp = pltpu.make_async_remote_copy(
                src_ref=s_ref.at[pid], dst_ref=r_ref.at[me],
                send_sem=send_sems.at[k], recv_sem=recv_sems.at[k],
                device_id=pos, device_id_type=pl.DeviceIdType.MESH)
            cp.start()
            copies.append(cp)
        for k in range(1, N_DEV):
            pos, pid = peer(k)
            pltpu.make_async_remote_copy(
                src_ref=s_ref.at[pid], dst_ref=r_ref.at[pid],
                send_sem=send_sems.at[k], recv_sem=recv_sems.at[k],
                device_id=pos, device_id_type=pl.DeviceIdType.MESH).wait_recv()
        for cp in copies:
            cp.wait_send()
        local.wait()

    return pl.pallas_call(
        body, name="grad_exchange",
        out_shape=jax.ShapeDtypeStruct((N_DEV, R, 128), send.dtype),
        in_specs=[pl.BlockSpec(memory_space=pl.ANY)],
        out_specs=pl.BlockSpec(memory_space=pl.ANY),
        scratch_shapes=[pltpu.SemaphoreType.DMA((N_DEV,)), pltpu.SemaphoreType.DMA((N_DEV,)),
                        pltpu.SemaphoreType.DMA(())],
    )(send)


def _reduce_adamw(recv, w, m, v):
    R = w.shape[0]
    tr = 128 if R % 128 == 0 else 8
    c1 = 1.0 - ADAM_B1 ** ADAM_STEP
    c2 = 1.0 - ADAM_B2 ** ADAM_STEP

    def body(r_ref, w_ref, m_ref, v_ref, g_ref, d_ref, nm_ref, nv_ref):
        g = r_ref[0].astype(F32)
        for s in range(1, N_DEV):
            g = g + r_ref[s].astype(F32)
        m2 = ADAM_B1 * m_ref[...] + (1.0 - ADAM_B1) * g
        v2 = ADAM_B2 * v_ref[...] + (1.0 - ADAM_B2) * (g * g)
        m_hat = m2 / c1
        v_hat = v2 / c2
        g_ref[...] = g
        d_ref[...] = -ADAM_LR * (m_hat / (jnp.sqrt(v_hat) + ADAM_EPS) + ADAM_WD * w_ref[...])
        nm_ref[...] = m2
        nv_ref[...] = v2

    row = pl.BlockSpec((tr, 1024), lambda i: (i, 0))
    return pl.pallas_call(
        body, name="reduce_adamw", grid=(R // tr,),
        in_specs=[pl.BlockSpec((N_DEV, tr, 1024), lambda i: (0, i, 0)), row, row, row],
        out_specs=[row, row, row, row],
        out_shape=[jax.ShapeDtypeStruct((R, 1024), F32)] * 4,
        compiler_params=_params(("parallel",), 40),
    )(recv, w, m, v)


def _inproj_fwd(x, pre_g, w_pad):
    S = x.shape[0]
    tm, tn = min(512, S), 1536

    def body(x_ref, g_ref, w_ref, proj_ref, xn_ref):
        @pl.when(pl.program_id(1) == 0)
        def _():
            n, _ = _rms(x_ref[...])
            xn_ref[...] = (n * g_ref[...]).astype(xn_ref.dtype)
        proj_ref[...] = jnp.dot(xn_ref[...], w_ref[...], preferred_element_type=F32)

    return pl.pallas_call(
        body, name="inproj_fwd", grid=(S // tm, DP // tn),
        in_specs=[pl.BlockSpec((tm, D), lambda i, j: (i, 0)), pl.BlockSpec((1, D), lambda i, j: (0, 0)),
                  pl.BlockSpec((D, tn), lambda i, j: (0, j))],
        out_specs=[pl.BlockSpec((tm, tn), lambda i, j: (i, j)), pl.BlockSpec((tm, D), lambda i, j: (i, 0))],
        out_shape=[jax.ShapeDtypeStruct((S, DP), F32), jax.ShapeDtypeStruct((S, D), MXU_DTYPE)],
        compiler_params=_params(("parallel", "arbitrary"), 48),
    )(x, pre_g, w_pad)


def _sgu_tri():
    return lax.broadcasted_iota(jnp.int32, (SGU_T, SGU_T), 0) >= lax.broadcasted_iota(jnp.int32, (SGU_T, SGU_T), 1)


def _sgu_mix(ws_m, vb, bfull, grp):
    mixed = bfull
    for g in range(SGU_G):
        mixed = mixed + jnp.where(grp == g, _mm(ws_m[g], vb), 0.0)
    return mixed


def _layernorm(v, g, b):
    xc = v - jnp.mean(v, axis=-1, keepdims=True)
    rstd = lax.rsqrt(jnp.mean(xc * xc, axis=-1, keepdims=True) + EPS)
    vhat = xc * rstd
    return vhat * g + b, vhat, rstd


def _sgu_fwd(proj, ln_g, ln_b, ws, bfull):
    S = proj.shape[0]
    tm = min(512, S)

    def body(u_ref, v_ref, z_ref, g_ref, b_ref, ws_ref, bf_ref, ya_ref):
        tri = _sgu_tri()
        ws_m = [jnp.where(tri, ws_ref[g], 0.0).astype(MXU_DTYPE) for g in range(SGU_G)]
        grp = lax.broadcasted_iota(jnp.int32, (1, BW), 1) // (BW // SGU_G)
        for b in range(tm // SGU_T):
            r = slice(b * SGU_T, (b + 1) * SGU_T)
            vln, _, _ = _layernorm(v_ref[r, :], g_ref[...], b_ref[...])
            mixed = _sgu_mix(ws_m, vln.astype(MXU_DTYPE), bf_ref[...], grp)
            z = z_ref[r, :]
            ya_ref[r, :] = (u_ref[r, :] * mixed * (z * _sigmoid(z))).astype(ya_ref.dtype)

    blk = lambda cb: pl.BlockSpec((tm, BW), lambda i, cb=cb: (i, cb))
    vec = pl.BlockSpec((1, BW), lambda i: (0, 0))
    return pl.pallas_call(
        body, name="sgu_fwd", grid=(S // tm,),
        in_specs=[blk(UA // BW), blk(VA // BW), blk(ZA // BW), vec, vec,
                  pl.BlockSpec((SGU_G, SGU_T, SGU_T), lambda i: (0, 0, 0)),
                  pl.BlockSpec((SGU_T, BW), lambda i: (0, 0))],
        out_specs=pl.BlockSpec((tm, BW), lambda i: (i, 0)),
        out_shape=jax.ShapeDtypeStruct((S, BW), MXU_DTYPE),
        compiler_params=_params(("parallel",), 32),
    )(proj, proj, proj, ln_g, ln_b, ws, bfull)


def _mla_prep_fwd(proj, gq, gkv, wuq, wuk, wuv, cos_t, sin_a, sin_b):
    S = proj.shape[0]
    tm = min(512, S)

    def body(p_ref, gq_ref, gkv_ref, wuq_ref, wuk_ref, wuv_ref, c_ref, sa_ref, sb_ref, q_ref, k_ref, v_ref):
        nq, _ = _rms(p_ref[:, 0:Q_RANK])
        nkv, _ = _rms(p_ref[:, Q_RANK:Q_RANK + KV_RANK])
        cq = (nq * gq_ref[...]).astype(MXU_DTYPE)
        ckv = (nkv * gkv_ref[...]).astype(MXU_DTYPE)
        qf = _mm(cq, wuq_ref[...])
        kf = _mm(ckv, wuk_ref[...])
        v_ref[...] = _mm(ckv, wuv_ref[...]).astype(v_ref.dtype)
        c, sa, sb = c_ref[...], sa_ref[...], sb_ref[...]
        krr = _rope(p_ref[:, Q_RANK + KV_RANK:BW], c, sa, sb)
        for h in range(MLA_H):
            sl = slice(128 * h, 128 * (h + 1))
            q_ref[:, sl] = (_rope(qf[:, sl], c, sa, sb) * MLA_SCALE_LOG2).astype(q_ref.dtype)
            k_ref[:, sl] = (kf[:, sl] + krr).astype(k_ref.dtype)

    full = lambda a: pl.BlockSpec(a.shape, lambda i: (0,) * a.ndim)
    tab = pl.BlockSpec((tm, 128), lambda i: (i, 0))
    return pl.pallas_call(
        body, name="mla_prep_fwd", grid=(S // tm,),
        in_specs=[pl.BlockSpec((tm, BW), lambda i: (i, QD // BW)), full(gq), full(gkv), full(wuq), full(wuk),
                  full(wuv), tab, tab, tab],
        out_specs=[pl.BlockSpec((tm, 1024), lambda i: (i, 0)), pl.BlockSpec((tm, 1024), lambda i: (i, 0)),
                   pl.BlockSpec((tm, BW), lambda i: (i, 0))],
        out_shape=[jax.ShapeDtypeStruct((S, 1024), MXU_DTYPE), jax.ShapeDtypeStruct((S, 1024), MXU_DTYPE),
                   jax.ShapeDtypeStruct((S, BW), MXU_DTYPE)],
        compiler_params=_params(("parallel",), 40),
    )(proj, gq, gkv, wuq, wuk, wuv, cos_t, sin_a, sin_b)


MLA_SCALE = MLA_QK ** -0.5
MLA_SCALE_LOG2 = MLA_SCALE * 1.4426950408889634


def _tri_tables(n, q_major):
    if q_major:
        pairs = [(qi, ki) for qi in range(n) for ki in range(qi + 1)]
    else:
        pairs = [(qi, ki) for ki in range(n) for qi in range(ki, n)]
    return (jnp.asarray([p[0] for p in pairs], jnp.int32), jnp.asarray([p[1] for p in pairs], jnp.int32))


def _chunk_mask_t(T):
    kc = lax.broadcasted_iota(jnp.int32, (T, T), 0) >> 6
    qc = lax.broadcasted_iota(jnp.int32, (T, T), 1) >> 6
    return kc <= qc


def _row_to_col(row, T):
    return jnp.transpose(jnp.broadcast_to(row, (128, T)))


def _flash_fwd(qh, kh, vh):
    S = qh.shape[0]
    T = min(512, S)
    n = S // T
    qt, kt = _tri_tables(n, q_major=True)

    def body(qt_ref, kt_ref, q_ref, k_ref, v_ref, o_ref, lse_ref, m_scr, l_scr, acc_scr):
        t = pl.program_id(1)
        qi, ki = qt_ref[t], kt_ref[t]

        @pl.when(ki == 0)
        def _():
            m_scr[...] = jnp.full(m_scr.shape, NEG, F32)
            l_scr[...] = jnp.zeros(l_scr.shape, F32)
            acc_scr[...] = jnp.zeros(acc_scr.shape, F32)

        def step(masked):
            head = lax.broadcasted_iota(jnp.int32, (1, 128), 1) // 64
            for j in range(2):
                sl = slice(128 * j, 128 * (j + 1))
                s = _mm_nt(k_ref[:, sl], q_ref[:, sl])
                if masked:
                    s = jnp.where(_chunk_mask_t(T), s, NEG)
                m_old = m_scr[j]
                m_new = jnp.maximum(m_old, jnp.max(s, axis=0, keepdims=True))
                alpha = jnp.exp2(m_old - m_new)
                p = jnp.exp2(s - m_new)
                l_scr[j] = alpha * l_scr[j] + jnp.sum(p, axis=0, keepdims=True)
                vj = jnp.where(head == j, v_ref[...], 0)
                acc_scr[j] = _row_to_col(alpha, T) * acc_scr[j] + _mm_tn(p, vj)
                m_scr[j] = m_new

        @pl.when(ki < qi)
        def _():
            step(False)

        @pl.when(ki == qi)
        def _():
            step(True)
            o_ref[...] = (acc_scr[0] / _row_to_col(l_scr[0], T) + acc_scr[1] / _row_to_col(l_scr[1], T))
            for j in range(2):
                lse_ref[0, j:j + 1, :] = m_scr[j] + jnp.log2(l_scr[j])

    qmap = lambda hp, t, qt, kt: (qt[t], hp)
    kmap = lambda hp, t, qt, kt: (kt[t], hp)
    return pl.pallas_call(
        body, name="flash_fwd",
        grid_spec=pltpu.PrefetchScalarGridSpec(
            num_scalar_prefetch=2, grid=(MLA_H // 2, qt.shape[0]),
            in_specs=[pl.BlockSpec((T, 256), qmap), pl.BlockSpec((T, 256), kmap), pl.BlockSpec((T, 128), kmap)],
            out_specs=[pl.BlockSpec((T, 128), qmap),
                       pl.BlockSpec((1, 2, T), lambda hp, t, qt, kt: (hp, 0, qt[t]))],
            scratch_shapes=[pltpu.VMEM((2, 1, T), F32), pltpu.VMEM((2, 1, T), F32),
                            pltpu.VMEM((2, T, 128), F32)]),
        out_shape=[jax.ShapeDtypeStruct((S, BW), F32), jax.ShapeDtypeStruct((MLA_H // 2, 2, S), F32)],
        compiler_params=_params(("parallel", "arbitrary"), 40),
    )(qt, kt, qh, kh, vh)


def _attn_delta(o, do):
    S = o.shape[0]
    T = min(512, S)

    def body(o_ref, do_ref, d_ref):
        head = lax.broadcasted_iota(jnp.int32, (1, 128), 1) // 64
        prod = o_ref[...] * do_ref[...]
        for j in range(2):
            d_ref[0, j:j + 1, :] = jnp.sum(jnp.transpose(jnp.where(head == j, prod, 0.0)), axis=0, keepdims=True)

    blk = pl.BlockSpec((T, 128), lambda hp, i: (i, hp))
    return pl.pallas_call(
        body, name="attn_delta", grid=(MLA_H // 2, S // T),
        in_specs=[blk, blk],
        out_specs=pl.BlockSpec((1, 2, T), lambda hp, i: (hp, 0, i)),
        out_shape=jax.ShapeDtypeStruct((MLA_H // 2, 2, S), F32),
        compiler_params=_params(("parallel", "parallel"), 32),
    )(o, do)


def _band_specs(S):
    q = pl.BlockSpec((BAND_TQ, 128), lambda hp, qi: (qi, QC // 128 + hp))
    ks = [pl.BlockSpec((BAND_TQ, 128), lambda hp, qi, t=t: (jnp.maximum(qi - 2 + t, 0), KC // 128 + hp))
          for t in range(3)]
    vs = [pl.BlockSpec((BAND_TQ, 128), lambda hp, qi, t=t: (jnp.maximum(qi - 2 + t, 0), VC // 128 + hp))
          for t in range(3)]
    bias = pl.BlockSpec((2, BAND_TQ, BAND_W), lambda hp, qi: (hp, 0, 0))
    return q, ks, vs, bias


def _band_probs(qj, kcat, bias_j, valid, scale):
    s = _mm_nt(qj, kcat) * scale + bias_j
    s = jnp.where(valid, s, NEG)
    p = jnp.exp(s - jnp.max(s, axis=1, keepdims=True))
    return p / jnp.sum(p, axis=1, keepdims=True)


def _band_valid(qi):
    tile = lax.broadcasted_iota(jnp.int32, (1, BAND_W), 1) // BAND_TQ
    return tile + qi >= 2


def _band_fwd(proj, bias):
    S = proj.shape[0]
    scale = CA_DH ** -0.5

    def body(q_ref, k0, k1, k2, v0, v1, v2, b_ref, o_ref):
        qi = pl.program_id(1)
        head = lax.broadcasted_iota(jnp.int32, (1, 128), 1) // 64
        kcat = jnp.concatenate([k0[...], k1[...], k2[...]], axis=0).astype(MXU_DTYPE)
        vcat = jnp.concatenate([v0[...], v1[...], v2[...]], axis=0).astype(MXU_DTYPE)
        valid = _band_valid(qi)
        q = q_ref[...]
        o = jnp.zeros((BAND_TQ, 128), F32)
        for j in range(2):
            pn = _band_probs(jnp.where(head == j, q, 0.0), kcat, b_ref[j], valid, scale)
            o = o + _mm(pn, jnp.where(head == j, vcat, 0))
        o_ref[...] = o

    q, ks, vs, bspec = _band_specs(S)
    return pl.pallas_call(
        body, name="band_fwd", grid=(CA_H // 2, S // BAND_TQ),
        in_specs=[q, *ks, *vs, bspec],
        out_specs=pl.BlockSpec((BAND_TQ, 128), lambda hp, qi: (qi, hp)),
        out_shape=jax.ShapeDtypeStruct((S, BW), F32),
        compiler_params=_params(("parallel", "arbitrary"), 40),
    )(proj, proj, proj, proj, proj, proj, proj, bias)


def _merge_fwd(x, ya, ob, oc, proj, gate_b, wbr, wout, post_g):
    S = x.shape[0]
    tm = min(256, S)

    def body(x_ref, ya_ref, ob_ref, oc_ref, zb_ref, zc_ref, g0, g1, g2, gb_ref, wbr_ref, wo_ref, pg_ref,
             xo_ref, mg_ref, h_ref):
        zb, zc = zb_ref[...], zc_ref[...]
        ys = [ya_ref[...], ob_ref[...] * (zb * _sigmoid(zb)), oc_ref[...] * (zc * _sigmoid(zc))]
        merged = jnp.zeros((tm, D), F32)
        for i, g_ref in enumerate((g0, g1, g2)):
            merged = merged + _sigmoid(g_ref[...] + gb_ref[i:i + 1, :]) * _mm(ys[i], wbr_ref[i])
        mg_ref[...] = merged.astype(mg_ref.dtype)
        h = _mm(merged, wo_ref[...])
        h_ref[...] = h
        n, _ = _rms(h)
        xo_ref[...] = x_ref[...] + n * pg_ref[...]

    row = lambda w, cb=0: pl.BlockSpec((tm, w), lambda i, cb=cb: (i, cb))
    full = lambda a: pl.BlockSpec(a.shape, lambda i: (0,) * a.ndim)
    return pl.pallas_call(
        body, name="merge_fwd", grid=(S // tm,),
        in_specs=[row(D), row(BW), row(BW), row(BW), row(BW, ZB // BW), row(BW, ZC // BW),
                  row(D, 0), row(D, 1), row(D, 2), full(gate_b), full(wbr), full(wout), full(post_g)],
        out_specs=[row(D), row(D), row(D)],
        out_shape=[jax.ShapeDtypeStruct((S, D), F32), jax.ShapeDtypeStruct((S, D), MXU_DTYPE),
                   jax.ShapeDtypeStruct((S, D), F32)],
        compiler_params=_params(("parallel",), 56),
    )(x, ya, ob, oc, proj, proj, proj, proj, proj, gate_b, wbr, wout, post_g)


def _loss_fwd_bwd(y, target):
    S = y.shape[0]
    tm = min(512, S)

    def body(y_ref, t_ref, loss_ref, dy_ref):
        @pl.when(pl.program_id(0) == 0)
        def _():
            loss_ref[...] = jnp.zeros((1, 1), F32)
        err = y_ref[...] - t_ref[...]
        loss_ref[...] += 0.5 * jnp.sum(jnp.mean(err * err, axis=-1, keepdims=True), axis=0, keepdims=True)
        dy_ref[...] = err * (1.0 / D)

    row = pl.BlockSpec((tm, D), lambda i: (i, 0))
    return pl.pallas_call(
        body, name="loss", grid=(S // tm,),
        in_specs=[row, row],
        out_specs=[pl.BlockSpec((1, 1), lambda i: (0, 0)), row],
        out_shape=[jax.ShapeDtypeStruct((1, 1), F32), jax.ShapeDtypeStruct((S, D), F32)],
        compiler_params=_params(("arbitrary",), 32),
    )(y, target)


def _first_step_zero(refs, first):
    @pl.when(first)
    def _():
        for r in refs:
            r[...] = jnp.zeros(r.shape, r.dtype)


def _out_bwd(dxo, h, merged, post_g, wout_t):
    S = dxo.shape[0]
    tm = min(256, S)

    def body(d_ref, h_ref, mg_ref, pg_ref, wt_ref, dm_ref, dw_ref, dg_ref):
        _first_step_zero((dw_ref, dg_ref), pl.program_id(0) == 0)
        d = d_ref[...]
        hn, r = _rms(h_ref[...])
        dg_ref[...] += jnp.sum(d * hn, axis=0, keepdims=True)
        dh = _rms_bwd(d * pg_ref[...], hn, r)
        dm_ref[...] = _mm(dh, wt_ref[...])
        dw_ref[...] += _mm_tn(mg_ref[...], dh)

    row = pl.BlockSpec((tm, D), lambda i: (i, 0))
    full = lambda shape: pl.BlockSpec(shape, lambda i: (0,) * len(shape))
    return pl.pallas_call(
        body, name="out_bwd", grid=(S // tm,),
        in_specs=[row, row, row, full((1, D)), full((D, D))],
        out_specs=[row, full((D, D)), full((1, D))],
        out_shape=[jax.ShapeDtypeStruct((S, D), F32), jax.ShapeDtypeStruct((D, D), F32),
                   jax.ShapeDtypeStruct((1, D), F32)],
        compiler_params=_params(("arbitrary",), 40),
    )(dxo, h, merged, post_g, wout_t)


def _gate_bwd(dm, ya, ob, oc, proj, gate_b, wbr, wbr_t):
    S = dm.shape[0]
    tm = min(128, S)

    def body(dm_ref, ya_ref, ob_ref, oc_ref, zb_ref, zc_ref, g0, g1, g2, gb_ref, wbr_ref, wbt_ref,
             dg_ref, dzb_ref, dzc_ref, dya_ref, dob_ref, doc_ref, dwbr_ref, dgb_ref):
        _first_step_zero((dwbr_ref, dgb_ref), pl.program_id(0) == 0)
        dmv = dm_ref[...]
        zb, zc = zb_ref[...], zc_ref[...]
        sgb, sgc = _sigmoid(zb), _sigmoid(zc)
        ob, oc = ob_ref[...], oc_ref[...]
        ys = [ya_ref[...], (ob * (zb * sgb)).astype(MXU_DTYPE), (oc * (zc * sgc)).astype(MXU_DTYPE)]
        dys = []
        for i, g_ref in enumerate((g0, g1, g2)):
            br = _mm(ys[i], wbr_ref[i])
            gate = _sigmoid(g_ref[...] + gb_ref[i:i + 1, :])
            dgl = dmv * br * (gate * (1.0 - gate))
            dg_ref[:, D * i:D * (i + 1)] = dgl.astype(dg_ref.dtype)
            dgb_ref[i:i + 1, :] += jnp.sum(dgl, axis=0, keepdims=True)
            dbr = dmv * gate
            dys.append(_mm(dbr, wbt_ref[i]))
            dwbr_ref[i] += _mm_tn(dbr, ys[i])
        dya_ref[...] = dys[0]
        dob_ref[...] = dys[1] * (zb * sgb)
        dzb_ref[...] = (dys[1] * ob * (sgb * (1.0 + zb * (1.0 - sgb)))).astype(dzb_ref.dtype)
        doc_ref[...] = dys[2] * (zc * sgc)
        dzc_ref[...] = (dys[2] * oc * (sgc * (1.0 + zc * (1.0 - sgc)))).astype(dzc_ref.dtype)

    row = lambda w, cb=0: pl.BlockSpec((tm, w), lambda i, cb=cb: (i, cb))
    full = lambda a: pl.BlockSpec(a.shape, lambda i: (0,) * a.ndim)
    sds = lambda w, dt=F32: jax.ShapeDtypeStruct((S, w), dt)
    return pl.pallas_call(
        body, name="gate_bwd", grid=(S // tm,),
        in_specs=[row(D), row(BW), row(BW), row(BW), row(BW, ZB // BW), row(BW, ZC // BW),
                  row(D, 0), row(D, 1), row(D, 2), full(gate_b), full(wbr), full(wbr_t)],
        out_specs=[row(3 * D), row(BW), row(BW), row(BW), row(BW), row(BW),
                   pl.BlockSpec((N_BRANCH, D, BW), lambda i: (0, 0, 0)), pl.BlockSpec((N_BRANCH, D), lambda i: (0, 0))],
        out_shape=[sds(3 * D, MXU_DTYPE), sds(BW, MXU_DTYPE), sds(BW, MXU_DTYPE), sds(BW), sds(BW), sds(BW),
                   jax.ShapeDtypeStruct((N_BRANCH, D, BW), F32), jax.ShapeDtypeStruct((N_BRANCH, D), F32)],
        compiler_params=_params(("arbitrary",), 56),
    )(dm, ya, ob, oc, proj, proj, proj, proj, proj, gate_b, wbr, wbr_t)


def _band_bwd(proj, bias, do):
    S = proj.shape[0]
    scale = CA_DH ** -0.5

    def body(q_ref, k0, k1, k2, v0, v1, v2, b_ref, do_ref, dq_ref, dk_ref, dv_ref, db_ref):
        qi = pl.program_id(1)
        _first_step_zero((dk_ref, dv_ref, db_ref), qi == 0)
        head = lax.broadcasted_iota(jnp.int32, (1, 128), 1) // 64
        kcat = jnp.concatenate([k0[...], k1[...], k2[...]], axis=0).astype(MXU_DTYPE)
        vcat = jnp.concatenate([v0[...], v1[...], v2[...]], axis=0).astype(MXU_DTYPE)
        valid = _band_valid(qi)
        q, dov = q_ref[...], do_ref[...]
        dq = jnp.zeros((BAND_TQ, 128), F32)
        dk = jnp.zeros((BAND_W, 128), F32)
        dv = jnp.zeros((BAND_W, 128), F32)
        for j in range(2):
            qj = jnp.where(head == j, q, 0.0).astype(MXU_DTYPE)
            doj = jnp.where(head == j, dov, 0.0).astype(MXU_DTYPE)
            pn = _band_probs(qj, kcat, b_ref[j], valid, scale)
            dv = dv + _mm_tn(pn, doj)
            dp = _mm_nt(doj, vcat)
            ds = pn * (dp - jnp.sum(pn * dp, axis=1, keepdims=True))
            db_ref[j] += ds
            dsb = (ds * scale).astype(MXU_DTYPE)
            dq = dq + _mm(dsb, jnp.where(head == j, kcat, 0))
            dk = dk + _mm_tn(dsb, qj)
        dq_ref[...] = dq.astype(dq_ref.dtype)
        for t in range(3):
            @pl.when(qi - 2 + t >= 0)
            def _(t=t):
                rows = pl.ds(pl.multiple_of((qi - 2 + t) * BAND_TQ, BAND_TQ), BAND_TQ)
                dk_ref[rows, :] += dk[t * BAND_TQ:(t + 1) * BAND_TQ]
                dv_ref[rows, :] += dv[t * BAND_TQ:(t + 1) * BAND_TQ]

    q, ks, vs, bspec = _band_specs(S)
    col = pl.BlockSpec((S, 128), lambda hp, qi: (0, hp))
    return pl.pallas_call(
        body, name="band_bwd", grid=(CA_H // 2, S // BAND_TQ),
        in_specs=[q, *ks, *vs, bspec, pl.BlockSpec((BAND_TQ, 128), lambda hp, qi: (qi, hp))],
        out_specs=[pl.BlockSpec((BAND_TQ, 128), lambda hp, qi: (qi, hp)), col, col, bspec],
        out_shape=[jax.ShapeDtypeStruct((S, BW), MXU_DTYPE), jax.ShapeDtypeStruct((S, BW), F32),
                   jax.ShapeDtypeStruct((S, BW), F32), jax.ShapeDtypeStruct((CA_H, BAND_TQ, BAND_W), F32)],
        compiler_params=_params(("parallel", "arbitrary"), 56),
    )(proj, proj, proj, proj, proj, proj, proj, bias, do)


BIAS_LO = REL_CLIP - (CHUNK - 1)
BIAS_FAR = 2 * REL_CLIP
BIAS_NEAR0 = BAND_W // 2


def _band_index(col0, ncol):
    i = lax.broadcasted_iota(jnp.int32, (BAND_TQ, ncol), 0)
    j = lax.broadcasted_iota(jnp.int32, (BAND_TQ, ncol), 1) + col0
    idx = jnp.clip(i + 2 * BAND_TQ - j, -REL_CLIP, REL_CLIP) + REL_CLIP
    ci, cj = i // CHUNK, j // CHUNK
    return jnp.where((ci <= cj) & (cj <= ci + LEFT_CHUNKS), idx, -1)


def _skew(x, right):
    row = lax.broadcasted_iota(jnp.int32, (BAND_TQ, 1), 0)
    for b in range(BAND_TQ.bit_length() - 1):
        shift = (1 << b) if right else BAND_W - (1 << b)
        x = jnp.where(((row >> b) & 1) == 1, pltpu.roll(x, shift, 1), x)
    return x


SKEW_NEAR0 = 2 * BAND_TQ - REL_CLIP + 1
SKEW_NEAR1 = 2 * BAND_TQ + CHUNK
SKEW_WRAP0 = BAND_W - (CHUNK - 1)


def _bias_build(rel_table):
    far = rel_table[:, BIAS_FAR:]
    base = jnp.concatenate([jnp.broadcast_to(far, (CA_H, SKEW_NEAR0)), rel_table[:, BIAS_LO:BIAS_FAR][:, ::-1],
                            jnp.broadcast_to(far, (CA_H, BAND_W - SKEW_NEAR1))], axis=1)[:, None, :]

    def body(base_ref, out_ref):
        valid = _band_index(0, BAND_W) >= 0
        for h in range(CA_H):
            tile = _skew(jnp.broadcast_to(base_ref[h], (BAND_TQ, BAND_W)), right=True)
            out_ref[h] = jnp.where(valid, tile, NEG)

    return pl.pallas_call(
        body, name="bias_build",
        out_shape=jax.ShapeDtypeStruct((CA_H, BAND_TQ, BAND_W), F32),
        in_specs=[pl.BlockSpec(memory_space=pltpu.VMEM)],
        out_specs=pl.BlockSpec(memory_space=pltpu.VMEM),
        compiler_params=pltpu.CompilerParams(vmem_limit_bytes=40 << 20),
    )(base)


def _bias_fold(db):
    def body(db_ref, sums_ref, far_ref):
        col = lax.broadcasted_iota(jnp.int32, (1, BAND_W), 1)
        is_far = (col < SKEW_NEAR0) | (col >= SKEW_WRAP0)
        for h in range(CA_H):
            sums = jnp.sum(_skew(db_ref[h], right=False), axis=0, keepdims=True)
            sums_ref[h] = sums
            far_ref[h] = jnp.broadcast_to(jnp.sum(jnp.where(is_far, sums, 0.0), axis=1, keepdims=True), (1, 128))

    sums, far = pl.pallas_call(
        body, name="bias_fold",
        out_shape=[jax.ShapeDtypeStruct((CA_H, 1, BAND_W), F32), jax.ShapeDtypeStruct((CA_H, 1, 128), F32)],
        in_specs=[pl.BlockSpec(memory_space=pltpu.VMEM)],
        out_specs=[pl.BlockSpec(memory_space=pltpu.VMEM), pl.BlockSpec(memory_space=pltpu.VMEM)],
        compiler_params=pltpu.CompilerParams(vmem_limit_bytes=40 << 20),
    )(db)
    near = sums[:, 0, SKEW_NEAR0:SKEW_NEAR1][:, ::-1]
    return jnp.concatenate([jnp.zeros((CA_H, BIAS_LO), F32), near, far[:, 0, :1]], axis=1)


def _flash_bwd(qh, kh, vh, lse, delta, do):
    S = qh.shape[0]
    T = min(512, S)
    n = S // T
    qt, kt = _tri_tables(n, q_major=False)

    def body(qt_ref, kt_ref, q_ref, k_ref, v_ref, lse_ref, dl_ref, do_ref, dq_ref, dk_ref, dv_ref, dk_scr, dv_scr):
        t = pl.program_id(1)
        qi, ki = qt_ref[t], kt_ref[t]

        @pl.when(qi == ki)
        def _():
            dk_scr[...] = jnp.zeros(dk_scr.shape, F32)
            dv_scr[...] = jnp.zeros(dv_scr.shape, F32)

        def step(masked):
            head = lax.broadcasted_iota(jnp.int32, (1, 128), 1) // 64
            dov = do_ref[...]
            rows = pl.ds(pl.multiple_of(qi * T, T), T)
            dv = jnp.zeros((T, 128), F32)
            for j in range(2):
                sl = slice(128 * j, 128 * (j + 1))
                qj, kj = q_ref[:, sl], k_ref[:, sl]
                s = _mm_nt(kj, qj)
                if masked:
                    s = jnp.where(_chunk_mask_t(T), s, NEG)
                p = jnp.exp2(s - lse_ref[0, j:j + 1, :])
                doj = jnp.where(head == j, dov, 0.0).astype(MXU_DTYPE)
                dv = dv + _mm(p, doj)
                dp = _mm_nt(v_ref[...], doj)
                ds = (p * (dp - dl_ref[0, j:j + 1, :])).astype(MXU_DTYPE)
                dk_scr[:, sl] += _mm(ds, qj)
                dqj = _mm_tn(ds, kj) * MLA_SCALE

                @pl.when(ki == 0)
                def _():
                    dq_ref[rows, sl] = dqj

                @pl.when(ki > 0)
                def _():
                    dq_ref[rows, sl] += dqj
            dv_scr[...] += dv

        @pl.when(qi > ki)
        def _():
            step(False)

        @pl.when(qi == ki)
        def _():
            step(True)

        @pl.when(qi == n - 1)
        def _():
            dk_ref[...] = dk_scr[...] * 0.6931471805599453
            dv_ref[...] = dv_scr[...]

    qmap = lambda hp, t, qt, kt: (qt[t], hp)
    kmap = lambda hp, t, qt, kt: (kt[t], hp)
    stat = pl.BlockSpec((1, 2, T), lambda hp, t, qt, kt: (hp, 0, qt[t]))
    return pl.pallas_call(
        body, name="flash_bwd",
        grid_spec=pltpu.PrefetchScalarGridSpec(
            num_scalar_prefetch=2, grid=(MLA_H // 2, qt.shape[0]),
            in_specs=[pl.BlockSpec((T, 256), qmap), pl.BlockSpec((T, 256), kmap), pl.BlockSpec((T, 128), kmap),
                      stat, stat, pl.BlockSpec((T, 128), qmap)],
            out_specs=[pl.BlockSpec((S, 256), lambda hp, t, qt, kt: (0, hp)), pl.BlockSpec((T, 256), kmap),
                       pl.BlockSpec((T, 128), kmap)],
            scratch_shapes=[pltpu.VMEM((T, 256), F32), pltpu.VMEM((T, 128), F32)]),
        out_shape=[jax.ShapeDtypeStruct((S, 1024), F32), jax.ShapeDtypeStruct((S, 1024), F32),
                   jax.ShapeDtypeStruct((S, BW), F32)],
        compiler_params=_params(("parallel", "arbitrary"), 56),
    )(qt, kt, qh, kh, vh, lse, delta, do)


def _mla_prep_bwd(proj, dqf, dkf, dvf, gq, gkv, wuq_t, wuk_t, wuv_t, cos_t, sin_a, sin_b):
    S = proj.shape[0]
    tm = min(512, S)

    def body(p_ref, dq_ref, dk_ref, dv_ref, gq_ref, gkv_ref, wq_ref, wk_ref, wv_ref, c_ref, sa_ref, sb_ref,
             db_ref, dwq_ref, dwk_ref, dwv_ref, dgq_ref, dgkv_ref):
        _first_step_zero((dwq_ref, dwk_ref, dwv_ref, dgq_ref, dgkv_ref), pl.program_id(0) == 0)
        c, sa, sb = c_ref[...], sa_ref[...], sb_ref[...]
        nq, rq = _rms(p_ref[:, 0:Q_RANK])
        nkv, rkv = _rms(p_ref[:, Q_RANK:Q_RANK + KV_RANK])
        cq = (nq * gq_ref[...]).astype(MXU_DTYPE)
        ckv = (nkv * gkv_ref[...]).astype(MXU_DTYPE)
        dkr = jnp.zeros((tm, 128), F32)
        dq_pre = []
        for h in range(MLA_H):
            sl = slice(128 * h, 128 * (h + 1))
            dq_pre.append(_rope_t(dq_ref[:, sl], c, sa, sb).astype(MXU_DTYPE))
            dkr = dkr + dk_ref[:, sl]
        dq_pre = jnp.concatenate(dq_pre, axis=1)
        dcq = _mm(dq_pre, wq_ref[...])
        dwq_ref[...] += _mm_tn(dq_pre, cq)
        dgq_ref[...] += jnp.sum(dcq * nq, axis=0, keepdims=True)
        db_ref[:, 0:Q_RANK] = _rms_bwd(dcq * gq_ref[...], nq, rq).astype(db_ref.dtype)
        dk = dk_ref[...].astype(MXU_DTYPE)
        dv = dv_ref[...].astype(MXU_DTYPE)
        dckv = _mm(dk, wk_ref[...]) + _mm(dv, wv_ref[...])
        dwk_ref[...] += _mm_tn(dk, ckv)
        dwv_ref[...] += _mm_tn(dv, ckv)
        dgkv_ref[...] += jnp.sum(dckv * nkv, axis=0, keepdims=True)
        db_ref[:, Q_RANK:Q_RANK + KV_RANK] = _rms_bwd(dckv * gkv_ref[...], nkv, rkv).astype(db_ref.dtype)
        lane = lax.broadcasted_iota(jnp.int32, (1, 128), 1)
        rope_lanes = (lane >= MLA_NOPE) & (lane < MLA_QK)
        db_ref[:, Q_RANK + KV_RANK:BW] = _rope_t(jnp.where(rope_lanes, dkr, 0.0), c, sa, sb).astype(db_ref.dtype)

    full = lambda a: pl.BlockSpec(a.shape, lambda i: (0,) * a.ndim)
    fulls = lambda shape: pl.BlockSpec(shape, lambda i: (0,) * len(shape))
    tab = pl.BlockSpec((tm, 128), lambda i: (i, 0))
    row = lambda w, cb=0: pl.BlockSpec((tm, w), lambda i, cb=cb: (i, cb))
    return pl.pallas_call(
        body, name="mla_prep_bwd", grid=(S // tm,),
        in_specs=[row(BW, QD // BW), row(1024), row(1024), row(BW), full(gq), full(gkv), full(wuq_t),
                  full(wuk_t), full(wuv_t), tab, tab, tab],
        out_specs=[row(BW), fulls((1024, Q_RANK)), fulls((1024, KV_RANK)), fulls((BW, KV_RANK)),
                   fulls((1, Q_RANK)), fulls((1, KV_RANK))],
        out_shape=[jax.ShapeDtypeStruct((S, BW), MXU_DTYPE), jax.ShapeDtypeStruct((1024, Q_RANK), F32),
                   jax.ShapeDtypeStruct((1024, KV_RANK), F32), jax.ShapeDtypeStruct((BW, KV_RANK), F32),
                   jax.ShapeDtypeStruct((1, Q_RANK), F32), jax.ShapeDtypeStruct((1, KV_RANK), F32)],
        compiler_params=_params(("arbitrary",), 56),
    )(proj, dqf, dkf, dvf, gq, gkv, wuq_t, wuk_t, wuv_t, cos_t, sin_a, sin_b)


def _sgu_bwd(proj, dya, ln_g, ln_b, ws, bfull):
    S = proj.shape[0]
    tm = min(512, S)

    def body(u_ref, v_ref, z_ref, dy_ref, g_ref, b_ref, ws_ref, bf_ref,
             da_ref, dws_ref, dbf_ref, dlg_ref, dlb_ref, dbs_ref):
        i = pl.program_id(0)
        _first_step_zero((dws_ref, dbf_ref, dlg_ref, dlb_ref, dbs_ref), i == 0)
        tri = _sgu_tri()
        ws_m = [jnp.where(tri, ws_ref[g], 0.0).astype(MXU_DTYPE) for g in range(SGU_G)]
        grp = lax.broadcasted_iota(jnp.int32, (1, BW), 1) // (BW // SGU_G)
        for b in range(tm // SGU_T):
            r = slice(b * SGU_T, (b + 1) * SGU_T)
            vln, vhat, rstd = _layernorm(v_ref[r, :], g_ref[...], b_ref[...])
            vb = vln.astype(MXU_DTYPE)
            mixed = _sgu_mix(ws_m, vb, bf_ref[...], grp)
            u, z, dy = u_ref[r, :], z_ref[r, :], dy_ref[r, :]
            sg = _sigmoid(z)
            sz = z * sg
            da_ref[r, 0:BW] = (dy * mixed * sz).astype(da_ref.dtype)
            da_ref[r, 2 * BW:3 * BW] = (dy * u * mixed * (sg * (1.0 + z * (1.0 - sg)))).astype(da_ref.dtype)
            dmix = dy * u * sz
            dbf_ref[...] += dmix
            dvln = jnp.zeros((SGU_T, BW), F32)
            for g in range(SGU_G):
                dmg = jnp.where(grp == g, dmix, 0.0).astype(MXU_DTYPE)
                dvln = dvln + _mm_tn(ws_m[g], dmg)
                dws_ref[g] += jnp.where(tri, _mm_nt(dmg, vb), 0.0)
            dlg_ref[...] += jnp.sum(dvln * vhat, axis=0, keepdims=True)
            dlb_ref[...] += jnp.sum(dvln, axis=0, keepdims=True)
            dvh = dvln * g_ref[...]
            da_ref[r, BW:2 * BW] = (rstd * (dvh - jnp.mean(dvh, axis=-1, keepdims=True)
                                            - vhat * jnp.mean(dvh * vhat, axis=-1, keepdims=True))).astype(da_ref.dtype)

        @pl.when(i == pl.num_programs(0) - 1)
        def _():
            dbf = dbf_ref[...]
            for g in range(SGU_G):
                dbs_ref[:, g:g + 1] = jnp.sum(jnp.where(grp == g, dbf, 0.0), axis=1, keepdims=True)

    blk = lambda cb: pl.BlockSpec((tm, BW), lambda i, cb=cb: (i, cb))
    fulls = lambda shape: pl.BlockSpec(shape, lambda i: (0,) * len(shape))
    return pl.pallas_call(
        body, name="sgu_bwd", grid=(S // tm,),
        in_specs=[blk(UA // BW), blk(VA // BW), blk(ZA // BW), blk(0), fulls((1, BW)), fulls((1, BW)),
                  fulls((SGU_G, SGU_T, SGU_T)), fulls((SGU_T, BW))],
        out_specs=[pl.BlockSpec((tm, 3 * BW), lambda i: (i, 0)), fulls((SGU_G, SGU_T, SGU_T)),
                   fulls((SGU_T, BW)), fulls((1, BW)), fulls((1, BW)), fulls((SGU_T, SGU_G))],
        out_shape=[jax.ShapeDtypeStruct((S, 3 * BW), MXU_DTYPE), jax.ShapeDtypeStruct((SGU_G, SGU_T, SGU_T), F32),
                   jax.ShapeDtypeStruct((SGU_T, BW), F32), jax.ShapeDtypeStruct((1, BW), F32),
                   jax.ShapeDtypeStruct((1, BW), F32), jax.ShapeDtypeStruct((SGU_T, SGU_G), F32)],
        compiler_params=_params(("arbitrary",), 40),
    )(proj, proj, proj, dya, ln_g, ln_b, ws, bfull)


DX_TK = 3 * BW


def _inproj_bwd_dx(dgl, da, small, w_t, x, pre_g, dxo):
    S = x.shape[0]
    tm = min(512, S)
    nk = DP // DX_TK

    def body(g_ref, a_ref, s0, s1, s2, s3, s4, s5, w_ref, x_ref, pg_ref, dxo_ref, dx_ref, dg_ref, acc):
        i, k = pl.program_id(0), pl.program_id(1)
        _first_step_zero((dg_ref,), (i == 0) & (k == 0))

        @pl.when(k == 0)
        def _():
            acc[...] = jnp.zeros(acc.shape, F32)

        @pl.when(k < 2)
        def _():
            acc[...] += _mm(g_ref[...], w_ref[...])

        @pl.when(k == 2)
        def _():
            acc[...] += _mm(a_ref[...], w_ref[...])

        for kk, trio in ((3, (s0, s1, s2)), (4, (s3, s4, s5))):
            @pl.when(k == kk)
            def _(trio=trio):
                t = acc[...]
                for n, ref in enumerate(trio):
                    t = t + _mm(ref[...], w_ref[n * BW:(n + 1) * BW, :])
                acc[...] = t

        @pl.when(k == nk - 1)
        def _():
            n, r = _rms(x_ref[...])
            dxn = acc[...]
            dg_ref[...] += jnp.sum(dxn * n, axis=0, keepdims=True)
            dx_ref[...] = dxo_ref[...] + _rms_bwd(dxn * pg_ref[...], n, r)

    row = pl.BlockSpec((tm, D), lambda i, k: (i, 0))
    vec = pl.BlockSpec((1, D), lambda i, k: (0, 0))
    fixed = lambda w: pl.BlockSpec((tm, w), lambda i, k: (i, 0))
    return pl.pallas_call(
        body, name="inproj_bwd_dx", grid=(S // tm, nk),
        in_specs=[pl.BlockSpec((tm, DX_TK), lambda i, k: (i, jnp.minimum(k, 1))), fixed(DX_TK)]
                 + [fixed(BW)] * 6 + [pl.BlockSpec((DX_TK, D), lambda i, k: (k, 0)), row, vec, row],
        out_specs=[row, vec],
        out_shape=[jax.ShapeDtypeStruct((S, D), F32), jax.ShapeDtypeStruct((1, D), F32)],
        scratch_shapes=[pltpu.VMEM((tm, D), F32)],
        compiler_params=_params(("arbitrary", "arbitrary"), 56),
    )(dgl, da, *small, w_t, x, pre_g, dxo)


def _matmul_tn(a, b):
    S, W = a.shape
    K = b.shape[1]
    tk, tn = min(2048, S), BW

    def body(a_ref, b_ref, o_ref):
        _first_step_zero((o_ref,), pl.program_id(1) == 0)
        o_ref[...] += _mm_tn(a_ref[...], b_ref[...])

    return pl.pallas_call(
        body, name="matmul_tn", grid=(W // tn, S // tk),
        in_specs=[pl.BlockSpec((tk, tn), lambda j, k: (k, j)), pl.BlockSpec((tk, K), lambda j, k: (k, 0))],
        out_specs=pl.BlockSpec((tn, K), lambda j, k: (j, 0)),
        out_shape=jax.ShapeDtypeStruct((W, K), F32),
        compiler_params=_params(("parallel", "arbitrary"), 48),
    )(a, b)


def _pad_rows_w_in(wt):
    z = lambda n: jnp.zeros((n, wt.shape[1]), wt.dtype)
    return jnp.concatenate([wt[4512:], wt[:1920], z(64), wt[1920:1952], z(32), wt[1952:4512]], axis=0)


def _unpad_rows_w_in(dwt):
    a = dwt[3 * D:]
    return jnp.concatenate([a[:1920], a[1984:2016], a[2048:], dwt[:3 * D]], axis=0)


def _pad_head_rows(wt, width):
    k = wt.shape[1]
    return jnp.pad(wt.reshape(MLA_H, width, k), ((0, 0), (0, 128 - width), (0, 0))).reshape(MLA_H * 128, k)


def _unpad_head_rows(wt, width):
    k = wt.shape[1]
    return wt.reshape(MLA_H, 128, k)[:, :width].reshape(MLA_H * width, k)


def _rope_tables(S):
    half = MLA_ROPE // 2
    inv = 10000.0 ** (-jnp.arange(half, dtype=F32) / half)
    ang = jnp.arange(S, dtype=F32)[:, None] * inv[None, :]
    cos, sin = jnp.cos(ang), jnp.sin(ang)
    one = lambda n: jnp.ones((S, n), F32)
    zero = lambda n: jnp.zeros((S, n), F32)
    cos_t = jnp.concatenate([one(MLA_NOPE), cos, cos, one(128 - MLA_QK)], axis=1)
    sin_a = jnp.concatenate([zero(MLA_NOPE), -sin, zero(128 - MLA_NOPE - half)], axis=1)
    sin_b = jnp.concatenate([zero(MLA_NOPE + half), sin, zero(128 - MLA_QK)], axis=1)
    return cos_t, sin_a, sin_b


SHARDED = ("w_in", "mla_w_uq", "mla_w_ukv", "w_branch", "w_out")
SHARD_PACK = SHARDED + ("gate_b",)
REPLICATED = ("pre_g", "post_g", "sgu_ln_g", "sgu_ln_b", "sgu_w", "sgu_b", "mla_q_norm_g", "mla_kv_norm_g",
              "ca_rel_bias")
OUT_ORDER = ("w_in", "pre_g", "post_g", "sgu_ln_g", "sgu_ln_b", "sgu_w", "sgu_b", "mla_q_norm_g", "mla_kv_norm_g",
             "mla_w_uq", "mla_w_ukv", "ca_rel_bias", "w_branch", "gate_b", "w_out")
TRANSPOSED = {"w_in": (0, 2, 1), "mla_w_uq": (0, 2, 1), "mla_w_ukv": (0, 2, 1), "w_branch": (0, 1, 3, 2)}


def _canon(name, a):
    return jnp.transpose(a, TRANSPOSED[name]) if name in TRANSPOSED else a


def _to_rows(flat, width, mult):
    n = flat.shape[-1]
    pad = (-n) % (width * mult)
    flat = jnp.pad(flat, [(0, 0)] * (flat.ndim - 1) + [(0, pad)])
    return flat.reshape(flat.shape[:-1] + (-1, width))


def _gather_shards(shards, dtype):
    names = list(shards)
    flat = jnp.concatenate([shards[n].astype(dtype).reshape(-1) for n in names])
    got = _all_gather(_to_rows(flat, 128, 16)).reshape(N_DEV, -1)
    out, off = {}, 0
    for n in names:
        size = shards[n].size
        out[n] = got[:, off:off + size].reshape((N_DEV,) + shards[n].shape)
        off += size
    return out


def _pack(d, names):
    return _to_rows(jnp.concatenate([d[n].reshape(-1) for n in names]), 1024, 128)


def _unpack(p, like, names):
    flat, out, off = p.reshape(-1), {}, 0
    for n in names:
        out[n] = flat[off:off + like[n].size].reshape(like[n].shape)
        off += like[n].size
    return out


def kernel(x, w_in, pre_g, post_g, sgu_ln_g, sgu_ln_b, sgu_w, sgu_b, mla_q_norm_g, mla_kv_norm_g, mla_w_uq, mla_w_ukv, ca_rel_bias, w_branch, gate_b, w_out, loss_target, m_w_in, m_pre_g, m_post_g, m_sgu_ln_g, m_sgu_ln_b, m_sgu_w, m_sgu_b, m_mla_q_norm_g, m_mla_kv_norm_g, m_mla_w_uq, m_mla_w_ukv, m_ca_rel_bias, m_w_branch, m_gate_b, m_w_out, v_w_in, v_pre_g, v_post_g, v_sgu_ln_g, v_sgu_ln_b, v_sgu_w, v_sgu_b, v_mla_q_norm_g, v_mla_kv_norm_g, v_mla_w_uq, v_mla_w_ukv, v_ca_rel_bias, v_w_branch, v_gate_b, v_w_out):
    weights = dict(w_in=w_in, pre_g=pre_g, post_g=post_g, sgu_ln_g=sgu_ln_g, sgu_ln_b=sgu_ln_b, sgu_w=sgu_w,
                   sgu_b=sgu_b, mla_q_norm_g=mla_q_norm_g, mla_kv_norm_g=mla_kv_norm_g, mla_w_uq=mla_w_uq,
                   mla_w_ukv=mla_w_ukv, ca_rel_bias=ca_rel_bias, w_branch=w_branch, gate_b=gate_b, w_out=w_out)
    mom_m = dict(w_in=m_w_in, pre_g=m_pre_g, post_g=m_post_g, sgu_ln_g=m_sgu_ln_g, sgu_ln_b=m_sgu_ln_b,
                 sgu_w=m_sgu_w, sgu_b=m_sgu_b, mla_q_norm_g=m_mla_q_norm_g, mla_kv_norm_g=m_mla_kv_norm_g,
                 mla_w_uq=m_mla_w_uq, mla_w_ukv=m_mla_w_ukv, ca_rel_bias=m_ca_rel_bias, w_branch=m_w_branch,
                 gate_b=m_gate_b, w_out=m_w_out)
    mom_v = dict(w_in=v_w_in, pre_g=v_pre_g, post_g=v_post_g, sgu_ln_g=v_sgu_ln_g, sgu_ln_b=v_sgu_ln_b,
                 sgu_w=v_sgu_w, sgu_b=v_sgu_b, mla_q_norm_g=v_mla_q_norm_g, mla_kv_norm_g=v_mla_kv_norm_g,
                 mla_w_uq=v_mla_w_uq, mla_w_ukv=v_mla_w_ukv, ca_rel_bias=v_ca_rel_bias, w_branch=v_w_branch,
                 gate_b=v_gate_b, w_out=v_w_out)
    depth = w_in.shape[0]
    S = x.shape[1]
    xs = x.reshape(S, D)

    cw = {n: _canon(n, weights[n]) for n in SHARD_PACK}
    gw = _gather_shards({n: cw[n] for n in SHARDED}, MXU_DTYPE)
    gw.update(_gather_shards({"gate_b": gate_b}, F32))
    cos_t, sin_a, sin_b = _rope_tables(S)

    layers = []
    for l in range(depth):
        w_t = _pad_rows_w_in(gw["w_in"][:, l].reshape(D_IN, D))
        wuq_t = _pad_head_rows(gw["mla_w_uq"][:, l].reshape(MLA_H * MLA_QK, Q_RANK), MLA_QK)
        ukv_t = gw["mla_w_ukv"][:, l]
        wuk_t = jnp.pad(ukv_t[:, :MLA_NOPE], ((0, 0), (0, 128 - MLA_NOPE), (0, 0))).reshape(MLA_H * 128, KV_RANK)
        wuv_t = ukv_t[:, MLA_NOPE:].reshape(BW, KV_RANK)
        wbr_t = jnp.swapaxes(gw["w_branch"][:, l], 0, 1).reshape(N_BRANCH, D, BW)
        lw = dict(
            w_t=w_t, w_pad=w_t.T, wuq_t=wuq_t, wuq=wuq_t.T, wuk_t=wuk_t, wuk=wuk_t.T, wuv_t=wuv_t, wuv=wuv_t.T,
            wbr_t=wbr_t, wbr=jnp.swapaxes(wbr_t, 1, 2), wout=gw["w_out"][:, l].reshape(D, D),
            gate_b=jnp.swapaxes(gw["gate_b"][:, l], 0, 1).reshape(N_BRANCH, D),
            pre_g=pre_g[l][None], post_g=post_g[l][None], ln_g=sgu_ln_g[l][None], ln_b=sgu_ln_b[l][None],
            ws=sgu_w[l], bfull=jnp.repeat(sgu_b[l].T, BW // SGU_G, axis=1),
            gq=mla_q_norm_g[l][None], gkv=mla_kv_norm_g[l][None], bias=_bias_build(ca_rel_bias[l]))
        layers.append(lw)

    saved = []
    h_x = xs
    for lw in layers:
        proj, xn = _inproj_fwd(h_x, lw["pre_g"], lw["w_pad"])
        ya = _sgu_fwd(proj, lw["ln_g"], lw["ln_b"], lw["ws"], lw["bfull"])
        qh, kh, vh = _mla_prep_fwd(proj, lw["gq"], lw["gkv"], lw["wuq"], lw["wuk"], lw["wuv"], cos_t, sin_a, sin_b)
        ob, lse = _flash_fwd(qh, kh, vh)
        oc = _band_fwd(proj, lw["bias"])
        x_new, merged, hh = _merge_fwd(h_x, ya, ob, oc, proj, lw["gate_b"], lw["wbr"], lw["wout"], lw["post_g"])
        saved.append(dict(x=h_x, proj=proj, xn=xn, ya=ya, qh=qh, kh=kh, vh=vh, ob=ob, lse=lse, oc=oc,
                          merged=merged, h=hh))
        h_x = x_new

    loss_part, dx = _loss_fwd_bwd(h_x, loss_target.reshape(S, D))
    loss = lax.psum(loss_part[0, 0], MESH_AXES)

    rows = {n: [None] * depth for n in SHARD_PACK}
    rep = {n: [None] * depth for n in REPLICATED}
    for l in reversed(range(depth)):
        lw, sv = layers[l], saved[l]
        proj = sv["proj"]
        dmerged, dw_out, dg_post = _out_bwd(dx, sv["h"], sv["merged"], lw["post_g"], lw["wout"].T)
        dgl, dzb, dzc, dya, dob, doc, dwbr_t, dgb = _gate_bwd(
            dmerged, sv["ya"], sv["ob"], sv["oc"], proj, lw["gate_b"], lw["wbr"], lw["wbr_t"])
        dqc, dkc, dvc, dbias = _band_bwd(proj, lw["bias"], doc)
        drel = _bias_fold(dbias)[:, :2 * REL_CLIP + 1]
        dqf, dkf, dvf = _flash_bwd(sv["qh"], sv["kh"], sv["vh"], sv["lse"], _attn_delta(sv["ob"], dob), dob)
        db, dwuq_t, dwuk_t, dwuv_t, dgq, dgkv = _mla_prep_bwd(
            proj, dqf, dkf, dvf, lw["gq"], lw["gkv"], lw["wuq_t"], lw["wuk_t"], lw["wuv_t"], cos_t, sin_a, sin_b)
        da, dws, _, dlg, dlb, dbs = _sgu_bwd(proj, dya, lw["ln_g"], lw["ln_b"], lw["ws"], lw["bfull"])
        small = (db, dzb, dqc, dkc, dvc, dzc)
        dx, dg_pre = _inproj_bwd_dx(dgl, da, small, lw["w_t"], sv["x"], lw["pre_g"], dx)
        dw_t = jnp.concatenate([_matmul_tn(a, sv["xn"]) for a in (dgl, da) + small], axis=0)

        rows["w_in"][l] = _unpad_rows_w_in(dw_t).reshape(N_DEV, -1)
        rows["mla_w_uq"][l] = _unpad_head_rows(dwuq_t, MLA_QK).reshape(N_DEV, -1)
        dk3 = dwuk_t.reshape(MLA_H, 128, KV_RANK)[:, :MLA_NOPE]
        dv3 = dwuv_t.reshape(MLA_H, 64, KV_RANK)
        rows["mla_w_ukv"][l] = jnp.concatenate([dk3, dv3], axis=1).reshape(N_DEV, -1)
        rows["w_branch"][l] = jnp.swapaxes(dwbr_t.reshape(N_BRANCH, N_DEV, D // N_DEV, BW), 0, 1).reshape(N_DEV, -1)
        rows["w_out"][l] = dw_out.reshape(N_DEV, -1)
        rows["gate_b"][l] = jnp.swapaxes(dgb.reshape(N_BRANCH, N_DEV, D // N_DEV), 0, 1).reshape(N_DEV, -1)
        rep["pre_g"][l] = dg_pre[0]
        rep["post_g"][l] = dg_post[0]
        rep["sgu_ln_g"][l] = dlg[0]
        rep["sgu_ln_b"][l] = dlb[0]
        rep["sgu_w"][l] = dws
        rep["sgu_b"][l] = dbs.T
        rep["mla_q_norm_g"][l] = dgq[0]
        rep["mla_kv_norm_g"][l] = dgkv[0]
        rep["ca_rel_bias"][l] = drel
    grad_x = dx.reshape(x.shape)

    send = jnp.concatenate([jnp.stack(rows[n], axis=1).reshape(N_DEV, -1) for n in SHARD_PACK], axis=1)
    send = _to_rows(send.astype(MXU_DTYPE), 1024, 128)
    R = send.shape[1]
    recv = _exchange(send.reshape(N_DEV, R * 8, 128)).reshape(N_DEV, R, 1024)
    cm = {n: _canon(n, mom_m[n]) for n in SHARD_PACK}
    cv = {n: _canon(n, mom_v[n]) for n in SHARD_PACK}
    outs_sh = _reduce_adamw(recv, _pack(cw, SHARD_PACK), _pack(cm, SHARD_PACK), _pack(cv, SHARD_PACK))
    outs_sh = [{n: _canon(n, a) for n, a in _unpack(p, cw, SHARD_PACK).items()} for p in outs_sh]

    part = _pack({n: jnp.stack(rep[n]) for n in REPLICATED}, REPLICATED)
    Rs = part.shape[0]
    allp = _all_gather(part.reshape(Rs * 8, 128)).reshape(N_DEV, Rs, 1024)
    outs_rep = _reduce_adamw(allp, _pack(weights, REPLICATED), _pack(mom_m, REPLICATED), _pack(mom_v, REPLICATED))
    outs_rep = [_unpack(p, weights, REPLICATED) for p in outs_rep]

    outs = [{**a, **b} for a, b in zip(outs_sh, outs_rep)]
    return (loss, grad_x, *[o[n] for o in outs for n in OUT_ORDER])
```

```python
import functools

import jax
import jax.numpy as jnp
from jax import lax
from jax.experimental import pallas as pl
from jax.experimental.pallas import tpu as pltpu

F32 = jnp.float32
MXU_DTYPE = jnp.bfloat16
EPS = 1e-6
NEG = -1e30
MESH_AXES = ("x", "y", "c")
N_DEV = 8

D = 1024
BW = 512
N_BRANCH = 3
CHUNK = 64
SGU_T = 128
SGU_G = 8
MLA_H = 8
MLA_NOPE = 64
MLA_ROPE = 32
MLA_QK = MLA_NOPE + MLA_ROPE
Q_RANK = 256
KV_RANK = 128
CA_H = 8
CA_DH = 64
LEFT_CHUNKS = 8
REL_CLIP = 128
D_IN = 7584

G_OFF, UA, VA, ZA, QD, ZB, QC, KC, VC, ZC, DP = 0, 3072, 3584, 4096, 4608, 5120, 5632, 6144, 6656, 7168, 7680
BAND_TQ = 256
BAND_W = 3 * BAND_TQ

ADAM_LR, ADAM_B1, ADAM_B2, ADAM_EPS, ADAM_WD, ADAM_STEP = 0.001, 0.9, 0.999, 1e-08, 0.01, 10


def _params(sem, mib):
    return pltpu.CompilerParams(dimension_semantics=sem, vmem_limit_bytes=mib << 20)


def _mm(a, b):
    return jnp.dot(a.astype(MXU_DTYPE), b.astype(MXU_DTYPE), preferred_element_type=F32)


def _mm_nt(a, b):
    return lax.dot_general(a.astype(MXU_DTYPE), b.astype(MXU_DTYPE), (((1,), (1,)), ((), ())),
                           preferred_element_type=F32)


def _mm_tn(a, b):
    return lax.dot_general(a.astype(MXU_DTYPE), b.astype(MXU_DTYPE), (((0,), (0,)), ((), ())),
                           preferred_element_type=F32)


def _sigmoid(z):
    return 1.0 / (1.0 + jnp.exp(-z))


def _rms(x):
    r = lax.rsqrt(jnp.mean(x * x, axis=-1, keepdims=True) + EPS)
    return x * r, r


def _rms_bwd(dn, n, r):
    return r * (dn - n * jnp.mean(dn * n, axis=-1, keepdims=True))


def _rope(b, c, sa, sb):
    return b * c + pltpu.roll(b, 112, 1) * sa + pltpu.roll(b, 16, 1) * sb


def _rope_t(d, c, sa, sb):
    return d * c + pltpu.roll(d * sa, 16, 1) + pltpu.roll(d * sb, 112, 1)


def _all_gather(blk):
    R = blk.shape[0]

    def body(x_ref, out_ref, send_sems, recv_sems, local_sem):
        x, y, c = lax.axis_index("x"), lax.axis_index("y"), lax.axis_index("c")
        me, sibling = (x, y, c), (x, y, 1 - c)
        chips = [(1 - x, y), (x, 1 - y), (1 - x, 1 - y)]

        def slot(px, py, pc):
            return out_ref.at[4 * px + 2 * py + pc]

        def copy(k, block, to, src=None):
            return pltpu.make_async_remote_copy(
                src_ref=slot(*block) if src is None else src, dst_ref=slot(*block),
                send_sem=send_sems.at[k], recv_sem=recv_sems.at[k],
                device_id=to, device_id_type=pl.DeviceIdType.MESH)

        mine = pltpu.make_async_copy(x_ref, slot(*me), local_sem)
        mine.start()
        first = [copy(0, me, sibling, src=x_ref)]
        first += [copy(1 + j, me, (*chip, c), src=x_ref) for j, chip in enumerate(chips)]
        for cp in first:
            cp.start()
        passed = [copy(4 + j, (*chip, c), sibling) for j, chip in enumerate(chips)]
        for j, chip in enumerate(chips):
            copy(1 + j, (*chip, c), me).wait_recv()
            passed[j].start()
        copy(0, sibling, me).wait_recv()
        for j, chip in enumerate(chips):
            copy(4 + j, (*chip, 1 - c), me).wait_recv()
        for cp in first + passed:
            cp.wait_send()
        mine.wait()

    return pl.pallas_call(
        body, name="all_gather",
        out_shape=jax.ShapeDtypeStruct((N_DEV,) + blk.shape, blk.dtype),
        in_specs=[pl.BlockSpec(memory_space=pl.ANY)],
        out_specs=pl.BlockSpec(memory_space=pl.ANY),
        scratch_shapes=[pltpu.SemaphoreType.DMA((7,)), pltpu.SemaphoreType.DMA((7,)), pltpu.SemaphoreType.DMA(())],
    )(blk)


def _exchange(send):
    R = send.shape[1]

    def body(s_ref, r_ref, send_sems, recv_sems, local_sem):
        x, y, c = lax.axis_index("x"), lax.axis_index("y"), lax.axis_index("c")
        me = 4 * x + 2 * y + c
        local = pltpu.make_async_copy(s_ref.at[me], r_ref.at[me], local_sem)
        local.start()

        def peer(k):
            px = 1 - x if (k >> 2) & 1 else x
            py = 1 - y if (k >> 1) & 1 else y
            pc = 1 - c if k & 1 else c
            return (px, py, pc), 4 * px + 2 * py + pc

        copies = []
        for k in range(1, N_DEV):
            pos, pid = peer(k)
            cp = pltpu.make_async_remote_copy(
                src_ref=s_ref.at[pid], dst_ref=r_ref.at[me],
                send_sem=send_sems.at[k], recv_sem=recv_sems.at[k],
                device_id=pos, device_id_type=pl.DeviceIdType.MESH)
            cp.start()
            copies.append(cp)
        for k in range(1, N_DEV):
            pos, pid = peer(k)
            pltpu.make_async_remote_copy(
                src_ref=s_ref.at[pid], dst_ref=r_ref.at[pid],
                send_sem=send_sems.at[k], recv_sem=recv_sems.at[k],
                device_id=pos, device_id_type=pl.DeviceIdType.MESH).wait_recv()
        for cp in copies:
            cp.wait_send()
        local.wait()

    return pl.pallas_call(
        body, name="grad_exchange",
        out_shape=jax.ShapeDtypeStruct(send.shape, send.dtype),
        in_specs=[pl.BlockSpec(memory_space=pl.ANY)],
        out_specs=pl.BlockSpec(memory_space=pl.ANY),
        scratch_shapes=[pltpu.SemaphoreType.DMA((N_DEV,)), pltpu.SemaphoreType.DMA((N_DEV,)),
                        pltpu.SemaphoreType.DMA(())],
    )(send)


def _reduce_adamw(recv, w, m, v):
    R = w.shape[0]
    tr = 128 if R % 128 == 0 else 8
    c1 = 1.0 - ADAM_B1 ** ADAM_STEP
    c2 = 1.0 - ADAM_B2 ** ADAM_STEP

    def body(r_ref, w_ref, m_ref, v_ref, g_ref, d_ref, nm_ref, nv_ref):
        g = r_ref[0].astype(F32)
        for s in range(1, N_DEV):
            g = g + r_ref[s].astype(F32)
        m2 = ADAM_B1 * m_ref[...] + (1.0 - ADAM_B1) * g
        v2 = ADAM_B2 * v_ref[...] + (1.0 - ADAM_B2) * (g * g)
        m_hat = m2 / c1
        v_hat = v2 / c2
        g_ref[...] = g
        d_ref[...] = -ADAM_LR * (m_hat / (jnp.sqrt(v_hat) + ADAM_EPS) + ADAM_WD * w_ref[...])
        nm_ref[...] = m2
        nv_ref[...] = v2

    row = pl.BlockSpec((tr, 1024), lambda i: (i, 0))
    return pl.pallas_call(
        body, name="reduce_adamw", grid=(R // tr,),
        in_specs=[pl.BlockSpec((N_DEV, tr, 1024), lambda i: (0, i, 0)), row, row, row],
        out_specs=[row, row, row, row],
        out_shape=[jax.ShapeDtypeStruct((R, 1024), F32)] * 4,
        compiler_params=_params(("parallel",), 40),
    )(recv, w, m, v)


def _inproj_fwd(x, pre_g, w_pad):
    S = x.shape[0]
    tm, tn = min(512, S), 1536

    def body(x_ref, g_ref, w_ref, proj_ref, xn_ref):
        @pl.when(pl.program_id(1) == 0)
        def _():
            n, _ = _rms(x_ref[...])
            xn_ref[...] = (n * g_ref[...]).astype(xn_ref.dtype)
        proj_ref[...] = jnp.dot(xn_ref[...], w_ref[...], preferred_element_type=F32)

    return pl.pallas_call(
        body, name="inproj_fwd", grid=(S // tm, DP // tn),
        in_specs=[pl.BlockSpec((tm, D), lambda i, j: (i, 0)), pl.BlockSpec((1, D), lambda i, j: (0, 0)),
                  pl.BlockSpec((D, tn), lambda i, j: (0, j))],
        out_specs=[pl.BlockSpec((tm, tn), lambda i, j: (i, j)), pl.BlockSpec((tm, D), lambda i, j: (i, 0))],
        out_shape=[jax.ShapeDtypeStruct((S, DP), F32), jax.ShapeDtypeStruct((S, D), MXU_DTYPE)],
        compiler_params=_params(("parallel", "arbitrary"), 48),
    )(x, pre_g, w_pad)


def _sgu_tri():
    return lax.broadcasted_iota(jnp.int32, (SGU_T, SGU_T), 0) >= lax.broadcasted_iota(jnp.int32, (SGU_T, SGU_T), 1)


def _sgu_mix(ws_m, vb, bfull, grp):
    mixed = bfull
    for g in range(SGU_G):
        mixed = mixed + jnp.where(grp == g, _mm(ws_m[g], vb), 0.0)
    return mixed


def _layernorm(v, g, b):
    xc = v - jnp.mean(v, axis=-1, keepdims=True)
    rstd = lax.rsqrt(jnp.mean(xc * xc, axis=-1, keepdims=True) + EPS)
    vhat = xc * rstd
    return vhat * g + b, vhat, rstd


def _sgu_fwd(proj, ln_g, ln_b, ws, bfull):
    S = proj.shape[0]
    tm = min(512, S)

    def body(u_ref, v_ref, z_ref, g_ref, b_ref, ws_ref, bf_ref, ya_ref):
        tri = _sgu_tri()
        ws_m = [jnp.where(tri, ws_ref[g], 0.0).astype(MXU_DTYPE) for g in range(SGU_G)]
        grp = lax.broadcasted_iota(jnp.int32, (1, BW), 1) // (BW // SGU_G)
        for b in range(tm // SGU_T):
            r = slice(b * SGU_T, (b + 1) * SGU_T)
            vln, _, _ = _layernorm(v_ref[r, :], g_ref[...], b_ref[...])
            mixed = _sgu_mix(ws_m, vln.astype(MXU_DTYPE), bf_ref[...], grp)
            z = z_ref[r, :]
            ya_ref[r, :] = (u_ref[r, :] * mixed * (z * _sigmoid(z))).astype(ya_ref.dtype)

    blk = lambda cb: pl.BlockSpec((tm, BW), lambda i, cb=cb: (i, cb))
    vec = pl.BlockSpec((1, BW), lambda i: (0, 0))
    return pl.pallas_call(
        body, name="sgu_fwd", grid=(S // tm,),
        in_specs=[blk(UA // BW), blk(VA // BW), blk(ZA // BW), vec, vec,
                  pl.BlockSpec((SGU_G, SGU_T, SGU_T), lambda i: (0, 0, 0)),
                  pl.BlockSpec((SGU_T, BW), lambda i: (0, 0))],
        out_specs=pl.BlockSpec((tm, BW), lambda i: (i, 0)),
        out_shape=jax.ShapeDtypeStruct((S, BW), MXU_DTYPE),
        compiler_params=_params(("parallel",), 32),
    )(proj, proj, proj, ln_g, ln_b, ws, bfull)


def _mla_prep_fwd(proj, gq, gkv, wuq, wuk, wuv, cos_t, sin_a, sin_b):
    S = proj.shape[0]
    tm = min(512, S)

    def body(p_ref, gq_ref, gkv_ref, wuq_ref, wuk_ref, wuv_ref, c_ref, sa_ref, sb_ref,
             q_ref, k_ref, v_ref, kt_ref, vt_ref):
        nq, _ = _rms(p_ref[:, 0:Q_RANK])
        nkv, _ = _rms(p_ref[:, Q_RANK:Q_RANK + KV_RANK])
        cq = (nq * gq_ref[...]).astype(MXU_DTYPE)
        ckv = (nkv * gkv_ref[...]).astype(MXU_DTYPE)
        qf = _mm(cq, wuq_ref[...])
        kf = _mm(ckv, wuk_ref[...])
        v = _mm(ckv, wuv_ref[...])
        v_ref[...] = v.astype(v_ref.dtype)
        vt_ref[...] = jnp.transpose(v).astype(vt_ref.dtype)
        c, sa, sb = c_ref[...], sa_ref[...], sb_ref[...]
        krr = _rope(p_ref[:, Q_RANK + KV_RANK:BW], c, sa, sb)
        for h in range(MLA_H):
            sl = slice(128 * h, 128 * (h + 1))
            q_ref[:, sl] = (_rope(qf[:, sl], c, sa, sb) * MLA_SCALE_LOG2).astype(q_ref.dtype)
            kh = kf[:, sl] + krr
            k_ref[:, sl] = kh.astype(k_ref.dtype)
            kt_ref[sl, :] = jnp.transpose(kh).astype(kt_ref.dtype)

    full = lambda a: pl.BlockSpec(a.shape, lambda i: (0,) * a.ndim)
    tab = pl.BlockSpec((tm, 128), lambda i: (i, 0))
    return pl.pallas_call(
        body, name="mla_prep_fwd", grid=(S // tm,),
        in_specs=[pl.BlockSpec((tm, BW), lambda i: (i, QD // BW)), full(gq), full(gkv), full(wuq), full(wuk),
                  full(wuv), tab, tab, tab],
        out_specs=[pl.BlockSpec((tm, 1024), lambda i: (i, 0)), pl.BlockSpec((tm, 1024), lambda i: (i, 0)),
                   pl.BlockSpec((tm, BW), lambda i: (i, 0)), pl.BlockSpec((1024, tm), lambda i: (0, i)),
                   pl.BlockSpec((BW, tm), lambda i: (0, i))],
        out_shape=[jax.ShapeDtypeStruct((S, 1024), MXU_DTYPE), jax.ShapeDtypeStruct((S, 1024), MXU_DTYPE),
                   jax.ShapeDtypeStruct((S, BW), MXU_DTYPE), jax.ShapeDtypeStruct((1024, S), MXU_DTYPE),
                   jax.ShapeDtypeStruct((BW, S), MXU_DTYPE)],
        compiler_params=_params(("parallel",), 48),
    )(proj, gq, gkv, wuq, wuk, wuv, cos_t, sin_a, sin_b)


MLA_SCALE = MLA_QK ** -0.5
MLA_SCALE_LOG2 = MLA_SCALE * 1.4426950408889634


def _tri_tables(n, q_major):
    if q_major:
        pairs = [(qi, ki) for qi in range(n) for ki in range(qi + 1)]
    else:
        pairs = [(qi, ki) for ki in range(n) for qi in range(ki, n)]
    return (jnp.asarray([p[0] for p in pairs], jnp.int32), jnp.asarray([p[1] for p in pairs], jnp.int32))


def _chunk_mask_t(T):
    kc = lax.broadcasted_iota(jnp.int32, (T, T), 0) >> 6
    qc = lax.broadcasted_iota(jnp.int32, (T, T), 1) >> 6
    return kc <= qc


FLASH_SUB = 512


def _col_reduce(op, reduce_fn, x):
    r = x.shape[0]
    while r > 8 and r % 16 == 0:
        r //= 2
        x = op(x[:r], x[r:])
    return reduce_fn(x, axis=0, keepdims=True)


def _flash_fwd(qh, kh, vt):
    S = qh.shape[0]
    T = min(512, S)
    TS = min(FLASH_SUB, T)
    n = S // T
    qt, kt = _tri_tables(n, q_major=True)

    def body(qt_ref, kt_ref, q_ref, k_ref, vt_ref, o_ref, lse_ref, m_scr, l_scr, acc_scr):
        t = pl.program_id(1)
        qi, ki = qt_ref[t], kt_ref[t]

        @pl.when(ki == 0)
        def _():
            m_scr[...] = jnp.full(m_scr.shape, NEG, F32)
            l_scr[...] = jnp.zeros(l_scr.shape, F32)
            acc_scr[...] = jnp.zeros(acc_scr.shape, F32)

        def step(diag):
            for j in range(2):
                sl = slice(128 * j, 128 * (j + 1))
                for qs in range(T // TS):
                    cq = slice(qs * TS, (qs + 1) * TS)
                    qsub = q_ref[cq, sl]
                    for ks in range(T // TS):
                        if diag and ks > qs:
                            continue
                        rk = slice(ks * TS, (ks + 1) * TS)
                        s = _mm_nt(k_ref[rk, sl], qsub)
                        if diag and ks == qs:
                            s = jnp.where(_chunk_mask_t(TS), s, NEG)
                        m_old = m_scr[j, :, cq]
                        m_new = jnp.maximum(m_old, _col_reduce(jnp.maximum, jnp.max, s))
                        alpha = jnp.exp2(m_old - m_new)
                        p = jnp.exp2(s - m_new)
                        l_scr[j, :, cq] = alpha * l_scr[j, :, cq] + _col_reduce(jnp.add, jnp.sum, p)
                        acc_scr[j, :, cq] = alpha * acc_scr[j, :, cq] + _mm(vt_ref[:, rk], p)
                        m_scr[j, :, cq] = m_new

        @pl.when(ki < qi)
        def _():
            step(False)

        @pl.when(ki == qi)
        def _():
            step(True)
            first = lax.broadcasted_iota(jnp.int32, (128, 1), 0) < 64
            o_ref[...] = jnp.transpose(jnp.where(first, acc_scr[0] / l_scr[0], acc_scr[1] / l_scr[1]))
            for j in range(2):
                lse_ref[0, j:j + 1, :] = m_scr[j] + jnp.log2(l_scr[j])

    qmap = lambda hp, t, qt, kt: (qt[t], hp)
    kmap = lambda hp, t, qt, kt: (kt[t], hp)
    return pl.pallas_call(
        body, name="flash_fwd",
        grid_spec=pltpu.PrefetchScalarGridSpec(
            num_scalar_prefetch=2, grid=(MLA_H // 2, qt.shape[0]),
            in_specs=[pl.BlockSpec((T, 256), qmap), pl.BlockSpec((T, 256), kmap),
                      pl.BlockSpec((128, T), lambda hp, t, qt, kt: (hp, kt[t]))],
            out_specs=[pl.BlockSpec((T, 128), qmap),
                       pl.BlockSpec((1, 2, T), lambda hp, t, qt, kt: (hp, 0, qt[t]))],
            scratch_shapes=[pltpu.VMEM((2, 1, T), F32), pltpu.VMEM((2, 1, T), F32),
                            pltpu.VMEM((2, 128, T), F32)]),
        out_shape=[jax.ShapeDtypeStruct((S, BW), F32), jax.ShapeDtypeStruct((MLA_H // 2, 2, S), F32)],
        compiler_params=_params(("parallel", "arbitrary"), 40),
    )(qt, kt, qh, kh, vt)


def _attn_delta(o, do):
    S = o.shape[0]
    T = min(512, S)

    def body(o_ref, do_ref, d_ref):
        head = lax.broadcasted_iota(jnp.int32, (1, 128), 1) // 64
        prod = o_ref[...] * do_ref[...]
        for j in range(2):
            d_ref[0, j:j + 1, :] = jnp.sum(jnp.transpose(jnp.where(head == j, prod, 0.0)), axis=0, keepdims=True)

    blk = pl.BlockSpec((T, 128), lambda hp, i: (i, hp))
    return pl.pallas_call(
        body, name="attn_delta", grid=(MLA_H // 2, S // T),
        in_specs=[blk, blk],
        out_specs=pl.BlockSpec((1, 2, T), lambda hp, i: (hp, 0, i)),
        out_shape=jax.ShapeDtypeStruct((MLA_H // 2, 2, S), F32),
        compiler_params=_params(("parallel", "parallel"), 32),
    )(o, do)


def _band_specs(S):
    q = pl.BlockSpec((BAND_TQ, 128), lambda hp, qi: (qi, QC // 128 + hp))
    ks = [pl.BlockSpec((BAND_TQ, 128), lambda hp, qi, t=t: (jnp.maximum(qi - 2 + t, 0), KC // 128 + hp))
          for t in range(3)]
    vs = [pl.BlockSpec((BAND_TQ, 128), lambda hp, qi, t=t: (jnp.maximum(qi - 2 + t, 0), VC // 128 + hp))
          for t in range(3)]
    bias = pl.BlockSpec((2, BAND_TQ, BAND_W), lambda hp, qi: (hp, 0, 0))
    return q, ks, vs, bias


def _band_probs(qj, kcat, bias_j, valid, scale):
    s = _mm_nt(qj, kcat) * scale + bias_j
    s = jnp.where(valid, s, NEG)
    p = jnp.exp(s - jnp.max(s, axis=1, keepdims=True))
    return p / jnp.sum(p, axis=1, keepdims=True)


def _band_valid(qi):
    tile = lax.broadcasted_iota(jnp.int32, (1, BAND_W), 1) // BAND_TQ
    return tile + qi >= 2


def _band_fwd(proj, bias):
    S = proj.shape[0]
    scale = CA_DH ** -0.5

    def body(q_ref, k0, k1, k2, v0, v1, v2, b_ref, o_ref):
        qi = pl.program_id(1)
        head = lax.broadcasted_iota(jnp.int32, (1, 128), 1) // 64
        kcat = jnp.concatenate([k0[...], k1[...], k2[...]], axis=0).astype(MXU_DTYPE)
        vcat = jnp.concatenate([v0[...], v1[...], v2[...]], axis=0).astype(MXU_DTYPE)
        valid = _band_valid(qi)
        q = q_ref[...]
        o = jnp.zeros((BAND_TQ, 128), F32)
        for j in range(2):
            pn = _band_probs(jnp.where(head == j, q, 0.0), kcat, b_ref[j], valid, scale)
            o = o + _mm(pn, jnp.where(head == j, vcat, 0))
        o_ref[...] = o

    q, ks, vs, bspec = _band_specs(S)
    return pl.pallas_call(
        body, name="band_fwd", grid=(CA_H // 2, S // BAND_TQ),
        in_specs=[q, *ks, *vs, bspec],
        out_specs=pl.BlockSpec((BAND_TQ, 128), lambda hp, qi: (qi, hp)),
        out_shape=jax.ShapeDtypeStruct((S, BW), F32),
        compiler_params=_params(("parallel", "arbitrary"), 40),
    )(proj, proj, proj, proj, proj, proj, proj, bias)


def _merge_fwd(x, ya, ob, oc, proj, gate_b, wbr, wout, post_g):
    S = x.shape[0]
    tm = min(256, S)

    def body(x_ref, ya_ref, ob_ref, oc_ref, zb_ref, zc_ref, g0, g1, g2, gb_ref, wbr_ref, wo_ref, pg_ref,
             xo_ref, mg_ref, h_ref):
        zb, zc = zb_ref[...], zc_ref[...]
        ys = [ya_ref[...], ob_ref[...] * (zb * _sigmoid(zb)), oc_ref[...] * (zc * _sigmoid(zc))]
        merged = jnp.zeros((tm, D), F32)
        for i, g_ref in enumerate((g0, g1, g2)):
            merged = merged + _sigmoid(g_ref[...] + gb_ref[i:i + 1, :]) * _mm(ys[i], wbr_ref[i])
        mg_ref[...] = merged.astype(mg_ref.dtype)
        h = _mm(merged, wo_ref[...])
        h_ref[...] = h
        n, _ = _rms(h)
        xo_ref[...] = x_ref[...] + n * pg_ref[...]

    row = lambda w, cb=0: pl.BlockSpec((tm, w), lambda i, cb=cb: (i, cb))
    full = lambda a: pl.BlockSpec(a.shape, lambda i: (0,) * a.ndim)
    return pl.pallas_call(
        body, name="merge_fwd", grid=(S // tm,),
        in_specs=[row(D), row(BW), row(BW), row(BW), row(BW, ZB // BW), row(BW, ZC // BW),
                  row(D, 0), row(D, 1), row(D, 2), full(gate_b), full(wbr), full(wout), full(post_g)],
        out_specs=[row(D), row(D), row(D)],
        out_shape=[jax.ShapeDtypeStruct((S, D), F32), jax.ShapeDtypeStruct((S, D), MXU_DTYPE),
                   jax.ShapeDtypeStruct((S, D), F32)],
        compiler_params=_params(("parallel",), 56),
    )(x, ya, ob, oc, proj, proj, proj, proj, proj, gate_b, wbr, wout, post_g)


def _loss_fwd_bwd(y, target):
    S = y.shape[0]
    tm = min(512, S)

    def body(y_ref, t_ref, loss_ref, dy_ref):
        @pl.when(pl.program_id(0) == 0)
        def _():
            loss_ref[...] = jnp.zeros((1, 1), F32)
        err = y_ref[...] - t_ref[...]
        loss_ref[...] += 0.5 * jnp.sum(jnp.mean(err * err, axis=-1, keepdims=True), axis=0, keepdims=True)
        dy_ref[...] = err * (1.0 / D)

    row = pl.BlockSpec((tm, D), lambda i: (i, 0))
    return pl.pallas_call(
        body, name="loss", grid=(S // tm,),
        in_specs=[row, row],
        out_specs=[pl.BlockSpec((1, 1), lambda i: (0, 0)), row],
        out_shape=[jax.ShapeDtypeStruct((1, 1), F32), jax.ShapeDtypeStruct((S, D), F32)],
        compiler_params=_params(("arbitrary",), 32),
    )(y, target)


def _first_step_zero(refs, first):
    @pl.when(first)
    def _():
        for r in refs:
            r[...] = jnp.zeros(r.shape, r.dtype)


def _out_bwd(dxo, h, merged, post_g, wout_t):
    S = dxo.shape[0]
    tm = min(256, S)

    def body(d_ref, h_ref, mg_ref, pg_ref, wt_ref, dm_ref, dw_ref, dg_ref):
        _first_step_zero((dw_ref, dg_ref), pl.program_id(0) == 0)
        d = d_ref[...]
        hn, r = _rms(h_ref[...])
        dg_ref[...] += jnp.sum(d * hn, axis=0, keepdims=True)
        dh = _rms_bwd(d * pg_ref[...], hn, r)
        dm_ref[...] = _mm(dh, wt_ref[...])
        dw_ref[...] += _mm_tn(mg_ref[...], dh)

    row = pl.BlockSpec((tm, D), lambda i: (i, 0))
    full = lambda shape: pl.BlockSpec(shape, lambda i: (0,) * len(shape))
    return pl.pallas_call(
        body, name="out_bwd", grid=(S // tm,),
        in_specs=[row, row, row, full((1, D)), full((D, D))],
        out_specs=[row, full((D, D)), full((1, D))],
        out_shape=[jax.ShapeDtypeStruct((S, D), F32), jax.ShapeDtypeStruct((D, D), F32),
                   jax.ShapeDtypeStruct((1, D), F32)],
        compiler_params=_params(("arbitrary",), 40),
    )(dxo, h, merged, post_g, wout_t)


def _gate_bwd(dm, ya, ob, oc, proj, gate_b, wbr, wbr_t):
    S = dm.shape[0]
    tm = min(128, S)

    def body(dm_ref, ya_ref, ob_ref, oc_ref, zb_ref, zc_ref, g0, g1, g2, gb_ref, wbr_ref, wbt_ref,
             dg_ref, dzb_ref, dzc_ref, dya_ref, dob_ref, doc_ref, dwbr_ref, dgb_ref):
        _first_step_zero((dwbr_ref, dgb_ref), pl.program_id(0) == 0)
        dmv = dm_ref[...]
        zb, zc = zb_ref[...], zc_ref[...]
        sgb, sgc = _sigmoid(zb), _sigmoid(zc)
        ob, oc = ob_ref[...], oc_ref[...]
        ys = [ya_ref[...], (ob * (zb * sgb)).astype(MXU_DTYPE), (oc * (zc * sgc)).astype(MXU_DTYPE)]
        dys = []
        for i, g_ref in enumerate((g0, g1, g2)):
            br = _mm(ys[i], wbr_ref[i])
            gate = _sigmoid(g_ref[...] + gb_ref[i:i + 1, :])
            dgl = dmv * br * (gate * (1.0 - gate))
            dg_ref[:, D * i:D * (i + 1)] = dgl.astype(dg_ref.dtype)
            dgb_ref[i:i + 1, :] += jnp.sum(dgl, axis=0, keepdims=True)
            dbr = dmv * gate
            dys.append(_mm(dbr, wbt_ref[i]))
            dwbr_ref[i] += _mm_tn(dbr, ys[i])
        dya_ref[...] = dys[0]
        dob_ref[...] = dys[1] * (zb * sgb)
        dzb_ref[...] = (dys[1] * ob * (sgb * (1.0 + zb * (1.0 - sgb)))).astype(dzb_ref.dtype)
        doc_ref[...] = dys[2] * (zc * sgc)
        dzc_ref[...] = (dys[2] * oc * (sgc * (1.0 + zc * (1.0 - sgc)))).astype(dzc_ref.dtype)

    row = lambda w, cb=0: pl.BlockSpec((tm, w), lambda i, cb=cb: (i, cb))
    full = lambda a: pl.BlockSpec(a.shape, lambda i: (0,) * a.ndim)
    sds = lambda w, dt=F32: jax.ShapeDtypeStruct((S, w), dt)
    return pl.pallas_call(
        body, name="gate_bwd", grid=(S // tm,),
        in_specs=[row(D), row(BW), row(BW), row(BW), row(BW, ZB // BW), row(BW, ZC // BW),
                  row(D, 0), row(D, 1), row(D, 2), full(gate_b), full(wbr), full(wbr_t)],
        out_specs=[row(3 * D), row(BW), row(BW), row(BW), row(BW), row(BW),
                   pl.BlockSpec((N_BRANCH, D, BW), lambda i: (0, 0, 0)), pl.BlockSpec((N_BRANCH, D), lambda i: (0, 0))],
        out_shape=[sds(3 * D, MXU_DTYPE), sds(BW, MXU_DTYPE), sds(BW, MXU_DTYPE), sds(BW), sds(BW), sds(BW),
                   jax.ShapeDtypeStruct((N_BRANCH, D, BW), F32), jax.ShapeDtypeStruct((N_BRANCH, D), F32)],
        compiler_params=_params(("arbitrary",), 56),
    )(dm, ya, ob, oc, proj, proj, proj, proj, proj, gate_b, wbr, wbr_t)


def _band_bwd(proj, bias, do):
    S = proj.shape[0]
    scale = CA_DH ** -0.5

    def body(q_ref, k0, k1, k2, v0, v1, v2, b_ref, do_ref, dq_ref, dk_ref, dv_ref, db_ref):
        qi = pl.program_id(1)
        _first_step_zero((dk_ref, dv_ref, db_ref), qi == 0)
        head = lax.broadcasted_iota(jnp.int32, (1, 128), 1) // 64
        kcat = jnp.concatenate([k0[...], k1[...], k2[...]], axis=0).astype(MXU_DTYPE)
        vcat = jnp.concatenate([v0[...], v1[...], v2[...]], axis=0).astype(MXU_DTYPE)
        valid = _band_valid(qi)
        q, dov = q_ref[...], do_ref[...]
        dq = jnp.zeros((BAND_TQ, 128), F32)
        dk = jnp.zeros((BAND_W, 128), F32)
        dv = jnp.zeros((BAND_W, 128), F32)
        for j in range(2):
            qj = jnp.where(head == j, q, 0.0).astype(MXU_DTYPE)
            doj = jnp.where(head == j, dov, 0.0).astype(MXU_DTYPE)
            pn = _band_probs(qj, kcat, b_ref[j], valid, scale)
            dv = dv + _mm_tn(pn, doj)
            dp = _mm_nt(doj, vcat)
            ds = pn * (dp - jnp.sum(pn * dp, axis=1, keepdims=True))
            db_ref[j] += ds
            dsb = (ds * scale).astype(MXU_DTYPE)
            dq = dq + _mm(dsb, jnp.where(head == j, kcat, 0))
            dk = dk + _mm_tn(dsb, qj)
        dq_ref[...] = dq.astype(dq_ref.dtype)
        for t in range(3):
            @pl.when(qi - 2 + t >= 0)
            def _(t=t):
                rows = pl.ds(pl.multiple_of((qi - 2 + t) * BAND_TQ, BAND_TQ), BAND_TQ)
                dk_ref[rows, :] += dk[t * BAND_TQ:(t + 1) * BAND_TQ]
                dv_ref[rows, :] += dv[t * BAND_TQ:(t + 1) * BAND_TQ]

    q, ks, vs, bspec = _band_specs(S)
    col = pl.BlockSpec((S, 128), lambda hp, qi: (0, hp))
    return pl.pallas_call(
        body, name="band_bwd", grid=(CA_H // 2, S // BAND_TQ),
        in_specs=[q, *ks, *vs, bspec, pl.BlockSpec((BAND_TQ, 128), lambda hp, qi: (qi, hp))],
        out_specs=[pl.BlockSpec((BAND_TQ, 128), lambda hp, qi: (qi, hp)), col, col, bspec],
        out_shape=[jax.ShapeDtypeStruct((S, BW), MXU_DTYPE), jax.ShapeDtypeStruct((S, BW), F32),
                   jax.ShapeDtypeStruct((S, BW), F32), jax.ShapeDtypeStruct((CA_H, BAND_TQ, BAND_W), F32)],
        compiler_params=_params(("parallel", "arbitrary"), 56),
    )(proj, proj, proj, proj, proj, proj, proj, bias, do)


BIAS_LO = REL_CLIP - (CHUNK - 1)
BIAS_FAR = 2 * REL_CLIP
BIAS_NEAR0 = BAND_W // 2


def _band_index(col0, ncol):
    i = lax.broadcasted_iota(jnp.int32, (BAND_TQ, ncol), 0)
    j = lax.broadcasted_iota(jnp.int32, (BAND_TQ, ncol), 1) + col0
    idx = jnp.clip(i + 2 * BAND_TQ - j, -REL_CLIP, REL_CLIP) + REL_CLIP
    ci, cj = i // CHUNK, j // CHUNK
    return jnp.where((ci <= cj) & (cj <= ci + LEFT_CHUNKS), idx, -1)


def _skew(x, right):
    row = lax.broadcasted_iota(jnp.int32, (BAND_TQ, 1), 0)
    for b in range(BAND_TQ.bit_length() - 1):
        shift = (1 << b) if right else BAND_W - (1 << b)
        x = jnp.where(((row >> b) & 1) == 1, pltpu.roll(x, shift, 1), x)
    return x


SKEW_NEAR0 = 2 * BAND_TQ - REL_CLIP + 1
SKEW_NEAR1 = 2 * BAND_TQ + CHUNK
SKEW_WRAP0 = BAND_W - (CHUNK - 1)


def _bias_build(rel_table):
    far = rel_table[:, BIAS_FAR:]
    base = jnp.concatenate([jnp.broadcast_to(far, (CA_H, SKEW_NEAR0)), rel_table[:, BIAS_LO:BIAS_FAR][:, ::-1],
                            jnp.broadcast_to(far, (CA_H, BAND_W - SKEW_NEAR1))], axis=1)[:, None, :]

    def body(base_ref, out_ref):
        valid = _band_index(0, BAND_W) >= 0
        for h in range(CA_H):
            tile = _skew(jnp.broadcast_to(base_ref[h], (BAND_TQ, BAND_W)), right=True)
            out_ref[h] = jnp.where(valid, tile, NEG)

    return pl.pallas_call(
        body, name="bias_build",
        out_shape=jax.ShapeDtypeStruct((CA_H, BAND_TQ, BAND_W), F32),
        in_specs=[pl.BlockSpec(memory_space=pltpu.VMEM)],
        out_specs=pl.BlockSpec(memory_space=pltpu.VMEM),
        compiler_params=pltpu.CompilerParams(vmem_limit_bytes=40 << 20),
    )(base)


def _bias_fold(db):
    def body(db_ref, sums_ref, far_ref):
        col = lax.broadcasted_iota(jnp.int32, (1, BAND_W), 1)
        is_far = (col < SKEW_NEAR0) | (col >= SKEW_WRAP0)
        for h in range(CA_H):
            sums = jnp.sum(_skew(db_ref[h], right=False), axis=0, keepdims=True)
            sums_ref[h] = sums
            far_ref[h] = jnp.broadcast_to(jnp.sum(jnp.where(is_far, sums, 0.0), axis=1, keepdims=True), (1, 128))

    sums, far = pl.pallas_call(
        body, name="bias_fold",
        out_shape=[jax.ShapeDtypeStruct((CA_H, 1, BAND_W), F32), jax.ShapeDtypeStruct((CA_H, 1, 128), F32)],
        in_specs=[pl.BlockSpec(memory_space=pltpu.VMEM)],
        out_specs=[pl.BlockSpec(memory_space=pltpu.VMEM), pl.BlockSpec(memory_space=pltpu.VMEM)],
        compiler_params=pltpu.CompilerParams(vmem_limit_bytes=40 << 20),
    )(db)
    near = sums[:, 0, SKEW_NEAR0:SKEW_NEAR1][:, ::-1]
    return jnp.concatenate([jnp.zeros((CA_H, BIAS_LO), F32), near, far[:, 0, :1]], axis=1)


def _flash_bwd(qh, kh, ktr, vh, lse, delta, do):
    S = qh.shape[0]
    T = min(512, S)
    TS = min(FLASH_SUB, T)
    n = S // T
    qt, kt = _tri_tables(n, q_major=False)

    def body(qt_ref, kt_ref, q_ref, k_ref, ktr_ref, v_ref, lse_ref, dl_ref, do_ref,
             dqt_ref, dk_ref, dv_ref, dk_scr, dv_scr):
        t = pl.program_id(1)
        qi, ki = qt_ref[t], kt_ref[t]

        @pl.when(t == 0)
        def _():
            dqt_ref[...] = jnp.zeros(dqt_ref.shape, F32)

        @pl.when(qi == ki)
        def _():
            dk_scr[...] = jnp.zeros(dk_scr.shape, F32)
            dv_scr[...] = jnp.zeros(dv_scr.shape, F32)

        def step(diag):
            head = lax.broadcasted_iota(jnp.int32, (1, 128), 1) // 64
            for j in range(2):
                sl = slice(128 * j, 128 * (j + 1))
                for qs in range(T // TS):
                    cq = slice(qs * TS, (qs + 1) * TS)
                    qsub = q_ref[cq, sl]
                    doj = jnp.where(head == j, do_ref[cq, :], 0.0).astype(MXU_DTYPE)
                    lse, dlt = lse_ref[0, j:j + 1, cq], dl_ref[0, j:j + 1, cq]
                    cols = pl.ds(pl.multiple_of(qi * T + qs * TS, TS), TS)
                    for ks in range(T // TS):
                        if diag and ks > qs:
                            continue
                        rk = slice(ks * TS, (ks + 1) * TS)
                        s = _mm_nt(k_ref[rk, sl], qsub)
                        if diag and ks == qs:
                            s = jnp.where(_chunk_mask_t(TS), s, NEG)
                        p = jnp.exp2(s - lse)
                        dv_scr[rk, :] += _mm(p, doj)
                        dp = _mm_nt(v_ref[rk, :], doj)
                        ds = (p * (dp - dlt)).astype(MXU_DTYPE)
                        dk_scr[rk, sl] += _mm(ds, qsub)
                        dqt_ref[sl, cols] += _mm(ktr_ref[sl, rk], ds) * MLA_SCALE

        @pl.when(qi > ki)
        def _():
            step(False)

        @pl.when(qi == ki)
        def _():
            step(True)

        @pl.when(qi == n - 1)
        def _():
            dk_ref[...] = dk_scr[...] * 0.6931471805599453
            dv_ref[...] = dv_scr[...]

    qmap = lambda hp, t, qt, kt: (qt[t], hp)
    kmap = lambda hp, t, qt, kt: (kt[t], hp)
    stat = pl.BlockSpec((1, 2, T), lambda hp, t, qt, kt: (hp, 0, qt[t]))
    return pl.pallas_call(
        body, name="flash_bwd",
        grid_spec=pltpu.PrefetchScalarGridSpec(
            num_scalar_prefetch=2, grid=(MLA_H // 2, qt.shape[0]),
            in_specs=[pl.BlockSpec((T, 256), qmap), pl.BlockSpec((T, 256), kmap),
                      pl.BlockSpec((256, T), lambda hp, t, qt, kt: (hp, kt[t])), pl.BlockSpec((T, 128), kmap),
                      stat, stat, pl.BlockSpec((T, 128), qmap)],
            out_specs=[pl.BlockSpec((256, S), lambda hp, t, qt, kt: (hp, 0)), pl.BlockSpec((T, 256), kmap),
                       pl.BlockSpec((T, 128), kmap)],
            scratch_shapes=[pltpu.VMEM((T, 256), F32), pltpu.VMEM((T, 128), F32)]),
        out_shape=[jax.ShapeDtypeStruct((1024, S), F32), jax.ShapeDtypeStruct((S, 1024), F32),
                   jax.ShapeDtypeStruct((S, BW), F32)],
        compiler_params=_params(("parallel", "arbitrary"), 56),
    )(qt, kt, qh, kh, ktr, vh, lse, delta, do)


def _mla_prep_bwd(proj, dqf, dkf, dvf, gq, gkv, wuq_t, wuk_t, wuv_t, cos_t, sin_a, sin_b):
    S = proj.shape[0]
    tm = min(512, S)

    def body(p_ref, dq_ref, dk_ref, dv_ref, gq_ref, gkv_ref, wq_ref, wk_ref, wv_ref, c_ref, sa_ref, sb_ref,
             db_ref, dwq_ref, dwk_ref, dwv_ref, dgq_ref, dgkv_ref):
        _first_step_zero((dwq_ref, dwk_ref, dwv_ref, dgq_ref, dgkv_ref), pl.program_id(0) == 0)
        c, sa, sb = c_ref[...], sa_ref[...], sb_ref[...]
        nq, rq = _rms(p_ref[:, 0:Q_RANK])
        nkv, rkv = _rms(p_ref[:, Q_RANK:Q_RANK + KV_RANK])
        cq = (nq * gq_ref[...]).astype(MXU_DTYPE)
        ckv = (nkv * gkv_ref[...]).astype(MXU_DTYPE)
        dkr = jnp.zeros((tm, 128), F32)
        dq_pre = []
        for h in range(MLA_H):
            sl = slice(128 * h, 128 * (h + 1))
            dq_pre.append(_rope_t(jnp.transpose(dq_ref[sl, :]), c, sa, sb).astype(MXU_DTYPE))
            dkr = dkr + dk_ref[:, sl]
        dq_pre = jnp.concatenate(dq_pre, axis=1)
        dcq = _mm(dq_pre, wq_ref[...])
        dwq_ref[...] += _mm_tn(dq_pre, cq)
        dgq_ref[...] += jnp.sum(dcq * nq, axis=0, keepdims=True)
        db_ref[:, 0:Q_RANK] = _rms_bwd(dcq * gq_ref[...], nq, rq).astype(db_ref.dtype)
        dk = dk_ref[...].astype(MXU_DTYPE)
        dv = dv_ref[...].astype(MXU_DTYPE)
        dckv = _mm(dk, wk_ref[...]) + _mm(dv, wv_ref[...])
        dwk_ref[...] += _mm_tn(dk, ckv)
        dwv_ref[...] += _mm_tn(dv, ckv)
        dgkv_ref[...] += jnp.sum(dckv * nkv, axis=0, keepdims=True)
        db_ref[:, Q_RANK:Q_RANK + KV_RANK] = _rms_bwd(dckv * gkv_ref[...], nkv, rkv).astype(db_ref.dtype)
        lane = lax.broadcasted_iota(jnp.int32, (1, 128), 1)
        rope_lanes = (lane >= MLA_NOPE) & (lane < MLA_QK)
        db_ref[:, Q_RANK + KV_RANK:BW] = _rope_t(jnp.where(rope_lanes, dkr, 0.0), c, sa, sb).astype(db_ref.dtype)

    full = lambda a: pl.BlockSpec(a.shape, lambda i: (0,) * a.ndim)
    fulls = lambda shape: pl.BlockSpec(shape, lambda i: (0,) * len(shape))
    tab = pl.BlockSpec((tm, 128), lambda i: (i, 0))
    row = lambda w, cb=0: pl.BlockSpec((tm, w), lambda i, cb=cb: (i, cb))
    return pl.pallas_call(
        body, name="mla_prep_bwd", grid=(S // tm,),
        in_specs=[row(BW, QD // BW), pl.BlockSpec((1024, tm), lambda i: (0, i)), row(1024), row(BW),
                  full(gq), full(gkv), full(wuq_t),
                  full(wuk_t), full(wuv_t), tab, tab, tab],
        out_specs=[row(BW), fulls((1024, Q_RANK)), fulls((1024, KV_RANK)), fulls((BW, KV_RANK)),
                   fulls((1, Q_RANK)), fulls((1, KV_RANK))],
        out_shape=[jax.ShapeDtypeStruct((S, BW), MXU_DTYPE), jax.ShapeDtypeStruct((1024, Q_RANK), F32),
                   jax.ShapeDtypeStruct((1024, KV_RANK), F32), jax.ShapeDtypeStruct((BW, KV_RANK), F32),
                   jax.ShapeDtypeStruct((1, Q_RANK), F32), jax.ShapeDtypeStruct((1, KV_RANK), F32)],
        compiler_params=_params(("arbitrary",), 56),
    )(proj, dqf, dkf, dvf, gq, gkv, wuq_t, wuk_t, wuv_t, cos_t, sin_a, sin_b)


def _sgu_bwd(proj, dya, ln_g, ln_b, ws, bfull):
    S = proj.shape[0]
    tm = min(512, S)

    def body(u_ref, v_ref, z_ref, dy_ref, g_ref, b_ref, ws_ref, bf_ref,
             da_ref, dws_ref, dbf_ref, dlg_ref, dlb_ref, dbs_ref):
        i = pl.program_id(0)
        _first_step_zero((dws_ref, dbf_ref, dlg_ref, dlb_ref, dbs_ref), i == 0)
        tri = _sgu_tri()
        ws_m = [jnp.where(tri, ws_ref[g], 0.0).astype(MXU_DTYPE) for g in range(SGU_G)]
        grp = lax.broadcasted_iota(jnp.int32, (1, BW), 1) // (BW // SGU_G)
        for b in range(tm // SGU_T):
            r = slice(b * SGU_T, (b + 1) * SGU_T)
            vln, vhat, rstd = _layernorm(v_ref[r, :], g_ref[...], b_ref[...])
            vb = vln.astype(MXU_DTYPE)
            mixed = _sgu_mix(ws_m, vb, bf_ref[...], grp)
            u, z, dy = u_ref[r, :], z_ref[r, :], dy_ref[r, :]
            sg = _sigmoid(z)
            sz = z * sg
            da_ref[r, 0:BW] = (dy * mixed * sz).astype(da_ref.dtype)
            da_ref[r, 2 * BW:3 * BW] = (dy * u * mixed * (sg * (1.0 + z * (1.0 - sg)))).astype(da_ref.dtype)
            dmix = dy * u * sz
            dbf_ref[...] += dmix
            dvln = jnp.zeros((SGU_T, BW), F32)
            for g in range(SGU_G):
                dmg = jnp.where(grp == g, dmix, 0.0).astype(MXU_DTYPE)
                dvln = dvln + _mm_tn(ws_m[g], dmg)
                dws_ref[g] += jnp.where(tri, _mm_nt(dmg, vb), 0.0)
            dlg_ref[...] += jnp.sum(dvln * vhat, axis=0, keepdims=True)
            dlb_ref[...] += jnp.sum(dvln, axis=0, keepdims=True)
            dvh = dvln * g_ref[...]
            da_ref[r, BW:2 * BW] = (rstd * (dvh - jnp.mean(dvh, axis=-1, keepdims=True)
                                            - vhat * jnp.mean(dvh * vhat, axis=-1, keepdims=True))).astype(da_ref.dtype)

        @pl.when(i == pl.num_programs(0) - 1)
        def _():
            dbf = dbf_ref[...]
            for g in range(SGU_G):
                dbs_ref[:, g:g + 1] = jnp.sum(jnp.where(grp == g, dbf, 0.0), axis=1, keepdims=True)

    blk = lambda cb: pl.BlockSpec((tm, BW), lambda i, cb=cb: (i, cb))
    fulls = lambda shape: pl.BlockSpec(shape, lambda i: (0,) * len(shape))
    return pl.pallas_call(
        body, name="sgu_bwd", grid=(S // tm,),
        in_specs=[blk(UA // BW), blk(VA // BW), blk(ZA // BW), blk(0), fulls((1, BW)), fulls((1, BW)),
                  fulls((SGU_G, SGU_T, SGU_T)), fulls((SGU_T, BW))],
        out_specs=[pl.BlockSpec((tm, 3 * BW), lambda i: (i, 0)), fulls((SGU_G, SGU_T, SGU_T)),
                   fulls((SGU_T, BW)), fulls((1, BW)), fulls((1, BW)), fulls((SGU_T, SGU_G))],
        out_shape=[jax.ShapeDtypeStruct((S, 3 * BW), MXU_DTYPE), jax.ShapeDtypeStruct((SGU_G, SGU_T, SGU_T), F32),
                   jax.ShapeDtypeStruct((SGU_T, BW), F32), jax.ShapeDtypeStruct((1, BW), F32),
                   jax.ShapeDtypeStruct((1, BW), F32), jax.ShapeDtypeStruct((SGU_T, SGU_G), F32)],
        compiler_params=_params(("arbitrary",), 40),
    )(proj, proj, proj, dya, ln_g, ln_b, ws, bfull)


DX_TK = 3 * BW


def _inproj_bwd_dx(dgl, da, small, w_t, x, pre_g, dxo):
    S = x.shape[0]
    tm = min(512, S)
    nk = DP // DX_TK

    def body(g_ref, a_ref, s0, s1, s2, s3, s4, s5, w_ref, x_ref, pg_ref, dxo_ref, dx_ref, dg_ref, acc):
        i, k = pl.program_id(0), pl.program_id(1)
        _first_step_zero((dg_ref,), (i == 0) & (k == 0))

        @pl.when(k == 0)
        def _():
            acc[...] = jnp.zeros(acc.shape, F32)

        @pl.when(k < 2)
        def _():
            acc[...] += _mm(g_ref[...], w_ref[...])

        @pl.when(k == 2)
        def _():
            acc[...] += _mm(a_ref[...], w_ref[...])

        for kk, trio in ((3, (s0, s1, s2)), (4, (s3, s4, s5))):
            @pl.when(k == kk)
            def _(trio=trio):
                t = acc[...]
                for n, ref in enumerate(trio):
                    t = t + _mm(ref[...], w_ref[n * BW:(n + 1) * BW, :])
                acc[...] = t

        @pl.when(k == nk - 1)
        def _():
            n, r = _rms(x_ref[...])
            dxn = acc[...]
            dg_ref[...] += jnp.sum(dxn * n, axis=0, keepdims=True)
            dx_ref[...] = dxo_ref[...] + _rms_bwd(dxn * pg_ref[...], n, r)

    row = pl.BlockSpec((tm, D), lambda i, k: (i, 0))
    vec = pl.BlockSpec((1, D), lambda i, k: (0, 0))
    fixed = lambda w: pl.BlockSpec((tm, w), lambda i, k: (i, 0))
    return pl.pallas_call(
        body, name="inproj_bwd_dx", grid=(S // tm, nk),
        in_specs=[pl.BlockSpec((tm, DX_TK), lambda i, k: (i, jnp.minimum(k, 1))), fixed(DX_TK)]
                 + [fixed(BW)] * 6 + [pl.BlockSpec((DX_TK, D), lambda i, k: (k, 0)), row, vec, row],
        out_specs=[row, vec],
        out_shape=[jax.ShapeDtypeStruct((S, D), F32), jax.ShapeDtypeStruct((1, D), F32)],
        scratch_shapes=[pltpu.VMEM((tm, D), F32)],
        compiler_params=_params(("arbitrary", "arbitrary"), 56),
    )(dgl, da, *small, w_t, x, pre_g, dxo)


def _matmul_tn(a, b):
    S, W = a.shape
    K = b.shape[1]
    tk, tn = min(2048, S), BW

    def body(a_ref, b_ref, o_ref):
        _first_step_zero((o_ref,), pl.program_id(1) == 0)
        o_ref[...] += _mm_tn(a_ref[...], b_ref[...])

    return pl.pallas_call(
        body, name="matmul_tn", grid=(W // tn, S // tk),
        in_specs=[pl.BlockSpec((tk, tn), lambda j, k: (k, j)), pl.BlockSpec((tk, K), lambda j, k: (k, 0))],
        out_specs=pl.BlockSpec((tn, K), lambda j, k: (j, 0)),
        out_shape=jax.ShapeDtypeStruct((W, K), F32),
        compiler_params=_params(("parallel", "arbitrary"), 48),
    )(a, b)


def _pad_rows_w_in(wt):
    z = lambda n: jnp.zeros((n, wt.shape[1]), wt.dtype)
    return jnp.concatenate([wt[4512:], wt[:1920], z(64), wt[1920:1952], z(32), wt[1952:4512]], axis=0)


def _unpad_rows_w_in(dwt):
    a = dwt[3 * D:]
    return jnp.concatenate([a[:1920], a[1984:2016], a[2048:], dwt[:3 * D]], axis=0)


def _pad_head_rows(wt, width):
    k = wt.shape[1]
    return jnp.pad(wt.reshape(MLA_H, width, k), ((0, 0), (0, 128 - width), (0, 0))).reshape(MLA_H * 128, k)


def _unpad_head_rows(wt, width):
    k = wt.shape[1]
    return wt.reshape(MLA_H, 128, k)[:, :width].reshape(MLA_H * width, k)


def _rope_tables(S):
    half = MLA_ROPE // 2
    inv = 10000.0 ** (-jnp.arange(half, dtype=F32) / half)
    ang = jnp.arange(S, dtype=F32)[:, None] * inv[None, :]
    cos, sin = jnp.cos(ang), jnp.sin(ang)
    one = lambda n: jnp.ones((S, n), F32)
    zero = lambda n: jnp.zeros((S, n), F32)
    cos_t = jnp.concatenate([one(MLA_NOPE), cos, cos, one(128 - MLA_QK)], axis=1)
    sin_a = jnp.concatenate([zero(MLA_NOPE), -sin, zero(128 - MLA_NOPE - half)], axis=1)
    sin_b = jnp.concatenate([zero(MLA_NOPE + half), sin, zero(128 - MLA_QK)], axis=1)
    return cos_t, sin_a, sin_b


SHARDED = ("w_in", "mla_w_uq", "mla_w_ukv", "w_branch", "w_out")
SHARD_PACK = SHARDED + ("gate_b",)
REPLICATED = ("pre_g", "post_g", "sgu_ln_g", "sgu_ln_b", "sgu_w", "sgu_b", "mla_q_norm_g", "mla_kv_norm_g",
              "ca_rel_bias")
OUT_ORDER = ("w_in", "pre_g", "post_g", "sgu_ln_g", "sgu_ln_b", "sgu_w", "sgu_b", "mla_q_norm_g", "mla_kv_norm_g",
             "mla_w_uq", "mla_w_ukv", "ca_rel_bias", "w_branch", "gate_b", "w_out")
TRANSPOSED = {"w_in": (0, 2, 1), "mla_w_uq": (0, 2, 1), "mla_w_ukv": (0, 2, 1), "w_branch": (0, 1, 3, 2)}


def _canon(name, a):
    return jnp.transpose(a, TRANSPOSED[name]) if name in TRANSPOSED else a


def _to_rows(flat, width, mult):
    n = flat.shape[-1]
    pad = (-n) % (width * mult)
    flat = jnp.pad(flat, [(0, 0)] * (flat.ndim - 1) + [(0, pad)])
    return flat.reshape(flat.shape[:-1] + (-1, width))


def _gather_shards(shards, dtype):
    names = list(shards)
    flat = jnp.concatenate([shards[n].astype(dtype).reshape(-1) for n in names])
    got = _all_gather(_to_rows(flat, 128, 16)).reshape(N_DEV, -1)
    out, off = {}, 0
    for n in names:
        size = shards[n].size
        out[n] = got[:, off:off + size].reshape((N_DEV,) + shards[n].shape)
        off += size
    return out


def _pack(d, names):
    return _to_rows(jnp.concatenate([d[n].reshape(-1) for n in names]), 1024, 128)


def _unpack(p, like, names):
    flat, out, off = p.reshape(-1), {}, 0
    for n in names:
        out[n] = flat[off:off + like[n].size].reshape(like[n].shape)
        off += like[n].size
    return out


def kernel(x, w_in, pre_g, post_g, sgu_ln_g, sgu_ln_b, sgu_w, sgu_b, mla_q_norm_g, mla_kv_norm_g, mla_w_uq, mla_w_ukv, ca_rel_bias, w_branch, gate_b, w_out, loss_target, m_w_in, m_pre_g, m_post_g, m_sgu_ln_g, m_sgu_ln_b, m_sgu_w, m_sgu_b, m_mla_q_norm_g, m_mla_kv_norm_g, m_mla_w_uq, m_mla_w_ukv, m_ca_rel_bias, m_w_branch, m_gate_b, m_w_out, v_w_in, v_pre_g, v_post_g, v_sgu_ln_g, v_sgu_ln_b, v_sgu_w, v_sgu_b, v_mla_q_norm_g, v_mla_kv_norm_g, v_mla_w_uq, v_mla_w_ukv, v_ca_rel_bias, v_w_branch, v_gate_b, v_w_out):
    weights = dict(w_in=w_in, pre_g=pre_g, post_g=post_g, sgu_ln_g=sgu_ln_g, sgu_ln_b=sgu_ln_b, sgu_w=sgu_w,
                   sgu_b=sgu_b, mla_q_norm_g=mla_q_norm_g, mla_kv_norm_g=mla_kv_norm_g, mla_w_uq=mla_w_uq,
                   mla_w_ukv=mla_w_ukv, ca_rel_bias=ca_rel_bias, w_branch=w_branch, gate_b=gate_b, w_out=w_out)
    mom_m = dict(w_in=m_w_in, pre_g=m_pre_g, post_g=m_post_g, sgu_ln_g=m_sgu_ln_g, sgu_ln_b=m_sgu_ln_b,
                 sgu_w=m_sgu_w, sgu_b=m_sgu_b, mla_q_norm_g=m_mla_q_norm_g, mla_kv_norm_g=m_mla_kv_norm_g,
                 mla_w_uq=m_mla_w_uq, mla_w_ukv=m_mla_w_ukv, ca_rel_bias=m_ca_rel_bias, w_branch=m_w_branch,
                 gate_b=m_gate_b, w_out=m_w_out)
    mom_v = dict(w_in=v_w_in, pre_g=v_pre_g, post_g=v_post_g, sgu_ln_g=v_sgu_ln_g, sgu_ln_b=v_sgu_ln_b,
                 sgu_w=v_sgu_w, sgu_b=v_sgu_b, mla_q_norm_g=v_mla_q_norm_g, mla_kv_norm_g=v_mla_kv_norm_g,
                 mla_w_uq=v_mla_w_uq, mla_w_ukv=v_mla_w_ukv, ca_rel_bias=v_ca_rel_bias, w_branch=v_w_branch,
                 gate_b=v_gate_b, w_out=v_w_out)
    depth = w_in.shape[0]
    S = x.shape[1]
    xs = x.reshape(S, D)

    cw = {n: _canon(n, weights[n]) for n in SHARD_PACK}
    gw = _gather_shards({n: cw[n] for n in SHARDED}, MXU_DTYPE)
    gw.update(_gather_shards({"gate_b": gate_b}, F32))
    cos_t, sin_a, sin_b = _rope_tables(S)

    layers = []
    for l in range(depth):
        w_t = _pad_rows_w_in(gw["w_in"][:, l].reshape(D_IN, D))
        wuq_t = _pad_head_rows(gw["mla_w_uq"][:, l].reshape(MLA_H * MLA_QK, Q_RANK), MLA_QK)
        ukv_t = gw["mla_w_ukv"][:, l]
        wuk_t = jnp.pad(ukv_t[:, :MLA_NOPE], ((0, 0), (0, 128 - MLA_NOPE), (0, 0))).reshape(MLA_H * 128, KV_RANK)
        wuv_t = ukv_t[:, MLA_NOPE:].reshape(BW, KV_RANK)
        wbr_t = jnp.swapaxes(gw["w_branch"][:, l], 0, 1).reshape(N_BRANCH, D, BW)
        lw = dict(
            w_t=w_t, w_pad=w_t.T, wuq_t=wuq_t, wuq=wuq_t.T, wuk_t=wuk_t, wuk=wuk_t.T, wuv_t=wuv_t, wuv=wuv_t.T,
            wbr_t=wbr_t, wbr=jnp.swapaxes(wbr_t, 1, 2), wout=gw["w_out"][:, l].reshape(D, D),
            gate_b=jnp.swapaxes(gw["gate_b"][:, l], 0, 1).reshape(N_BRANCH, D),
            pre_g=pre_g[l][None], post_g=post_g[l][None], ln_g=sgu_ln_g[l][None], ln_b=sgu_ln_b[l][None],
            ws=sgu_w[l], bfull=jnp.repeat(sgu_b[l].T, BW // SGU_G, axis=1),
            gq=mla_q_norm_g[l][None], gkv=mla_kv_norm_g[l][None], bias=_bias_build(ca_rel_bias[l]))
        layers.append(lw)

    saved = []
    h_x = xs
    for lw in layers:
        proj, xn = _inproj_fwd(h_x, lw["pre_g"], lw["w_pad"])
        ya = _sgu_fwd(proj, lw["ln_g"], lw["ln_b"], lw["ws"], lw["bfull"])
        qh, kh, vh, ktr, vtr = _mla_prep_fwd(proj, lw["gq"], lw["gkv"], lw["wuq"], lw["wuk"], lw["wuv"],
                                             cos_t, sin_a, sin_b)
        ob, lse = _flash_fwd(qh, kh, vtr)
        oc = _band_fwd(proj, lw["bias"])
        x_new, merged, hh = _merge_fwd(h_x, ya, ob, oc, proj, lw["gate_b"], lw["wbr"], lw["wout"], lw["post_g"])
        saved.append(dict(x=h_x, proj=proj, xn=xn, ya=ya, qh=qh, kh=kh, vh=vh, ktr=ktr, ob=ob, lse=lse, oc=oc,
                          merged=merged, h=hh))
        h_x = x_new

    loss_part, dx = _loss_fwd_bwd(h_x, loss_target.reshape(S, D))
    loss = lax.psum(loss_part[0, 0], MESH_AXES)

    rows = {n: [None] * depth for n in SHARD_PACK}
    rep = {n: [None] * depth for n in REPLICATED}
    for l in reversed(range(depth)):
        lw, sv = layers[l], saved[l]
        proj = sv["proj"]
        dmerged, dw_out, dg_post = _out_bwd(dx, sv["h"], sv["merged"], lw["post_g"], lw["wout"].T)
        dgl, dzb, dzc, dya, dob, doc, dwbr_t, dgb = _gate_bwd(
            dmerged, sv["ya"], sv["ob"], sv["oc"], proj, lw["gate_b"], lw["wbr"], lw["wbr_t"])
        dqc, dkc, dvc, dbias = _band_bwd(proj, lw["bias"], doc)
        drel = _bias_fold(dbias)[:, :2 * REL_CLIP + 1]
        dqf, dkf, dvf = _flash_bwd(sv["qh"], sv["kh"], sv["ktr"], sv["vh"], sv["lse"],
                                   _attn_delta(sv["ob"], dob), dob)
        db, dwuq_t, dwuk_t, dwuv_t, dgq, dgkv = _mla_prep_bwd(
            proj, dqf, dkf, dvf, lw["gq"], lw["gkv"], lw["wuq_t"], lw["wuk_t"], lw["wuv_t"], cos_t, sin_a, sin_b)
        da, dws, _, dlg, dlb, dbs = _sgu_bwd(proj, dya, lw["ln_g"], lw["ln_b"], lw["ws"], lw["bfull"])
        small = (db, dzb, dqc, dkc, dvc, dzc)
        dx, dg_pre = _inproj_bwd_dx(dgl, da, small, lw["w_t"], sv["x"], lw["pre_g"], dx)
        dw_t = jnp.concatenate([_matmul_tn(a, sv["xn"]) for a in (dgl, da) + small], axis=0)

        rows["w_in"][l] = _unpad_rows_w_in(dw_t).reshape(N_DEV, -1)
        rows["mla_w_uq"][l] = _unpad_head_rows(dwuq_t, MLA_QK).reshape(N_DEV, -1)
        dk3 = dwuk_t.reshape(MLA_H, 128, KV_RANK)[:, :MLA_NOPE]
        dv3 = dwuv_t.reshape(MLA_H, 64, KV_RANK)
        rows["mla_w_ukv"][l] = jnp.concatenate([dk3, dv3], axis=1).reshape(N_DEV, -1)
        rows["w_branch"][l] = jnp.swapaxes(dwbr_t.reshape(N_BRANCH, N_DEV, D // N_DEV, BW), 0, 1).reshape(N_DEV, -1)
        rows["w_out"][l] = dw_out.reshape(N_DEV, -1)
        rows["gate_b"][l] = jnp.swapaxes(dgb.reshape(N_BRANCH, N_DEV, D // N_DEV), 0, 1).reshape(N_DEV, -1)
        rep["pre_g"][l] = dg_pre[0]
        rep["post_g"][l] = dg_post[0]
        rep["sgu_ln_g"][l] = dlg[0]
        rep["sgu_ln_b"][l] = dlb[0]
        rep["sgu_w"][l] = dws
        rep["sgu_b"][l] = dbs.T
        rep["mla_q_norm_g"][l] = dgq[0]
        rep["mla_kv_norm_g"][l] = dgkv[0]
        rep["ca_rel_bias"][l] = drel
    grad_x = dx.reshape(x.shape)

    send = jnp.concatenate([jnp.stack(rows[n], axis=1).reshape(N_DEV, -1) for n in SHARD_PACK], axis=1)
    recv = _exchange(_to_rows(send.astype(MXU_DTYPE), 1024, 128))
    cm = {n: _canon(n, mom_m[n]) for n in SHARD_PACK}
    cv = {n: _canon(n, mom_v[n]) for n in SHARD_PACK}
    outs_sh = _reduce_adamw(recv, _pack(cw, SHARD_PACK), _pack(cm, SHARD_PACK), _pack(cv, SHARD_PACK))
    outs_sh = [{n: _canon(n, a) for n, a in _unpack(p, cw, SHARD_PACK).items()} for p in outs_sh]

    part = _pack({n: jnp.stack(rep[n]) for n in REPLICATED}, REPLICATED)
    allp = _all_gather(part)
    outs_rep = _reduce_adamw(allp, _pack(weights, REPLICATED), _pack(mom_m, REPLICATED), _pack(mom_v, REPLICATED))
    outs_rep = [_unpack(p, weights, REPLICATED) for p in outs_rep]

    outs = [{**a, **b} for a, b in zip(outs_sh, outs_rep)]
    return (loss, grad_x, *[o[n] for o in outs for n in OUT_ORDER])
```

```python
import functools

import jax
import jax.numpy as jnp
from jax import lax
from jax.experimental import pallas as pl
from jax.experimental.pallas import tpu as pltpu

F32 = jnp.float32
MXU_DTYPE = jnp.bfloat16
EPS = 1e-6
NEG = -1e30
MESH_AXES = ("x", "y", "c")
N_DEV = 8

D = 1024
BW = 512
N_BRANCH = 3
CHUNK = 64
SGU_T = 128
SGU_G = 8
MLA_H = 8
MLA_NOPE = 64
MLA_ROPE = 32
MLA_QK = MLA_NOPE + MLA_ROPE
Q_RANK = 256
KV_RANK = 128
CA_H = 8
CA_DH = 64
LEFT_CHUNKS = 8
REL_CLIP = 128
D_IN = 7584

G_OFF, UA, VA, ZA, QD, ZB, QC, KC, VC, ZC, DP = 0, 3072, 3584, 4096, 4608, 5120, 5632, 6144, 6656, 7168, 7680
BAND_TQ = 256
BAND_W = 3 * BAND_TQ

ADAM_LR, ADAM_B1, ADAM_B2, ADAM_EPS, ADAM_WD, ADAM_STEP = 0.001, 0.9, 0.999, 1e-08, 0.01, 10


def _params(sem, mib):
    return pltpu.CompilerParams(dimension_semantics=sem, vmem_limit_bytes=mib << 20)


def _mm(a, b):
    return jnp.dot(a.astype(MXU_DTYPE), b.astype(MXU_DTYPE), preferred_element_type=F32)


def _mm_nt(a, b):
    return lax.dot_general(a.astype(MXU_DTYPE), b.astype(MXU_DTYPE), (((1,), (1,)), ((), ())),
                           preferred_element_type=F32)


def _mm_tn(a, b):
    return lax.dot_general(a.astype(MXU_DTYPE), b.astype(MXU_DTYPE), (((0,), (0,)), ((), ())),
                           preferred_element_type=F32)


def _sigmoid(z):
    return 1.0 / (1.0 + jnp.exp(-z))


def _rms(x):
    r = lax.rsqrt(jnp.mean(x * x, axis=-1, keepdims=True) + EPS)
    return x * r, r


def _rms_bwd(dn, n, r):
    return r * (dn - n * jnp.mean(dn * n, axis=-1, keepdims=True))


def _rope(b, c, sa, sb):
    return b * c + pltpu.roll(b, 112, 1) * sa + pltpu.roll(b, 16, 1) * sb


def _rope_t(d, c, sa, sb):
    return d * c + pltpu.roll(d * sa, 16, 1) + pltpu.roll(d * sb, 112, 1)


def _all_gather(blk):
    R = blk.shape[0]

    def body(x_ref, out_ref, send_sems, recv_sems, local_sem):
        x, y, c = lax.axis_index("x"), lax.axis_index("y"), lax.axis_index("c")
        me, sibling = (x, y, c), (x, y, 1 - c)
        chips = [(1 - x, y), (x, 1 - y), (1 - x, 1 - y)]

        def slot(px, py, pc):
            return out_ref.at[4 * px + 2 * py + pc]

        def copy(k, block, to, src=None):
            return pltpu.make_async_remote_copy(
                src_ref=slot(*block) if src is None else src, dst_ref=slot(*block),
                send_sem=send_sems.at[k], recv_sem=recv_sems.at[k],
                device_id=to, device_id_type=pl.DeviceIdType.MESH)

        mine = pltpu.make_async_copy(x_ref, slot(*me), local_sem)
        mine.start()
        first = [copy(0, me, sibling, src=x_ref)]
        first += [copy(1 + j, me, (*chip, c), src=x_ref) for j, chip in enumerate(chips)]
        for cp in first:
            cp.start()
        passed = [copy(4 + j, (*chip, c), sibling) for j, chip in enumerate(chips)]
        for j, chip in enumerate(chips):
            copy(1 + j, (*chip, c), me).wait_recv()
            passed[j].start()
        copy(0, sibling, me).wait_recv()
        for j, chip in enumerate(chips):
            copy(4 + j, (*chip, 1 - c), me).wait_recv()
        for cp in first + passed:
            cp.wait_send()
        mine.wait()

    return pl.pallas_call(
        body, name="all_gather",
        out_shape=jax.ShapeDtypeStruct((N_DEV,) + blk.shape, blk.dtype),
        in_specs=[pl.BlockSpec(memory_space=pl.ANY)],
        out_specs=pl.BlockSpec(memory_space=pl.ANY),
        scratch_shapes=[pltpu.SemaphoreType.DMA((7,)), pltpu.SemaphoreType.DMA((7,)), pltpu.SemaphoreType.DMA(())],
    )(blk)


def _exchange(send):
    R = send.shape[1]

    def body(s_ref, r_ref, send_sems, recv_sems, local_sem):
        x, y, c = lax.axis_index("x"), lax.axis_index("y"), lax.axis_index("c")
        me = 4 * x + 2 * y + c
        local = pltpu.make_async_copy(s_ref.at[me], r_ref.at[me], local_sem)
        local.start()

        def peer(k):
            px = 1 - x if (k >> 2) & 1 else x
            py = 1 - y if (k >> 1) & 1 else y
            pc = 1 - c if k & 1 else c
            return (px, py, pc), 4 * px + 2 * py + pc

        copies = []
        for k in range(1, N_DEV):
            pos, pid = peer(k)
            cp = pltpu.make_async_remote_copy(
                src_ref=s_ref.at[pid], dst_ref=r_ref.at[me],
                send_sem=send_sems.at[k], recv_sem=recv_sems.at[k],
                device_id=pos, device_id_type=pl.DeviceIdType.MESH)
            cp.start()
            copies.append(cp)
        for k in range(1, N_DEV):
            pos, pid = peer(k)
            pltpu.make_async_remote_copy(
                src_ref=s_ref.at[pid], dst_ref=r_ref.at[pid],
                send_sem=send_sems.at[k], recv_sem=recv_sems.at[k],
                device_id=pos, device_id_type=pl.DeviceIdType.MESH).wait_recv()
        for cp in copies:
            cp.wait_send()
        local.wait()

    return pl.pallas_call(
        body, name="grad_exchange",
        out_shape=jax.ShapeDtypeStruct(send.shape, send.dtype),
        in_specs=[pl.BlockSpec(memory_space=pl.ANY)],
        out_specs=pl.BlockSpec(memory_space=pl.ANY),
        scratch_shapes=[pltpu.SemaphoreType.DMA((N_DEV,)), pltpu.SemaphoreType.DMA((N_DEV,)),
                        pltpu.SemaphoreType.DMA(())],
    )(send)


def _reduce_adamw(recv, w, m, v):
    R = w.shape[0]
    tr = 128 if R % 128 == 0 else 8
    c1 = 1.0 - ADAM_B1 ** ADAM_STEP
    c2 = 1.0 - ADAM_B2 ** ADAM_STEP

    def body(r_ref, w_ref, m_ref, v_ref, g_ref, d_ref, nm_ref, nv_ref):
        g = r_ref[0].astype(F32)
        for s in range(1, N_DEV):
            g = g + r_ref[s].astype(F32)
        m2 = ADAM_B1 * m_ref[...] + (1.0 - ADAM_B1) * g
        v2 = ADAM_B2 * v_ref[...] + (1.0 - ADAM_B2) * (g * g)
        m_hat = m2 / c1
        v_hat = v2 / c2
        g_ref[...] = g
        d_ref[...] = -ADAM_LR * (m_hat / (jnp.sqrt(v_hat) + ADAM_EPS) + ADAM_WD * w_ref[...])
        nm_ref[...] = m2
        nv_ref[...] = v2

    row = pl.BlockSpec((tr, 1024), lambda i: (i, 0))
    return pl.pallas_call(
        body, name="reduce_adamw", grid=(R // tr,),
        in_specs=[pl.BlockSpec((N_DEV, tr, 1024), lambda i: (0, i, 0)), row, row, row],
        out_specs=[row, row, row, row],
        out_shape=[jax.ShapeDtypeStruct((R, 1024), F32)] * 4,
        compiler_params=_params(("parallel",), 40),
    )(recv, w, m, v)


def _inproj_fwd(x, pre_g, w_pad):
    S = x.shape[0]
    tm, tn = min(1024, S), 1536

    def body(x_ref, g_ref, w_ref, proj_ref, xn_ref):
        @pl.when(pl.program_id(1) == 0)
        def _():
            n, _ = _rms(x_ref[...])
            xn_ref[...] = (n * g_ref[...]).astype(xn_ref.dtype)
        proj_ref[...] = jnp.dot(xn_ref[...], w_ref[...], preferred_element_type=F32)

    return pl.pallas_call(
        body, name="inproj_fwd", grid=(S // tm, DP // tn),
        in_specs=[pl.BlockSpec((tm, D), lambda i, j: (i, 0)), pl.BlockSpec((1, D), lambda i, j: (0, 0)),
                  pl.BlockSpec((D, tn), lambda i, j: (0, j))],
        out_specs=[pl.BlockSpec((tm, tn), lambda i, j: (i, j)), pl.BlockSpec((tm, D), lambda i, j: (i, 0))],
        out_shape=[jax.ShapeDtypeStruct((S, DP), F32), jax.ShapeDtypeStruct((S, D), MXU_DTYPE)],
        compiler_params=_params(("parallel", "arbitrary"), 48),
    )(x, pre_g, w_pad)


def _sgu_tri():
    return lax.broadcasted_iota(jnp.int32, (SGU_T, SGU_T), 0) >= lax.broadcasted_iota(jnp.int32, (SGU_T, SGU_T), 1)


def _sgu_mix(ws_m, vb, bfull, grp):
    mixed = bfull
    for g in range(SGU_G):
        mixed = mixed + jnp.where(grp == g, _mm(ws_m[g], vb), 0.0)
    return mixed


def _layernorm(v, g, b):
    xc = v - jnp.mean(v, axis=-1, keepdims=True)
    rstd = lax.rsqrt(jnp.mean(xc * xc, axis=-1, keepdims=True) + EPS)
    vhat = xc * rstd
    return vhat * g + b, vhat, rstd


def _sgu_fwd(proj, ln_g, ln_b, ws, bfull):
    S = proj.shape[0]
    tm = min(512, S)

    def body(u_ref, v_ref, z_ref, g_ref, b_ref, ws_ref, bf_ref, ya_ref):
        tri = _sgu_tri()
        ws_m = [jnp.where(tri, ws_ref[g], 0.0).astype(MXU_DTYPE) for g in range(SGU_G)]
        grp = lax.broadcasted_iota(jnp.int32, (1, BW), 1) // (BW // SGU_G)
        for b in range(tm // SGU_T):
            r = slice(b * SGU_T, (b + 1) * SGU_T)
            vln, _, _ = _layernorm(v_ref[r, :], g_ref[...], b_ref[...])
            mixed = _sgu_mix(ws_m, vln.astype(MXU_DTYPE), bf_ref[...], grp)
            z = z_ref[r, :]
            ya_ref[r, :] = (u_ref[r, :] * mixed * (z * _sigmoid(z))).astype(ya_ref.dtype)

    blk = lambda cb: pl.BlockSpec((tm, BW), lambda i, cb=cb: (i, cb))
    vec = pl.BlockSpec((1, BW), lambda i: (0, 0))
    return pl.pallas_call(
        body, name="sgu_fwd", grid=(S // tm,),
        in_specs=[blk(UA // BW), blk(VA // BW), blk(ZA // BW), vec, vec,
                  pl.BlockSpec((SGU_G, SGU_T, SGU_T), lambda i: (0, 0, 0)),
                  pl.BlockSpec((SGU_T, BW), lambda i: (0, 0))],
        out_specs=pl.BlockSpec((tm, BW), lambda i: (i, 0)),
        out_shape=jax.ShapeDtypeStruct((S, BW), MXU_DTYPE),
        compiler_params=_params(("parallel",), 32),
    )(proj, proj, proj, ln_g, ln_b, ws, bfull)


def _mla_prep_fwd(proj, gq, gkv, wuq, wuk, wuv, cos_t, sin_a, sin_b):
    S = proj.shape[0]
    tm = min(512, S)

    def body(p_ref, gq_ref, gkv_ref, wuq_ref, wuk_ref, wuv_ref, c_ref, sa_ref, sb_ref,
             q_ref, k_ref, v_ref, kt_ref, vt_ref):
        nq, _ = _rms(p_ref[:, 0:Q_RANK])
        nkv, _ = _rms(p_ref[:, Q_RANK:Q_RANK + KV_RANK])
        cq = (nq * gq_ref[...]).astype(MXU_DTYPE)
        ckv = (nkv * gkv_ref[...]).astype(MXU_DTYPE)
        qf = _mm(cq, wuq_ref[...])
        kf = _mm(ckv, wuk_ref[...])
        v = _mm(ckv, wuv_ref[...])
        v_ref[...] = v.astype(v_ref.dtype)
        vt_ref[...] = jnp.transpose(v).astype(vt_ref.dtype)
        c, sa, sb = c_ref[...], sa_ref[...], sb_ref[...]
        krr = _rope(p_ref[:, Q_RANK + KV_RANK:BW], c, sa, sb)
        for h in range(MLA_H):
            sl = slice(128 * h, 128 * (h + 1))
            q_ref[:, sl] = (_rope(qf[:, sl], c, sa, sb) * MLA_SCALE_LOG2).astype(q_ref.dtype)
            kh = kf[:, sl] + krr
            k_ref[:, sl] = kh.astype(k_ref.dtype)
            kt_ref[sl, :] = jnp.transpose(kh).astype(kt_ref.dtype)

    full = lambda a: pl.BlockSpec(a.shape, lambda i: (0,) * a.ndim)
    tab = pl.BlockSpec((tm, 128), lambda i: (i, 0))
    return pl.pallas_call(
        body, name="mla_prep_fwd", grid=(S // tm,),
        in_specs=[pl.BlockSpec((tm, BW), lambda i: (i, QD // BW)), full(gq), full(gkv), full(wuq), full(wuk),
                  full(wuv), tab, tab, tab],
        out_specs=[pl.BlockSpec((tm, 1024), lambda i: (i, 0)), pl.BlockSpec((tm, 1024), lambda i: (i, 0)),
                   pl.BlockSpec((tm, BW), lambda i: (i, 0)), pl.BlockSpec((1024, tm), lambda i: (0, i)),
                   pl.BlockSpec((BW, tm), lambda i: (0, i))],
        out_shape=[jax.ShapeDtypeStruct((S, 1024), MXU_DTYPE), jax.ShapeDtypeStruct((S, 1024), MXU_DTYPE),
                   jax.ShapeDtypeStruct((S, BW), MXU_DTYPE), jax.ShapeDtypeStruct((1024, S), MXU_DTYPE),
                   jax.ShapeDtypeStruct((BW, S), MXU_DTYPE)],
        compiler_params=_params(("parallel",), 48),
    )(proj, gq, gkv, wuq, wuk, wuv, cos_t, sin_a, sin_b)


MLA_SCALE = MLA_QK ** -0.5
MLA_SCALE_LOG2 = MLA_SCALE * 1.4426950408889634


def _tri_tables(n, q_major):
    if q_major:
        pairs = [(qi, ki) for qi in range(n) for ki in range(qi + 1)]
    else:
        pairs = [(qi, ki) for ki in range(n) for qi in range(ki, n)]
    return (jnp.asarray([p[0] for p in pairs], jnp.int32), jnp.asarray([p[1] for p in pairs], jnp.int32))


def _chunk_mask_t(T):
    kc = lax.broadcasted_iota(jnp.int32, (T, T), 0) >> 6
    qc = lax.broadcasted_iota(jnp.int32, (T, T), 1) >> 6
    return kc <= qc


FLASH_SUB = 512


def _col_reduce(op, reduce_fn, x):
    r = x.shape[0]
    while r > 8 and r % 16 == 0:
        r //= 2
        x = op(x[:r], x[r:])
    return reduce_fn(x, axis=0, keepdims=True)


def _flash_fwd(qh, kh, vt):
    S = qh.shape[0]
    T = min(512, S)
    TS = min(FLASH_SUB, T)
    n = S // T
    qt, kt = _tri_tables(n, q_major=True)

    def body(qt_ref, kt_ref, q_ref, k_ref, vt_ref, o_ref, lse_ref, m_scr, acc_scr):
        t = pl.program_id(1)
        qi, ki = qt_ref[t], kt_ref[t]
        row = lax.broadcasted_iota(jnp.int32, (128, 1), 0)
        ones_row = (64, 0)

        @pl.when(ki == 0)
        def _():
            m_scr[...] = jnp.full(m_scr.shape, NEG, F32)
            acc_scr[...] = jnp.zeros(acc_scr.shape, F32)

        def step(diag):
            for j in range(2):
                sl = slice(128 * j, 128 * (j + 1))
                for qs in range(T // TS):
                    cq = slice(qs * TS, (qs + 1) * TS)
                    qsub = q_ref[cq, sl]
                    for ks in range(T // TS):
                        if diag and ks > qs:
                            continue
                        rk = slice(ks * TS, (ks + 1) * TS)
                        s = _mm_nt(k_ref[rk, sl], qsub)
                        if diag and ks == qs:
                            s = jnp.where(_chunk_mask_t(TS), s, NEG)
                        m_old = m_scr[j, :, cq]
                        m_new = jnp.maximum(m_old, _col_reduce(jnp.maximum, jnp.max, s))
                        alpha = jnp.exp2(m_old - m_new)
                        p = jnp.exp2(s - m_new)
                        vt1 = jnp.where(row == ones_row[j], 1.0, vt_ref[:, rk]).astype(MXU_DTYPE)
                        acc_scr[j, :, cq] = alpha * acc_scr[j, :, cq] + _mm(vt1, p)
                        m_scr[j, :, cq] = m_new

        @pl.when(ki < qi)
        def _():
            step(False)

        @pl.when(ki == qi)
        def _():
            step(True)
            l = [acc_scr[j, ones_row[j]:ones_row[j] + 1, :] for j in range(2)]
            o_ref[...] = jnp.transpose(jnp.where(row < 64, acc_scr[0] / l[0], acc_scr[1] / l[1]))
            for j in range(2):
                lse_ref[0, j:j + 1, :] = m_scr[j] + jnp.log2(l[j])

    qmap = lambda hp, t, qt, kt: (qt[t], hp)
    kmap = lambda hp, t, qt, kt: (kt[t], hp)
    return pl.pallas_call(
        body, name="flash_fwd",
        grid_spec=pltpu.PrefetchScalarGridSpec(
            num_scalar_prefetch=2, grid=(MLA_H // 2, qt.shape[0]),
            in_specs=[pl.BlockSpec((T, 256), qmap), pl.BlockSpec((T, 256), kmap),
                      pl.BlockSpec((128, T), lambda hp, t, qt, kt: (hp, kt[t]))],
            out_specs=[pl.BlockSpec((T, 128), qmap),
                       pl.BlockSpec((1, 2, T), lambda hp, t, qt, kt: (hp, 0, qt[t]))],
            scratch_shapes=[pltpu.VMEM((2, 1, T), F32), pltpu.VMEM((2, 128, T), F32)]),
        out_shape=[jax.ShapeDtypeStruct((S, BW), F32), jax.ShapeDtypeStruct((MLA_H // 2, 2, S), F32)],
        compiler_params=_params(("parallel", "arbitrary"), 40),
    )(qt, kt, qh, kh, vt)


def _attn_delta(o, do):
    S = o.shape[0]
    T = min(512, S)

    def body(o_ref, do_ref, d_ref):
        head = lax.broadcasted_iota(jnp.int32, (1, 128), 1) // 64
        prod = o_ref[...] * do_ref[...]
        for j in range(2):
            d_ref[0, j:j + 1, :] = jnp.sum(jnp.transpose(jnp.where(head == j, prod, 0.0)), axis=0, keepdims=True)

    blk = pl.BlockSpec((T, 128), lambda hp, i: (i, hp))
    return pl.pallas_call(
        body, name="attn_delta", grid=(MLA_H // 2, S // T),
        in_specs=[blk, blk],
        out_specs=pl.BlockSpec((1, 2, T), lambda hp, i: (hp, 0, i)),
        out_shape=jax.ShapeDtypeStruct((MLA_H // 2, 2, S), F32),
        compiler_params=_params(("parallel", "parallel"), 32),
    )(o, do)


def _band_specs(S):
    q = pl.BlockSpec((BAND_TQ, 128), lambda hp, qi: (qi, QC // 128 + hp))
    ks = [pl.BlockSpec((BAND_TQ, 128), lambda hp, qi, t=t: (jnp.maximum(qi - 2 + t, 0), KC // 128 + hp))
          for t in range(3)]
    vs = [pl.BlockSpec((BAND_TQ, 128), lambda hp, qi, t=t: (jnp.maximum(qi - 2 + t, 0), VC // 128 + hp))
          for t in range(3)]
    bias = pl.BlockSpec((2, BAND_W, BAND_TQ), lambda hp, qi: (hp, 0, 0))
    return q, ks, vs, bias


LOG2E = 1.4426950408889634
LN2 = 0.6931471805599453
CA_SCALE = CA_DH ** -0.5


def _band_probs(q2j, kcat, bias2_j, valid):
    s = _mm_nt(kcat, q2j) + bias2_j
    s = jnp.where(valid, s, NEG)
    p = jnp.exp2(s - _col_reduce(jnp.maximum, jnp.max, s))
    return p * (1.0 / _col_reduce(jnp.add, jnp.sum, p))


def _band_valid(qi):
    tile = lax.broadcasted_iota(jnp.int32, (BAND_W, 1), 0) // BAND_TQ
    return tile + qi >= 2


def _band_fwd(proj, bias):
    S = proj.shape[0]

    def body(q_ref, k0, k1, k2, v0, v1, v2, b_ref, o_ref):
        qi = pl.program_id(1)
        head = lax.broadcasted_iota(jnp.int32, (1, 128), 1) // 64
        kcat = jnp.concatenate([k0[...], k1[...], k2[...]], axis=0).astype(MXU_DTYPE)
        vcat = jnp.concatenate([v0[...], v1[...], v2[...]], axis=0).astype(MXU_DTYPE)
        valid = _band_valid(qi)
        q2 = q_ref[...] * (CA_SCALE * LOG2E)
        outs = []
        for j in range(2):
            pn = _band_probs(jnp.where(head == j, q2, 0.0), kcat, b_ref[j], valid)
            outs.append(_mm_tn(vcat, pn))
        first = lax.broadcasted_iota(jnp.int32, (128, 1), 0) < 64
        o_ref[...] = jnp.transpose(jnp.where(first, outs[0], outs[1]))

    q, ks, vs, bspec = _band_specs(S)
    return pl.pallas_call(
        body, name="band_fwd", grid=(CA_H // 2, S // BAND_TQ),
        in_specs=[q, *ks, *vs, bspec],
        out_specs=pl.BlockSpec((BAND_TQ, 128), lambda hp, qi: (qi, hp)),
        out_shape=jax.ShapeDtypeStruct((S, BW), F32),
        compiler_params=_params(("parallel", "arbitrary"), 40),
    )(proj, proj, proj, proj, proj, proj, proj, bias)


def _merge_fwd(x, ya, ob, oc, proj, gate_b, wbr, wout, post_g):
    S = x.shape[0]
    tm = min(256, S)

    def body(x_ref, ya_ref, ob_ref, oc_ref, zb_ref, zc_ref, g0, g1, g2, gb_ref, wbr_ref, wo_ref, pg_ref,
             xo_ref, mg_ref, h_ref):
        zb, zc = zb_ref[...], zc_ref[...]
        ys = [ya_ref[...], ob_ref[...] * (zb * _sigmoid(zb)), oc_ref[...] * (zc * _sigmoid(zc))]
        merged = jnp.zeros((tm, D), F32)
        for i, g_ref in enumerate((g0, g1, g2)):
            merged = merged + _sigmoid(g_ref[...] + gb_ref[i:i + 1, :]) * _mm(ys[i], wbr_ref[i])
        mg_ref[...] = merged.astype(mg_ref.dtype)
        h = _mm(merged, wo_ref[...])
        h_ref[...] = h
        n, _ = _rms(h)
        xo_ref[...] = x_ref[...] + n * pg_ref[...]

    row = lambda w, cb=0: pl.BlockSpec((tm, w), lambda i, cb=cb: (i, cb))
    full = lambda a: pl.BlockSpec(a.shape, lambda i: (0,) * a.ndim)
    return pl.pallas_call(
        body, name="merge_fwd", grid=(S // tm,),
        in_specs=[row(D), row(BW), row(BW), row(BW), row(BW, ZB // BW), row(BW, ZC // BW),
                  row(D, 0), row(D, 1), row(D, 2), full(gate_b), full(wbr), full(wout), full(post_g)],
        out_specs=[row(D), row(D), row(D)],
        out_shape=[jax.ShapeDtypeStruct((S, D), F32), jax.ShapeDtypeStruct((S, D), MXU_DTYPE),
                   jax.ShapeDtypeStruct((S, D), F32)],
        compiler_params=_params(("parallel",), 56),
    )(x, ya, ob, oc, proj, proj, proj, proj, proj, gate_b, wbr, wout, post_g)


def _loss_fwd_bwd(y, target):
    S = y.shape[0]
    tm = min(512, S)

    def body(y_ref, t_ref, loss_ref, dy_ref):
        @pl.when(pl.program_id(0) == 0)
        def _():
            loss_ref[...] = jnp.zeros((1, 1), F32)
        err = y_ref[...] - t_ref[...]
        loss_ref[...] += 0.5 * jnp.sum(jnp.mean(err * err, axis=-1, keepdims=True), axis=0, keepdims=True)
        dy_ref[...] = err * (1.0 / D)

    row = pl.BlockSpec((tm, D), lambda i: (i, 0))
    return pl.pallas_call(
        body, name="loss", grid=(S // tm,),
        in_specs=[row, row],
        out_specs=[pl.BlockSpec((1, 1), lambda i: (0, 0)), row],
        out_shape=[jax.ShapeDtypeStruct((1, 1), F32), jax.ShapeDtypeStruct((S, D), F32)],
        compiler_params=_params(("arbitrary",), 32),
    )(y, target)


def _first_step_zero(refs, first):
    @pl.when(first)
    def _():
        for r in refs:
            r[...] = jnp.zeros(r.shape, r.dtype)


def _out_bwd(dxo, h, merged, post_g, wout_t):
    S = dxo.shape[0]
    tm = min(256, S)

    def body(d_ref, h_ref, mg_ref, pg_ref, wt_ref, dm_ref, dw_ref, dg_ref):
        _first_step_zero((dw_ref, dg_ref), pl.program_id(0) == 0)
        d = d_ref[...]
        hn, r = _rms(h_ref[...])
        dg_ref[...] += jnp.sum(d * hn, axis=0, keepdims=True)
        dh = _rms_bwd(d * pg_ref[...], hn, r)
        dm_ref[...] = _mm(dh, wt_ref[...])
        dw_ref[...] += _mm_tn(mg_ref[...], dh)

    row = pl.BlockSpec((tm, D), lambda i: (i, 0))
    full = lambda shape: pl.BlockSpec(shape, lambda i: (0,) * len(shape))
    return pl.pallas_call(
        body, name="out_bwd", grid=(S // tm,),
        in_specs=[row, row, row, full((1, D)), full((D, D))],
        out_specs=[row, full((D, D)), full((1, D))],
        out_shape=[jax.ShapeDtypeStruct((S, D), F32), jax.ShapeDtypeStruct((D, D), F32),
                   jax.ShapeDtypeStruct((1, D), F32)],
        compiler_params=_params(("arbitrary",), 40),
    )(dxo, h, merged, post_g, wout_t)


def _gate_bwd(dm, ya, ob, oc, proj, gate_b, wbr, wbr_t):
    S = dm.shape[0]
    tm = min(128, S)

    def body(dm_ref, ya_ref, ob_ref, oc_ref, zb_ref, zc_ref, g0, g1, g2, gb_ref, wbr_ref, wbt_ref,
             dg_ref, dzb_ref, dzc_ref, dya_ref, dob_ref, doc_ref, dwbr_ref, dgb_ref):
        _first_step_zero((dwbr_ref, dgb_ref), pl.program_id(0) == 0)
        dmv = dm_ref[...]
        zb, zc = zb_ref[...], zc_ref[...]
        sgb, sgc = _sigmoid(zb), _sigmoid(zc)
        ob, oc = ob_ref[...], oc_ref[...]
        ys = [ya_ref[...], (ob * (zb * sgb)).astype(MXU_DTYPE), (oc * (zc * sgc)).astype(MXU_DTYPE)]
        dys = []
        for i, g_ref in enumerate((g0, g1, g2)):
            br = _mm(ys[i], wbr_ref[i])
            gate = _sigmoid(g_ref[...] + gb_ref[i:i + 1, :])
            dgl = dmv * br * (gate * (1.0 - gate))
            dg_ref[:, D * i:D * (i + 1)] = dgl.astype(dg_ref.dtype)
            dgb_ref[i:i + 1, :] += jnp.sum(dgl, axis=0, keepdims=True)
            dbr = dmv * gate
            dys.append(_mm(dbr, wbt_ref[i]))
            dwbr_ref[i] += _mm_tn(dbr, ys[i])
        dya_ref[...] = dys[0]
        dob_ref[...] = dys[1] * (zb * sgb)
        dzb_ref[...] = (dys[1] * ob * (sgb * (1.0 + zb * (1.0 - sgb)))).astype(dzb_ref.dtype)
        doc_ref[...] = dys[2] * (zc * sgc)
        dzc_ref[...] = (dys[2] * oc * (sgc * (1.0 + zc * (1.0 - sgc)))).astype(dzc_ref.dtype)

    row = lambda w, cb=0: pl.BlockSpec((tm, w), lambda i, cb=cb: (i, cb))
    full = lambda a: pl.BlockSpec(a.shape, lambda i: (0,) * a.ndim)
    sds = lambda w, dt=F32: jax.ShapeDtypeStruct((S, w), dt)
    return pl.pallas_call(
        body, name="gate_bwd", grid=(S // tm,),
        in_specs=[row(D), row(BW), row(BW), row(BW), row(BW, ZB // BW), row(BW, ZC // BW),
                  row(D, 0), row(D, 1), row(D, 2), full(gate_b), full(wbr), full(wbr_t)],
        out_specs=[row(3 * D), row(BW), row(BW), row(BW), row(BW), row(BW),
                   pl.BlockSpec((N_BRANCH, D, BW), lambda i: (0, 0, 0)), pl.BlockSpec((N_BRANCH, D), lambda i: (0, 0))],
        out_shape=[sds(3 * D, MXU_DTYPE), sds(BW, MXU_DTYPE), sds(BW, MXU_DTYPE), sds(BW), sds(BW), sds(BW),
                   jax.ShapeDtypeStruct((N_BRANCH, D, BW), F32), jax.ShapeDtypeStruct((N_BRANCH, D), F32)],
        compiler_params=_params(("arbitrary",), 56),
    )(dm, ya, ob, oc, proj, proj, proj, proj, proj, gate_b, wbr, wbr_t)


def _band_bwd(proj, bias, do):
    S = proj.shape[0]

    def body(q_ref, k0, k1, k2, v0, v1, v2, b_ref, do_ref, dq_ref, dk_ref, dv_ref, db_ref):
        qi = pl.program_id(1)
        _first_step_zero((dk_ref, dv_ref, db_ref), qi == 0)
        head = lax.broadcasted_iota(jnp.int32, (1, 128), 1) // 64
        kcat = jnp.concatenate([k0[...], k1[...], k2[...]], axis=0).astype(MXU_DTYPE)
        vcat = jnp.concatenate([v0[...], v1[...], v2[...]], axis=0).astype(MXU_DTYPE)
        valid = _band_valid(qi)
        q2, dov = q_ref[...] * (CA_SCALE * LOG2E), do_ref[...]
        dqt = jnp.zeros((128, BAND_TQ), F32)
        dk = jnp.zeros((BAND_W, 128), F32)
        dv = jnp.zeros((BAND_W, 128), F32)
        for j in range(2):
            q2j = jnp.where(head == j, q2, 0.0).astype(MXU_DTYPE)
            doj = jnp.where(head == j, dov, 0.0).astype(MXU_DTYPE)
            pn = _band_probs(q2j, kcat, b_ref[j], valid)
            dv = dv + _mm(pn, doj)
            dp = _mm_nt(vcat, doj)
            ds = pn * (dp - _col_reduce(jnp.add, jnp.sum, pn * dp))
            db_ref[j] += ds
            dsb = ds.astype(MXU_DTYPE)
            dqt = dqt + _mm_tn(jnp.where(head == j, kcat, 0), dsb)
            dk = dk + _mm(dsb, q2j)
        dq_ref[...] = (jnp.transpose(dqt) * CA_SCALE).astype(dq_ref.dtype)
        for t in range(3):
            @pl.when(qi - 2 + t >= 0)
            def _(t=t):
                rows = pl.ds(pl.multiple_of((qi - 2 + t) * BAND_TQ, BAND_TQ), BAND_TQ)
                dk_ref[rows, :] += dk[t * BAND_TQ:(t + 1) * BAND_TQ] * LN2
                dv_ref[rows, :] += dv[t * BAND_TQ:(t + 1) * BAND_TQ]

    q, ks, vs, bspec = _band_specs(S)
    col = pl.BlockSpec((S, 128), lambda hp, qi: (0, hp))
    return pl.pallas_call(
        body, name="band_bwd", grid=(CA_H // 2, S // BAND_TQ),
        in_specs=[q, *ks, *vs, bspec, pl.BlockSpec((BAND_TQ, 128), lambda hp, qi: (qi, hp))],
        out_specs=[pl.BlockSpec((BAND_TQ, 128), lambda hp, qi: (qi, hp)), col, col, bspec],
        out_shape=[jax.ShapeDtypeStruct((S, BW), MXU_DTYPE), jax.ShapeDtypeStruct((S, BW), F32),
                   jax.ShapeDtypeStruct((S, BW), F32), jax.ShapeDtypeStruct((CA_H, BAND_W, BAND_TQ), F32)],
        compiler_params=_params(("parallel", "arbitrary"), 56),
    )(proj, proj, proj, proj, proj, proj, proj, bias, do)


BIAS_LO = REL_CLIP - (CHUNK - 1)
BIAS_FAR = 2 * REL_CLIP
BIAS_NEAR0 = BAND_W // 2


def _band_index(col0, ncol):
    i = lax.broadcasted_iota(jnp.int32, (BAND_TQ, ncol), 0)
    j = lax.broadcasted_iota(jnp.int32, (BAND_TQ, ncol), 1) + col0
    idx = jnp.clip(i + 2 * BAND_TQ - j, -REL_CLIP, REL_CLIP) + REL_CLIP
    ci, cj = i // CHUNK, j // CHUNK
    return jnp.where((ci <= cj) & (cj <= ci + LEFT_CHUNKS), idx, -1)


def _skew(x, right):
    row = lax.broadcasted_iota(jnp.int32, (BAND_TQ, 1), 0)
    for b in range(BAND_TQ.bit_length() - 1):
        shift = (1 << b) if right else BAND_W - (1 << b)
        x = jnp.where(((row >> b) & 1) == 1, pltpu.roll(x, shift, 1), x)
    return x


SKEW_NEAR0 = 2 * BAND_TQ - REL_CLIP + 1
SKEW_NEAR1 = 2 * BAND_TQ + CHUNK
SKEW_WRAP0 = BAND_W - (CHUNK - 1)


def _bias_build(rel_table):
    far = rel_table[:, BIAS_FAR:]
    base = jnp.concatenate([jnp.broadcast_to(far, (CA_H, SKEW_NEAR0)), rel_table[:, BIAS_LO:BIAS_FAR][:, ::-1],
                            jnp.broadcast_to(far, (CA_H, BAND_W - SKEW_NEAR1))], axis=1)[:, None, :]

    def body(base_ref, out_ref):
        valid = _band_index(0, BAND_W) >= 0
        for h in range(CA_H):
            tile = _skew(jnp.broadcast_to(base_ref[h] * LOG2E, (BAND_TQ, BAND_W)), right=True)
            out_ref[h] = jnp.transpose(jnp.where(valid, tile, NEG))

    return pl.pallas_call(
        body, name="bias_build",
        out_shape=jax.ShapeDtypeStruct((CA_H, BAND_W, BAND_TQ), F32),
        in_specs=[pl.BlockSpec(memory_space=pltpu.VMEM)],
        out_specs=pl.BlockSpec(memory_space=pltpu.VMEM),
        compiler_params=pltpu.CompilerParams(vmem_limit_bytes=40 << 20),
    )(base)


def _bias_fold(db):
    def body(db_ref, sums_ref, far_ref):
        col = lax.broadcasted_iota(jnp.int32, (1, BAND_W), 1)
        is_far = (col < SKEW_NEAR0) | (col >= SKEW_WRAP0)
        for h in range(CA_H):
            sums = jnp.sum(_skew(jnp.transpose(db_ref[h]), right=False), axis=0, keepdims=True)
            sums_ref[h] = sums
            far_ref[h] = jnp.broadcast_to(jnp.sum(jnp.where(is_far, sums, 0.0), axis=1, keepdims=True), (1, 128))

    sums, far = pl.pallas_call(
        body, name="bias_fold",
        out_shape=[jax.ShapeDtypeStruct((CA_H, 1, BAND_W), F32), jax.ShapeDtypeStruct((CA_H, 1, 128), F32)],
        in_specs=[pl.BlockSpec(memory_space=pltpu.VMEM)],
        out_specs=[pl.BlockSpec(memory_space=pltpu.VMEM), pl.BlockSpec(memory_space=pltpu.VMEM)],
        compiler_params=pltpu.CompilerParams(vmem_limit_bytes=40 << 20),
    )(db)
    near = sums[:, 0, SKEW_NEAR0:SKEW_NEAR1][:, ::-1]
    return jnp.concatenate([jnp.zeros((CA_H, BIAS_LO), F32), near, far[:, 0, :1]], axis=1)


def _flash_bwd(qh, kh, ktr, vh, lse, delta, do):
    S = qh.shape[0]
    T = min(512, S)
    TS = min(FLASH_SUB, T)
    n = S // T
    qt, kt = _tri_tables(n, q_major=False)

    def body(qt_ref, kt_ref, q_ref, k_ref, ktr_ref, v_ref, lse_ref, dl_ref, do_ref,
             dqt_ref, dk_ref, dv_ref, dk_scr, dv_scr):
        t = pl.program_id(1)
        qi, ki = qt_ref[t], kt_ref[t]

        @pl.when(t == 0)
        def _():
            dqt_ref[...] = jnp.zeros(dqt_ref.shape, F32)

        @pl.when(qi == ki)
        def _():
            dk_scr[...] = jnp.zeros(dk_scr.shape, F32)
            dv_scr[...] = jnp.zeros(dv_scr.shape, F32)

        def step(diag):
            head = lax.broadcasted_iota(jnp.int32, (1, 128), 1) // 64
            for j in range(2):
                sl = slice(128 * j, 128 * (j + 1))
                for qs in range(T // TS):
                    cq = slice(qs * TS, (qs + 1) * TS)
                    qsub = q_ref[cq, sl]
                    doj = jnp.where(head == j, do_ref[cq, :], 0.0).astype(MXU_DTYPE)
                    lse, dlt = lse_ref[0, j:j + 1, cq], dl_ref[0, j:j + 1, cq]
                    cols = pl.ds(pl.multiple_of(qi * T + qs * TS, TS), TS)
                    for ks in range(T // TS):
                        if diag and ks > qs:
                            continue
                        rk = slice(ks * TS, (ks + 1) * TS)
                        s = _mm_nt(k_ref[rk, sl], qsub)
                        if diag and ks == qs:
                            s = jnp.where(_chunk_mask_t(TS), s, NEG)
                        p = jnp.exp2(s - lse)
                        dv_scr[rk, :] += _mm(p, doj)
                        dp = _mm_nt(v_ref[rk, :], doj)
                        ds = (p * (dp - dlt)).astype(MXU_DTYPE)
                        dk_scr[rk, sl] += _mm(ds, qsub)
                        dqt_ref[sl, cols] += _mm(ktr_ref[sl, rk], ds) * MLA_SCALE

        @pl.when(qi > ki)
        def _():
            step(False)

        @pl.when(qi == ki)
        def _():
            step(True)

        @pl.when(qi == n - 1)
        def _():
            dk_ref[...] = dk_scr[...] * 0.6931471805599453
            dv_ref[...] = dv_scr[...]

    qmap = lambda hp, t, qt, kt: (qt[t], hp)
    kmap = lambda hp, t, qt, kt: (kt[t], hp)
    stat = pl.BlockSpec((1, 2, T), lambda hp, t, qt, kt: (hp, 0, qt[t]))
    return pl.pallas_call(
        body, name="flash_bwd",
        grid_spec=pltpu.PrefetchScalarGridSpec(
            num_scalar_prefetch=2, grid=(MLA_H // 2, qt.shape[0]),
            in_specs=[pl.BlockSpec((T, 256), qmap), pl.BlockSpec((T, 256), kmap),
                      pl.BlockSpec((256, T), lambda hp, t, qt, kt: (hp, kt[t])), pl.BlockSpec((T, 128), kmap),
                      stat, stat, pl.BlockSpec((T, 128), qmap)],
            out_specs=[pl.BlockSpec((256, S), lambda hp, t, qt, kt: (hp, 0)), pl.BlockSpec((T, 256), kmap),
                       pl.BlockSpec((T, 128), kmap)],
            scratch_shapes=[pltpu.VMEM((T, 256), F32), pltpu.VMEM((T, 128), F32)]),
        out_shape=[jax.ShapeDtypeStruct((1024, S), F32), jax.ShapeDtypeStruct((S, 1024), F32),
                   jax.ShapeDtypeStruct((S, BW), F32)],
        compiler_params=_params(("parallel", "arbitrary"), 56),
    )(qt, kt, qh, kh, ktr, vh, lse, delta, do)


def _mla_prep_bwd(proj, dqf, dkf, dvf, gq, gkv, wuq_t, wuk_t, wuv_t, cos_t, sin_a, sin_b):
    S = proj.shape[0]
    tm = min(512, S)

    def body(p_ref, dq_ref, dk_ref, dv_ref, gq_ref, gkv_ref, wq_ref, wk_ref, wv_ref, c_ref, sa_ref, sb_ref,
             db_ref, dwq_ref, dwk_ref, dwv_ref, dgq_ref, dgkv_ref):
        _first_step_zero((dwq_ref, dwk_ref, dwv_ref, dgq_ref, dgkv_ref), pl.program_id(0) == 0)
        c, sa, sb = c_ref[...], sa_ref[...], sb_ref[...]
        nq, rq = _rms(p_ref[:, 0:Q_RANK])
        nkv, rkv = _rms(p_ref[:, Q_RANK:Q_RANK + KV_RANK])
        cq = (nq * gq_ref[...]).astype(MXU_DTYPE)
        ckv = (nkv * gkv_ref[...]).astype(MXU_DTYPE)
        dkr = jnp.zeros((tm, 128), F32)
        dq_pre = []
        for h in range(MLA_H):
            sl = slice(128 * h, 128 * (h + 1))
            dq_pre.append(_rope_t(jnp.transpose(dq_ref[sl, :]), c, sa, sb).astype(MXU_DTYPE))
            dkr = dkr + dk_ref[:, sl]
        dq_pre = jnp.concatenate(dq_pre, axis=1)
        dcq = _mm(dq_pre, wq_ref[...])
        dwq_ref[...] += _mm_tn(dq_pre, cq)
        dgq_ref[...] += jnp.sum(dcq * nq, axis=0, keepdims=True)
        db_ref[:, 0:Q_RANK] = _rms_bwd(dcq * gq_ref[...], nq, rq).astype(db_ref.dtype)
        dk = dk_ref[...].astype(MXU_DTYPE)
        dv = dv_ref[...].astype(MXU_DTYPE)
        dckv = _mm(dk, wk_ref[...]) + _mm(dv, wv_ref[...])
        dwk_ref[...] += _mm_tn(dk, ckv)
        dwv_ref[...] += _mm_tn(dv, ckv)
        dgkv_ref[...] += jnp.sum(dckv * nkv, axis=0, keepdims=True)
        db_ref[:, Q_RANK:Q_RANK + KV_RANK] = _rms_bwd(dckv * gkv_ref[...], nkv, rkv).astype(db_ref.dtype)
        lane = lax.broadcasted_iota(jnp.int32, (1, 128), 1)
        rope_lanes = (lane >= MLA_NOPE) & (lane < MLA_QK)
        db_ref[:, Q_RANK + KV_RANK:BW] = _rope_t(jnp.where(rope_lanes, dkr, 0.0), c, sa, sb).astype(db_ref.dtype)

    full = lambda a: pl.BlockSpec(a.shape, lambda i: (0,) * a.ndim)
    fulls = lambda shape: pl.BlockSpec(shape, lambda i: (0,) * len(shape))
    tab = pl.BlockSpec((tm, 128), lambda i: (i, 0))
    row = lambda w, cb=0: pl.BlockSpec((tm, w), lambda i, cb=cb: (i, cb))
    return pl.pallas_call(
        body, name="mla_prep_bwd", grid=(S // tm,),
        in_specs=[row(BW, QD // BW), pl.BlockSpec((1024, tm), lambda i: (0, i)), row(1024), row(BW),
                  full(gq), full(gkv), full(wuq_t),
                  full(wuk_t), full(wuv_t), tab, tab, tab],
        out_specs=[row(BW), fulls((1024, Q_RANK)), fulls((1024, KV_RANK)), fulls((BW, KV_RANK)),
                   fulls((1, Q_RANK)), fulls((1, KV_RANK))],
        out_shape=[jax.ShapeDtypeStruct((S, BW), MXU_DTYPE), jax.ShapeDtypeStruct((1024, Q_RANK), F32),
                   jax.ShapeDtypeStruct((1024, KV_RANK), F32), jax.ShapeDtypeStruct((BW, KV_RANK), F32),
                   jax.ShapeDtypeStruct((1, Q_RANK), F32), jax.ShapeDtypeStruct((1, KV_RANK), F32)],
        compiler_params=_params(("arbitrary",), 56),
    )(proj, dqf, dkf, dvf, gq, gkv, wuq_t, wuk_t, wuv_t, cos_t, sin_a, sin_b)


def _sgu_bwd(proj, dya, ln_g, ln_b, ws, bfull):
    S = proj.shape[0]
    tm = min(512, S)

    def body(u_ref, v_ref, z_ref, dy_ref, g_ref, b_ref, ws_ref, bf_ref,
             da_ref, dws_ref, dbf_ref, dlg_ref, dlb_ref, dbs_ref):
        i = pl.program_id(0)
        _first_step_zero((dws_ref, dbf_ref, dlg_ref, dlb_ref, dbs_ref), i == 0)
        tri = _sgu_tri()
        ws_m = [jnp.where(tri, ws_ref[g], 0.0).astype(MXU_DTYPE) for g in range(SGU_G)]
        grp = lax.broadcasted_iota(jnp.int32, (1, BW), 1) // (BW // SGU_G)
        for b in range(tm // SGU_T):
            r = slice(b * SGU_T, (b + 1) * SGU_T)
            vln, vhat, rstd = _layernorm(v_ref[r, :], g_ref[...], b_ref[...])
            vb = vln.astype(MXU_DTYPE)
            mixed = _sgu_mix(ws_m, vb, bf_ref[...], grp)
            u, z, dy = u_ref[r, :], z_ref[r, :], dy_ref[r, :]
            sg = _sigmoid(z)
            sz = z * sg
            da_ref[r, 0:BW] = (dy * mixed * sz).astype(da_ref.dtype)
            da_ref[r, 2 * BW:3 * BW] = (dy * u * mixed * (sg * (1.0 + z * (1.0 - sg)))).astype(da_ref.dtype)
            dmix = dy * u * sz
            dbf_ref[...] += dmix
            dvln = jnp.zeros((SGU_T, BW), F32)
            for g in range(SGU_G):
                dmg = jnp.where(grp == g, dmix, 0.0).astype(MXU_DTYPE)
                dvln = dvln + _mm_tn(ws_m[g], dmg)
                dws_ref[g] += jnp.where(tri, _mm_nt(dmg, vb), 0.0)
            dlg_ref[...] += jnp.sum(dvln * vhat, axis=0, keepdims=True)
            dlb_ref[...] += jnp.sum(dvln, axis=0, keepdims=True)
            dvh = dvln * g_ref[...]
            da_ref[r, BW:2 * BW] = (rstd * (dvh - jnp.mean(dvh, axis=-1, keepdims=True)
                                            - vhat * jnp.mean(dvh * vhat, axis=-1, keepdims=True))).astype(da_ref.dtype)

        @pl.when(i == pl.num_programs(0) - 1)
        def _():
            dbf = dbf_ref[...]
            for g in range(SGU_G):
                dbs_ref[:, g:g + 1] = jnp.sum(jnp.where(grp == g, dbf, 0.0), axis=1, keepdims=True)

    blk = lambda cb: pl.BlockSpec((tm, BW), lambda i, cb=cb: (i, cb))
    fulls = lambda shape: pl.BlockSpec(shape, lambda i: (0,) * len(shape))
    return pl.pallas_call(
        body, name="sgu_bwd", grid=(S // tm,),
        in_specs=[blk(UA // BW), blk(VA // BW), blk(ZA // BW), blk(0), fulls((1, BW)), fulls((1, BW)),
                  fulls((SGU_G, SGU_T, SGU_T)), fulls((SGU_T, BW))],
        out_specs=[pl.BlockSpec((tm, 3 * BW), lambda i: (i, 0)), fulls((SGU_G, SGU_T, SGU_T)),
                   fulls((SGU_T, BW)), fulls((1, BW)), fulls((1, BW)), fulls((SGU_T, SGU_G))],
        out_shape=[jax.ShapeDtypeStruct((S, 3 * BW), MXU_DTYPE), jax.ShapeDtypeStruct((SGU_G, SGU_T, SGU_T), F32),
                   jax.ShapeDtypeStruct((SGU_T, BW), F32), jax.ShapeDtypeStruct((1, BW), F32),
                   jax.ShapeDtypeStruct((1, BW), F32), jax.ShapeDtypeStruct((SGU_T, SGU_G), F32)],
        compiler_params=_params(("arbitrary",), 40),
    )(proj, proj, proj, dya, ln_g, ln_b, ws, bfull)


DX_TK = 3 * BW


def _inproj_bwd_dx(dgl, da, small, w_t, x, pre_g, dxo):
    S = x.shape[0]
    tm = min(512, S)
    nk = DP // DX_TK

    def body(g_ref, a_ref, s0, s1, s2, s3, s4, s5, w_ref, x_ref, pg_ref, dxo_ref, dx_ref, dg_ref, acc):
        i, k = pl.program_id(0), pl.program_id(1)
        _first_step_zero((dg_ref,), (i == 0) & (k == 0))

        @pl.when(k == 0)
        def _():
            acc[...] = jnp.zeros(acc.shape, F32)

        @pl.when(k < 2)
        def _():
            acc[...] += _mm(g_ref[...], w_ref[...])

        @pl.when(k == 2)
        def _():
            acc[...] += _mm(a_ref[...], w_ref[...])

        for kk, trio in ((3, (s0, s1, s2)), (4, (s3, s4, s5))):
            @pl.when(k == kk)
            def _(trio=trio):
                t = acc[...]
                for n, ref in enumerate(trio):
                    t = t + _mm(ref[...], w_ref[n * BW:(n + 1) * BW, :])
                acc[...] = t

        @pl.when(k == nk - 1)
        def _():
            n, r = _rms(x_ref[...])
            dxn = acc[...]
            dg_ref[...] += jnp.sum(dxn * n, axis=0, keepdims=True)
            dx_ref[...] = dxo_ref[...] + _rms_bwd(dxn * pg_ref[...], n, r)

    row = pl.BlockSpec((tm, D), lambda i, k: (i, 0))
    vec = pl.BlockSpec((1, D), lambda i, k: (0, 0))
    fixed = lambda w: pl.BlockSpec((tm, w), lambda i, k: (i, 0))
    return pl.pallas_call(
        body, name="inproj_bwd_dx", grid=(S // tm, nk),
        in_specs=[pl.BlockSpec((tm, DX_TK), lambda i, k: (i, jnp.minimum(k, 1))), fixed(DX_TK)]
                 + [fixed(BW)] * 6 + [pl.BlockSpec((DX_TK, D), lambda i, k: (k, 0)), row, vec, row],
        out_specs=[row, vec],
        out_shape=[jax.ShapeDtypeStruct((S, D), F32), jax.ShapeDtypeStruct((1, D), F32)],
        scratch_shapes=[pltpu.VMEM((tm, D), F32)],
        compiler_params=_params(("arbitrary", "arbitrary"), 56),
    )(dgl, da, *small, w_t, x, pre_g, dxo)


def _matmul_tn(a, b):
    S, W = a.shape
    K = b.shape[1]
    tk, tn = min(2048, S), BW

    def body(a_ref, b_ref, o_ref):
        _first_step_zero((o_ref,), pl.program_id(1) == 0)
        o_ref[...] += _mm_tn(a_ref[...], b_ref[...])

    return pl.pallas_call(
        body, name="matmul_tn", grid=(W // tn, S // tk),
        in_specs=[pl.BlockSpec((tk, tn), lambda j, k: (k, j)), pl.BlockSpec((tk, K), lambda j, k: (k, 0))],
        out_specs=pl.BlockSpec((tn, K), lambda j, k: (j, 0)),
        out_shape=jax.ShapeDtypeStruct((W, K), F32),
        compiler_params=_params(("parallel", "arbitrary"), 48),
    )(a, b)


def _pad_rows_w_in(wt):
    z = lambda n: jnp.zeros((n, wt.shape[1]), wt.dtype)
    return jnp.concatenate([wt[4512:], wt[:1920], z(64), wt[1920:1952], z(32), wt[1952:4512]], axis=0)


def _unpad_rows_w_in(dwt):
    a = dwt[3 * D:]
    return jnp.concatenate([a[:1920], a[1984:2016], a[2048:], dwt[:3 * D]], axis=0)


def _pad_head_rows(wt, width):
    k = wt.shape[1]
    return jnp.pad(wt.reshape(MLA_H, width, k), ((0, 0), (0, 128 - width), (0, 0))).reshape(MLA_H * 128, k)


def _unpad_head_rows(wt, width):
    k = wt.shape[1]
    return wt.reshape(MLA_H, 128, k)[:, :width].reshape(MLA_H * width, k)


def _rope_tables(S):
    half = MLA_ROPE // 2
    inv = 10000.0 ** (-jnp.arange(half, dtype=F32) / half)
    ang = jnp.arange(S, dtype=F32)[:, None] * inv[None, :]
    cos, sin = jnp.cos(ang), jnp.sin(ang)
    one = lambda n: jnp.ones((S, n), F32)
    zero = lambda n: jnp.zeros((S, n), F32)
    cos_t = jnp.concatenate([one(MLA_NOPE), cos, cos, one(128 - MLA_QK)], axis=1)
    sin_a = jnp.concatenate([zero(MLA_NOPE), -sin, zero(128 - MLA_NOPE - half)], axis=1)
    sin_b = jnp.concatenate([zero(MLA_NOPE + half), sin, zero(128 - MLA_QK)], axis=1)
    return cos_t, sin_a, sin_b


SHARDED = ("w_in", "mla_w_uq", "mla_w_ukv", "w_branch", "w_out")
SHARD_PACK = SHARDED + ("gate_b",)
REPLICATED = ("pre_g", "post_g", "sgu_ln_g", "sgu_ln_b", "sgu_w", "sgu_b", "mla_q_norm_g", "mla_kv_norm_g",
              "ca_rel_bias")
OUT_ORDER = ("w_in", "pre_g", "post_g", "sgu_ln_g", "sgu_ln_b", "sgu_w", "sgu_b", "mla_q_norm_g", "mla_kv_norm_g",
             "mla_w_uq", "mla_w_ukv", "ca_rel_bias", "w_branch", "gate_b", "w_out")
TRANSPOSED = {"w_in": (0, 2, 1), "mla_w_uq": (0, 2, 1), "mla_w_ukv": (0, 2, 1), "w_branch": (0, 1, 3, 2)}


def _canon(name, a):
    return jnp.transpose(a, TRANSPOSED[name]) if name in TRANSPOSED else a


def _to_rows(flat, width, mult):
    n = flat.shape[-1]
    pad = (-n) % (width * mult)
    flat = jnp.pad(flat, [(0, 0)] * (flat.ndim - 1) + [(0, pad)])
    return flat.reshape(flat.shape[:-1] + (-1, width))


def _gather_shards(shards, dtype):
    names = list(shards)
    flat = jnp.concatenate([shards[n].astype(dtype).reshape(-1) for n in names])
    got = _all_gather(_to_rows(flat, 128, 16)).reshape(N_DEV, -1)
    out, off = {}, 0
    for n in names:
        size = shards[n].size
        out[n] = got[:, off:off + size].reshape((N_DEV,) + shards[n].shape)
        off += size
    return out


def _pack(d, names):
    return _to_rows(jnp.concatenate([d[n].reshape(-1) for n in names]), 1024, 128)


def _unpack(p, like, names):
    flat, out, off = p.reshape(-1), {}, 0
    for n in names:
        out[n] = flat[off:off + like[n].size].reshape(like[n].shape)
        off += like[n].size
    return out


def kernel(x, w_in, pre_g, post_g, sgu_ln_g, sgu_ln_b, sgu_w, sgu_b, mla_q_norm_g, mla_kv_norm_g, mla_w_uq, mla_w_ukv, ca_rel_bias, w_branch, gate_b, w_out, loss_target, m_w_in, m_pre_g, m_post_g, m_sgu_ln_g, m_sgu_ln_b, m_sgu_w, m_sgu_b, m_mla_q_norm_g, m_mla_kv_norm_g, m_mla_w_uq, m_mla_w_ukv, m_ca_rel_bias, m_w_branch, m_gate_b, m_w_out, v_w_in, v_pre_g, v_post_g, v_sgu_ln_g, v_sgu_ln_b, v_sgu_w, v_sgu_b, v_mla_q_norm_g, v_mla_kv_norm_g, v_mla_w_uq, v_mla_w_ukv, v_ca_rel_bias, v_w_branch, v_gate_b, v_w_out):
    weights = dict(w_in=w_in, pre_g=pre_g, post_g=post_g, sgu_ln_g=sgu_ln_g, sgu_ln_b=sgu_ln_b, sgu_w=sgu_w,
                   sgu_b=sgu_b, mla_q_norm_g=mla_q_norm_g, mla_kv_norm_g=mla_kv_norm_g, mla_w_uq=mla_w_uq,
                   mla_w_ukv=mla_w_ukv, ca_rel_bias=ca_rel_bias, w_branch=w_branch, gate_b=gate_b, w_out=w_out)
    mom_m = dict(w_in=m_w_in, pre_g=m_pre_g, post_g=m_post_g, sgu_ln_g=m_sgu_ln_g, sgu_ln_b=m_sgu_ln_b,
                 sgu_w=m_sgu_w, sgu_b=m_sgu_b, mla_q_norm_g=m_mla_q_norm_g, mla_kv_norm_g=m_mla_kv_norm_g,
                 mla_w_uq=m_mla_w_uq, mla_w_ukv=m_mla_w_ukv, ca_rel_bias=m_ca_rel_bias, w_branch=m_w_branch,
                 gate_b=m_gate_b, w_out=m_w_out)
    mom_v = dict(w_in=v_w_in, pre_g=v_pre_g, post_g=v_post_g, sgu_ln_g=v_sgu_ln_g, sgu_ln_b=v_sgu_ln_b,
                 sgu_w=v_sgu_w, sgu_b=v_sgu_b, mla_q_norm_g=v_mla_q_norm_g, mla_kv_norm_g=v_mla_kv_norm_g,
                 mla_w_uq=v_mla_w_uq, mla_w_ukv=v_mla_w_ukv, ca_rel_bias=v_ca_rel_bias, w_branch=v_w_branch,
                 gate_b=v_gate_b, w_out=v_w_out)
    depth = w_in.shape[0]
    S = x.shape[1]
    xs = x.reshape(S, D)

    cw = {n: _canon(n, weights[n]) for n in SHARD_PACK}
    gw = _gather_shards({n: cw[n] for n in SHARDED}, MXU_DTYPE)
    gw.update(_gather_shards({"gate_b": gate_b}, F32))
    cos_t, sin_a, sin_b = _rope_tables(S)

    layers = []
    for l in range(depth):
        w_t = _pad_rows_w_in(gw["w_in"][:, l].reshape(D_IN, D))
        wuq_t = _pad_head_rows(gw["mla_w_uq"][:, l].reshape(MLA_H * MLA_QK, Q_RANK), MLA_QK)
        ukv_t = gw["mla_w_ukv"][:, l]
        wuk_t = jnp.pad(ukv_t[:, :MLA_NOPE], ((0, 0), (0, 128 - MLA_NOPE), (0, 0))).reshape(MLA_H * 128, KV_RANK)
        wuv_t = ukv_t[:, MLA_NOPE:].reshape(BW, KV_RANK)
        wbr_t = jnp.swapaxes(gw["w_branch"][:, l], 0, 1).reshape(N_BRANCH, D, BW)
        lw = dict(
            w_t=w_t, w_pad=w_t.T, wuq_t=wuq_t, wuq=wuq_t.T, wuk_t=wuk_t, wuk=wuk_t.T, wuv_t=wuv_t, wuv=wuv_t.T,
            wbr_t=wbr_t, wbr=jnp.swapaxes(wbr_t, 1, 2), wout=gw["w_out"][:, l].reshape(D, D),
            gate_b=jnp.swapaxes(gw["gate_b"][:, l], 0, 1).reshape(N_BRANCH, D),
            pre_g=pre_g[l][None], post_g=post_g[l][None], ln_g=sgu_ln_g[l][None], ln_b=sgu_ln_b[l][None],
            ws=sgu_w[l], bfull=jnp.repeat(sgu_b[l].T, BW // SGU_G, axis=1),
            gq=mla_q_norm_g[l][None], gkv=mla_kv_norm_g[l][None], bias=_bias_build(ca_rel_bias[l]))
        layers.append(lw)

    saved = []
    h_x = xs
    for lw in layers:
        proj, xn = _inproj_fwd(h_x, lw["pre_g"], lw["w_pad"])
        ya = _sgu_fwd(proj, lw["ln_g"], lw["ln_b"], lw["ws"], lw["bfull"])
        qh, kh, vh, ktr, vtr = _mla_prep_fwd(proj, lw["gq"], lw["gkv"], lw["wuq"], lw["wuk"], lw["wuv"],
                                             cos_t, sin_a, sin_b)
        ob, lse = _flash_fwd(qh, kh, vtr)
        oc = _band_fwd(proj, lw["bias"])
        x_new, merged, hh = _merge_fwd(h_x, ya, ob, oc, proj, lw["gate_b"], lw["wbr"], lw["wout"], lw["post_g"])
        saved.append(dict(x=h_x, proj=proj, xn=xn, ya=ya, qh=qh, kh=kh, vh=vh, ktr=ktr, ob=ob, lse=lse, oc=oc,
                          merged=merged, h=hh))
        h_x = x_new

    loss_part, dx = _loss_fwd_bwd(h_x, loss_target.reshape(S, D))
    loss = lax.psum(loss_part[0, 0], MESH_AXES)

    rows = {n: [None] * depth for n in SHARD_PACK}
    rep = {n: [None] * depth for n in REPLICATED}
    for l in reversed(range(depth)):
        lw, sv = layers[l], saved[l]
        proj = sv["proj"]
        dmerged, dw_out, dg_post = _out_bwd(dx, sv["h"], sv["merged"], lw["post_g"], lw["wout"].T)
        dgl, dzb, dzc, dya, dob, doc, dwbr_t, dgb = _gate_bwd(
            dmerged, sv["ya"], sv["ob"], sv["oc"], proj, lw["gate_b"], lw["wbr"], lw["wbr_t"])
        dqc, dkc, dvc, dbias = _band_bwd(proj, lw["bias"], doc)
        drel = _bias_fold(dbias)[:, :2 * REL_CLIP + 1]
        dqf, dkf, dvf = _flash_bwd(sv["qh"], sv["kh"], sv["ktr"], sv["vh"], sv["lse"],
                                   _attn_delta(sv["ob"], dob), dob)
        db, dwuq_t, dwuk_t, dwuv_t, dgq, dgkv = _mla_prep_bwd(
            proj, dqf, dkf, dvf, lw["gq"], lw["gkv"], lw["wuq_t"], lw["wuk_t"], lw["wuv_t"], cos_t, sin_a, sin_b)
        da, dws, _, dlg, dlb, dbs = _sgu_bwd(proj, dya, lw["ln_g"], lw["ln_b"], lw["ws"], lw["bfull"])
        small = (db, dzb, dqc, dkc, dvc, dzc)
        dx, dg_pre = _inproj_bwd_dx(dgl, da, small, lw["w_t"], sv["x"], lw["pre_g"], dx)
        dw_t = jnp.concatenate([_matmul_tn(a, sv["xn"]) for a in (dgl, da) + small], axis=0)

        rows["w_in"][l] = _unpad_rows_w_in(dw_t).reshape(N_DEV, -1)
        rows["mla_w_uq"][l] = _unpad_head_rows(dwuq_t, MLA_QK).reshape(N_DEV, -1)
        dk3 = dwuk_t.reshape(MLA_H, 128, KV_RANK)[:, :MLA_NOPE]
        dv3 = dwuv_t.reshape(MLA_H, 64, KV_RANK)
        rows["mla_w_ukv"][l] = jnp.concatenate([dk3, dv3], axis=1).reshape(N_DEV, -1)
        rows["w_branch"][l] = jnp.swapaxes(dwbr_t.reshape(N_BRANCH, N_DEV, D // N_DEV, BW), 0, 1).reshape(N_DEV, -1)
        rows["w_out"][l] = dw_out.reshape(N_DEV, -1)
        rows["gate_b"][l] = jnp.swapaxes(dgb.reshape(N_BRANCH, N_DEV, D // N_DEV), 0, 1).reshape(N_DEV, -1)
        rep["pre_g"][l] = dg_pre[0]
        rep["post_g"][l] = dg_post[0]
        rep["sgu_ln_g"][l] = dlg[0]
        rep["sgu_ln_b"][l] = dlb[0]
        rep["sgu_w"][l] = dws
        rep["sgu_b"][l] = dbs.T
        rep["mla_q_norm_g"][l] = dgq[0]
        rep["mla_kv_norm_g"][l] = dgkv[0]
        rep["ca_rel_bias"][l] = drel
    grad_x = dx.reshape(x.shape)

    send = jnp.concatenate([jnp.stack(rows[n], axis=1).reshape(N_DEV, -1) for n in SHARD_PACK], axis=1)
    recv = _exchange(_to_rows(send.astype(MXU_DTYPE), 1024, 128))
    cm = {n: _canon(n, mom_m[n]) for n in SHARD_PACK}
    cv = {n: _canon(n, mom_v[n]) for n in SHARD_PACK}
    outs_sh = _reduce_adamw(recv, _pack(cw, SHARD_PACK), _pack(cm, SHARD_PACK), _pack(cv, SHARD_PACK))
    outs_sh = [{n: _canon(n, a) for n, a in _unpack(p, cw, SHARD_PACK).items()} for p in outs_sh]

    part = _pack({n: jnp.stack(rep[n]) for n in REPLICATED}, REPLICATED)
    allp = _all_gather(part)
    outs_rep = _reduce_adamw(allp, _pack(weights, REPLICATED), _pack(mom_m, REPLICATED), _pack(mom_v, REPLICATED))
    outs_rep = [_unpack(p, weights, REPLICATED) for p in outs_rep]

    outs = [{**a, **b} for a, b in zip(outs_sh, outs_rep)]
    return (loss, grad_x, *[o[n] for o in outs for n in OUT_ORDER])
```

```python
import functools

import jax
import jax.numpy as jnp
from jax import lax
from jax.experimental import pallas as pl
from jax.experimental.pallas import tpu as pltpu

F32 = jnp.float32
MXU_DTYPE = jnp.bfloat16
EPS = 1e-6
NEG = -1e30
MESH_AXES = ("x", "y", "c")
N_DEV = 8

D = 1024
BW = 512
N_BRANCH = 3
CHUNK = 64
SGU_T = 128
SGU_G = 8
MLA_H = 8
MLA_NOPE = 64
MLA_ROPE = 32
MLA_QK = MLA_NOPE + MLA_ROPE
Q_RANK = 256
KV_RANK = 128
CA_H = 8
CA_DH = 64
LEFT_CHUNKS = 8
REL_CLIP = 128
D_IN = 7584

G_OFF, UA, VA, ZA, QD, ZB, QC, KC, VC, ZC, DP = 0, 3072, 3584, 4096, 4608, 5120, 5632, 6144, 6656, 7168, 7680
BAND_TQ = 256
BAND_W = 3 * BAND_TQ

ADAM_LR, ADAM_B1, ADAM_B2, ADAM_EPS, ADAM_WD, ADAM_STEP = 0.001, 0.9, 0.999, 1e-08, 0.01, 10


def _params(sem, mib):
    return pltpu.CompilerParams(dimension_semantics=sem, vmem_limit_bytes=mib << 20)


def _mm(a, b):
    return jnp.dot(a.astype(MXU_DTYPE), b.astype(MXU_DTYPE), preferred_element_type=F32)


def _mm_nt(a, b):
    return lax.dot_general(a.astype(MXU_DTYPE), b.astype(MXU_DTYPE), (((1,), (1,)), ((), ())),
                           preferred_element_type=F32)


def _mm_tn(a, b):
    return lax.dot_general(a.astype(MXU_DTYPE), b.astype(MXU_DTYPE), (((0,), (0,)), ((), ())),
                           preferred_element_type=F32)


def _sigmoid(z):
    return 1.0 / (1.0 + jnp.exp(-z))


def _rms(x):
    r = lax.rsqrt(jnp.mean(x * x, axis=-1, keepdims=True) + EPS)
    return x * r, r


def _rms_bwd(dn, n, r):
    return r * (dn - n * jnp.mean(dn * n, axis=-1, keepdims=True))


def _rope(b, c, sa, sb):
    return b * c + pltpu.roll(b, 112, 1) * sa + pltpu.roll(b, 16, 1) * sb


def _rope_t(d, c, sa, sb):
    return d * c + pltpu.roll(d * sa, 16, 1) + pltpu.roll(d * sb, 112, 1)


def _all_gather(blk):
    R = blk.shape[0]

    def body(x_ref, out_ref, send_sems, recv_sems, local_sem):
        x, y, c = lax.axis_index("x"), lax.axis_index("y"), lax.axis_index("c")
        me, sibling = (x, y, c), (x, y, 1 - c)
        chips = [(1 - x, y), (x, 1 - y), (1 - x, 1 - y)]

        def slot(px, py, pc):
            return out_ref.at[4 * px + 2 * py + pc]

        def copy(k, block, to, src=None):
            return pltpu.make_async_remote_copy(
                src_ref=slot(*block) if src is None else src, dst_ref=slot(*block),
                send_sem=send_sems.at[k], recv_sem=recv_sems.at[k],
                device_id=to, device_id_type=pl.DeviceIdType.MESH)

        mine = pltpu.make_async_copy(x_ref, slot(*me), local_sem)
        mine.start()
        first = [copy(0, me, sibling, src=x_ref)]
        first += [copy(1 + j, me, (*chip, c), src=x_ref) for j, chip in enumerate(chips)]
        for cp in first:
            cp.start()
        passed = [copy(4 + j, (*chip, c), sibling) for j, chip in enumerate(chips)]
        for j, chip in enumerate(chips):
            copy(1 + j, (*chip, c), me).wait_recv()
            passed[j].start()
        copy(0, sibling, me).wait_recv()
        for j, chip in enumerate(chips):
            copy(4 + j, (*chip, 1 - c), me).wait_recv()
        for cp in first + passed:
            cp.wait_send()
        mine.wait()

    return pl.pallas_call(
        body, name="all_gather",
        out_shape=jax.ShapeDtypeStruct((N_DEV,) + blk.shape, blk.dtype),
        in_specs=[pl.BlockSpec(memory_space=pl.ANY)],
        out_specs=pl.BlockSpec(memory_space=pl.ANY),
        scratch_shapes=[pltpu.SemaphoreType.DMA((7,)), pltpu.SemaphoreType.DMA((7,)), pltpu.SemaphoreType.DMA(())],
    )(blk)


def _peer_copies(s_ref, r_ref, send_sems, recv_sems, local_sem, same_source):
    x, y, c = lax.axis_index("x"), lax.axis_index("y"), lax.axis_index("c")
    me = 4 * x + 2 * y + c

    def src(pid):
        return s_ref if same_source else s_ref.at[pid]

    def peer(k):
        px = 1 - x if (k >> 2) & 1 else x
        py = 1 - y if (k >> 1) & 1 else y
        pc = 1 - c if k & 1 else c
        return (px, py, pc), 4 * px + 2 * py + pc

    def remote(k, row):
        pos, pid = peer(k)
        return pltpu.make_async_remote_copy(
            src_ref=src(pid), dst_ref=r_ref.at[me if row is None else pid],
            send_sem=send_sems.at[k], recv_sem=recv_sems.at[k],
            device_id=pos, device_id_type=pl.DeviceIdType.MESH)

    def local():
        return pltpu.make_async_copy(src(me), r_ref.at[me], local_sem)

    def start():
        local().start()
        for k in range(1, N_DEV):
            remote(k, None).start()

    def wait():
        for k in range(1, N_DEV):
            remote(k, "peer").wait_recv()
        for k in range(1, N_DEV):
            remote(k, None).wait_send()
        local().wait()

    return start, wait


COMM_SEMS = [pltpu.SemaphoreType.DMA((N_DEV,)), pltpu.SemaphoreType.DMA((N_DEV,)), pltpu.SemaphoreType.DMA(())]


def _hosted_comm(s_ref, r_ref, sems, same_source):
    start, wait = _peer_copies(s_ref, r_ref, *sems, same_source=same_source)
    i, j = pl.program_id(0), pl.program_id(1)

    @pl.when((i == 0) & (j == 0))
    def _():
        start()

    @pl.when((i == pl.num_programs(0) - 1) & (j == pl.num_programs(1) - 1))
    def _():
        wait()


def _exchange(send):
    def body(s_ref, r_ref, send_sems, recv_sems, local_sem):
        start, wait = _peer_copies(s_ref, r_ref, send_sems, recv_sems, local_sem, same_source=False)
        start()
        wait()

    return pl.pallas_call(
        body, name="grad_exchange",
        out_shape=jax.ShapeDtypeStruct(send.shape, send.dtype),
        in_specs=[pl.BlockSpec(memory_space=pl.ANY)],
        out_specs=pl.BlockSpec(memory_space=pl.ANY),
        scratch_shapes=list(COMM_SEMS),
    )(send)


def _reduce_adamw(recv, w, m, v):
    R = w.shape[0]
    tr = next(t for t in (128, 64, 32, 16) if R % t == 0)
    c1 = 1.0 - ADAM_B1 ** ADAM_STEP
    c2 = 1.0 - ADAM_B2 ** ADAM_STEP

    def body(r_ref, w_ref, m_ref, v_ref, g_ref, d_ref, nm_ref, nv_ref):
        g = r_ref[0].astype(F32)
        for s in range(1, N_DEV):
            g = g + r_ref[s].astype(F32)
        m2 = ADAM_B1 * m_ref[...] + (1.0 - ADAM_B1) * g
        v2 = ADAM_B2 * v_ref[...] + (1.0 - ADAM_B2) * (g * g)
        m_hat = m2 / c1
        v_hat = v2 / c2
        g_ref[...] = g
        d_ref[...] = -ADAM_LR * (m_hat / (jnp.sqrt(v_hat) + ADAM_EPS) + ADAM_WD * w_ref[...])
        nm_ref[...] = m2
        nv_ref[...] = v2

    row = pl.BlockSpec((tr, 1024), lambda i: (i, 0))
    return pl.pallas_call(
        body, name="reduce_adamw", grid=(R // tr,),
        in_specs=[pl.BlockSpec((N_DEV, tr, 1024), lambda i: (0, i, 0)), row, row, row],
        out_specs=[row, row, row, row],
        out_shape=[jax.ShapeDtypeStruct((R, 1024), F32)] * 4,
        compiler_params=_params(("parallel",), 40),
    )(recv, w, m, v)


def _inproj_fwd(x, pre_g, w_pad):
    S = x.shape[0]
    tm, tn = min(1024, S), 1536

    def body(x_ref, g_ref, w_ref, proj_ref, xn_ref):
        @pl.when(pl.program_id(1) == 0)
        def _():
            n, _ = _rms(x_ref[...])
            xn_ref[...] = (n * g_ref[...]).astype(xn_ref.dtype)
        proj_ref[...] = jnp.dot(xn_ref[...], w_ref[...], preferred_element_type=F32)

    return pl.pallas_call(
        body, name="inproj_fwd", grid=(S // tm, DP // tn),
        in_specs=[pl.BlockSpec((tm, D), lambda i, j: (i, 0)), pl.BlockSpec((1, D), lambda i, j: (0, 0)),
                  pl.BlockSpec((D, tn), lambda i, j: (0, j))],
        out_specs=[pl.BlockSpec((tm, tn), lambda i, j: (i, j)), pl.BlockSpec((tm, D), lambda i, j: (i, 0))],
        out_shape=[jax.ShapeDtypeStruct((S, DP), F32), jax.ShapeDtypeStruct((S, D), MXU_DTYPE)],
        compiler_params=_params(("parallel", "arbitrary"), 48),
    )(x, pre_g, w_pad)


def _sgu_tri():
    return lax.broadcasted_iota(jnp.int32, (SGU_T, SGU_T), 0) >= lax.broadcasted_iota(jnp.int32, (SGU_T, SGU_T), 1)


def _sgu_mix(ws_m, vb, bfull, grp):
    mixed = bfull
    for g in range(SGU_G):
        mixed = mixed + jnp.where(grp == g, _mm(ws_m[g], vb), 0.0)
    return mixed


def _layernorm(v, g, b):
    xc = v - jnp.mean(v, axis=-1, keepdims=True)
    rstd = lax.rsqrt(jnp.mean(xc * xc, axis=-1, keepdims=True) + EPS)
    vhat = xc * rstd
    return vhat * g + b, vhat, rstd


def _sgu_fwd(proj, ln_g, ln_b, ws, bfull):
    S = proj.shape[0]
    tm = min(512, S)

    def body(u_ref, v_ref, z_ref, g_ref, b_ref, ws_ref, bf_ref, ya_ref):
        tri = _sgu_tri()
        ws_m = [jnp.where(tri, ws_ref[g], 0.0).astype(MXU_DTYPE) for g in range(SGU_G)]
        grp = lax.broadcasted_iota(jnp.int32, (1, BW), 1) // (BW // SGU_G)
        for b in range(tm // SGU_T):
            r = slice(b * SGU_T, (b + 1) * SGU_T)
            vln, _, _ = _layernorm(v_ref[r, :], g_ref[...], b_ref[...])
            mixed = _sgu_mix(ws_m, vln.astype(MXU_DTYPE), bf_ref[...], grp)
            z = z_ref[r, :]
            ya_ref[r, :] = (u_ref[r, :] * mixed * (z * _sigmoid(z))).astype(ya_ref.dtype)

    blk = lambda cb: pl.BlockSpec((tm, BW), lambda i, cb=cb: (i, cb))
    vec = pl.BlockSpec((1, BW), lambda i: (0, 0))
    return pl.pallas_call(
        body, name="sgu_fwd", grid=(S // tm,),
        in_specs=[blk(UA // BW), blk(VA // BW), blk(ZA // BW), vec, vec,
                  pl.BlockSpec((SGU_G, SGU_T, SGU_T), lambda i: (0, 0, 0)),
                  pl.BlockSpec((SGU_T, BW), lambda i: (0, 0))],
        out_specs=pl.BlockSpec((tm, BW), lambda i: (i, 0)),
        out_shape=jax.ShapeDtypeStruct((S, BW), MXU_DTYPE),
        compiler_params=_params(("parallel",), 32),
    )(proj, proj, proj, ln_g, ln_b, ws, bfull)


def _mla_prep_fwd(proj, gq, gkv, wuq, wuk, wuv, cos_t, sin_a, sin_b):
    S = proj.shape[0]
    tm = min(512, S)

    def body(p_ref, gq_ref, gkv_ref, wuq_ref, wuk_ref, wuv_ref, c_ref, sa_ref, sb_ref,
             q_ref, k_ref, v_ref, kt_ref, vt_ref):
        nq, _ = _rms(p_ref[:, 0:Q_RANK])
        nkv, _ = _rms(p_ref[:, Q_RANK:Q_RANK + KV_RANK])
        cq = (nq * gq_ref[...]).astype(MXU_DTYPE)
        ckv = (nkv * gkv_ref[...]).astype(MXU_DTYPE)
        qf = _mm(cq, wuq_ref[...])
        kf = _mm(ckv, wuk_ref[...])
        v = _mm(ckv, wuv_ref[...])
        v_ref[...] = v.astype(v_ref.dtype)
        vt_ref[...] = jnp.transpose(v).astype(vt_ref.dtype)
        c, sa, sb = c_ref[...], sa_ref[...], sb_ref[...]
        krr = _rope(p_ref[:, Q_RANK + KV_RANK:BW], c, sa, sb)
        for h in range(MLA_H):
            sl = slice(128 * h, 128 * (h + 1))
            q_ref[:, sl] = (_rope(qf[:, sl], c, sa, sb) * MLA_SCALE_LOG2).astype(q_ref.dtype)
            kh = kf[:, sl] + krr
            k_ref[:, sl] = kh.astype(k_ref.dtype)
            kt_ref[sl, :] = jnp.transpose(kh).astype(kt_ref.dtype)

    full = lambda a: pl.BlockSpec(a.shape, lambda i: (0,) * a.ndim)
    tab = pl.BlockSpec((tm, 128), lambda i: (i, 0))
    return pl.pallas_call(
        body, name="mla_prep_fwd", grid=(S // tm,),
        in_specs=[pl.BlockSpec((tm, BW), lambda i: (i, QD // BW)), full(gq), full(gkv), full(wuq), full(wuk),
                  full(wuv), tab, tab, tab],
        out_specs=[pl.BlockSpec((tm, 1024), lambda i: (i, 0)), pl.BlockSpec((tm, 1024), lambda i: (i, 0)),
                   pl.BlockSpec((tm, BW), lambda i: (i, 0)), pl.BlockSpec((1024, tm), lambda i: (0, i)),
                   pl.BlockSpec((BW, tm), lambda i: (0, i))],
        out_shape=[jax.ShapeDtypeStruct((S, 1024), MXU_DTYPE), jax.ShapeDtypeStruct((S, 1024), MXU_DTYPE),
                   jax.ShapeDtypeStruct((S, BW), MXU_DTYPE), jax.ShapeDtypeStruct((1024, S), MXU_DTYPE),
                   jax.ShapeDtypeStruct((BW, S), MXU_DTYPE)],
        compiler_params=_params(("parallel",), 48),
    )(proj, gq, gkv, wuq, wuk, wuv, cos_t, sin_a, sin_b)


MLA_SCALE = MLA_QK ** -0.5
MLA_SCALE_LOG2 = MLA_SCALE * 1.4426950408889634


def _tri_tables(n, q_major):
    if q_major:
        pairs = [(qi, ki) for qi in range(n) for ki in range(qi + 1)]
    else:
        pairs = [(qi, ki) for ki in range(n) for qi in range(ki, n)]
    return (jnp.asarray([p[0] for p in pairs], jnp.int32), jnp.asarray([p[1] for p in pairs], jnp.int32))


def _chunk_mask_t(T):
    kc = lax.broadcasted_iota(jnp.int32, (T, T), 0) >> 6
    qc = lax.broadcasted_iota(jnp.int32, (T, T), 1) >> 6
    return kc <= qc


FLASH_SUB = 512


def _col_reduce(op, reduce_fn, x):
    r = x.shape[0]
    while r > 8 and r % 16 == 0:
        r //= 2
        x = op(x[:r], x[r:])
    return reduce_fn(x, axis=0, keepdims=True)


def _flash_fwd(qh, kh, vt, gather=None):
    S = qh.shape[0]
    T = min(512, S)
    TS = min(FLASH_SUB, T)
    n = S // T
    qt, kt = _tri_tables(n, q_major=True)

    def body(qt_ref, kt_ref, q_ref, k_ref, vt_ref, *rest):
        if gather is None:
            o_ref, lse_ref, m_scr, acc_scr = rest
        else:
            g_ref, o_ref, lse_ref, got_ref, m_scr, acc_scr, *sems = rest
            _hosted_comm(g_ref, got_ref, sems, same_source=True)
        t = pl.program_id(1)
        qi, ki = qt_ref[t], kt_ref[t]
        row = lax.broadcasted_iota(jnp.int32, (128, 1), 0)
        ones_row = (64, 0)

        @pl.when(ki == 0)
        def _():
            m_scr[...] = jnp.full(m_scr.shape, NEG, F32)
            acc_scr[...] = jnp.zeros(acc_scr.shape, F32)

        def step(diag):
            for j in range(2):
                sl = slice(128 * j, 128 * (j + 1))
                for qs in range(T // TS):
                    cq = slice(qs * TS, (qs + 1) * TS)
                    qsub = q_ref[cq, sl]
                    for ks in range(T // TS):
                        if diag and ks > qs:
                            continue
                        rk = slice(ks * TS, (ks + 1) * TS)
                        s = _mm_nt(k_ref[rk, sl], qsub)
                        if diag and ks == qs:
                            s = jnp.where(_chunk_mask_t(TS), s, NEG)
                        m_old = m_scr[j, :, cq]
                        m_new = jnp.maximum(m_old, _col_reduce(jnp.maximum, jnp.max, s))
                        alpha = jnp.exp2(m_old - m_new)
                        p = jnp.exp2(s - m_new)
                        vt1 = jnp.where(row == ones_row[j], 1.0, vt_ref[:, rk]).astype(MXU_DTYPE)
                        acc_scr[j, :, cq] = alpha * acc_scr[j, :, cq] + _mm(vt1, p)
                        m_scr[j, :, cq] = m_new

        @pl.when(ki < qi)
        def _():
            step(False)

        @pl.when(ki == qi)
        def _():
            step(True)
            l = [acc_scr[j, ones_row[j]:ones_row[j] + 1, :] for j in range(2)]
            o_ref[...] = jnp.transpose(jnp.where(row < 64, acc_scr[0] / l[0], acc_scr[1] / l[1]))
            for j in range(2):
                lse_ref[0, j:j + 1, :] = m_scr[j] + jnp.log2(l[j])

    qmap = lambda hp, t, qt, kt: (qt[t], hp)
    kmap = lambda hp, t, qt, kt: (kt[t], hp)
    hbm = pl.BlockSpec(memory_space=pl.ANY)
    hosted = gather is not None
    got_shape = [jax.ShapeDtypeStruct((N_DEV,) + gather.shape, gather.dtype)] if hosted else []
    return pl.pallas_call(
        body, name="flash_fwd_gather" if hosted else "flash_fwd",
        grid_spec=pltpu.PrefetchScalarGridSpec(
            num_scalar_prefetch=2, grid=(MLA_H // 2, qt.shape[0]),
            in_specs=[pl.BlockSpec((T, 256), qmap), pl.BlockSpec((T, 256), kmap),
                      pl.BlockSpec((128, T), lambda hp, t, qt, kt: (hp, kt[t]))] + [hbm] * hosted,
            out_specs=[pl.BlockSpec((T, 128), qmap),
                       pl.BlockSpec((1, 2, T), lambda hp, t, qt, kt: (hp, 0, qt[t]))] + [hbm] * hosted,
            scratch_shapes=[pltpu.VMEM((2, 1, T), F32), pltpu.VMEM((2, 128, T), F32)] + COMM_SEMS * hosted),
        out_shape=[jax.ShapeDtypeStruct((S, BW), F32), jax.ShapeDtypeStruct((MLA_H // 2, 2, S), F32)] + got_shape,
        compiler_params=_params(("arbitrary", "arbitrary"), 40),
    )(qt, kt, qh, kh, vt, *([gather] * hosted))


def _attn_delta(o, do):
    S = o.shape[0]
    T = min(512, S)

    def body(o_ref, do_ref, d_ref):
        head = lax.broadcasted_iota(jnp.int32, (1, 128), 1) // 64
        prod = o_ref[...] * do_ref[...]
        for j in range(2):
            d_ref[0, j:j + 1, :] = jnp.sum(jnp.transpose(jnp.where(head == j, prod, 0.0)), axis=0, keepdims=True)

    blk = pl.BlockSpec((T, 128), lambda hp, i: (i, hp))
    return pl.pallas_call(
        body, name="attn_delta", grid=(MLA_H // 2, S // T),
        in_specs=[blk, blk],
        out_specs=pl.BlockSpec((1, 2, T), lambda hp, i: (hp, 0, i)),
        out_shape=jax.ShapeDtypeStruct((MLA_H // 2, 2, S), F32),
        compiler_params=_params(("parallel", "parallel"), 32),
    )(o, do)


def _band_specs(S):
    q = pl.BlockSpec((BAND_TQ, 128), lambda hp, qi: (qi, QC // 128 + hp))
    ks = [pl.BlockSpec((BAND_TQ, 128), lambda hp, qi, t=t: (jnp.maximum(qi - 2 + t, 0), KC // 128 + hp))
          for t in range(3)]
    vs = [pl.BlockSpec((BAND_TQ, 128), lambda hp, qi, t=t: (jnp.maximum(qi - 2 + t, 0), VC // 128 + hp))
          for t in range(3)]
    bias = pl.BlockSpec((2, BAND_TQ, BAND_W), lambda hp, qi: (hp, 0, 0))
    return q, ks, vs, bias


LOG2E = 1.4426950408889634
LN2 = 0.6931471805599453
CA_SCALE = CA_DH ** -0.5


def _band_probs(q2j, kcat, bias2_j, valid):
    s = _mm_nt(q2j, kcat) + bias2_j
    s = jnp.where(valid, s, NEG)
    p = jnp.exp2(s - jnp.max(s, axis=1, keepdims=True))
    return p * (1.0 / jnp.sum(p, axis=1, keepdims=True))


def _band_valid(qi):
    tile = lax.broadcasted_iota(jnp.int32, (1, BAND_W), 1) // BAND_TQ
    return tile + qi >= 2


def _band_fwd(proj, bias):
    S = proj.shape[0]

    def body(q_ref, k0, k1, k2, v0, v1, v2, b_ref, o_ref):
        qi = pl.program_id(1)
        head = lax.broadcasted_iota(jnp.int32, (1, 128), 1) // 64
        kcat = jnp.concatenate([k0[...], k1[...], k2[...]], axis=0).astype(MXU_DTYPE)
        vcat = jnp.concatenate([v0[...], v1[...], v2[...]], axis=0).astype(MXU_DTYPE)
        valid = _band_valid(qi)
        q2 = q_ref[...] * (CA_SCALE * LOG2E)
        o = jnp.zeros((BAND_TQ, 128), F32)
        for j in range(2):
            pn = _band_probs(jnp.where(head == j, q2, 0.0), kcat, b_ref[j], valid)
            o = o + _mm(pn, jnp.where(head == j, vcat, 0))
        o_ref[...] = o

    q, ks, vs, bspec = _band_specs(S)
    return pl.pallas_call(
        body, name="band_fwd", grid=(CA_H // 2, S // BAND_TQ),
        in_specs=[q, *ks, *vs, bspec],
        out_specs=pl.BlockSpec((BAND_TQ, 128), lambda hp, qi: (qi, hp)),
        out_shape=jax.ShapeDtypeStruct((S, BW), F32),
        compiler_params=_params(("parallel", "arbitrary"), 40),
    )(proj, proj, proj, proj, proj, proj, proj, bias)


def _merge_fwd(x, ya, ob, oc, proj, gate_b, wbr, wout, post_g):
    S = x.shape[0]
    tm = min(256, S)

    def body(x_ref, ya_ref, ob_ref, oc_ref, zb_ref, zc_ref, g0, g1, g2, gb_ref, wbr_ref, wo_ref, pg_ref,
             xo_ref, mg_ref, h_ref):
        zb, zc = zb_ref[...], zc_ref[...]
        ys = [ya_ref[...], ob_ref[...] * (zb * _sigmoid(zb)), oc_ref[...] * (zc * _sigmoid(zc))]
        merged = jnp.zeros((tm, D), F32)
        for i, g_ref in enumerate((g0, g1, g2)):
            merged = merged + _sigmoid(g_ref[...] + gb_ref[i:i + 1, :]) * _mm(ys[i], wbr_ref[i])
        mg_ref[...] = merged.astype(mg_ref.dtype)
        h = _mm(merged, wo_ref[...])
        h_ref[...] = h
        n, _ = _rms(h)
        xo_ref[...] = x_ref[...] + n * pg_ref[...]

    row = lambda w, cb=0: pl.BlockSpec((tm, w), lambda i, cb=cb: (i, cb))
    full = lambda a: pl.BlockSpec(a.shape, lambda i: (0,) * a.ndim)
    return pl.pallas_call(
        body, name="merge_fwd", grid=(S // tm,),
        in_specs=[row(D), row(BW), row(BW), row(BW), row(BW, ZB // BW), row(BW, ZC // BW),
                  row(D, 0), row(D, 1), row(D, 2), full(gate_b), full(wbr), full(wout), full(post_g)],
        out_specs=[row(D), row(D), row(D)],
        out_shape=[jax.ShapeDtypeStruct((S, D), F32), jax.ShapeDtypeStruct((S, D), MXU_DTYPE),
                   jax.ShapeDtypeStruct((S, D), F32)],
        compiler_params=_params(("parallel",), 56),
    )(x, ya, ob, oc, proj, proj, proj, proj, proj, gate_b, wbr, wout, post_g)


def _loss_fwd_bwd(y, target):
    S = y.shape[0]
    tm = min(512, S)

    def body(y_ref, t_ref, loss_ref, dy_ref):
        @pl.when(pl.program_id(0) == 0)
        def _():
            loss_ref[...] = jnp.zeros((1, 1), F32)
        err = y_ref[...] - t_ref[...]
        loss_ref[...] += 0.5 * jnp.sum(jnp.mean(err * err, axis=-1, keepdims=True), axis=0, keepdims=True)
        dy_ref[...] = err * (1.0 / D)

    row = pl.BlockSpec((tm, D), lambda i: (i, 0))
    return pl.pallas_call(
        body, name="loss", grid=(S // tm,),
        in_specs=[row, row],
        out_specs=[pl.BlockSpec((1, 1), lambda i: (0, 0)), row],
        out_shape=[jax.ShapeDtypeStruct((1, 1), F32), jax.ShapeDtypeStruct((S, D), F32)],
        compiler_params=_params(("arbitrary",), 32),
    )(y, target)


def _first_step_zero(refs, first):
    @pl.when(first)
    def _():
        for r in refs:
            r[...] = jnp.zeros(r.shape, r.dtype)


def _out_bwd(dxo, h, merged, post_g, wout_t):
    S = dxo.shape[0]
    tm = min(256, S)

    def body(d_ref, h_ref, mg_ref, pg_ref, wt_ref, dm_ref, dw_ref, dg_ref):
        _first_step_zero((dw_ref, dg_ref), pl.program_id(0) == 0)
        d = d_ref[...]
        hn, r = _rms(h_ref[...])
        dg_ref[...] += jnp.sum(d * hn, axis=0, keepdims=True)
        dh = _rms_bwd(d * pg_ref[...], hn, r)
        dm_ref[...] = _mm(dh, wt_ref[...])
        dw_ref[...] += _mm_tn(mg_ref[...], dh)

    row = pl.BlockSpec((tm, D), lambda i: (i, 0))
    full = lambda shape: pl.BlockSpec(shape, lambda i: (0,) * len(shape))
    return pl.pallas_call(
        body, name="out_bwd", grid=(S // tm,),
        in_specs=[row, row, row, full((1, D)), full((D, D))],
        out_specs=[row, full((D, D)), full((1, D))],
        out_shape=[jax.ShapeDtypeStruct((S, D), F32), jax.ShapeDtypeStruct((D, D), F32),
                   jax.ShapeDtypeStruct((1, D), F32)],
        compiler_params=_params(("arbitrary",), 40),
    )(dxo, h, merged, post_g, wout_t)


def _gate_bwd(dm, ya, ob, oc, proj, gate_b, wbr, wbr_t):
    S = dm.shape[0]
    tm = min(128, S)

    def body(dm_ref, ya_ref, ob_ref, oc_ref, zb_ref, zc_ref, g0, g1, g2, gb_ref, wbr_ref, wbt_ref,
             dg_ref, dzb_ref, dzc_ref, dya_ref, dob_ref, doc_ref, dwbr_ref, dgb_ref):
        _first_step_zero((dwbr_ref, dgb_ref), pl.program_id(0) == 0)
        dmv = dm_ref[...]
        zb, zc = zb_ref[...], zc_ref[...]
        sgb, sgc = _sigmoid(zb), _sigmoid(zc)
        ob, oc = ob_ref[...], oc_ref[...]
        ys = [ya_ref[...], (ob * (zb * sgb)).astype(MXU_DTYPE), (oc * (zc * sgc)).astype(MXU_DTYPE)]
        dys = []
        for i, g_ref in enumerate((g0, g1, g2)):
            br = _mm(ys[i], wbr_ref[i])
            gate = _sigmoid(g_ref[...] + gb_ref[i:i + 1, :])
            dgl = dmv * br * (gate * (1.0 - gate))
            dg_ref[:, D * i:D * (i + 1)] = dgl.astype(dg_ref.dtype)
            dgb_ref[i:i + 1, :] += jnp.sum(dgl, axis=0, keepdims=True)
            dbr = dmv * gate
            dys.append(_mm(dbr, wbt_ref[i]))
            dwbr_ref[i] += _mm_tn(dbr, ys[i])
        dya_ref[...] = dys[0]
        dob_ref[...] = dys[1] * (zb * sgb)
        dzb_ref[...] = (dys[1] * ob * (sgb * (1.0 + zb * (1.0 - sgb)))).astype(dzb_ref.dtype)
        doc_ref[...] = dys[2] * (zc * sgc)
        dzc_ref[...] = (dys[2] * oc * (sgc * (1.0 + zc * (1.0 - sgc)))).astype(dzc_ref.dtype)

    row = lambda w, cb=0: pl.BlockSpec((tm, w), lambda i, cb=cb: (i, cb))
    full = lambda a: pl.BlockSpec(a.shape, lambda i: (0,) * a.ndim)
    sds = lambda w, dt=F32: jax.ShapeDtypeStruct((S, w), dt)
    return pl.pallas_call(
        body, name="gate_bwd", grid=(S // tm,),
        in_specs=[row(D), row(BW), row(BW), row(BW), row(BW, ZB // BW), row(BW, ZC // BW),
                  row(D, 0), row(D, 1), row(D, 2), full(gate_b), full(wbr), full(wbr_t)],
        out_specs=[row(3 * D), row(BW), row(BW), row(BW), row(BW), row(BW),
                   pl.BlockSpec((N_BRANCH, D, BW), lambda i: (0, 0, 0)), pl.BlockSpec((N_BRANCH, D), lambda i: (0, 0))],
        out_shape=[sds(3 * D, MXU_DTYPE), sds(BW, MXU_DTYPE), sds(BW, MXU_DTYPE), sds(BW), sds(BW), sds(BW),
                   jax.ShapeDtypeStruct((N_BRANCH, D, BW), F32), jax.ShapeDtypeStruct((N_BRANCH, D), F32)],
        compiler_params=_params(("arbitrary",), 56),
    )(dm, ya, ob, oc, proj, proj, proj, proj, proj, gate_b, wbr, wbr_t)


def _band_bwd(proj, bias, do):
    S = proj.shape[0]

    def body(q_ref, k0, k1, k2, v0, v1, v2, b_ref, do_ref, dq_ref, dk_ref, dv_ref, db_ref):
        qi = pl.program_id(1)
        _first_step_zero((dk_ref, dv_ref, db_ref), qi == 0)
        head = lax.broadcasted_iota(jnp.int32, (1, 128), 1) // 64
        kcat = jnp.concatenate([k0[...], k1[...], k2[...]], axis=0).astype(MXU_DTYPE)
        vcat = jnp.concatenate([v0[...], v1[...], v2[...]], axis=0).astype(MXU_DTYPE)
        valid = _band_valid(qi)
        q2, dov = q_ref[...] * (CA_SCALE * LOG2E), do_ref[...]
        dq = jnp.zeros((BAND_TQ, 128), F32)
        dk = jnp.zeros((BAND_W, 128), F32)
        dv = jnp.zeros((BAND_W, 128), F32)
        for j in range(2):
            q2j = jnp.where(head == j, q2, 0.0).astype(MXU_DTYPE)
            doj = jnp.where(head == j, dov, 0.0).astype(MXU_DTYPE)
            pn = _band_probs(q2j, kcat, b_ref[j], valid)
            dv = dv + _mm_tn(pn, doj)
            dp = _mm_nt(doj, vcat)
            ds = pn * (dp - jnp.sum(pn * dp, axis=1, keepdims=True))
            db_ref[j] += ds
            dsb = ds.astype(MXU_DTYPE)
            dq = dq + _mm(dsb, jnp.where(head == j, kcat, 0))
            dk = dk + _mm_tn(dsb, q2j)
        dq_ref[...] = (dq * CA_SCALE).astype(dq_ref.dtype)
        for t in range(3):
            @pl.when(qi - 2 + t >= 0)
            def _(t=t):
                rows = pl.ds(pl.multiple_of((qi - 2 + t) * BAND_TQ, BAND_TQ), BAND_TQ)
                dk_ref[rows, :] += dk[t * BAND_TQ:(t + 1) * BAND_TQ] * LN2
                dv_ref[rows, :] += dv[t * BAND_TQ:(t + 1) * BAND_TQ]

    q, ks, vs, bspec = _band_specs(S)
    col = pl.BlockSpec((S, 128), lambda hp, qi: (0, hp))
    return pl.pallas_call(
        body, name="band_bwd", grid=(CA_H // 2, S // BAND_TQ),
        in_specs=[q, *ks, *vs, bspec, pl.BlockSpec((BAND_TQ, 128), lambda hp, qi: (qi, hp))],
        out_specs=[pl.BlockSpec((BAND_TQ, 128), lambda hp, qi: (qi, hp)), col, col, bspec],
        out_shape=[jax.ShapeDtypeStruct((S, BW), MXU_DTYPE), jax.ShapeDtypeStruct((S, BW), F32),
                   jax.ShapeDtypeStruct((S, BW), F32), jax.ShapeDtypeStruct((CA_H, BAND_TQ, BAND_W), F32)],
        compiler_params=_params(("parallel", "arbitrary"), 56),
    )(proj, proj, proj, proj, proj, proj, proj, bias, do)


BIAS_LO = REL_CLIP - (CHUNK - 1)
BIAS_FAR = 2 * REL_CLIP
BIAS_NEAR0 = BAND_W // 2


def _band_index(col0, ncol):
    i = lax.broadcasted_iota(jnp.int32, (BAND_TQ, ncol), 0)
    j = lax.broadcasted_iota(jnp.int32, (BAND_TQ, ncol), 1) + col0
    idx = jnp.clip(i + 2 * BAND_TQ - j, -REL_CLIP, REL_CLIP) + REL_CLIP
    ci, cj = i // CHUNK, j // CHUNK
    return jnp.where((ci <= cj) & (cj <= ci + LEFT_CHUNKS), idx, -1)


def _skew(x, right):
    row = lax.broadcasted_iota(jnp.int32, (BAND_TQ, 1), 0)
    for b in range(BAND_TQ.bit_length() - 1):
        shift = (1 << b) if right else BAND_W - (1 << b)
        x = jnp.where(((row >> b) & 1) == 1, pltpu.roll(x, shift, 1), x)
    return x


SKEW_NEAR0 = 2 * BAND_TQ - REL_CLIP + 1
SKEW_NEAR1 = 2 * BAND_TQ + CHUNK
SKEW_WRAP0 = BAND_W - (CHUNK - 1)


def _bias_build(rel_table):
    far = rel_table[:, BIAS_FAR:]
    base = jnp.concatenate([jnp.broadcast_to(far, (CA_H, SKEW_NEAR0)), rel_table[:, BIAS_LO:BIAS_FAR][:, ::-1],
                            jnp.broadcast_to(far, (CA_H, BAND_W - SKEW_NEAR1))], axis=1)[:, None, :]

    def body(base_ref, out_ref):
        valid = _band_index(0, BAND_W) >= 0
        for h in range(CA_H):
            tile = _skew(jnp.broadcast_to(base_ref[h] * LOG2E, (BAND_TQ, BAND_W)), right=True)
            out_ref[h] = jnp.where(valid, tile, NEG)

    return pl.pallas_call(
        body, name="bias_build",
        out_shape=jax.ShapeDtypeStruct((CA_H, BAND_TQ, BAND_W), F32),
        in_specs=[pl.BlockSpec(memory_space=pltpu.VMEM)],
        out_specs=pl.BlockSpec(memory_space=pltpu.VMEM),
        compiler_params=pltpu.CompilerParams(vmem_limit_bytes=40 << 20),
    )(base)


def _bias_fold(db):
    def body(db_ref, sums_ref, far_ref):
        col = lax.broadcasted_iota(jnp.int32, (1, BAND_W), 1)
        is_far = (col < SKEW_NEAR0) | (col >= SKEW_WRAP0)
        for h in range(CA_H):
            sums = jnp.sum(_skew(db_ref[h], right=False), axis=0, keepdims=True)
            sums_ref[h] = sums
            far_ref[h] = jnp.broadcast_to(jnp.sum(jnp.where(is_far, sums, 0.0), axis=1, keepdims=True), (1, 128))

    sums, far = pl.pallas_call(
        body, name="bias_fold",
        out_shape=[jax.ShapeDtypeStruct((CA_H, 1, BAND_W), F32), jax.ShapeDtypeStruct((CA_H, 1, 128), F32)],
        in_specs=[pl.BlockSpec(memory_space=pltpu.VMEM)],
        out_specs=[pl.BlockSpec(memory_space=pltpu.VMEM), pl.BlockSpec(memory_space=pltpu.VMEM)],
        compiler_params=pltpu.CompilerParams(vmem_limit_bytes=40 << 20),
    )(db)
    near = sums[:, 0, SKEW_NEAR0:SKEW_NEAR1][:, ::-1]
    return jnp.concatenate([jnp.zeros((CA_H, BIAS_LO), F32), near, far[:, 0, :1]], axis=1)


def _flash_bwd(qh, kh, ktr, vh, lse, delta, do, send=None):
    S = qh.shape[0]
    T = min(512, S)
    TS = min(FLASH_SUB, T)
    n = S // T
    qt, kt = _tri_tables(n, q_major=False)

    def body(qt_ref, kt_ref, q_ref, k_ref, ktr_ref, v_ref, lse_ref, dl_ref, do_ref, *rest):
        if send is None:
            dqt_ref, dk_ref, dv_ref, dk_scr, dv_scr = rest
        else:
            send_ref, dqt_ref, dk_ref, dv_ref, recv_ref, dk_scr, dv_scr, *sems = rest
            _hosted_comm(send_ref, recv_ref, sems, same_source=False)
        t = pl.program_id(1)
        qi, ki = qt_ref[t], kt_ref[t]

        @pl.when(t == 0)
        def _():
            dqt_ref[...] = jnp.zeros(dqt_ref.shape, F32)

        @pl.when(qi == ki)
        def _():
            dk_scr[...] = jnp.zeros(dk_scr.shape, F32)
            dv_scr[...] = jnp.zeros(dv_scr.shape, F32)

        def step(diag):
            head = lax.broadcasted_iota(jnp.int32, (1, 128), 1) // 64
            for j in range(2):
                sl = slice(128 * j, 128 * (j + 1))
                for qs in range(T // TS):
                    cq = slice(qs * TS, (qs + 1) * TS)
                    qsub = q_ref[cq, sl]
                    doj = jnp.where(head == j, do_ref[cq, :], 0.0).astype(MXU_DTYPE)
                    lse, dlt = lse_ref[0, j:j + 1, cq], dl_ref[0, j:j + 1, cq]
                    cols = pl.ds(pl.multiple_of(qi * T + qs * TS, TS), TS)
                    for ks in range(T // TS):
                        if diag and ks > qs:
                            continue
                        rk = slice(ks * TS, (ks + 1) * TS)
                        s = _mm_nt(k_ref[rk, sl], qsub)
                        if diag and ks == qs:
                            s = jnp.where(_chunk_mask_t(TS), s, NEG)
                        p = jnp.exp2(s - lse)
                        dv_scr[rk, :] += _mm(p, doj)
                        dp = _mm_nt(v_ref[rk, :], doj)
                        ds = (p * (dp - dlt)).astype(MXU_DTYPE)
                        dk_scr[rk, sl] += _mm(ds, qsub)
                        dqt_ref[sl, cols] += _mm(ktr_ref[sl, rk], ds) * MLA_SCALE

        @pl.when(qi > ki)
        def _():
            step(False)

        @pl.when(qi == ki)
        def _():
            step(True)

        @pl.when(qi == n - 1)
        def _():
            dk_ref[...] = dk_scr[...] * 0.6931471805599453
            dv_ref[...] = dv_scr[...]

    qmap = lambda hp, t, qt, kt: (qt[t], hp)
    kmap = lambda hp, t, qt, kt: (kt[t], hp)
    stat = pl.BlockSpec((1, 2, T), lambda hp, t, qt, kt: (hp, 0, qt[t]))
    hbm = pl.BlockSpec(memory_space=pl.ANY)
    hosted = send is not None
    recv_shape = [jax.ShapeDtypeStruct(send.shape, send.dtype)] if hosted else []
    return pl.pallas_call(
        body, name="flash_bwd_exchange" if hosted else "flash_bwd",
        grid_spec=pltpu.PrefetchScalarGridSpec(
            num_scalar_prefetch=2, grid=(MLA_H // 2, qt.shape[0]),
            in_specs=[pl.BlockSpec((T, 256), qmap), pl.BlockSpec((T, 256), kmap),
                      pl.BlockSpec((256, T), lambda hp, t, qt, kt: (hp, kt[t])), pl.BlockSpec((T, 128), kmap),
                      stat, stat, pl.BlockSpec((T, 128), qmap)] + [hbm] * hosted,
            out_specs=[pl.BlockSpec((256, S), lambda hp, t, qt, kt: (hp, 0)), pl.BlockSpec((T, 256), kmap),
                       pl.BlockSpec((T, 128), kmap)] + [hbm] * hosted,
            scratch_shapes=[pltpu.VMEM((T, 256), F32), pltpu.VMEM((T, 128), F32)] + COMM_SEMS * hosted),
        out_shape=[jax.ShapeDtypeStruct((1024, S), F32), jax.ShapeDtypeStruct((S, 1024), F32),
                   jax.ShapeDtypeStruct((S, BW), F32)] + recv_shape,
        compiler_params=_params(("arbitrary", "arbitrary"), 56),
    )(qt, kt, qh, kh, ktr, vh, lse, delta, do, *([send] * hosted))


def _mla_prep_bwd(proj, dqf, dkf, dvf, gq, gkv, wuq_t, wuk_t, wuv_t, cos_t, sin_a, sin_b):
    S = proj.shape[0]
    tm = min(512, S)

    def body(p_ref, dq_ref, dk_ref, dv_ref, gq_ref, gkv_ref, wq_ref, wk_ref, wv_ref, c_ref, sa_ref, sb_ref,
             db_ref, dwq_ref, dwk_ref, dwv_ref, dgq_ref, dgkv_ref):
        _first_step_zero((dwq_ref, dwk_ref, dwv_ref, dgq_ref, dgkv_ref), pl.program_id(0) == 0)
        c, sa, sb = c_ref[...], sa_ref[...], sb_ref[...]
        nq, rq = _rms(p_ref[:, 0:Q_RANK])
        nkv, rkv = _rms(p_ref[:, Q_RANK:Q_RANK + KV_RANK])
        cq = (nq * gq_ref[...]).astype(MXU_DTYPE)
        ckv = (nkv * gkv_ref[...]).astype(MXU_DTYPE)
        dkr = jnp.zeros((tm, 128), F32)
        dq_pre = []
        for h in range(MLA_H):
            sl = slice(128 * h, 128 * (h + 1))
            dq_pre.append(_rope_t(jnp.transpose(dq_ref[sl, :]), c, sa, sb).astype(MXU_DTYPE))
            dkr = dkr + dk_ref[:, sl]
        dq_pre = jnp.concatenate(dq_pre, axis=1)
        dcq = _mm(dq_pre, wq_ref[...])
        dwq_ref[...] += _mm_tn(dq_pre, cq)
        dgq_ref[...] += jnp.sum(dcq * nq, axis=0, keepdims=True)
        db_ref[:, 0:Q_RANK] = _rms_bwd(dcq * gq_ref[...], nq, rq).astype(db_ref.dtype)
        dk = dk_ref[...].astype(MXU_DTYPE)
        dv = dv_ref[...].astype(MXU_DTYPE)
        dckv = _mm(dk, wk_ref[...]) + _mm(dv, wv_ref[...])
        dwk_ref[...] += _mm_tn(dk, ckv)
        dwv_ref[...] += _mm_tn(dv, ckv)
        dgkv_ref[...] += jnp.sum(dckv * nkv, axis=0, keepdims=True)
        db_ref[:, Q_RANK:Q_RANK + KV_RANK] = _rms_bwd(dckv * gkv_ref[...], nkv, rkv).astype(db_ref.dtype)
        lane = lax.broadcasted_iota(jnp.int32, (1, 128), 1)
        rope_lanes = (lane >= MLA_NOPE) & (lane < MLA_QK)
        db_ref[:, Q_RANK + KV_RANK:BW] = _rope_t(jnp.where(rope_lanes, dkr, 0.0), c, sa, sb).astype(db_ref.dtype)

    full = lambda a: pl.BlockSpec(a.shape, lambda i: (0,) * a.ndim)
    fulls = lambda shape: pl.BlockSpec(shape, lambda i: (0,) * len(shape))
    tab = pl.BlockSpec((tm, 128), lambda i: (i, 0))
    row = lambda w, cb=0: pl.BlockSpec((tm, w), lambda i, cb=cb: (i, cb))
    return pl.pallas_call(
        body, name="mla_prep_bwd", grid=(S // tm,),
        in_specs=[row(BW, QD // BW), pl.BlockSpec((1024, tm), lambda i: (0, i)), row(1024), row(BW),
                  full(gq), full(gkv), full(wuq_t),
                  full(wuk_t), full(wuv_t), tab, tab, tab],
        out_specs=[row(BW), fulls((1024, Q_RANK)), fulls((1024, KV_RANK)), fulls((BW, KV_RANK)),
                   fulls((1, Q_RANK)), fulls((1, KV_RANK))],
        out_shape=[jax.ShapeDtypeStruct((S, BW), MXU_DTYPE), jax.ShapeDtypeStruct((1024, Q_RANK), F32),
                   jax.ShapeDtypeStruct((1024, KV_RANK), F32), jax.ShapeDtypeStruct((BW, KV_RANK), F32),
                   jax.ShapeDtypeStruct((1, Q_RANK), F32), jax.ShapeDtypeStruct((1, KV_RANK), F32)],
        compiler_params=_params(("arbitrary",), 56),
    )(proj, dqf, dkf, dvf, gq, gkv, wuq_t, wuk_t, wuv_t, cos_t, sin_a, sin_b)


def _sgu_bwd(proj, dya, ln_g, ln_b, ws, bfull):
    S = proj.shape[0]
    tm = min(512, S)

    def body(u_ref, v_ref, z_ref, dy_ref, g_ref, b_ref, ws_ref, bf_ref,
             da_ref, dws_ref, dbf_ref, dlg_ref, dlb_ref, dbs_ref):
        i = pl.program_id(0)
        _first_step_zero((dws_ref, dbf_ref, dlg_ref, dlb_ref, dbs_ref), i == 0)
        tri = _sgu_tri()
        ws_m = [jnp.where(tri, ws_ref[g], 0.0).astype(MXU_DTYPE) for g in range(SGU_G)]
        grp = lax.broadcasted_iota(jnp.int32, (1, BW), 1) // (BW // SGU_G)
        for b in range(tm // SGU_T):
            r = slice(b * SGU_T, (b + 1) * SGU_T)
            vln, vhat, rstd = _layernorm(v_ref[r, :], g_ref[...], b_ref[...])
            vb = vln.astype(MXU_DTYPE)
            mixed = _sgu_mix(ws_m, vb, bf_ref[...], grp)
            u, z, dy = u_ref[r, :], z_ref[r, :], dy_ref[r, :]
            sg = _sigmoid(z)
            sz = z * sg
            da_ref[r, 0:BW] = (dy * mixed * sz).astype(da_ref.dtype)
            da_ref[r, 2 * BW:3 * BW] = (dy * u * mixed * (sg * (1.0 + z * (1.0 - sg)))).astype(da_ref.dtype)
            dmix = dy * u * sz
            dbf_ref[...] += dmix
            dvln = jnp.zeros((SGU_T, BW), F32)
            for g in range(SGU_G):
                dmg = jnp.where(grp == g, dmix, 0.0).astype(MXU_DTYPE)
                dvln = dvln + _mm_tn(ws_m[g], dmg)
                dws_ref[g] += jnp.where(tri, _mm_nt(dmg, vb), 0.0)
            dlg_ref[...] += jnp.sum(dvln * vhat, axis=0, keepdims=True)
            dlb_ref[...] += jnp.sum(dvln, axis=0, keepdims=True)
            dvh = dvln * g_ref[...]
            da_ref[r, BW:2 * BW] = (rstd * (dvh - jnp.mean(dvh, axis=-1, keepdims=True)
                                            - vhat * jnp.mean(dvh * vhat, axis=-1, keepdims=True))).astype(da_ref.dtype)

        @pl.when(i == pl.num_programs(0) - 1)
        def _():
            dbf = dbf_ref[...]
            for g in range(SGU_G):
                dbs_ref[:, g:g + 1] = jnp.sum(jnp.where(grp == g, dbf, 0.0), axis=1, keepdims=True)

    blk = lambda cb: pl.BlockSpec((tm, BW), lambda i, cb=cb: (i, cb))
    fulls = lambda shape: pl.BlockSpec(shape, lambda i: (0,) * len(shape))
    return pl.pallas_call(
        body, name="sgu_bwd", grid=(S // tm,),
        in_specs=[blk(UA // BW), blk(VA // BW), blk(ZA // BW), blk(0), fulls((1, BW)), fulls((1, BW)),
                  fulls((SGU_G, SGU_T, SGU_T)), fulls((SGU_T, BW))],
        out_specs=[pl.BlockSpec((tm, 3 * BW), lambda i: (i, 0)), fulls((SGU_G, SGU_T, SGU_T)),
                   fulls((SGU_T, BW)), fulls((1, BW)), fulls((1, BW)), fulls((SGU_T, SGU_G))],
        out_shape=[jax.ShapeDtypeStruct((S, 3 * BW), MXU_DTYPE), jax.ShapeDtypeStruct((SGU_G, SGU_T, SGU_T), F32),
                   jax.ShapeDtypeStruct((SGU_T, BW), F32), jax.ShapeDtypeStruct((1, BW), F32),
                   jax.ShapeDtypeStruct((1, BW), F32), jax.ShapeDtypeStruct((SGU_T, SGU_G), F32)],
        compiler_params=_params(("arbitrary",), 40),
    )(proj, proj, proj, dya, ln_g, ln_b, ws, bfull)


DX_TK = 3 * BW


def _inproj_bwd_dx(dgl, da, small, w_t, x, pre_g, dxo):
    S = x.shape[0]
    tm = min(512, S)
    nk = DP // DX_TK

    def body(g_ref, a_ref, s0, s1, s2, s3, s4, s5, w_ref, x_ref, pg_ref, dxo_ref, dx_ref, dg_ref, acc):
        i, k = pl.program_id(0), pl.program_id(1)
        _first_step_zero((dg_ref,), (i == 0) & (k == 0))

        @pl.when(k == 0)
        def _():
            acc[...] = jnp.zeros(acc.shape, F32)

        @pl.when(k < 2)
        def _():
            acc[...] += _mm(g_ref[...], w_ref[...])

        @pl.when(k == 2)
        def _():
            acc[...] += _mm(a_ref[...], w_ref[...])

        for kk, trio in ((3, (s0, s1, s2)), (4, (s3, s4, s5))):
            @pl.when(k == kk)
            def _(trio=trio):
                t = acc[...]
                for n, ref in enumerate(trio):
                    t = t + _mm(ref[...], w_ref[n * BW:(n + 1) * BW, :])
                acc[...] = t

        @pl.when(k == nk - 1)
        def _():
            n, r = _rms(x_ref[...])
            dxn = acc[...]
            dg_ref[...] += jnp.sum(dxn * n, axis=0, keepdims=True)
            dx_ref[...] = dxo_ref[...] + _rms_bwd(dxn * pg_ref[...], n, r)

    row = pl.BlockSpec((tm, D), lambda i, k: (i, 0))
    vec = pl.BlockSpec((1, D), lambda i, k: (0, 0))
    fixed = lambda w: pl.BlockSpec((tm, w), lambda i, k: (i, 0))
    return pl.pallas_call(
        body, name="inproj_bwd_dx", grid=(S // tm, nk),
        in_specs=[pl.BlockSpec((tm, DX_TK), lambda i, k: (i, jnp.minimum(k, 1))), fixed(DX_TK)]
                 + [fixed(BW)] * 6 + [pl.BlockSpec((DX_TK, D), lambda i, k: (k, 0)), row, vec, row],
        out_specs=[row, vec],
        out_shape=[jax.ShapeDtypeStruct((S, D), F32), jax.ShapeDtypeStruct((1, D), F32)],
        scratch_shapes=[pltpu.VMEM((tm, D), F32)],
        compiler_params=_params(("arbitrary", "arbitrary"), 56),
    )(dgl, da, *small, w_t, x, pre_g, dxo)


def _matmul_tn(a, b):
    S, W = a.shape
    K = b.shape[1]
    tk, tn = min(2048, S), BW

    def body(a_ref, b_ref, o_ref):
        _first_step_zero((o_ref,), pl.program_id(1) == 0)
        o_ref[...] += _mm_tn(a_ref[...], b_ref[...])

    return pl.pallas_call(
        body, name="matmul_tn", grid=(W // tn, S // tk),
        in_specs=[pl.BlockSpec((tk, tn), lambda j, k: (k, j)), pl.BlockSpec((tk, K), lambda j, k: (k, 0))],
        out_specs=pl.BlockSpec((tn, K), lambda j, k: (j, 0)),
        out_shape=jax.ShapeDtypeStruct((W, K), F32),
        compiler_params=_params(("parallel", "arbitrary"), 48),
    )(a, b)


def _pad_rows_w_in(wt):
    z = lambda n: jnp.zeros((n, wt.shape[1]), wt.dtype)
    return jnp.concatenate([wt[4512:], wt[:1920], z(64), wt[1920:1952], z(32), wt[1952:4512]], axis=0)


def _unpad_rows_w_in(dwt):
    a = dwt[3 * D:]
    return jnp.concatenate([a[:1920], a[1984:2016], a[2048:], dwt[:3 * D]], axis=0)


def _pad_head_rows(wt, width):
    k = wt.shape[1]
    return jnp.pad(wt.reshape(MLA_H, width, k), ((0, 0), (0, 128 - width), (0, 0))).reshape(MLA_H * 128, k)


def _unpad_head_rows(wt, width):
    k = wt.shape[1]
    return wt.reshape(MLA_H, 128, k)[:, :width].reshape(MLA_H * width, k)


def _rope_tables(S):
    half = MLA_ROPE // 2
    inv = 10000.0 ** (-jnp.arange(half, dtype=F32) / half)
    ang = jnp.arange(S, dtype=F32)[:, None] * inv[None, :]
    cos, sin = jnp.cos(ang), jnp.sin(ang)
    one = lambda n: jnp.ones((S, n), F32)
    zero = lambda n: jnp.zeros((S, n), F32)
    cos_t = jnp.concatenate([one(MLA_NOPE), cos, cos, one(128 - MLA_QK)], axis=1)
    sin_a = jnp.concatenate([zero(MLA_NOPE), -sin, zero(128 - MLA_NOPE - half)], axis=1)
    sin_b = jnp.concatenate([zero(MLA_NOPE + half), sin, zero(128 - MLA_QK)], axis=1)
    return cos_t, sin_a, sin_b


SHARDED = ("w_in", "mla_w_uq", "mla_w_ukv", "w_branch", "w_out")
SHARD_PACK = SHARDED + ("gate_b",)
REPLICATED = ("pre_g", "post_g", "sgu_ln_g", "sgu_ln_b", "sgu_w", "sgu_b", "mla_q_norm_g", "mla_kv_norm_g",
              "ca_rel_bias")
OUT_ORDER = ("w_in", "pre_g", "post_g", "sgu_ln_g", "sgu_ln_b", "sgu_w", "sgu_b", "mla_q_norm_g", "mla_kv_norm_g",
             "mla_w_uq", "mla_w_ukv", "ca_rel_bias", "w_branch", "gate_b", "w_out")
TRANSPOSED = {"w_in": (0, 2, 1), "mla_w_uq": (0, 2, 1), "mla_w_ukv": (0, 2, 1), "w_branch": (0, 1, 3, 2)}


def _canon(name, a):
    return jnp.transpose(a, TRANSPOSED[name]) if name in TRANSPOSED else a


def _to_rows(flat, width, mult):
    n = flat.shape[-1]
    pad = (-n) % (width * mult)
    flat = jnp.pad(flat, [(0, 0)] * (flat.ndim - 1) + [(0, pad)])
    return flat.reshape(flat.shape[:-1] + (-1, width))


def _gather_shards(shards, dtype):
    names = list(shards)
    flat = jnp.concatenate([shards[n].astype(dtype).reshape(-1) for n in names])
    got = _all_gather(_to_rows(flat, 128, 16)).reshape(N_DEV, -1)
    out, off = {}, 0
    for n in names:
        size = shards[n].size
        out[n] = got[:, off:off + size].reshape((N_DEV,) + shards[n].shape)
        off += size
    return out


ROW_MULT = 64


def _pack(d, names):
    return _to_rows(jnp.concatenate([d[n].reshape(-1) for n in names]), 1024, ROW_MULT)


def _unpack(p, like, names):
    flat, out, off = p.reshape(-1), {}, 0
    for n in names:
        out[n] = flat[off:off + like[n].size].reshape(like[n].shape)
        off += like[n].size
    return out


def kernel(x, w_in, pre_g, post_g, sgu_ln_g, sgu_ln_b, sgu_w, sgu_b, mla_q_norm_g, mla_kv_norm_g, mla_w_uq, mla_w_ukv, ca_rel_bias, w_branch, gate_b, w_out, loss_target, m_w_in, m_pre_g, m_post_g, m_sgu_ln_g, m_sgu_ln_b, m_sgu_w, m_sgu_b, m_mla_q_norm_g, m_mla_kv_norm_g, m_mla_w_uq, m_mla_w_ukv, m_ca_rel_bias, m_w_branch, m_gate_b, m_w_out, v_w_in, v_pre_g, v_post_g, v_sgu_ln_g, v_sgu_ln_b, v_sgu_w, v_sgu_b, v_mla_q_norm_g, v_mla_kv_norm_g, v_mla_w_uq, v_mla_w_ukv, v_ca_rel_bias, v_w_branch, v_gate_b, v_w_out):
    weights = dict(w_in=w_in, pre_g=pre_g, post_g=post_g, sgu_ln_g=sgu_ln_g, sgu_ln_b=sgu_ln_b, sgu_w=sgu_w,
                   sgu_b=sgu_b, mla_q_norm_g=mla_q_norm_g, mla_kv_norm_g=mla_kv_norm_g, mla_w_uq=mla_w_uq,
                   mla_w_ukv=mla_w_ukv, ca_rel_bias=ca_rel_bias, w_branch=w_branch, gate_b=gate_b, w_out=w_out)
    mom_m = dict(w_in=m_w_in, pre_g=m_pre_g, post_g=m_post_g, sgu_ln_g=m_sgu_ln_g, sgu_ln_b=m_sgu_ln_b,
                 sgu_w=m_sgu_w, sgu_b=m_sgu_b, mla_q_norm_g=m_mla_q_norm_g, mla_kv_norm_g=m_mla_kv_norm_g,
                 mla_w_uq=m_mla_w_uq, mla_w_ukv=m_mla_w_ukv, ca_rel_bias=m_ca_rel_bias, w_branch=m_w_branch,
                 gate_b=m_gate_b, w_out=m_w_out)
    mom_v = dict(w_in=v_w_in, pre_g=v_pre_g, post_g=v_post_g, sgu_ln_g=v_sgu_ln_g, sgu_ln_b=v_sgu_ln_b,
                 sgu_w=v_sgu_w, sgu_b=v_sgu_b, mla_q_norm_g=v_mla_q_norm_g, mla_kv_norm_g=v_mla_kv_norm_g,
                 mla_w_uq=v_mla_w_uq, mla_w_ukv=v_mla_w_ukv, ca_rel_bias=v_ca_rel_bias, w_branch=v_w_branch,
                 gate_b=v_gate_b, w_out=v_w_out)
    depth = w_in.shape[0]
    S = x.shape[1]
    xs = x.reshape(S, D)

    cw = {n: _canon(n, weights[n]) for n in SHARD_PACK}

    def layer_block(l):
        return _to_rows(jnp.concatenate([cw[n][l].astype(MXU_DTYPE).reshape(-1) for n in SHARDED]), 1024, 16)

    def split_block(got):
        flat, out, off = got.reshape(N_DEV, -1), {}, 0
        for n in SHARDED:
            shape = cw[n].shape[1:]
            size = cw[n][0].size
            out[n] = flat[:, off:off + size].reshape((N_DEV,) + shape)
            off += size
        return out

    gate_all = _all_gather(_to_rows(gate_b.reshape(-1), 1024, 8)).reshape(N_DEV, -1)[:, :gate_b.size]
    gate_all = gate_all.reshape((N_DEV,) + gate_b.shape)
    cos_t, sin_a, sin_b = _rope_tables(S)

    def layer_weights(gl, l):
        w_t = _pad_rows_w_in(gl["w_in"].reshape(D_IN, D))
        wuq_t = _pad_head_rows(gl["mla_w_uq"].reshape(MLA_H * MLA_QK, Q_RANK), MLA_QK)
        ukv_t = gl["mla_w_ukv"]
        wuk_t = jnp.pad(ukv_t[:, :MLA_NOPE], ((0, 0), (0, 128 - MLA_NOPE), (0, 0))).reshape(MLA_H * 128, KV_RANK)
        wuv_t = ukv_t[:, MLA_NOPE:].reshape(BW, KV_RANK)
        wbr_t = jnp.swapaxes(gl["w_branch"], 0, 1).reshape(N_BRANCH, D, BW)
        return dict(
            w_t=w_t, w_pad=w_t.T, wuq_t=wuq_t, wuq=wuq_t.T, wuk_t=wuk_t, wuk=wuk_t.T, wuv_t=wuv_t, wuv=wuv_t.T,
            wbr_t=wbr_t, wbr=jnp.swapaxes(wbr_t, 1, 2), wout=gl["w_out"].reshape(D, D),
            gate_b=jnp.swapaxes(gate_all[:, l], 0, 1).reshape(N_BRANCH, D),
            pre_g=pre_g[l][None], post_g=post_g[l][None], ln_g=sgu_ln_g[l][None], ln_b=sgu_ln_b[l][None],
            ws=sgu_w[l], bfull=jnp.repeat(sgu_b[l].T, BW // SGU_G, axis=1),
            gq=mla_q_norm_g[l][None], gkv=mla_kv_norm_g[l][None], bias=_bias_build(ca_rel_bias[l]))

    layers, saved = [], []
    h_x = xs
    got = _all_gather(layer_block(0))
    for l in range(depth):
        lw = layer_weights(split_block(got), l)
        layers.append(lw)
        proj, xn = _inproj_fwd(h_x, lw["pre_g"], lw["w_pad"])
        ya = _sgu_fwd(proj, lw["ln_g"], lw["ln_b"], lw["ws"], lw["bfull"])
        qh, kh, vh, ktr, vtr = _mla_prep_fwd(proj, lw["gq"], lw["gkv"], lw["wuq"], lw["wuk"], lw["wuv"],
                                             cos_t, sin_a, sin_b)
        if l + 1 < depth:
            ob, lse, got = _flash_fwd(qh, kh, vtr, gather=layer_block(l + 1))
        else:
            ob, lse = _flash_fwd(qh, kh, vtr)
        oc = _band_fwd(proj, lw["bias"])
        x_new, merged, hh = _merge_fwd(h_x, ya, ob, oc, proj, lw["gate_b"], lw["wbr"], lw["wout"], lw["post_g"])
        saved.append(dict(x=h_x, proj=proj, xn=xn, ya=ya, qh=qh, kh=kh, vh=vh, ktr=ktr, ob=ob, lse=lse, oc=oc,
                          merged=merged, h=hh))
        h_x = x_new

    loss_part, dx = _loss_fwd_bwd(h_x, loss_target.reshape(S, D))
    loss = lax.psum(loss_part[0, 0], MESH_AXES)

    rows = {}
    rep = {n: [None] * depth for n in REPLICATED}
    recvs = [None] * depth
    pending = None
    for l in reversed(range(depth)):
        lw, sv = layers[l], saved[l]
        proj = sv["proj"]
        dmerged, dw_out, dg_post = _out_bwd(dx, sv["h"], sv["merged"], lw["post_g"], lw["wout"].T)
        dgl, dzb, dzc, dya, dob, doc, dwbr_t, dgb = _gate_bwd(
            dmerged, sv["ya"], sv["ob"], sv["oc"], proj, lw["gate_b"], lw["wbr"], lw["wbr_t"])
        dqc, dkc, dvc, dbias = _band_bwd(proj, lw["bias"], doc)
        drel = _bias_fold(dbias)
        flash_args = (sv["qh"], sv["kh"], sv["ktr"], sv["vh"], sv["lse"], _attn_delta(sv["ob"], dob), dob)
        if pending is None:
            dqf, dkf, dvf = _flash_bwd(*flash_args)
        else:
            dqf, dkf, dvf, recvs[l + 1] = _flash_bwd(*flash_args, send=pending)
        db, dwuq_t, dwuk_t, dwuv_t, dgq, dgkv = _mla_prep_bwd(
            proj, dqf, dkf, dvf, lw["gq"], lw["gkv"], lw["wuq_t"], lw["wuk_t"], lw["wuv_t"], cos_t, sin_a, sin_b)
        da, dws, _, dlg, dlb, dbs = _sgu_bwd(proj, dya, lw["ln_g"], lw["ln_b"], lw["ws"], lw["bfull"])
        small = (db, dzb, dqc, dkc, dvc, dzc)
        dx, dg_pre = _inproj_bwd_dx(dgl, da, small, lw["w_t"], sv["x"], lw["pre_g"], dx)
        dw_t = jnp.concatenate([_matmul_tn(a, sv["xn"]) for a in (dgl, da) + small], axis=0)

        rows["w_in"] = _unpad_rows_w_in(dw_t).reshape(N_DEV, -1)
        rows["mla_w_uq"] = _unpad_head_rows(dwuq_t, MLA_QK).reshape(N_DEV, -1)
        dk3 = dwuk_t.reshape(MLA_H, 128, KV_RANK)[:, :MLA_NOPE]
        dv3 = dwuv_t.reshape(MLA_H, 64, KV_RANK)
        rows["mla_w_ukv"] = jnp.concatenate([dk3, dv3], axis=1).reshape(N_DEV, -1)
        rows["w_branch"] = jnp.swapaxes(dwbr_t.reshape(N_BRANCH, N_DEV, D // N_DEV, BW), 0, 1).reshape(N_DEV, -1)
        rows["w_out"] = dw_out.reshape(N_DEV, -1)
        rows["gate_b"] = jnp.swapaxes(dgb.reshape(N_BRANCH, N_DEV, D // N_DEV), 0, 1).reshape(N_DEV, -1)
        pending = _to_rows(jnp.concatenate([rows[n] for n in SHARD_PACK], axis=1).astype(MXU_DTYPE), 1024, ROW_MULT)
        rep["pre_g"][l] = dg_pre[0]
        rep["post_g"][l] = dg_post[0]
        rep["sgu_ln_g"][l] = dlg[0]
        rep["sgu_ln_b"][l] = dlb[0]
        rep["sgu_w"][l] = dws
        rep["sgu_b"][l] = dbs.T
        rep["mla_q_norm_g"][l] = dgq[0]
        rep["mla_kv_norm_g"][l] = dgkv[0]
        rep["ca_rel_bias"][l] = drel
    grad_x = dx.reshape(x.shape)

    recvs[0] = _exchange(pending)

    cm = {n: _canon(n, mom_m[n]) for n in SHARD_PACK}
    cv = {n: _canon(n, mom_v[n]) for n in SHARD_PACK}
    per_layer = []
    for l in range(depth):
        at = lambda d: {n: d[n][l] for n in SHARD_PACK}
        outs_l = _reduce_adamw(recvs[l], _pack(at(cw), SHARD_PACK), _pack(at(cm), SHARD_PACK),
                               _pack(at(cv), SHARD_PACK))
        per_layer.append([_unpack(p, at(cw), SHARD_PACK) for p in outs_l])
    outs_sh = [{n: _canon(n, jnp.stack([per_layer[l][k][n] for l in range(depth)])) for n in SHARD_PACK}
               for k in range(4)]

    part = _pack({n: jnp.stack(rep[n]) for n in REPLICATED}, REPLICATED)
    allp = _all_gather(part)
    outs_rep = _reduce_adamw(allp, _pack(weights, REPLICATED), _pack(mom_m, REPLICATED), _pack(mom_v, REPLICATED))
    outs_rep = [_unpack(p, weights, REPLICATED) for p in outs_rep]

    outs = [{**a, **b} for a, b in zip(outs_sh, outs_rep)]
    return (loss, grad_x, *[o[n] for o in outs for n in OUT_ORDER])
```

```python
import functools

import jax
import jax.numpy as jnp
from jax import lax
from jax.experimental import pallas as pl
from jax.experimental.pallas import tpu as pltpu

F32 = jnp.float32
MXU_DTYPE = jnp.bfloat16
EPS = 1e-6
NEG = -1e30
MESH_AXES = ("x", "y", "c")
N_DEV = 8

D = 1024
BW = 512
N_BRANCH = 3
CHUNK = 64
SGU_T = 128
SGU_G = 8
MLA_H = 8
MLA_NOPE = 64
MLA_ROPE = 32
MLA_QK = MLA_NOPE + MLA_ROPE
Q_RANK = 256
KV_RANK = 128
CA_H = 8
CA_DH = 64
LEFT_CHUNKS = 8
REL_CLIP = 128
D_IN = 7584

G_OFF, UA, VA, ZA, QD, ZB, QC, KC, VC, ZC, DP = 0, 3072, 3584, 4096, 4608, 5120, 5632, 6144, 6656, 7168, 7680
BAND_TQ = 256
BAND_W = 3 * BAND_TQ

ADAM_LR, ADAM_B1, ADAM_B2, ADAM_EPS, ADAM_WD, ADAM_STEP = 0.001, 0.9, 0.999, 1e-08, 0.01, 10


def _params(sem, mib):
    return pltpu.CompilerParams(dimension_semantics=sem, vmem_limit_bytes=mib << 20)


def _mm(a, b):
    return jnp.dot(a.astype(MXU_DTYPE), b.astype(MXU_DTYPE), preferred_element_type=F32)


def _mm_nt(a, b):
    return lax.dot_general(a.astype(MXU_DTYPE), b.astype(MXU_DTYPE), (((1,), (1,)), ((), ())),
                           preferred_element_type=F32)


def _mm_tn(a, b):
    return lax.dot_general(a.astype(MXU_DTYPE), b.astype(MXU_DTYPE), (((0,), (0,)), ((), ())),
                           preferred_element_type=F32)


def _sigmoid(z):
    return 1.0 / (1.0 + jnp.exp(-z))


def _rms(x):
    r = lax.rsqrt(jnp.mean(x * x, axis=-1, keepdims=True) + EPS)
    return x * r, r


def _rms_bwd(dn, n, r):
    return r * (dn - n * jnp.mean(dn * n, axis=-1, keepdims=True))


def _rope(b, c, sa, sb):
    return b * c + pltpu.roll(b, 112, 1) * sa + pltpu.roll(b, 16, 1) * sb


def _rope_t(d, c, sa, sb):
    return d * c + pltpu.roll(d * sa, 16, 1) + pltpu.roll(d * sb, 112, 1)


def _all_gather(blk):
    R = blk.shape[0]

    def body(x_ref, out_ref, send_sems, recv_sems, local_sem):
        x, y, c = lax.axis_index("x"), lax.axis_index("y"), lax.axis_index("c")
        me, sibling = (x, y, c), (x, y, 1 - c)
        chips = [(1 - x, y), (x, 1 - y), (1 - x, 1 - y)]

        def slot(px, py, pc):
            return out_ref.at[4 * px + 2 * py + pc]

        def copy(k, block, to, src=None):
            return pltpu.make_async_remote_copy(
                src_ref=slot(*block) if src is None else src, dst_ref=slot(*block),
                send_sem=send_sems.at[k], recv_sem=recv_sems.at[k],
                device_id=to, device_id_type=pl.DeviceIdType.MESH)

        mine = pltpu.make_async_copy(x_ref, slot(*me), local_sem)
        mine.start()
        first = [copy(0, me, sibling, src=x_ref)]
        first += [copy(1 + j, me, (*chip, c), src=x_ref) for j, chip in enumerate(chips)]
        for cp in first:
            cp.start()
        passed = [copy(4 + j, (*chip, c), sibling) for j, chip in enumerate(chips)]
        for j, chip in enumerate(chips):
            copy(1 + j, (*chip, c), me).wait_recv()
            passed[j].start()
        copy(0, sibling, me).wait_recv()
        for j, chip in enumerate(chips):
            copy(4 + j, (*chip, 1 - c), me).wait_recv()
        for cp in first + passed:
            cp.wait_send()
        mine.wait()

    return pl.pallas_call(
        body, name="all_gather",
        out_shape=jax.ShapeDtypeStruct((N_DEV,) + blk.shape, blk.dtype),
        in_specs=[pl.BlockSpec(memory_space=pl.ANY)],
        out_specs=pl.BlockSpec(memory_space=pl.ANY),
        scratch_shapes=[pltpu.SemaphoreType.DMA((7,)), pltpu.SemaphoreType.DMA((7,)), pltpu.SemaphoreType.DMA(())],
    )(blk)


def _peer_copies(s_ref, r_ref, send_sems, recv_sems, local_sem, same_source):
    x, y, c = lax.axis_index("x"), lax.axis_index("y"), lax.axis_index("c")
    me = 4 * x + 2 * y + c

    def src(pid):
        return s_ref if same_source else s_ref.at[pid]

    def peer(k):
        px = 1 - x if (k >> 2) & 1 else x
        py = 1 - y if (k >> 1) & 1 else y
        pc = 1 - c if k & 1 else c
        return (px, py, pc), 4 * px + 2 * py + pc

    def remote(k, row):
        pos, pid = peer(k)
        return pltpu.make_async_remote_copy(
            src_ref=src(pid), dst_ref=r_ref.at[me if row is None else pid],
            send_sem=send_sems.at[k], recv_sem=recv_sems.at[k],
            device_id=pos, device_id_type=pl.DeviceIdType.MESH)

    def local():
        return pltpu.make_async_copy(src(me), r_ref.at[me], local_sem)

    def start():
        local().start()
        for k in range(1, N_DEV):
            remote(k, None).start()

    def wait():
        for k in range(1, N_DEV):
            remote(k, "peer").wait_recv()
        for k in range(1, N_DEV):
            remote(k, None).wait_send()
        local().wait()

    return start, wait


COMM_SEMS = [pltpu.SemaphoreType.DMA((N_DEV,)), pltpu.SemaphoreType.DMA((N_DEV,)), pltpu.SemaphoreType.DMA(())]


def _hosted_comm(s_ref, r_ref, sems, same_source):
    start, wait = _peer_copies(s_ref, r_ref, *sems, same_source=same_source)
    i, j = pl.program_id(0), pl.program_id(1)

    @pl.when((i == 0) & (j == 0))
    def _():
        start()

    @pl.when((i == pl.num_programs(0) - 1) & (j == pl.num_programs(1) - 1))
    def _():
        wait()


def _exchange(send):
    def body(s_ref, r_ref, send_sems, recv_sems, local_sem):
        start, wait = _peer_copies(s_ref, r_ref, send_sems, recv_sems, local_sem, same_source=False)
        start()
        wait()

    return pl.pallas_call(
        body, name="grad_exchange",
        out_shape=jax.ShapeDtypeStruct(send.shape, send.dtype),
        in_specs=[pl.BlockSpec(memory_space=pl.ANY)],
        out_specs=pl.BlockSpec(memory_space=pl.ANY),
        scratch_shapes=list(COMM_SEMS),
    )(send)


def _reduce_adamw(recv, w, m, v):
    R = w.shape[0]
    tr = next(t for t in (128, 64, 32, 16) if R % t == 0)
    c1 = 1.0 - ADAM_B1 ** ADAM_STEP
    c2 = 1.0 - ADAM_B2 ** ADAM_STEP

    def body(r_ref, w_ref, m_ref, v_ref, g_ref, d_ref, nm_ref, nv_ref):
        g = r_ref[0].astype(F32)
        for s in range(1, N_DEV):
            g = g + r_ref[s].astype(F32)
        m2 = ADAM_B1 * m_ref[...] + (1.0 - ADAM_B1) * g
        v2 = ADAM_B2 * v_ref[...] + (1.0 - ADAM_B2) * (g * g)
        m_hat = m2 / c1
        v_hat = v2 / c2
        g_ref[...] = g
        d_ref[...] = -ADAM_LR * (m_hat / (jnp.sqrt(v_hat) + ADAM_EPS) + ADAM_WD * w_ref[...])
        nm_ref[...] = m2
        nv_ref[...] = v2

    row = pl.BlockSpec((tr, 1024), lambda i: (i, 0))
    return pl.pallas_call(
        body, name="reduce_adamw", grid=(R // tr,),
        in_specs=[pl.BlockSpec((N_DEV, tr, 1024), lambda i: (0, i, 0)), row, row, row],
        out_specs=[row, row, row, row],
        out_shape=[jax.ShapeDtypeStruct((R, 1024), F32)] * 4,
        compiler_params=_params(("parallel",), 40),
    )(recv, w, m, v)


def _inproj_fwd(x, pre_g, w_pad):
    S = x.shape[0]
    tm, tn = min(1024, S), 1536

    def body(x_ref, g_ref, w_ref, proj_ref, xn_ref):
        @pl.when(pl.program_id(1) == 0)
        def _():
            n, _ = _rms(x_ref[...])
            xn_ref[...] = (n * g_ref[...]).astype(xn_ref.dtype)
        proj_ref[...] = jnp.dot(xn_ref[...], w_ref[...], preferred_element_type=F32)

    return pl.pallas_call(
        body, name="inproj_fwd", grid=(S // tm, DP // tn),
        in_specs=[pl.BlockSpec((tm, D), lambda i, j: (i, 0)), pl.BlockSpec((1, D), lambda i, j: (0, 0)),
                  pl.BlockSpec((D, tn), lambda i, j: (0, j))],
        out_specs=[pl.BlockSpec((tm, tn), lambda i, j: (i, j)), pl.BlockSpec((tm, D), lambda i, j: (i, 0))],
        out_shape=[jax.ShapeDtypeStruct((S, DP), F32), jax.ShapeDtypeStruct((S, D), MXU_DTYPE)],
        compiler_params=_params(("parallel", "arbitrary"), 48),
    )(x, pre_g, w_pad)


def _sgu_tri():
    return lax.broadcasted_iota(jnp.int32, (SGU_T, SGU_T), 0) >= lax.broadcasted_iota(jnp.int32, (SGU_T, SGU_T), 1)


def _sgu_mix(ws_m, vb, bfull, grp):
    mixed = bfull
    for g in range(SGU_G):
        mixed = mixed + jnp.where(grp == g, _mm(ws_m[g], vb), 0.0)
    return mixed


def _layernorm(v, g, b):
    xc = v - jnp.mean(v, axis=-1, keepdims=True)
    rstd = lax.rsqrt(jnp.mean(xc * xc, axis=-1, keepdims=True) + EPS)
    vhat = xc * rstd
    return vhat * g + b, vhat, rstd


def _sgu_fwd(proj, ln_g, ln_b, ws, bfull):
    S = proj.shape[0]
    tm = min(512, S)

    def body(u_ref, v_ref, z_ref, g_ref, b_ref, ws_ref, bf_ref, ya_ref):
        tri = _sgu_tri()
        ws_m = [jnp.where(tri, ws_ref[g], 0.0).astype(MXU_DTYPE) for g in range(SGU_G)]
        grp = lax.broadcasted_iota(jnp.int32, (1, BW), 1) // (BW // SGU_G)
        for b in range(tm // SGU_T):
            r = slice(b * SGU_T, (b + 1) * SGU_T)
            vln, _, _ = _layernorm(v_ref[r, :], g_ref[...], b_ref[...])
            mixed = _sgu_mix(ws_m, vln.astype(MXU_DTYPE), bf_ref[...], grp)
            z = z_ref[r, :]
            ya_ref[r, :] = (u_ref[r, :] * mixed * (z * _sigmoid(z))).astype(ya_ref.dtype)

    blk = lambda cb: pl.BlockSpec((tm, BW), lambda i, cb=cb: (i, cb))
    vec = pl.BlockSpec((1, BW), lambda i: (0, 0))
    return pl.pallas_call(
        body, name="sgu_fwd", grid=(S // tm,),
        in_specs=[blk(UA // BW), blk(VA // BW), blk(ZA // BW), vec, vec,
                  pl.BlockSpec((SGU_G, SGU_T, SGU_T), lambda i: (0, 0, 0)),
                  pl.BlockSpec((SGU_T, BW), lambda i: (0, 0))],
        out_specs=pl.BlockSpec((tm, BW), lambda i: (i, 0)),
        out_shape=jax.ShapeDtypeStruct((S, BW), MXU_DTYPE),
        compiler_params=_params(("parallel",), 32),
    )(proj, proj, proj, ln_g, ln_b, ws, bfull)


def _mla_prep_fwd(proj, gq, gkv, wuq, wuk, wuv, cos_t, sin_a, sin_b):
    S = proj.shape[0]
    tm = min(512, S)

    def body(p_ref, gq_ref, gkv_ref, wuq_ref, wuk_ref, wuv_ref, c_ref, sa_ref, sb_ref,
             q_ref, k_ref, v_ref, kt_ref, vt_ref):
        nq, _ = _rms(p_ref[:, 0:Q_RANK])
        nkv, _ = _rms(p_ref[:, Q_RANK:Q_RANK + KV_RANK])
        cq = (nq * gq_ref[...]).astype(MXU_DTYPE)
        ckv = (nkv * gkv_ref[...]).astype(MXU_DTYPE)
        qf = _mm(cq, wuq_ref[...])
        kf = _mm(ckv, wuk_ref[...])
        v = _mm(ckv, wuv_ref[...])
        v_ref[...] = v.astype(v_ref.dtype)
        vt_ref[...] = jnp.transpose(v).astype(vt_ref.dtype)
        c, sa, sb = c_ref[...], sa_ref[...], sb_ref[...]
        krr = _rope(p_ref[:, Q_RANK + KV_RANK:BW], c, sa, sb)
        for h in range(MLA_H):
            sl = slice(128 * h, 128 * (h + 1))
            q_ref[:, sl] = (_rope(qf[:, sl], c, sa, sb) * MLA_SCALE_LOG2).astype(q_ref.dtype)
            kh = kf[:, sl] + krr
            k_ref[:, sl] = kh.astype(k_ref.dtype)
            kt_ref[sl, :] = jnp.transpose(kh).astype(kt_ref.dtype)

    full = lambda a: pl.BlockSpec(a.shape, lambda i: (0,) * a.ndim)
    tab = pl.BlockSpec((tm, 128), lambda i: (i, 0))
    return pl.pallas_call(
        body, name="mla_prep_fwd", grid=(S // tm,),
        in_specs=[pl.BlockSpec((tm, BW), lambda i: (i, QD // BW)), full(gq), full(gkv), full(wuq), full(wuk),
                  full(wuv), tab, tab, tab],
        out_specs=[pl.BlockSpec((tm, 1024), lambda i: (i, 0)), pl.BlockSpec((tm, 1024), lambda i: (i, 0)),
                   pl.BlockSpec((tm, BW), lambda i: (i, 0)), pl.BlockSpec((1024, tm), lambda i: (0, i)),
                   pl.BlockSpec((BW, tm), lambda i: (0, i))],
        out_shape=[jax.ShapeDtypeStruct((S, 1024), MXU_DTYPE), jax.ShapeDtypeStruct((S, 1024), MXU_DTYPE),
                   jax.ShapeDtypeStruct((S, BW), MXU_DTYPE), jax.ShapeDtypeStruct((1024, S), MXU_DTYPE),
                   jax.ShapeDtypeStruct((BW, S), MXU_DTYPE)],
        compiler_params=_params(("parallel",), 48),
    )(proj, gq, gkv, wuq, wuk, wuv, cos_t, sin_a, sin_b)


MLA_SCALE = MLA_QK ** -0.5
MLA_SCALE_LOG2 = MLA_SCALE * 1.4426950408889634


def _tri_tables(n, q_major):
    if q_major:
        pairs = [(qi, ki) for qi in range(n) for ki in range(qi + 1)]
    else:
        pairs = [(qi, ki) for ki in range(n) for qi in range(ki, n)]
    return (jnp.asarray([p[0] for p in pairs], jnp.int32), jnp.asarray([p[1] for p in pairs], jnp.int32))


def _chunk_mask_t(T):
    kc = lax.broadcasted_iota(jnp.int32, (T, T), 0) >> 6
    qc = lax.broadcasted_iota(jnp.int32, (T, T), 1) >> 6
    return kc <= qc


FLASH_SUB = 512


def _col_reduce(op, reduce_fn, x):
    r = x.shape[0]
    while r > 8 and r % 16 == 0:
        r //= 2
        x = op(x[:r], x[r:])
    return reduce_fn(x, axis=0, keepdims=True)


def _flash_fwd(qh, kh, vt, gather=None):
    S = qh.shape[0]
    T = min(512, S)
    TS = min(FLASH_SUB, T)
    n = S // T
    qt, kt = _tri_tables(n, q_major=True)

    def body(qt_ref, kt_ref, q_ref, k_ref, vt_ref, *rest):
        if gather is None:
            o_ref, lse_ref, m_scr, acc_scr = rest
        else:
            g_ref, o_ref, lse_ref, got_ref, m_scr, acc_scr, *sems = rest
            _hosted_comm(g_ref, got_ref, sems, same_source=True)
        t = pl.program_id(1)
        qi, ki = qt_ref[t], kt_ref[t]
        row = lax.broadcasted_iota(jnp.int32, (128, 1), 0)
        ones_row = (64, 0)

        @pl.when(ki == 0)
        def _():
            m_scr[...] = jnp.full(m_scr.shape, NEG, F32)
            acc_scr[...] = jnp.zeros(acc_scr.shape, F32)

        def step(diag):
            for j in range(2):
                sl = slice(128 * j, 128 * (j + 1))
                for qs in range(T // TS):
                    cq = slice(qs * TS, (qs + 1) * TS)
                    qsub = q_ref[cq, sl]
                    for ks in range(T // TS):
                        if diag and ks > qs:
                            continue
                        rk = slice(ks * TS, (ks + 1) * TS)
                        s = _mm_nt(k_ref[rk, sl], qsub)
                        if diag and ks == qs:
                            s = jnp.where(_chunk_mask_t(TS), s, NEG)
                        m_old = m_scr[j, :, cq]
                        m_new = jnp.maximum(m_old, _col_reduce(jnp.maximum, jnp.max, s))
                        alpha = jnp.exp2(m_old - m_new)
                        p = jnp.exp2(s - m_new)
                        vt1 = jnp.where(row == ones_row[j], 1.0, vt_ref[:, rk]).astype(MXU_DTYPE)
                        acc_scr[j, :, cq] = alpha * acc_scr[j, :, cq] + _mm(vt1, p)
                        m_scr[j, :, cq] = m_new

        @pl.when(ki < qi)
        def _():
            step(False)

        @pl.when(ki == qi)
        def _():
            step(True)
            l = [acc_scr[j, ones_row[j]:ones_row[j] + 1, :] for j in range(2)]
            o_ref[...] = jnp.transpose(jnp.where(row < 64, acc_scr[0] / l[0], acc_scr[1] / l[1]))
            for j in range(2):
                lse_ref[0, j:j + 1, :] = m_scr[j] + jnp.log2(l[j])

    qmap = lambda hp, t, qt, kt: (qt[t], hp)
    kmap = lambda hp, t, qt, kt: (kt[t], hp)
    hbm = pl.BlockSpec(memory_space=pl.ANY)
    hosted = gather is not None
    got_shape = [jax.ShapeDtypeStruct((N_DEV,) + gather.shape, gather.dtype)] if hosted else []
    return pl.pallas_call(
        body, name="flash_fwd_gather" if hosted else "flash_fwd",
        grid_spec=pltpu.PrefetchScalarGridSpec(
            num_scalar_prefetch=2, grid=(MLA_H // 2, qt.shape[0]),
            in_specs=[pl.BlockSpec((T, 256), qmap), pl.BlockSpec((T, 256), kmap),
                      pl.BlockSpec((128, T), lambda hp, t, qt, kt: (hp, kt[t]))] + [hbm] * hosted,
            out_specs=[pl.BlockSpec((T, 128), qmap),
                       pl.BlockSpec((1, 2, T), lambda hp, t, qt, kt: (hp, 0, qt[t]))] + [hbm] * hosted,
            scratch_shapes=[pltpu.VMEM((2, 1, T), F32), pltpu.VMEM((2, 128, T), F32)] + COMM_SEMS * hosted),
        out_shape=[jax.ShapeDtypeStruct((S, BW), F32), jax.ShapeDtypeStruct((MLA_H // 2, 2, S), F32)] + got_shape,
        compiler_params=_params(("arbitrary", "arbitrary"), 40),
    )(qt, kt, qh, kh, vt, *([gather] * hosted))


def _attn_delta(o, do):
    S = o.shape[0]
    T = min(512, S)

    def body(o_ref, do_ref, d_ref):
        head = lax.broadcasted_iota(jnp.int32, (1, 128), 1) // 64
        prod = o_ref[...] * do_ref[...]
        for j in range(2):
            d_ref[0, j:j + 1, :] = jnp.sum(jnp.transpose(jnp.where(head == j, prod, 0.0)), axis=0, keepdims=True)

    blk = pl.BlockSpec((T, 128), lambda hp, i: (i, hp))
    return pl.pallas_call(
        body, name="attn_delta", grid=(MLA_H // 2, S // T),
        in_specs=[blk, blk],
        out_specs=pl.BlockSpec((1, 2, T), lambda hp, i: (hp, 0, i)),
        out_shape=jax.ShapeDtypeStruct((MLA_H // 2, 2, S), F32),
        compiler_params=_params(("parallel", "parallel"), 32),
    )(o, do)


def _band_specs(S):
    q = pl.BlockSpec((BAND_TQ, 128), lambda hp, qi: (qi, QC // 128 + hp))
    ks = [pl.BlockSpec((BAND_TQ, 128), lambda hp, qi, t=t: (jnp.maximum(qi - 2 + t, 0), KC // 128 + hp))
          for t in range(3)]
    vs = [pl.BlockSpec((BAND_TQ, 128), lambda hp, qi, t=t: (jnp.maximum(qi - 2 + t, 0), VC // 128 + hp))
          for t in range(3)]
    bias = pl.BlockSpec((2, BAND_TQ, BAND_W), lambda hp, qi: (hp, 0, 0))
    return q, ks, vs, bias


LOG2E = 1.4426950408889634
LN2 = 0.6931471805599453
CA_SCALE = CA_DH ** -0.5


def _band_probs(q2j, kcat, bias2_j, valid):
    s = _mm_nt(q2j, kcat) + bias2_j
    s = jnp.where(valid, s, NEG)
    p = jnp.exp2(s - jnp.max(s, axis=1, keepdims=True))
    return p * (1.0 / jnp.sum(p, axis=1, keepdims=True))


def _band_valid(qi):
    tile = lax.broadcasted_iota(jnp.int32, (1, BAND_W), 1) // BAND_TQ
    return tile + qi >= 2


def _band_fwd(proj, bias):
    S = proj.shape[0]

    def body(q_ref, k0, k1, k2, v0, v1, v2, b_ref, o_ref):
        qi = pl.program_id(1)
        head = lax.broadcasted_iota(jnp.int32, (1, 128), 1) // 64
        kcat = jnp.concatenate([k0[...], k1[...], k2[...]], axis=0).astype(MXU_DTYPE)
        vcat = jnp.concatenate([v0[...], v1[...], v2[...]], axis=0).astype(MXU_DTYPE)
        valid = _band_valid(qi)
        q2 = q_ref[...] * (CA_SCALE * LOG2E)
        o = jnp.zeros((BAND_TQ, 128), F32)
        for j in range(2):
            pn = _band_probs(jnp.where(head == j, q2, 0.0), kcat, b_ref[j], valid)
            o = o + _mm(pn, jnp.where(head == j, vcat, 0))
        o_ref[...] = o

    q, ks, vs, bspec = _band_specs(S)
    return pl.pallas_call(
        body, name="band_fwd", grid=(CA_H // 2, S // BAND_TQ),
        in_specs=[q, *ks, *vs, bspec],
        out_specs=pl.BlockSpec((BAND_TQ, 128), lambda hp, qi: (qi, hp)),
        out_shape=jax.ShapeDtypeStruct((S, BW), F32),
        compiler_params=_params(("parallel", "arbitrary"), 40),
    )(proj, proj, proj, proj, proj, proj, proj, bias)


def _merge_fwd(x, ya, ob, oc, proj, gate_b, wbr, wout, post_g):
    S = x.shape[0]
    tm = min(256, S)

    def body(x_ref, ya_ref, ob_ref, oc_ref, zb_ref, zc_ref, g0, g1, g2, gb_ref, wbr_ref, wo_ref, pg_ref,
             xo_ref, mg_ref, h_ref):
        zb, zc = zb_ref[...], zc_ref[...]
        ys = [ya_ref[...], ob_ref[...] * (zb * _sigmoid(zb)), oc_ref[...] * (zc * _sigmoid(zc))]
        merged = jnp.zeros((tm, D), F32)
        for i, g_ref in enumerate((g0, g1, g2)):
            merged = merged + _sigmoid(g_ref[...] + gb_ref[i:i + 1, :]) * _mm(ys[i], wbr_ref[i])
        mg_ref[...] = merged.astype(mg_ref.dtype)
        h = _mm(merged, wo_ref[...])
        h_ref[...] = h
        n, _ = _rms(h)
        xo_ref[...] = x_ref[...] + n * pg_ref[...]

    row = lambda w, cb=0: pl.BlockSpec((tm, w), lambda i, cb=cb: (i, cb))
    full = lambda a: pl.BlockSpec(a.shape, lambda i: (0,) * a.ndim)
    return pl.pallas_call(
        body, name="merge_fwd", grid=(S // tm,),
        in_specs=[row(D), row(BW), row(BW), row(BW), row(BW, ZB // BW), row(BW, ZC // BW),
                  row(D, 0), row(D, 1), row(D, 2), full(gate_b), full(wbr), full(wout), full(post_g)],
        out_specs=[row(D), row(D), row(D)],
        out_shape=[jax.ShapeDtypeStruct((S, D), F32), jax.ShapeDtypeStruct((S, D), MXU_DTYPE),
                   jax.ShapeDtypeStruct((S, D), F32)],
        compiler_params=_params(("parallel",), 56),
    )(x, ya, ob, oc, proj, proj, proj, proj, proj, gate_b, wbr, wout, post_g)


def _loss_fwd_bwd(y, target):
    S = y.shape[0]
    tm = min(512, S)

    def body(y_ref, t_ref, loss_ref, dy_ref):
        @pl.when(pl.program_id(0) == 0)
        def _():
            loss_ref[...] = jnp.zeros((1, 1), F32)
        err = y_ref[...] - t_ref[...]
        loss_ref[...] += 0.5 * jnp.sum(jnp.mean(err * err, axis=-1, keepdims=True), axis=0, keepdims=True)
        dy_ref[...] = err * (1.0 / D)

    row = pl.BlockSpec((tm, D), lambda i: (i, 0))
    return pl.pallas_call(
        body, name="loss", grid=(S // tm,),
        in_specs=[row, row],
        out_specs=[pl.BlockSpec((1, 1), lambda i: (0, 0)), row],
        out_shape=[jax.ShapeDtypeStruct((1, 1), F32), jax.ShapeDtypeStruct((S, D), F32)],
        compiler_params=_params(("arbitrary",), 32),
    )(y, target)


def _first_step_zero(refs, first):
    @pl.when(first)
    def _():
        for r in refs:
            r[...] = jnp.zeros(r.shape, r.dtype)


def _out_bwd(dxo, h, merged, post_g, wout_t):
    S = dxo.shape[0]
    tm = min(256, S)

    def body(d_ref, h_ref, mg_ref, pg_ref, wt_ref, dm_ref, dw_ref, dg_ref):
        _first_step_zero((dw_ref, dg_ref), pl.program_id(0) == 0)
        d = d_ref[...]
        hn, r = _rms(h_ref[...])
        dg_ref[...] += jnp.sum(d * hn, axis=0, keepdims=True)
        dh = _rms_bwd(d * pg_ref[...], hn, r)
        dm_ref[...] = _mm(dh, wt_ref[...])
        dw_ref[...] += _mm_tn(mg_ref[...], dh)

    row = pl.BlockSpec((tm, D), lambda i: (i, 0))
    full = lambda shape: pl.BlockSpec(shape, lambda i: (0,) * len(shape))
    return pl.pallas_call(
        body, name="out_bwd", grid=(S // tm,),
        in_specs=[row, row, row, full((1, D)), full((D, D))],
        out_specs=[row, full((D, D)), full((1, D))],
        out_shape=[jax.ShapeDtypeStruct((S, D), F32), jax.ShapeDtypeStruct((D, D), F32),
                   jax.ShapeDtypeStruct((1, D), F32)],
        compiler_params=_params(("arbitrary",), 40),
    )(dxo, h, merged, post_g, wout_t)


def _gate_bwd(dm, ya, ob, oc, proj, gate_b, wbr, wbr_t):
    S = dm.shape[0]
    tm = min(128, S)

    def body(dm_ref, ya_ref, ob_ref, oc_ref, zb_ref, zc_ref, g0, g1, g2, gb_ref, wbr_ref, wbt_ref,
             dg_ref, dzb_ref, dzc_ref, dya_ref, dob_ref, doc_ref, dwbr_ref, dgb_ref):
        _first_step_zero((dwbr_ref, dgb_ref), pl.program_id(0) == 0)
        dmv = dm_ref[...]
        zb, zc = zb_ref[...], zc_ref[...]
        sgb, sgc = _sigmoid(zb), _sigmoid(zc)
        ob, oc = ob_ref[...], oc_ref[...]
        ys = [ya_ref[...], (ob * (zb * sgb)).astype(MXU_DTYPE), (oc * (zc * sgc)).astype(MXU_DTYPE)]
        dys = []
        for i, g_ref in enumerate((g0, g1, g2)):
            br = _mm(ys[i], wbr_ref[i])
            gate = _sigmoid(g_ref[...] + gb_ref[i:i + 1, :])
            dgl = dmv * br * (gate * (1.0 - gate))
            dg_ref[:, D * i:D * (i + 1)] = dgl.astype(dg_ref.dtype)
            dgb_ref[i:i + 1, :] += jnp.sum(dgl, axis=0, keepdims=True)
            dbr = dmv * gate
            dys.append(_mm(dbr, wbt_ref[i]))
            dwbr_ref[i] += _mm_tn(dbr, ys[i])
        dya_ref[...] = dys[0]
        dob_ref[...] = dys[1] * (zb * sgb)
        dzb_ref[...] = (dys[1] * ob * (sgb * (1.0 + zb * (1.0 - sgb)))).astype(dzb_ref.dtype)
        doc_ref[...] = dys[2] * (zc * sgc)
        dzc_ref[...] = (dys[2] * oc * (sgc * (1.0 + zc * (1.0 - sgc)))).astype(dzc_ref.dtype)

    row = lambda w, cb=0: pl.BlockSpec((tm, w), lambda i, cb=cb: (i, cb))
    full = lambda a: pl.BlockSpec(a.shape, lambda i: (0,) * a.ndim)
    sds = lambda w, dt=F32: jax.ShapeDtypeStruct((S, w), dt)
    return pl.pallas_call(
        body, name="gate_bwd", grid=(S // tm,),
        in_specs=[row(D), row(BW), row(BW), row(BW), row(BW, ZB // BW), row(BW, ZC // BW),
                  row(D, 0), row(D, 1), row(D, 2), full(gate_b), full(wbr), full(wbr_t)],
        out_specs=[row(3 * D), row(BW), row(BW), row(BW), row(BW), row(BW),
                   pl.BlockSpec((N_BRANCH, D, BW), lambda i: (0, 0, 0)), pl.BlockSpec((N_BRANCH, D), lambda i: (0, 0))],
        out_shape=[sds(3 * D, MXU_DTYPE), sds(BW, MXU_DTYPE), sds(BW, MXU_DTYPE), sds(BW), sds(BW), sds(BW),
                   jax.ShapeDtypeStruct((N_BRANCH, D, BW), F32), jax.ShapeDtypeStruct((N_BRANCH, D), F32)],
        compiler_params=_params(("arbitrary",), 56),
    )(dm, ya, ob, oc, proj, proj, proj, proj, proj, gate_b, wbr, wbr_t)


def _band_bwd(proj, bias, do):
    S = proj.shape[0]

    def body(q_ref, k0, k1, k2, v0, v1, v2, b_ref, do_ref, dq_ref, dk_ref, dv_ref, db_ref):
        qi = pl.program_id(1)
        _first_step_zero((dk_ref, dv_ref, db_ref), qi == 0)
        head = lax.broadcasted_iota(jnp.int32, (1, 128), 1) // 64
        kcat = jnp.concatenate([k0[...], k1[...], k2[...]], axis=0).astype(MXU_DTYPE)
        vcat = jnp.concatenate([v0[...], v1[...], v2[...]], axis=0).astype(MXU_DTYPE)
        valid = _band_valid(qi)
        q2, dov = q_ref[...] * (CA_SCALE * LOG2E), do_ref[...]
        dq = jnp.zeros((BAND_TQ, 128), F32)
        dk = jnp.zeros((BAND_W, 128), F32)
        dv = jnp.zeros((BAND_W, 128), F32)
        for j in range(2):
            q2j = jnp.where(head == j, q2, 0.0).astype(MXU_DTYPE)
            doj = jnp.where(head == j, dov, 0.0).astype(MXU_DTYPE)
            pn = _band_probs(q2j, kcat, b_ref[j], valid)
            dv = dv + _mm_tn(pn, doj)
            dp = _mm_nt(doj, vcat)
            ds = pn * (dp - jnp.sum(pn * dp, axis=1, keepdims=True))
            db_ref[j] += ds
            dsb = ds.astype(MXU_DTYPE)
            dq = dq + _mm(dsb, jnp.where(head == j, kcat, 0))
            dk = dk + _mm_tn(dsb, q2j)
        dq_ref[...] = (dq * CA_SCALE).astype(dq_ref.dtype)
        for t in range(3):
            @pl.when(qi - 2 + t >= 0)
            def _(t=t):
                rows = pl.ds(pl.multiple_of((qi - 2 + t) * BAND_TQ, BAND_TQ), BAND_TQ)
                dk_ref[rows, :] += dk[t * BAND_TQ:(t + 1) * BAND_TQ] * LN2
                dv_ref[rows, :] += dv[t * BAND_TQ:(t + 1) * BAND_TQ]

    q, ks, vs, bspec = _band_specs(S)
    col = pl.BlockSpec((S, 128), lambda hp, qi: (0, hp))
    return pl.pallas_call(
        body, name="band_bwd", grid=(CA_H // 2, S // BAND_TQ),
        in_specs=[q, *ks, *vs, bspec, pl.BlockSpec((BAND_TQ, 128), lambda hp, qi: (qi, hp))],
        out_specs=[pl.BlockSpec((BAND_TQ, 128), lambda hp, qi: (qi, hp)), col, col, bspec],
        out_shape=[jax.ShapeDtypeStruct((S, BW), MXU_DTYPE), jax.ShapeDtypeStruct((S, BW), F32),
                   jax.ShapeDtypeStruct((S, BW), F32), jax.ShapeDtypeStruct((CA_H, BAND_TQ, BAND_W), F32)],
        compiler_params=_params(("parallel", "arbitrary"), 56),
    )(proj, proj, proj, proj, proj, proj, proj, bias, do)


BIAS_LO = REL_CLIP - (CHUNK - 1)
BIAS_FAR = 2 * REL_CLIP
BIAS_NEAR0 = BAND_W // 2


def _band_index(col0, ncol):
    i = lax.broadcasted_iota(jnp.int32, (BAND_TQ, ncol), 0)
    j = lax.broadcasted_iota(jnp.int32, (BAND_TQ, ncol), 1) + col0
    idx = jnp.clip(i + 2 * BAND_TQ - j, -REL_CLIP, REL_CLIP) + REL_CLIP
    ci, cj = i // CHUNK, j // CHUNK
    return jnp.where((ci <= cj) & (cj <= ci + LEFT_CHUNKS), idx, -1)


def _skew(x, right):
    row = lax.broadcasted_iota(jnp.int32, (BAND_TQ, 1), 0)
    for b in range(BAND_TQ.bit_length() - 1):
        shift = (1 << b) if right else BAND_W - (1 << b)
        x = jnp.where(((row >> b) & 1) == 1, pltpu.roll(x, shift, 1), x)
    return x


SKEW_NEAR0 = 2 * BAND_TQ - REL_CLIP + 1
SKEW_NEAR1 = 2 * BAND_TQ + CHUNK
SKEW_WRAP0 = BAND_W - (CHUNK - 1)


def _bias_build(rel_table):
    far = rel_table[:, BIAS_FAR:]
    base = jnp.concatenate([jnp.broadcast_to(far, (CA_H, SKEW_NEAR0)), rel_table[:, BIAS_LO:BIAS_FAR][:, ::-1],
                            jnp.broadcast_to(far, (CA_H, BAND_W - SKEW_NEAR1))], axis=1)[:, None, :]

    def body(base_ref, out_ref):
        valid = _band_index(0, BAND_W) >= 0
        for h in range(CA_H):
            tile = _skew(jnp.broadcast_to(base_ref[h] * LOG2E, (BAND_TQ, BAND_W)), right=True)
            out_ref[h] = jnp.where(valid, tile, NEG)

    return pl.pallas_call(
        body, name="bias_build",
        out_shape=jax.ShapeDtypeStruct((CA_H, BAND_TQ, BAND_W), F32),
        in_specs=[pl.BlockSpec(memory_space=pltpu.VMEM)],
        out_specs=pl.BlockSpec(memory_space=pltpu.VMEM),
        compiler_params=pltpu.CompilerParams(vmem_limit_bytes=40 << 20),
    )(base)


def _bias_fold(db):
    def body(db_ref, sums_ref, far_ref):
        col = lax.broadcasted_iota(jnp.int32, (1, BAND_W), 1)
        is_far = (col < SKEW_NEAR0) | (col >= SKEW_WRAP0)
        for h in range(CA_H):
            sums = jnp.sum(_skew(db_ref[h], right=False), axis=0, keepdims=True)
            sums_ref[h] = sums
            far_ref[h] = jnp.broadcast_to(jnp.sum(jnp.where(is_far, sums, 0.0), axis=1, keepdims=True), (1, 128))

    sums, far = pl.pallas_call(
        body, name="bias_fold",
        out_shape=[jax.ShapeDtypeStruct((CA_H, 1, BAND_W), F32), jax.ShapeDtypeStruct((CA_H, 1, 128), F32)],
        in_specs=[pl.BlockSpec(memory_space=pltpu.VMEM)],
        out_specs=[pl.BlockSpec(memory_space=pltpu.VMEM), pl.BlockSpec(memory_space=pltpu.VMEM)],
        compiler_params=pltpu.CompilerParams(vmem_limit_bytes=40 << 20),
    )(db)
    near = sums[:, 0, SKEW_NEAR0:SKEW_NEAR1][:, ::-1]
    return jnp.concatenate([jnp.zeros((CA_H, BIAS_LO), F32), near, far[:, 0, :1]], axis=1)


def _flash_bwd(qh, kh, ktr, vh, lse, delta, do, send=None):
    S = qh.shape[0]
    T = min(512, S)
    TS = min(FLASH_SUB, T)
    n = S // T
    qt, kt = _tri_tables(n, q_major=False)

    def body(qt_ref, kt_ref, q_ref, k_ref, ktr_ref, v_ref, lse_ref, dl_ref, do_ref, *rest):
        if send is None:
            dqt_ref, dk_ref, dv_ref, dk_scr, dv_scr = rest
        else:
            send_ref, dqt_ref, dk_ref, dv_ref, recv_ref, dk_scr, dv_scr, *sems = rest
            _hosted_comm(send_ref, recv_ref, sems, same_source=False)
        t = pl.program_id(1)
        qi, ki = qt_ref[t], kt_ref[t]

        @pl.when(t == 0)
        def _():
            dqt_ref[...] = jnp.zeros(dqt_ref.shape, F32)

        @pl.when(qi == ki)
        def _():
            dk_scr[...] = jnp.zeros(dk_scr.shape, F32)
            dv_scr[...] = jnp.zeros(dv_scr.shape, F32)

        def step(diag):
            head = lax.broadcasted_iota(jnp.int32, (1, 128), 1) // 64
            for j in range(2):
                sl = slice(128 * j, 128 * (j + 1))
                for qs in range(T // TS):
                    cq = slice(qs * TS, (qs + 1) * TS)
                    qsub = q_ref[cq, sl]
                    doj = jnp.where(head == j, do_ref[cq, :], 0.0).astype(MXU_DTYPE)
                    lse, dlt = lse_ref[0, j:j + 1, cq], dl_ref[0, j:j + 1, cq]
                    cols = pl.ds(pl.multiple_of(qi * T + qs * TS, TS), TS)
                    for ks in range(T // TS):
                        if diag and ks > qs:
                            continue
                        rk = slice(ks * TS, (ks + 1) * TS)
                        s = _mm_nt(k_ref[rk, sl], qsub)
                        if diag and ks == qs:
                            s = jnp.where(_chunk_mask_t(TS), s, NEG)
                        p = jnp.exp2(s - lse)
                        dv_scr[rk, :] += _mm(p, doj)
                        dp = _mm_nt(v_ref[rk, :], doj)
                        ds = (p * (dp - dlt)).astype(MXU_DTYPE)
                        dk_scr[rk, sl] += _mm(ds, qsub)
                        dqt_ref[sl, cols] += _mm(ktr_ref[sl, rk], ds) * MLA_SCALE

        @pl.when(qi > ki)
        def _():
            step(False)

        @pl.when(qi == ki)
        def _():
            step(True)

        @pl.when(qi == n - 1)
        def _():
            dk_ref[...] = dk_scr[...] * 0.6931471805599453
            dv_ref[...] = dv_scr[...]

    qmap = lambda hp, t, qt, kt: (qt[t], hp)
    kmap = lambda hp, t, qt, kt: (kt[t], hp)
    stat = pl.BlockSpec((1, 2, T), lambda hp, t, qt, kt: (hp, 0, qt[t]))
    hbm = pl.BlockSpec(memory_space=pl.ANY)
    hosted = send is not None
    recv_shape = [jax.ShapeDtypeStruct(send.shape, send.dtype)] if hosted else []
    return pl.pallas_call(
        body, name="flash_bwd_exchange" if hosted else "flash_bwd",
        grid_spec=pltpu.PrefetchScalarGridSpec(
            num_scalar_prefetch=2, grid=(MLA_H // 2, qt.shape[0]),
            in_specs=[pl.BlockSpec((T, 256), qmap), pl.BlockSpec((T, 256), kmap),
                      pl.BlockSpec((256, T), lambda hp, t, qt, kt: (hp, kt[t])), pl.BlockSpec((T, 128), kmap),
                      stat, stat, pl.BlockSpec((T, 128), qmap)] + [hbm] * hosted,
            out_specs=[pl.BlockSpec((256, S), lambda hp, t, qt, kt: (hp, 0)), pl.BlockSpec((T, 256), kmap),
                       pl.BlockSpec((T, 128), kmap)] + [hbm] * hosted,
            scratch_shapes=[pltpu.VMEM((T, 256), F32), pltpu.VMEM((T, 128), F32)] + COMM_SEMS * hosted),
        out_shape=[jax.ShapeDtypeStruct((1024, S), F32), jax.ShapeDtypeStruct((S, 1024), F32),
                   jax.ShapeDtypeStruct((S, BW), F32)] + recv_shape,
        compiler_params=_params(("arbitrary", "arbitrary"), 56),
    )(qt, kt, qh, kh, ktr, vh, lse, delta, do, *([send] * hosted))


def _mla_prep_bwd(proj, dqf, dkf, dvf, gq, gkv, wuq_t, wuk_t, wuv_t, cos_t, sin_a, sin_b):
    S = proj.shape[0]
    tm = min(512, S)

    def body(p_ref, dq_ref, dk_ref, dv_ref, gq_ref, gkv_ref, wq_ref, wk_ref, wv_ref, c_ref, sa_ref, sb_ref,
             db_ref, dwq_ref, dwk_ref, dwv_ref, dgq_ref, dgkv_ref):
        _first_step_zero((dwq_ref, dwk_ref, dwv_ref, dgq_ref, dgkv_ref), pl.program_id(0) == 0)
        c, sa, sb = c_ref[...], sa_ref[...], sb_ref[...]
        nq, rq = _rms(p_ref[:, 0:Q_RANK])
        nkv, rkv = _rms(p_ref[:, Q_RANK:Q_RANK + KV_RANK])
        cq = (nq * gq_ref[...]).astype(MXU_DTYPE)
        ckv = (nkv * gkv_ref[...]).astype(MXU_DTYPE)
        dkr = jnp.zeros((tm, 128), F32)
        dq_pre = []
        for h in range(MLA_H):
            sl = slice(128 * h, 128 * (h + 1))
            dq_pre.append(_rope_t(jnp.transpose(dq_ref[sl, :]), c, sa, sb).astype(MXU_DTYPE))
            dkr = dkr + dk_ref[:, sl]
        dq_pre = jnp.concatenate(dq_pre, axis=1)
        dcq = _mm(dq_pre, wq_ref[...])
        dwq_ref[...] += _mm_tn(dq_pre, cq)
        dgq_ref[...] += jnp.sum(dcq * nq, axis=0, keepdims=True)
        db_ref[:, 0:Q_RANK] = _rms_bwd(dcq * gq_ref[...], nq, rq).astype(db_ref.dtype)
        dk = dk_ref[...].astype(MXU_DTYPE)
        dv = dv_ref[...].astype(MXU_DTYPE)
        dckv = _mm(dk, wk_ref[...]) + _mm(dv, wv_ref[...])
        dwk_ref[...] += _mm_tn(dk, ckv)
        dwv_ref[...] += _mm_tn(dv, ckv)
        dgkv_ref[...] += jnp.sum(dckv * nkv, axis=0, keepdims=True)
        db_ref[:, Q_RANK:Q_RANK + KV_RANK] = _rms_bwd(dckv * gkv_ref[...], nkv, rkv).astype(db_ref.dtype)
        lane = lax.broadcasted_iota(jnp.int32, (1, 128), 1)
        rope_lanes = (lane >= MLA_NOPE) & (lane < MLA_QK)
        db_ref[:, Q_RANK + KV_RANK:BW] = _rope_t(jnp.where(rope_lanes, dkr, 0.0), c, sa, sb).astype(db_ref.dtype)

    full = lambda a: pl.BlockSpec(a.shape, lambda i: (0,) * a.ndim)
    fulls = lambda shape: pl.BlockSpec(shape, lambda i: (0,) * len(shape))
    tab = pl.BlockSpec((tm, 128), lambda i: (i, 0))
    row = lambda w, cb=0: pl.BlockSpec((tm, w), lambda i, cb=cb: (i, cb))
    return pl.pallas_call(
        body, name="mla_prep_bwd", grid=(S // tm,),
        in_specs=[row(BW, QD // BW), pl.BlockSpec((1024, tm), lambda i: (0, i)), row(1024), row(BW),
                  full(gq), full(gkv), full(wuq_t),
                  full(wuk_t), full(wuv_t), tab, tab, tab],
        out_specs=[row(BW), fulls((1024, Q_RANK)), fulls((1024, KV_RANK)), fulls((BW, KV_RANK)),
                   fulls((1, Q_RANK)), fulls((1, KV_RANK))],
        out_shape=[jax.ShapeDtypeStruct((S, BW), MXU_DTYPE), jax.ShapeDtypeStruct((1024, Q_RANK), F32),
                   jax.ShapeDtypeStruct((1024, KV_RANK), F32), jax.ShapeDtypeStruct((BW, KV_RANK), F32),
                   jax.ShapeDtypeStruct((1, Q_RANK), F32), jax.ShapeDtypeStruct((1, KV_RANK), F32)],
        compiler_params=_params(("arbitrary",), 56),
    )(proj, dqf, dkf, dvf, gq, gkv, wuq_t, wuk_t, wuv_t, cos_t, sin_a, sin_b)


def _sgu_bwd(proj, dya, ln_g, ln_b, ws, bfull):
    S = proj.shape[0]
    tm = min(512, S)

    def body(u_ref, v_ref, z_ref, dy_ref, g_ref, b_ref, ws_ref, bf_ref,
             da_ref, dws_ref, dbf_ref, dlg_ref, dlb_ref, dbs_ref):
        i = pl.program_id(0)
        _first_step_zero((dws_ref, dbf_ref, dlg_ref, dlb_ref, dbs_ref), i == 0)
        tri = _sgu_tri()
        ws_m = [jnp.where(tri, ws_ref[g], 0.0).astype(MXU_DTYPE) for g in range(SGU_G)]
        grp = lax.broadcasted_iota(jnp.int32, (1, BW), 1) // (BW // SGU_G)
        for b in range(tm // SGU_T):
            r = slice(b * SGU_T, (b + 1) * SGU_T)
            vln, vhat, rstd = _layernorm(v_ref[r, :], g_ref[...], b_ref[...])
            vb = vln.astype(MXU_DTYPE)
            mixed = _sgu_mix(ws_m, vb, bf_ref[...], grp)
            u, z, dy = u_ref[r, :], z_ref[r, :], dy_ref[r, :]
            sg = _sigmoid(z)
            sz = z * sg
            da_ref[r, 0:BW] = (dy * mixed * sz).astype(da_ref.dtype)
            da_ref[r, 2 * BW:3 * BW] = (dy * u * mixed * (sg * (1.0 + z * (1.0 - sg)))).astype(da_ref.dtype)
            dmix = dy * u * sz
            dbf_ref[...] += dmix
            dvln = jnp.zeros((SGU_T, BW), F32)
            for g in range(SGU_G):
                dmg = jnp.where(grp == g, dmix, 0.0).astype(MXU_DTYPE)
                dvln = dvln + _mm_tn(ws_m[g], dmg)
                dws_ref[g] += jnp.where(tri, _mm_nt(dmg, vb), 0.0)
            dlg_ref[...] += jnp.sum(dvln * vhat, axis=0, keepdims=True)
            dlb_ref[...] += jnp.sum(dvln, axis=0, keepdims=True)
            dvh = dvln * g_ref[...]
            da_ref[r, BW:2 * BW] = (rstd * (dvh - jnp.mean(dvh, axis=-1, keepdims=True)
                                            - vhat * jnp.mean(dvh * vhat, axis=-1, keepdims=True))).astype(da_ref.dtype)

        @pl.when(i == pl.num_programs(0) - 1)
        def _():
            dbf = dbf_ref[...]
            for g in range(SGU_G):
                dbs_ref[:, g:g + 1] = jnp.sum(jnp.where(grp == g, dbf, 0.0), axis=1, keepdims=True)

    blk = lambda cb: pl.BlockSpec((tm, BW), lambda i, cb=cb: (i, cb))
    fulls = lambda shape: pl.BlockSpec(shape, lambda i: (0,) * len(shape))
    return pl.pallas_call(
        body, name="sgu_bwd", grid=(S // tm,),
        in_specs=[blk(UA // BW), blk(VA // BW), blk(ZA // BW), blk(0), fulls((1, BW)), fulls((1, BW)),
                  fulls((SGU_G, SGU_T, SGU_T)), fulls((SGU_T, BW))],
        out_specs=[pl.BlockSpec((tm, 3 * BW), lambda i: (i, 0)), fulls((SGU_G, SGU_T, SGU_T)),
                   fulls((SGU_T, BW)), fulls((1, BW)), fulls((1, BW)), fulls((SGU_T, SGU_G))],
        out_shape=[jax.ShapeDtypeStruct((S, 3 * BW), MXU_DTYPE), jax.ShapeDtypeStruct((SGU_G, SGU_T, SGU_T), F32),
                   jax.ShapeDtypeStruct((SGU_T, BW), F32), jax.ShapeDtypeStruct((1, BW), F32),
                   jax.ShapeDtypeStruct((1, BW), F32), jax.ShapeDtypeStruct((SGU_T, SGU_G), F32)],
        compiler_params=_params(("arbitrary",), 40),
    )(proj, proj, proj, dya, ln_g, ln_b, ws, bfull)


DX_TK = 3 * BW


def _inproj_bwd_dx(dgl, da, small, w_t, x, pre_g, dxo):
    S = x.shape[0]
    tm = min(512, S)
    nk = DP // DX_TK

    def body(g_ref, a_ref, s0, s1, s2, s3, s4, s5, w_ref, x_ref, pg_ref, dxo_ref, dx_ref, dg_ref, acc):
        i, k = pl.program_id(0), pl.program_id(1)
        _first_step_zero((dg_ref,), (i == 0) & (k == 0))

        @pl.when(k == 0)
        def _():
            acc[...] = jnp.zeros(acc.shape, F32)

        @pl.when(k < 2)
        def _():
            acc[...] += _mm(g_ref[...], w_ref[...])

        @pl.when(k == 2)
        def _():
            acc[...] += _mm(a_ref[...], w_ref[...])

        for kk, trio in ((3, (s0, s1, s2)), (4, (s3, s4, s5))):
            @pl.when(k == kk)
            def _(trio=trio):
                t = acc[...]
                for n, ref in enumerate(trio):
                    t = t + _mm(ref[...], w_ref[n * BW:(n + 1) * BW, :])
                acc[...] = t

        @pl.when(k == nk - 1)
        def _():
            n, r = _rms(x_ref[...])
            dxn = acc[...]
            dg_ref[...] += jnp.sum(dxn * n, axis=0, keepdims=True)
            dx_ref[...] = dxo_ref[...] + _rms_bwd(dxn * pg_ref[...], n, r)

    row = pl.BlockSpec((tm, D), lambda i, k: (i, 0))
    vec = pl.BlockSpec((1, D), lambda i, k: (0, 0))
    fixed = lambda w: pl.BlockSpec((tm, w), lambda i, k: (i, 0))
    return pl.pallas_call(
        body, name="inproj_bwd_dx", grid=(S // tm, nk),
        in_specs=[pl.BlockSpec((tm, DX_TK), lambda i, k: (i, jnp.minimum(k, 1))), fixed(DX_TK)]
                 + [fixed(BW)] * 6 + [pl.BlockSpec((DX_TK, D), lambda i, k: (k, 0)), row, vec, row],
        out_specs=[row, vec],
        out_shape=[jax.ShapeDtypeStruct((S, D), F32), jax.ShapeDtypeStruct((1, D), F32)],
        scratch_shapes=[pltpu.VMEM((tm, D), F32)],
        compiler_params=_params(("arbitrary", "arbitrary"), 56),
    )(dgl, da, *small, w_t, x, pre_g, dxo)


def _matmul_tn(segs, b):
    S, K = b.shape
    tk = min(1024, S)
    widths = [a.shape[1] for a in segs]
    offs = [sum(widths[:n]) for n in range(len(segs))]

    def body(*refs):
        seg_refs, b_ref, o_ref = refs[:len(segs)], refs[len(segs)], refs[len(segs) + 1]
        _first_step_zero((o_ref,), pl.program_id(0) == 0)
        bv = b_ref[...]
        for ref, off, w in zip(seg_refs, offs, widths):
            o_ref[off:off + w, :] += _mm_tn(ref[...], bv)

    return pl.pallas_call(
        body, name="matmul_tn", grid=(S // tk,),
        in_specs=[pl.BlockSpec((tk, w), lambda k: (k, 0)) for w in widths] + [pl.BlockSpec((tk, K), lambda k: (k, 0))],
        out_specs=pl.BlockSpec((sum(widths), K), lambda k: (0, 0)),
        out_shape=jax.ShapeDtypeStruct((sum(widths), K), F32),
        compiler_params=_params(("arbitrary",), 56),
    )(*segs, b)


def _pad_rows_w_in(wt):
    z = lambda n: jnp.zeros((n, wt.shape[1]), wt.dtype)
    return jnp.concatenate([wt[4512:], wt[:1920], z(64), wt[1920:1952], z(32), wt[1952:4512]], axis=0)


def _unpad_rows_w_in(dwt):
    a = dwt[3 * D:]
    return jnp.concatenate([a[:1920], a[1984:2016], a[2048:], dwt[:3 * D]], axis=0)


def _pad_head_rows(wt, width):
    k = wt.shape[1]
    return jnp.pad(wt.reshape(MLA_H, width, k), ((0, 0), (0, 128 - width), (0, 0))).reshape(MLA_H * 128, k)


def _unpad_head_rows(wt, width):
    k = wt.shape[1]
    return wt.reshape(MLA_H, 128, k)[:, :width].reshape(MLA_H * width, k)


def _rope_tables(S):
    half = MLA_ROPE // 2
    inv = 10000.0 ** (-jnp.arange(half, dtype=F32) / half)
    ang = jnp.arange(S, dtype=F32)[:, None] * inv[None, :]
    cos, sin = jnp.cos(ang), jnp.sin(ang)
    one = lambda n: jnp.ones((S, n), F32)
    zero = lambda n: jnp.zeros((S, n), F32)
    cos_t = jnp.concatenate([one(MLA_NOPE), cos, cos, one(128 - MLA_QK)], axis=1)
    sin_a = jnp.concatenate([zero(MLA_NOPE), -sin, zero(128 - MLA_NOPE - half)], axis=1)
    sin_b = jnp.concatenate([zero(MLA_NOPE + half), sin, zero(128 - MLA_QK)], axis=1)
    return cos_t, sin_a, sin_b


SHARDED = ("w_in", "mla_w_uq", "mla_w_ukv", "w_branch", "w_out")
SHARD_PACK = SHARDED + ("gate_b",)
REPLICATED = ("pre_g", "post_g", "sgu_ln_g", "sgu_ln_b", "sgu_w", "sgu_b", "mla_q_norm_g", "mla_kv_norm_g",
              "ca_rel_bias")
OUT_ORDER = ("w_in", "pre_g", "post_g", "sgu_ln_g", "sgu_ln_b", "sgu_w", "sgu_b", "mla_q_norm_g", "mla_kv_norm_g",
             "mla_w_uq", "mla_w_ukv", "ca_rel_bias", "w_branch", "gate_b", "w_out")
TRANSPOSED = {"w_in": (0, 2, 1), "mla_w_uq": (0, 2, 1), "mla_w_ukv": (0, 2, 1), "w_branch": (0, 1, 3, 2)}


def _canon(name, a):
    return jnp.transpose(a, TRANSPOSED[name]) if name in TRANSPOSED else a


ROW_MULT = 64
PIECE_MULT = 16


def _piece_rows(shape):
    size = 1
    for s in shape:
        size *= s
    return -(-size // (1024 * PIECE_MULT)) * PIECE_MULT


def _pack_rows(pieces, lead, mult=ROW_MULT):
    rows = []
    for a in pieces:
        flat = a.reshape(a.shape[:lead] + (-1,))
        nr = _piece_rows(a.shape[lead:])
        flat = jnp.pad(flat, [(0, 0)] * lead + [(0, nr * 1024 - flat.shape[-1])])
        rows.append(flat.reshape(a.shape[:lead] + (nr, 1024)))
    total = sum(r.shape[lead] for r in rows)
    tail = (-total) % mult
    if tail:
        rows.append(jnp.zeros(rows[0].shape[:lead] + (tail, 1024), rows[0].dtype))
    return jnp.concatenate(rows, axis=lead)


def _unpack_rows(p, shapes, lead):
    out, r = [], 0
    for s in shapes:
        nr, size = _piece_rows(s), 1
        for d in s:
            size *= d
        piece = lax.slice_in_dim(p, r, r + nr, axis=lead).reshape(p.shape[:lead] + (-1,))
        out.append(lax.slice_in_dim(piece, 0, size, axis=lead).reshape(p.shape[:lead] + tuple(s)))
        r += nr
    return out


def kernel(x, w_in, pre_g, post_g, sgu_ln_g, sgu_ln_b, sgu_w, sgu_b, mla_q_norm_g, mla_kv_norm_g, mla_w_uq, mla_w_ukv, ca_rel_bias, w_branch, gate_b, w_out, loss_target, m_w_in, m_pre_g, m_post_g, m_sgu_ln_g, m_sgu_ln_b, m_sgu_w, m_sgu_b, m_mla_q_norm_g, m_mla_kv_norm_g, m_mla_w_uq, m_mla_w_ukv, m_ca_rel_bias, m_w_branch, m_gate_b, m_w_out, v_w_in, v_pre_g, v_post_g, v_sgu_ln_g, v_sgu_ln_b, v_sgu_w, v_sgu_b, v_mla_q_norm_g, v_mla_kv_norm_g, v_mla_w_uq, v_mla_w_ukv, v_ca_rel_bias, v_w_branch, v_gate_b, v_w_out):
    weights = dict(w_in=w_in, pre_g=pre_g, post_g=post_g, sgu_ln_g=sgu_ln_g, sgu_ln_b=sgu_ln_b, sgu_w=sgu_w,
                   sgu_b=sgu_b, mla_q_norm_g=mla_q_norm_g, mla_kv_norm_g=mla_kv_norm_g, mla_w_uq=mla_w_uq,
                   mla_w_ukv=mla_w_ukv, ca_rel_bias=ca_rel_bias, w_branch=w_branch, gate_b=gate_b, w_out=w_out)
    mom_m = dict(w_in=m_w_in, pre_g=m_pre_g, post_g=m_post_g, sgu_ln_g=m_sgu_ln_g, sgu_ln_b=m_sgu_ln_b,
                 sgu_w=m_sgu_w, sgu_b=m_sgu_b, mla_q_norm_g=m_mla_q_norm_g, mla_kv_norm_g=m_mla_kv_norm_g,
                 mla_w_uq=m_mla_w_uq, mla_w_ukv=m_mla_w_ukv, ca_rel_bias=m_ca_rel_bias, w_branch=m_w_branch,
                 gate_b=m_gate_b, w_out=m_w_out)
    mom_v = dict(w_in=v_w_in, pre_g=v_pre_g, post_g=v_post_g, sgu_ln_g=v_sgu_ln_g, sgu_ln_b=v_sgu_ln_b,
                 sgu_w=v_sgu_w, sgu_b=v_sgu_b, mla_q_norm_g=v_mla_q_norm_g, mla_kv_norm_g=v_mla_kv_norm_g,
                 mla_w_uq=v_mla_w_uq, mla_w_ukv=v_mla_w_ukv, ca_rel_bias=v_ca_rel_bias, w_branch=v_w_branch,
                 gate_b=v_gate_b, w_out=v_w_out)
    depth = w_in.shape[0]
    S = x.shape[1]
    xs = x.reshape(S, D)

    cw = {n: _canon(n, weights[n]) for n in SHARD_PACK}

    def layer_block(l):
        return _pack_rows([cw[n][l].astype(MXU_DTYPE) for n in SHARDED], 0, PIECE_MULT)

    def split_block(got):
        return dict(zip(SHARDED, _unpack_rows(got, [cw[n].shape[1:] for n in SHARDED], 1)))

    gate_all, = _unpack_rows(_all_gather(_pack_rows([gate_b], 0, PIECE_MULT)), [gate_b.shape], 1)
    cos_t, sin_a, sin_b = _rope_tables(S)

    def layer_weights(gl, l):
        w_t = _pad_rows_w_in(gl["w_in"].reshape(D_IN, D))
        wuq_t = _pad_head_rows(gl["mla_w_uq"].reshape(MLA_H * MLA_QK, Q_RANK), MLA_QK)
        ukv_t = gl["mla_w_ukv"]
        wuk_t = jnp.pad(ukv_t[:, :MLA_NOPE], ((0, 0), (0, 128 - MLA_NOPE), (0, 0))).reshape(MLA_H * 128, KV_RANK)
        wuv_t = ukv_t[:, MLA_NOPE:].reshape(BW, KV_RANK)
        wbr_t = jnp.swapaxes(gl["w_branch"], 0, 1).reshape(N_BRANCH, D, BW)
        return dict(
            w_t=w_t, w_pad=w_t.T, wuq_t=wuq_t, wuq=wuq_t.T, wuk_t=wuk_t, wuk=wuk_t.T, wuv_t=wuv_t, wuv=wuv_t.T,
            wbr_t=wbr_t, wbr=jnp.swapaxes(wbr_t, 1, 2), wout=gl["w_out"].reshape(D, D),
            gate_b=jnp.swapaxes(gate_all[:, l], 0, 1).reshape(N_BRANCH, D),
            pre_g=pre_g[l][None], post_g=post_g[l][None], ln_g=sgu_ln_g[l][None], ln_b=sgu_ln_b[l][None],
            ws=sgu_w[l], bfull=jnp.repeat(sgu_b[l].T, BW // SGU_G, axis=1),
            gq=mla_q_norm_g[l][None], gkv=mla_kv_norm_g[l][None], bias=_bias_build(ca_rel_bias[l]))

    layers, saved = [], []
    h_x = xs
    got = _all_gather(layer_block(0))
    for l in range(depth):
        lw = layer_weights(split_block(got), l)
        layers.append(lw)
        proj, xn = _inproj_fwd(h_x, lw["pre_g"], lw["w_pad"])
        ya = _sgu_fwd(proj, lw["ln_g"], lw["ln_b"], lw["ws"], lw["bfull"])
        qh, kh, vh, ktr, vtr = _mla_prep_fwd(proj, lw["gq"], lw["gkv"], lw["wuq"], lw["wuk"], lw["wuv"],
                                             cos_t, sin_a, sin_b)
        if l + 1 < depth:
            ob, lse, got = _flash_fwd(qh, kh, vtr, gather=layer_block(l + 1))
        else:
            ob, lse = _flash_fwd(qh, kh, vtr)
        oc = _band_fwd(proj, lw["bias"])
        x_new, merged, hh = _merge_fwd(h_x, ya, ob, oc, proj, lw["gate_b"], lw["wbr"], lw["wout"], lw["post_g"])
        saved.append(dict(x=h_x, proj=proj, xn=xn, ya=ya, qh=qh, kh=kh, vh=vh, ktr=ktr, ob=ob, lse=lse, oc=oc,
                          merged=merged, h=hh))
        h_x = x_new

    loss_part, dx = _loss_fwd_bwd(h_x, loss_target.reshape(S, D))
    loss = lax.psum(loss_part[0, 0], MESH_AXES)

    rows = {}
    rep = {n: [None] * depth for n in REPLICATED}
    recvs = [None] * depth
    pending = None
    for l in reversed(range(depth)):
        lw, sv = layers[l], saved[l]
        proj = sv["proj"]
        dmerged, dw_out, dg_post = _out_bwd(dx, sv["h"], sv["merged"], lw["post_g"], lw["wout"].T)
        dgl, dzb, dzc, dya, dob, doc, dwbr_t, dgb = _gate_bwd(
            dmerged, sv["ya"], sv["ob"], sv["oc"], proj, lw["gate_b"], lw["wbr"], lw["wbr_t"])
        dqc, dkc, dvc, dbias = _band_bwd(proj, lw["bias"], doc)
        drel = _bias_fold(dbias)
        flash_args = (sv["qh"], sv["kh"], sv["ktr"], sv["vh"], sv["lse"], _attn_delta(sv["ob"], dob), dob)
        if pending is None:
            dqf, dkf, dvf = _flash_bwd(*flash_args)
        else:
            dqf, dkf, dvf, recvs[l + 1] = _flash_bwd(*flash_args, send=pending)
        db, dwuq_t, dwuk_t, dwuv_t, dgq, dgkv = _mla_prep_bwd(
            proj, dqf, dkf, dvf, lw["gq"], lw["gkv"], lw["wuq_t"], lw["wuk_t"], lw["wuv_t"], cos_t, sin_a, sin_b)
        da, dws, _, dlg, dlb, dbs = _sgu_bwd(proj, dya, lw["ln_g"], lw["ln_b"], lw["ws"], lw["bfull"])
        small = (db, dzb, dqc, dkc, dvc, dzc)
        dx, dg_pre = _inproj_bwd_dx(dgl, da, small, lw["w_t"], sv["x"], lw["pre_g"], dx)
        dw_t = jnp.concatenate([_matmul_tn([dgl], sv["xn"]), _matmul_tn([da, db, dzb], sv["xn"]),
                                _matmul_tn([dqc, dkc, dvc, dzc], sv["xn"])], axis=0)

        rows["w_in"] = _unpad_rows_w_in(dw_t).reshape(N_DEV, D_IN // N_DEV, D)
        rows["mla_w_uq"] = _unpad_head_rows(dwuq_t, MLA_QK).reshape(N_DEV, MLA_QK, Q_RANK)
        dk3 = dwuk_t.reshape(MLA_H, 128, KV_RANK)[:, :MLA_NOPE]
        dv3 = dwuv_t.reshape(MLA_H, 64, KV_RANK)
        rows["mla_w_ukv"] = jnp.concatenate([dk3, dv3], axis=1)
        rows["w_branch"] = jnp.swapaxes(dwbr_t.reshape(N_BRANCH, N_DEV, D // N_DEV, BW), 0, 1)
        rows["w_out"] = dw_out.reshape(N_DEV, D // N_DEV, D)
        rows["gate_b"] = jnp.swapaxes(dgb.reshape(N_BRANCH, N_DEV, D // N_DEV), 0, 1)
        pending = _pack_rows([rows[n].astype(MXU_DTYPE) for n in SHARD_PACK], 1)
        rep["pre_g"][l] = dg_pre[0]
        rep["post_g"][l] = dg_post[0]
        rep["sgu_ln_g"][l] = dlg[0]
        rep["sgu_ln_b"][l] = dlb[0]
        rep["sgu_w"][l] = dws
        rep["sgu_b"][l] = dbs.T
        rep["mla_q_norm_g"][l] = dgq[0]
        rep["mla_kv_norm_g"][l] = dgkv[0]
        rep["ca_rel_bias"][l] = drel
    grad_x = dx.reshape(x.shape)

    recvs[0] = _exchange(pending)

    cm = {n: _canon(n, mom_m[n]) for n in SHARD_PACK}
    cv = {n: _canon(n, mom_v[n]) for n in SHARD_PACK}
    shard_shapes = [cw[n].shape[1:] for n in SHARD_PACK]
    per_layer = []
    for l in range(depth):
        at = lambda d: _pack_rows([d[n][l] for n in SHARD_PACK], 0)
        outs_l = _reduce_adamw(recvs[l], at(cw), at(cm), at(cv))
        per_layer.append([_unpack_rows(p, shard_shapes, 0) for p in outs_l])
    outs_sh = [{n: _canon(n, jnp.stack([per_layer[l][k][i] for l in range(depth)]))
                for i, n in enumerate(SHARD_PACK)} for k in range(4)]

    rp = lambda d: _pack_rows([d[n] for n in REPLICATED], 0)
    allp = _all_gather(rp({n: jnp.stack(rep[n]) for n in REPLICATED}))
    outs_rep = _reduce_adamw(allp, rp(weights), rp(mom_m), rp(mom_v))
    rep_shapes = [weights[n].shape for n in REPLICATED]
    outs_rep = [dict(zip(REPLICATED, _unpack_rows(p, rep_shapes, 0))) for p in outs_rep]

    outs = [{**a, **b} for a, b in zip(outs_sh, outs_rep)]
    return (loss, grad_x, *[o[n] for o in outs for n in OUT_ORDER])
```

```python
import functools

import jax
import jax.numpy as jnp
from jax import lax
from jax.experimental import pallas as pl
from jax.experimental.pallas import tpu as pltpu

F32 = jnp.float32
MXU_DTYPE = jnp.bfloat16
EPS = 1e-6
NEG = -1e30
MESH_AXES = ("x", "y", "c")
N_DEV = 8

D = 1024
BW = 512
N_BRANCH = 3
CHUNK = 64
SGU_T = 128
SGU_G = 8
MLA_H = 8
MLA_NOPE = 64
MLA_ROPE = 32
MLA_QK = MLA_NOPE + MLA_ROPE
Q_RANK = 256
KV_RANK = 128
CA_H = 8
CA_DH = 64
LEFT_CHUNKS = 8
REL_CLIP = 128
D_IN = 7584

G_OFF, UA, VA, ZA, QD, ZB, QC, KC, VC, ZC, DP = 0, 3072, 3584, 4096, 4608, 5120, 5632, 6144, 6656, 7168, 7680
BAND_TQ = 256
BAND_W = 3 * BAND_TQ

ADAM_LR, ADAM_B1, ADAM_B2, ADAM_EPS, ADAM_WD, ADAM_STEP = 0.001, 0.9, 0.999, 1e-08, 0.01, 10


def _params(sem, mib):
    return pltpu.CompilerParams(dimension_semantics=sem, vmem_limit_bytes=mib << 20)


def _mm(a, b):
    return jnp.dot(a.astype(MXU_DTYPE), b.astype(MXU_DTYPE), preferred_element_type=F32)


def _mm_nt(a, b):
    return lax.dot_general(a.astype(MXU_DTYPE), b.astype(MXU_DTYPE), (((1,), (1,)), ((), ())),
                           preferred_element_type=F32)


def _mm_tn(a, b):
    return lax.dot_general(a.astype(MXU_DTYPE), b.astype(MXU_DTYPE), (((0,), (0,)), ((), ())),
                           preferred_element_type=F32)


def _sigmoid(z):
    return 1.0 / (1.0 + jnp.exp(-z))


def _rms(x):
    r = lax.rsqrt(jnp.mean(x * x, axis=-1, keepdims=True) + EPS)
    return x * r, r


def _rms_bwd(dn, n, r):
    return r * (dn - n * jnp.mean(dn * n, axis=-1, keepdims=True))


def _rope(b, c, sa, sb):
    return b * c + pltpu.roll(b, 112, 1) * sa + pltpu.roll(b, 16, 1) * sb


def _rope_t(d, c, sa, sb):
    return d * c + pltpu.roll(d * sa, 16, 1) + pltpu.roll(d * sb, 112, 1)


def _all_gather(blk):
    R = blk.shape[0]

    def body(x_ref, out_ref, send_sems, recv_sems, local_sem):
        x, y, c = lax.axis_index("x"), lax.axis_index("y"), lax.axis_index("c")
        me, sibling = (x, y, c), (x, y, 1 - c)
        chips = [(1 - x, y), (x, 1 - y), (1 - x, 1 - y)]

        def slot(px, py, pc):
            return out_ref.at[4 * px + 2 * py + pc]

        def copy(k, block, to, src=None):
            return pltpu.make_async_remote_copy(
                src_ref=slot(*block) if src is None else src, dst_ref=slot(*block),
                send_sem=send_sems.at[k], recv_sem=recv_sems.at[k],
                device_id=to, device_id_type=pl.DeviceIdType.MESH)

        mine = pltpu.make_async_copy(x_ref, slot(*me), local_sem)
        mine.start()
        first = [copy(0, me, sibling, src=x_ref)]
        first += [copy(1 + j, me, (*chip, c), src=x_ref) for j, chip in enumerate(chips)]
        for cp in first:
            cp.start()
        passed = [copy(4 + j, (*chip, c), sibling) for j, chip in enumerate(chips)]
        for j, chip in enumerate(chips):
            copy(1 + j, (*chip, c), me).wait_recv()
            passed[j].start()
        copy(0, sibling, me).wait_recv()
        for j, chip in enumerate(chips):
            copy(4 + j, (*chip, 1 - c), me).wait_recv()
        for cp in first + passed:
            cp.wait_send()
        mine.wait()

    return pl.pallas_call(
        body, name="all_gather",
        out_shape=jax.ShapeDtypeStruct((N_DEV,) + blk.shape, blk.dtype),
        in_specs=[pl.BlockSpec(memory_space=pl.ANY)],
        out_specs=pl.BlockSpec(memory_space=pl.ANY),
        scratch_shapes=[pltpu.SemaphoreType.DMA((7,)), pltpu.SemaphoreType.DMA((7,)), pltpu.SemaphoreType.DMA(())],
    )(blk)


def _peer_copies(s_ref, r_ref, send_sems, recv_sems, local_sem, same_source):
    x, y, c = lax.axis_index("x"), lax.axis_index("y"), lax.axis_index("c")
    me = 4 * x + 2 * y + c

    def src(pid):
        return s_ref if same_source else s_ref.at[pid]

    def peer(k):
        px = 1 - x if (k >> 2) & 1 else x
        py = 1 - y if (k >> 1) & 1 else y
        pc = 1 - c if k & 1 else c
        return (px, py, pc), 4 * px + 2 * py + pc

    def remote(k, row):
        pos, pid = peer(k)
        return pltpu.make_async_remote_copy(
            src_ref=src(pid), dst_ref=r_ref.at[me if row is None else pid],
            send_sem=send_sems.at[k], recv_sem=recv_sems.at[k],
            device_id=pos, device_id_type=pl.DeviceIdType.MESH)

    def local():
        return pltpu.make_async_copy(src(me), r_ref.at[me], local_sem)

    def start():
        local().start()
        for k in range(1, N_DEV):
            remote(k, None).start()

    def wait():
        for k in range(1, N_DEV):
            remote(k, "peer").wait_recv()
        for k in range(1, N_DEV):
            remote(k, None).wait_send()
        local().wait()

    return start, wait


COMM_SEMS = [pltpu.SemaphoreType.DMA((N_DEV,)), pltpu.SemaphoreType.DMA((N_DEV,)), pltpu.SemaphoreType.DMA(())]


def _hosted_comm(s_ref, r_ref, sems, same_source):
    start, wait = _peer_copies(s_ref, r_ref, *sems, same_source=same_source)
    i, j = pl.program_id(0), pl.program_id(1)

    @pl.when((i == 0) & (j == 0))
    def _():
        start()

    @pl.when((i == pl.num_programs(0) - 1) & (j == pl.num_programs(1) - 1))
    def _():
        wait()


def _reduce_adamw(recv, w, m, v):
    R = w.shape[0]
    tr = next(t for t in (128, 64, 32, 16) if R % t == 0)
    c1 = 1.0 - ADAM_B1 ** ADAM_STEP
    c2 = 1.0 - ADAM_B2 ** ADAM_STEP

    def body(r_ref, w_ref, m_ref, v_ref, g_ref, d_ref, nm_ref, nv_ref):
        g = r_ref[0].astype(F32)
        for s in range(1, N_DEV):
            g = g + r_ref[s].astype(F32)
        m2 = ADAM_B1 * m_ref[...] + (1.0 - ADAM_B1) * g
        v2 = ADAM_B2 * v_ref[...] + (1.0 - ADAM_B2) * (g * g)
        m_hat = m2 / c1
        v_hat = v2 / c2
        g_ref[...] = g
        d_ref[...] = -ADAM_LR * (m_hat / (jnp.sqrt(v_hat) + ADAM_EPS) + ADAM_WD * w_ref[...])
        nm_ref[...] = m2
        nv_ref[...] = v2

    row = pl.BlockSpec((tr, 1024), lambda i: (i, 0))
    return pl.pallas_call(
        body, name="reduce_adamw", grid=(R // tr,),
        in_specs=[pl.BlockSpec((N_DEV, tr, 1024), lambda i: (0, i, 0)), row, row, row],
        out_specs=[row, row, row, row],
        out_shape=[jax.ShapeDtypeStruct((R, 1024), F32)] * 4,
        compiler_params=_params(("parallel",), 40),
    )(recv, w, m, v)


def _inproj_fwd(x, pre_g, w_pad):
    S = x.shape[0]
    tm, tn = min(1024, S), 1536

    def body(x_ref, g_ref, w_ref, proj_ref, xn_ref):
        @pl.when(pl.program_id(1) == 0)
        def _():
            n, _ = _rms(x_ref[...])
            xn_ref[...] = (n * g_ref[...]).astype(xn_ref.dtype)
        proj_ref[...] = jnp.dot(xn_ref[...], w_ref[...], preferred_element_type=F32)

    return pl.pallas_call(
        body, name="inproj_fwd", grid=(S // tm, DP // tn),
        in_specs=[pl.BlockSpec((tm, D), lambda i, j: (i, 0)), pl.BlockSpec((1, D), lambda i, j: (0, 0)),
                  pl.BlockSpec((D, tn), lambda i, j: (0, j))],
        out_specs=[pl.BlockSpec((tm, tn), lambda i, j: (i, j)), pl.BlockSpec((tm, D), lambda i, j: (i, 0))],
        out_shape=[jax.ShapeDtypeStruct((S, DP), F32), jax.ShapeDtypeStruct((S, D), MXU_DTYPE)],
        compiler_params=_params(("parallel", "arbitrary"), 48),
    )(x, pre_g, w_pad)


def _sgu_tri():
    return lax.broadcasted_iota(jnp.int32, (SGU_T, SGU_T), 0) >= lax.broadcasted_iota(jnp.int32, (SGU_T, SGU_T), 1)


def _sgu_mix(ws_m, vb, bfull, grp):
    mixed = bfull
    for g in range(SGU_G):
        mixed = mixed + jnp.where(grp == g, _mm(ws_m[g], vb), 0.0)
    return mixed


def _layernorm(v, g, b):
    xc = v - jnp.mean(v, axis=-1, keepdims=True)
    rstd = lax.rsqrt(jnp.mean(xc * xc, axis=-1, keepdims=True) + EPS)
    vhat = xc * rstd
    return vhat * g + b, vhat, rstd


def _sgu_fwd(proj, ln_g, ln_b, ws, bfull):
    S = proj.shape[0]
    tm = min(512, S)

    def body(u_ref, v_ref, z_ref, g_ref, b_ref, ws_ref, bf_ref, ya_ref):
        tri = _sgu_tri()
        ws_m = [jnp.where(tri, ws_ref[g], 0.0).astype(MXU_DTYPE) for g in range(SGU_G)]
        grp = lax.broadcasted_iota(jnp.int32, (1, BW), 1) // (BW // SGU_G)
        for b in range(tm // SGU_T):
            r = slice(b * SGU_T, (b + 1) * SGU_T)
            vln, _, _ = _layernorm(v_ref[r, :], g_ref[...], b_ref[...])
            mixed = _sgu_mix(ws_m, vln.astype(MXU_DTYPE), bf_ref[...], grp)
            z = z_ref[r, :]
            ya_ref[r, :] = (u_ref[r, :] * mixed * (z * _sigmoid(z))).astype(ya_ref.dtype)

    blk = lambda cb: pl.BlockSpec((tm, BW), lambda i, cb=cb: (i, cb))
    vec = pl.BlockSpec((1, BW), lambda i: (0, 0))
    return pl.pallas_call(
        body, name="sgu_fwd", grid=(S // tm,),
        in_specs=[blk(UA // BW), blk(VA // BW), blk(ZA // BW), vec, vec,
                  pl.BlockSpec((SGU_G, SGU_T, SGU_T), lambda i: (0, 0, 0)),
                  pl.BlockSpec((SGU_T, BW), lambda i: (0, 0))],
        out_specs=pl.BlockSpec((tm, BW), lambda i: (i, 0)),
        out_shape=jax.ShapeDtypeStruct((S, BW), MXU_DTYPE),
        compiler_params=_params(("parallel",), 32),
    )(proj, proj, proj, ln_g, ln_b, ws, bfull)


def _mla_prep_fwd(proj, gq, gkv, wuq, wuk, wuv, cos_t, sin_a, sin_b):
    S = proj.shape[0]
    tm = min(512, S)

    def body(p_ref, gq_ref, gkv_ref, wuq_ref, wuk_ref, wuv_ref, c_ref, sa_ref, sb_ref,
             q_ref, k_ref, v_ref, kt_ref, vt_ref):
        nq, _ = _rms(p_ref[:, 0:Q_RANK])
        nkv, _ = _rms(p_ref[:, Q_RANK:Q_RANK + KV_RANK])
        cq = (nq * gq_ref[...]).astype(MXU_DTYPE)
        ckv = (nkv * gkv_ref[...]).astype(MXU_DTYPE)
        qf = _mm(cq, wuq_ref[...])
        kf = _mm(ckv, wuk_ref[...])
        v = _mm(ckv, wuv_ref[...])
        v_ref[...] = v.astype(v_ref.dtype)
        vt_ref[...] = jnp.transpose(v).astype(vt_ref.dtype)
        c, sa, sb = c_ref[...], sa_ref[...], sb_ref[...]
        krr = _rope(p_ref[:, Q_RANK + KV_RANK:BW], c, sa, sb)
        for h in range(MLA_H):
            sl = slice(128 * h, 128 * (h + 1))
            q_ref[:, sl] = (_rope(qf[:, sl], c, sa, sb) * MLA_SCALE_LOG2).astype(q_ref.dtype)
            kh = kf[:, sl] + krr
            k_ref[:, sl] = kh.astype(k_ref.dtype)
            kt_ref[sl, :] = jnp.transpose(kh).astype(kt_ref.dtype)

    full = lambda a: pl.BlockSpec(a.shape, lambda i: (0,) * a.ndim)
    tab = pl.BlockSpec((tm, 128), lambda i: (i, 0))
    return pl.pallas_call(
        body, name="mla_prep_fwd", grid=(S // tm,),
        in_specs=[pl.BlockSpec((tm, BW), lambda i: (i, QD // BW)), full(gq), full(gkv), full(wuq), full(wuk),
                  full(wuv), tab, tab, tab],
        out_specs=[pl.BlockSpec((tm, 1024), lambda i: (i, 0)), pl.BlockSpec((tm, 1024), lambda i: (i, 0)),
                   pl.BlockSpec((tm, BW), lambda i: (i, 0)), pl.BlockSpec((1024, tm), lambda i: (0, i)),
                   pl.BlockSpec((BW, tm), lambda i: (0, i))],
        out_shape=[jax.ShapeDtypeStruct((S, 1024), MXU_DTYPE), jax.ShapeDtypeStruct((S, 1024), MXU_DTYPE),
                   jax.ShapeDtypeStruct((S, BW), MXU_DTYPE), jax.ShapeDtypeStruct((1024, S), MXU_DTYPE),
                   jax.ShapeDtypeStruct((BW, S), MXU_DTYPE)],
        compiler_params=_params(("parallel",), 48),
    )(proj, gq, gkv, wuq, wuk, wuv, cos_t, sin_a, sin_b)


MLA_SCALE = MLA_QK ** -0.5
MLA_SCALE_LOG2 = MLA_SCALE * 1.4426950408889634


def _tri_tables(n, q_major):
    if q_major:
        pairs = [(qi, ki) for qi in range(n) for ki in range(qi + 1)]
    else:
        pairs = [(qi, ki) for ki in range(n) for qi in range(ki, n)]
    return (jnp.asarray([p[0] for p in pairs], jnp.int32), jnp.asarray([p[1] for p in pairs], jnp.int32))


def _chunk_mask_t(T):
    kc = lax.broadcasted_iota(jnp.int32, (T, T), 0) >> 6
    qc = lax.broadcasted_iota(jnp.int32, (T, T), 1) >> 6
    return kc <= qc


FLASH_SUB = 512


def _col_reduce(op, reduce_fn, x):
    r = x.shape[0]
    while r > 8 and r % 16 == 0:
        r //= 2
        x = op(x[:r], x[r:])
    return reduce_fn(x, axis=0, keepdims=True)


def _flash_fwd(qh, kh, vt, gather=None):
    S = qh.shape[0]
    T = min(512, S)
    TS = min(FLASH_SUB, T)
    n = S // T
    qt, kt = _tri_tables(n, q_major=True)

    def body(qt_ref, kt_ref, q_ref, k_ref, vt_ref, *rest):
        if gather is None:
            o_ref, lse_ref, m_scr, acc_scr = rest
        else:
            g_ref, o_ref, lse_ref, got_ref, m_scr, acc_scr, *sems = rest
            _hosted_comm(g_ref, got_ref, sems, same_source=True)
        t = pl.program_id(1)
        qi, ki = qt_ref[t], kt_ref[t]
        row = lax.broadcasted_iota(jnp.int32, (128, 1), 0)
        ones_row = (64, 0)

        @pl.when(ki == 0)
        def _():
            m_scr[...] = jnp.full(m_scr.shape, NEG, F32)
            acc_scr[...] = jnp.zeros(acc_scr.shape, F32)

        def step(diag):
            for j in range(2):
                sl = slice(128 * j, 128 * (j + 1))
                for qs in range(T // TS):
                    cq = slice(qs * TS, (qs + 1) * TS)
                    qsub = q_ref[cq, sl]
                    for ks in range(T // TS):
                        if diag and ks > qs:
                            continue
                        rk = slice(ks * TS, (ks + 1) * TS)
                        s = _mm_nt(k_ref[rk, sl], qsub)
                        if diag and ks == qs:
                            s = jnp.where(_chunk_mask_t(TS), s, NEG)
                        m_old = m_scr[j, :, cq]
                        m_new = jnp.maximum(m_old, _col_reduce(jnp.maximum, jnp.max, s))
                        alpha = jnp.exp2(m_old - m_new)
                        p = jnp.exp2(s - m_new)
                        vt1 = jnp.where(row == ones_row[j], 1.0, vt_ref[:, rk]).astype(MXU_DTYPE)
                        acc_scr[j, :, cq] = alpha * acc_scr[j, :, cq] + _mm(vt1, p)
                        m_scr[j, :, cq] = m_new

        @pl.when(ki < qi)
        def _():
            step(False)

        @pl.when(ki == qi)
        def _():
            step(True)
            l = [acc_scr[j, ones_row[j]:ones_row[j] + 1, :] for j in range(2)]
            o_ref[...] = jnp.transpose(jnp.where(row < 64, acc_scr[0] / l[0], acc_scr[1] / l[1]))
            for j in range(2):
                lse_ref[0, j:j + 1, :] = m_scr[j] + jnp.log2(l[j])

    qmap = lambda hp, t, qt, kt: (qt[t], hp)
    kmap = lambda hp, t, qt, kt: (kt[t], hp)
    hbm = pl.BlockSpec(memory_space=pl.ANY)
    hosted = gather is not None
    got_shape = [jax.ShapeDtypeStruct((N_DEV,) + gather.shape, gather.dtype)] if hosted else []
    return pl.pallas_call(
        body, name="flash_fwd_gather" if hosted else "flash_fwd",
        grid_spec=pltpu.PrefetchScalarGridSpec(
            num_scalar_prefetch=2, grid=(MLA_H // 2, qt.shape[0]),
            in_specs=[pl.BlockSpec((T, 256), qmap), pl.BlockSpec((T, 256), kmap),
                      pl.BlockSpec((128, T), lambda hp, t, qt, kt: (hp, kt[t]))] + [hbm] * hosted,
            out_specs=[pl.BlockSpec((T, 128), qmap),
                       pl.BlockSpec((1, 2, T), lambda hp, t, qt, kt: (hp, 0, qt[t]))] + [hbm] * hosted,
            scratch_shapes=[pltpu.VMEM((2, 1, T), F32), pltpu.VMEM((2, 128, T), F32)] + COMM_SEMS * hosted),
        out_shape=[jax.ShapeDtypeStruct((S, BW), F32), jax.ShapeDtypeStruct((MLA_H // 2, 2, S), F32)] + got_shape,
        compiler_params=_params(("arbitrary", "arbitrary"), 40),
    )(qt, kt, qh, kh, vt, *([gather] * hosted))


def _attn_delta(o, do):
    S = o.shape[0]
    T = min(2048, S)

    def body(o_ref, do_ref, d_ref):
        head = lax.broadcasted_iota(jnp.int32, (1, 128), 1) // 64
        prod = o_ref[...] * do_ref[...]
        for j in range(2):
            d_ref[0, j:j + 1, :] = jnp.sum(jnp.transpose(jnp.where(head == j, prod, 0.0)), axis=0, keepdims=True)

    blk = pl.BlockSpec((T, 128), lambda hp, i: (i, hp))
    return pl.pallas_call(
        body, name="attn_delta", grid=(MLA_H // 2, S // T),
        in_specs=[blk, blk],
        out_specs=pl.BlockSpec((1, 2, T), lambda hp, i: (hp, 0, i)),
        out_shape=jax.ShapeDtypeStruct((MLA_H // 2, 2, S), F32),
        compiler_params=_params(("parallel", "parallel"), 32),
    )(o, do)


def _band_specs(S):
    q = pl.BlockSpec((BAND_TQ, 128), lambda hp, qi: (qi, QC // 128 + hp))
    ks = [pl.BlockSpec((BAND_TQ, 128), lambda hp, qi, t=t: (jnp.maximum(qi - 2 + t, 0), KC // 128 + hp))
          for t in range(3)]
    vs = [pl.BlockSpec((BAND_TQ, 128), lambda hp, qi, t=t: (jnp.maximum(qi - 2 + t, 0), VC // 128 + hp))
          for t in range(3)]
    bias = pl.BlockSpec((2, BAND_TQ, BAND_W), lambda hp, qi: (hp, 0, 0))
    return q, ks, vs, bias


LOG2E = 1.4426950408889634
LN2 = 0.6931471805599453
CA_SCALE = CA_DH ** -0.5


def _band_probs(q2j, kcat, bias2_j, valid):
    s = _mm_nt(q2j, kcat) + bias2_j
    s = jnp.where(valid, s, NEG)
    p = jnp.exp2(s - jnp.max(s, axis=1, keepdims=True))
    return p * (1.0 / jnp.sum(p, axis=1, keepdims=True))


def _band_valid(qi):
    tile = lax.broadcasted_iota(jnp.int32, (1, BAND_W), 1) // BAND_TQ
    return tile + qi >= 2


def _band_fwd(proj, bias):
    S = proj.shape[0]

    def body(q_ref, k0, k1, k2, v0, v1, v2, b_ref, o_ref):
        qi = pl.program_id(1)
        head = lax.broadcasted_iota(jnp.int32, (1, 128), 1) // 64
        kcat = jnp.concatenate([k0[...], k1[...], k2[...]], axis=0).astype(MXU_DTYPE)
        vcat = jnp.concatenate([v0[...], v1[...], v2[...]], axis=0).astype(MXU_DTYPE)
        valid = _band_valid(qi)
        q2 = q_ref[...] * (CA_SCALE * LOG2E)
        o = jnp.zeros((BAND_TQ, 128), F32)
        for j in range(2):
            pn = _band_probs(jnp.where(head == j, q2, 0.0), kcat, b_ref[j], valid)
            o = o + _mm(pn, jnp.where(head == j, vcat, 0))
        o_ref[...] = o

    q, ks, vs, bspec = _band_specs(S)
    return pl.pallas_call(
        body, name="band_fwd", grid=(CA_H // 2, S // BAND_TQ),
        in_specs=[q, *ks, *vs, bspec],
        out_specs=pl.BlockSpec((BAND_TQ, 128), lambda hp, qi: (qi, hp)),
        out_shape=jax.ShapeDtypeStruct((S, BW), F32),
        compiler_params=_params(("parallel", "arbitrary"), 40),
    )(proj, proj, proj, proj, proj, proj, proj, bias)


def _merge_fwd(x, ya, ob, oc, proj, gate_b, wbr, wout, post_g):
    S = x.shape[0]
    tm = min(256, S)

    def body(x_ref, ya_ref, ob_ref, oc_ref, zb_ref, zc_ref, g0, g1, g2, gb_ref, wbr_ref, wo_ref, pg_ref,
             xo_ref, mg_ref, h_ref):
        zb, zc = zb_ref[...], zc_ref[...]
        ys = [ya_ref[...], ob_ref[...] * (zb * _sigmoid(zb)), oc_ref[...] * (zc * _sigmoid(zc))]
        merged = jnp.zeros((tm, D), F32)
        for i, g_ref in enumerate((g0, g1, g2)):
            merged = merged + _sigmoid(g_ref[...] + gb_ref[i:i + 1, :]) * _mm(ys[i], wbr_ref[i])
        mg_ref[...] = merged.astype(mg_ref.dtype)
        h = _mm(merged, wo_ref[...])
        h_ref[...] = h
        n, _ = _rms(h)
        xo_ref[...] = x_ref[...] + n * pg_ref[...]

    row = lambda w, cb=0: pl.BlockSpec((tm, w), lambda i, cb=cb: (i, cb))
    full = lambda a: pl.BlockSpec(a.shape, lambda i: (0,) * a.ndim)
    return pl.pallas_call(
        body, name="merge_fwd", grid=(S // tm,),
        in_specs=[row(D), row(BW), row(BW), row(BW), row(BW, ZB // BW), row(BW, ZC // BW),
                  row(D, 0), row(D, 1), row(D, 2), full(gate_b), full(wbr), full(wout), full(post_g)],
        out_specs=[row(D), row(D), row(D)],
        out_shape=[jax.ShapeDtypeStruct((S, D), F32), jax.ShapeDtypeStruct((S, D), MXU_DTYPE),
                   jax.ShapeDtypeStruct((S, D), F32)],
        compiler_params=_params(("parallel",), 56),
    )(x, ya, ob, oc, proj, proj, proj, proj, proj, gate_b, wbr, wout, post_g)


def _loss_fwd_bwd(y, target):
    S = y.shape[0]
    tm = min(512, S)

    def body(y_ref, t_ref, loss_ref, dy_ref):
        @pl.when(pl.program_id(0) == 0)
        def _():
            loss_ref[...] = jnp.zeros((1, 1), F32)
        err = y_ref[...] - t_ref[...]
        loss_ref[...] += 0.5 * jnp.sum(jnp.mean(err * err, axis=-1, keepdims=True), axis=0, keepdims=True)
        dy_ref[...] = err * (1.0 / D)

    row = pl.BlockSpec((tm, D), lambda i: (i, 0))
    return pl.pallas_call(
        body, name="loss", grid=(S // tm,),
        in_specs=[row, row],
        out_specs=[pl.BlockSpec((1, 1), lambda i: (0, 0)), row],
        out_shape=[jax.ShapeDtypeStruct((1, 1), F32), jax.ShapeDtypeStruct((S, D), F32)],
        compiler_params=_params(("arbitrary",), 32),
    )(y, target)


def _first_step_zero(refs, first):
    @pl.when(first)
    def _():
        for r in refs:
            r[...] = jnp.zeros(r.shape, r.dtype)


def _out_bwd(dxo, h, merged, post_g, wout_t):
    S = dxo.shape[0]
    tm = min(256, S)

    def body(d_ref, h_ref, mg_ref, pg_ref, wt_ref, dm_ref, dw_ref, dg_ref):
        _first_step_zero((dw_ref, dg_ref), pl.program_id(0) == 0)
        d = d_ref[...]
        hn, r = _rms(h_ref[...])
        dg_ref[...] += jnp.sum(d * hn, axis=0, keepdims=True)
        dh = _rms_bwd(d * pg_ref[...], hn, r)
        dm_ref[...] = _mm(dh, wt_ref[...])
        dw_ref[...] += _mm_tn(mg_ref[...], dh)

    row = pl.BlockSpec((tm, D), lambda i: (i, 0))
    full = lambda shape: pl.BlockSpec(shape, lambda i: (0,) * len(shape))
    return pl.pallas_call(
        body, name="out_bwd", grid=(S // tm,),
        in_specs=[row, row, row, full((1, D)), full((D, D))],
        out_specs=[row, full((D, D)), full((1, D))],
        out_shape=[jax.ShapeDtypeStruct((S, D), F32), jax.ShapeDtypeStruct((D, D), F32),
                   jax.ShapeDtypeStruct((1, D), F32)],
        compiler_params=_params(("arbitrary",), 40),
    )(dxo, h, merged, post_g, wout_t)


def _gate_bwd(dm, ya, ob, oc, proj, gate_b, wbr, wbr_t):
    S = dm.shape[0]
    tm = min(128, S)

    def body(dm_ref, ya_ref, ob_ref, oc_ref, zb_ref, zc_ref, g0, g1, g2, gb_ref, wbr_ref, wbt_ref,
             dg_ref, dzb_ref, dzc_ref, dya_ref, dob_ref, doc_ref, dwbr_ref, dgb_ref):
        _first_step_zero((dwbr_ref, dgb_ref), pl.program_id(0) == 0)
        dmv = dm_ref[...]
        zb, zc = zb_ref[...], zc_ref[...]
        sgb, sgc = _sigmoid(zb), _sigmoid(zc)
        ob, oc = ob_ref[...], oc_ref[...]
        ys = [ya_ref[...], (ob * (zb * sgb)).astype(MXU_DTYPE), (oc * (zc * sgc)).astype(MXU_DTYPE)]
        dys = []
        for i, g_ref in enumerate((g0, g1, g2)):
            br = _mm(ys[i], wbr_ref[i])
            gate = _sigmoid(g_ref[...] + gb_ref[i:i + 1, :])
            dgl = dmv * br * (gate * (1.0 - gate))
            dg_ref[:, D * i:D * (i + 1)] = dgl.astype(dg_ref.dtype)
            dgb_ref[i:i + 1, :] += jnp.sum(dgl, axis=0, keepdims=True)
            dbr = dmv * gate
            dys.append(_mm(dbr, wbt_ref[i]))
            dwbr_ref[i] += _mm_tn(dbr, ys[i])
        dya_ref[...] = dys[0]
        dob_ref[...] = dys[1] * (zb * sgb)
        dzb_ref[...] = (dys[1] * ob * (sgb * (1.0 + zb * (1.0 - sgb)))).astype(dzb_ref.dtype)
        doc_ref[...] = dys[2] * (zc * sgc)
        dzc_ref[...] = (dys[2] * oc * (sgc * (1.0 + zc * (1.0 - sgc)))).astype(dzc_ref.dtype)

    row = lambda w, cb=0: pl.BlockSpec((tm, w), lambda i, cb=cb: (i, cb))
    full = lambda a: pl.BlockSpec(a.shape, lambda i: (0,) * a.ndim)
    sds = lambda w, dt=F32: jax.ShapeDtypeStruct((S, w), dt)
    return pl.pallas_call(
        body, name="gate_bwd", grid=(S // tm,),
        in_specs=[row(D), row(BW), row(BW), row(BW), row(BW, ZB // BW), row(BW, ZC // BW),
                  row(D, 0), row(D, 1), row(D, 2), full(gate_b), full(wbr), full(wbr_t)],
        out_specs=[row(3 * D), row(BW), row(BW), row(BW), row(BW), row(BW),
                   pl.BlockSpec((N_BRANCH, D, BW), lambda i: (0, 0, 0)), pl.BlockSpec((N_BRANCH, D), lambda i: (0, 0))],
        out_shape=[sds(3 * D, MXU_DTYPE), sds(BW, MXU_DTYPE), sds(BW, MXU_DTYPE), sds(BW), sds(BW), sds(BW),
                   jax.ShapeDtypeStruct((N_BRANCH, D, BW), F32), jax.ShapeDtypeStruct((N_BRANCH, D), F32)],
        compiler_params=_params(("arbitrary",), 56),
    )(dm, ya, ob, oc, proj, proj, proj, proj, proj, gate_b, wbr, wbr_t)


def _band_bwd(proj, bias, do):
    S = proj.shape[0]

    def body(q_ref, k0, k1, k2, v0, v1, v2, b_ref, do_ref, dq_ref, dk_ref, dv_ref, db_ref):
        qi = pl.program_id(1)
        _first_step_zero((dk_ref, dv_ref, db_ref), qi == 0)
        head = lax.broadcasted_iota(jnp.int32, (1, 128), 1) // 64
        kcat = jnp.concatenate([k0[...], k1[...], k2[...]], axis=0).astype(MXU_DTYPE)
        vcat = jnp.concatenate([v0[...], v1[...], v2[...]], axis=0).astype(MXU_DTYPE)
        valid = _band_valid(qi)
        q2, dov = q_ref[...] * (CA_SCALE * LOG2E), do_ref[...]
        dq = jnp.zeros((BAND_TQ, 128), F32)
        dk = jnp.zeros((BAND_W, 128), F32)
        dv = jnp.zeros((BAND_W, 128), F32)
        for j in range(2):
            q2j = jnp.where(head == j, q2, 0.0).astype(MXU_DTYPE)
            doj = jnp.where(head == j, dov, 0.0).astype(MXU_DTYPE)
            pn = _band_probs(q2j, kcat, b_ref[j], valid)
            dv = dv + _mm_tn(pn, doj)
            dp = _mm_nt(doj, vcat)
            ds = pn * (dp - jnp.sum(pn * dp, axis=1, keepdims=True))
            db_ref[j] += ds
            dsb = ds.astype(MXU_DTYPE)
            dq = dq + _mm(dsb, jnp.where(head == j, kcat, 0))
            dk = dk + _mm_tn(dsb, q2j)
        dq_ref[...] = (dq * CA_SCALE).astype(dq_ref.dtype)
        for t in range(3):
            @pl.when(qi - 2 + t >= 0)
            def _(t=t):
                rows = pl.ds(pl.multiple_of((qi - 2 + t) * BAND_TQ, BAND_TQ), BAND_TQ)
                dk_ref[rows, :] += dk[t * BAND_TQ:(t + 1) * BAND_TQ] * LN2
                dv_ref[rows, :] += dv[t * BAND_TQ:(t + 1) * BAND_TQ]

    q, ks, vs, bspec = _band_specs(S)
    col = pl.BlockSpec((S, 128), lambda hp, qi: (0, hp))
    return pl.pallas_call(
        body, name="band_bwd", grid=(CA_H // 2, S // BAND_TQ),
        in_specs=[q, *ks, *vs, bspec, pl.BlockSpec((BAND_TQ, 128), lambda hp, qi: (qi, hp))],
        out_specs=[pl.BlockSpec((BAND_TQ, 128), lambda hp, qi: (qi, hp)), col, col, bspec],
        out_shape=[jax.ShapeDtypeStruct((S, BW), MXU_DTYPE), jax.ShapeDtypeStruct((S, BW), F32),
                   jax.ShapeDtypeStruct((S, BW), F32), jax.ShapeDtypeStruct((CA_H, BAND_TQ, BAND_W), F32)],
        compiler_params=_params(("parallel", "arbitrary"), 56),
    )(proj, proj, proj, proj, proj, proj, proj, bias, do)


BIAS_LO = REL_CLIP - (CHUNK - 1)
BIAS_FAR = 2 * REL_CLIP
BIAS_NEAR0 = BAND_W // 2


def _band_index(col0, ncol):
    i = lax.broadcasted_iota(jnp.int32, (BAND_TQ, ncol), 0)
    j = lax.broadcasted_iota(jnp.int32, (BAND_TQ, ncol), 1) + col0
    idx = jnp.clip(i + 2 * BAND_TQ - j, -REL_CLIP, REL_CLIP) + REL_CLIP
    ci, cj = i // CHUNK, j // CHUNK
    return jnp.where((ci <= cj) & (cj <= ci + LEFT_CHUNKS), idx, -1)


def _skew(x, right):
    row = lax.broadcasted_iota(jnp.int32, (BAND_TQ, 1), 0)
    for b in range(BAND_TQ.bit_length() - 1):
        shift = (1 << b) if right else BAND_W - (1 << b)
        x = jnp.where(((row >> b) & 1) == 1, pltpu.roll(x, shift, 1), x)
    return x


SKEW_NEAR0 = 2 * BAND_TQ - REL_CLIP + 1
SKEW_NEAR1 = 2 * BAND_TQ + CHUNK
SKEW_WRAP0 = BAND_W - (CHUNK - 1)


def _bias_build(rel_table):
    far = rel_table[:, BIAS_FAR:]
    base = jnp.concatenate([jnp.broadcast_to(far, (CA_H, SKEW_NEAR0)), rel_table[:, BIAS_LO:BIAS_FAR][:, ::-1],
                            jnp.broadcast_to(far, (CA_H, BAND_W - SKEW_NEAR1))], axis=1)[:, None, :]

    def body(base_ref, out_ref):
        valid = _band_index(0, BAND_W) >= 0
        for h in range(CA_H):
            tile = _skew(jnp.broadcast_to(base_ref[h] * LOG2E, (BAND_TQ, BAND_W)), right=True)
            out_ref[h] = jnp.where(valid, tile, NEG)

    return pl.pallas_call(
        body, name="bias_build",
        out_shape=jax.ShapeDtypeStruct((CA_H, BAND_TQ, BAND_W), F32),
        in_specs=[pl.BlockSpec(memory_space=pltpu.VMEM)],
        out_specs=pl.BlockSpec(memory_space=pltpu.VMEM),
        compiler_params=pltpu.CompilerParams(vmem_limit_bytes=40 << 20),
    )(base)


def _bias_fold(db):
    def body(db_ref, sums_ref, far_ref):
        col = lax.broadcasted_iota(jnp.int32, (1, BAND_W), 1)
        is_far = (col < SKEW_NEAR0) | (col >= SKEW_WRAP0)
        for h in range(CA_H):
            sums = jnp.sum(_skew(db_ref[h], right=False), axis=0, keepdims=True)
            sums_ref[h] = sums
            far_ref[h] = jnp.broadcast_to(jnp.sum(jnp.where(is_far, sums, 0.0), axis=1, keepdims=True), (1, 128))

    sums, far = pl.pallas_call(
        body, name="bias_fold",
        out_shape=[jax.ShapeDtypeStruct((CA_H, 1, BAND_W), F32), jax.ShapeDtypeStruct((CA_H, 1, 128), F32)],
        in_specs=[pl.BlockSpec(memory_space=pltpu.VMEM)],
        out_specs=[pl.BlockSpec(memory_space=pltpu.VMEM), pl.BlockSpec(memory_space=pltpu.VMEM)],
        compiler_params=pltpu.CompilerParams(vmem_limit_bytes=40 << 20),
    )(db)
    near = sums[:, 0, SKEW_NEAR0:SKEW_NEAR1][:, ::-1]
    return jnp.concatenate([jnp.zeros((CA_H, BIAS_LO), F32), near, far[:, 0, :1]], axis=1)


def _flash_bwd(qh, kh, ktr, vh, lse, delta, do, send=None):
    S = qh.shape[0]
    T = min(512, S)
    TS = min(FLASH_SUB, T)
    n = S // T
    qt, kt = _tri_tables(n, q_major=False)

    def body(qt_ref, kt_ref, q_ref, k_ref, ktr_ref, v_ref, lse_ref, dl_ref, do_ref, *rest):
        if send is None:
            dqt_ref, dk_ref, dv_ref, dk_scr, dv_scr = rest
        else:
            send_ref, dqt_ref, dk_ref, dv_ref, recv_ref, dk_scr, dv_scr, *sems = rest
            _hosted_comm(send_ref, recv_ref, sems, same_source=False)
        t = pl.program_id(1)
        qi, ki = qt_ref[t], kt_ref[t]

        @pl.when(t == 0)
        def _():
            dqt_ref[...] = jnp.zeros(dqt_ref.shape, F32)

        @pl.when(qi == ki)
        def _():
            dk_scr[...] = jnp.zeros(dk_scr.shape, F32)
            dv_scr[...] = jnp.zeros(dv_scr.shape, F32)

        def step(diag):
            head = lax.broadcasted_iota(jnp.int32, (1, 128), 1) // 64
            for j in range(2):
                sl = slice(128 * j, 128 * (j + 1))
                for qs in range(T // TS):
                    cq = slice(qs * TS, (qs + 1) * TS)
                    qsub = q_ref[cq, sl]
                    doj = jnp.where(head == j, do_ref[cq, :], 0.0).astype(MXU_DTYPE)
                    lse, dlt = lse_ref[0, j:j + 1, cq], dl_ref[0, j:j + 1, cq]
                    cols = pl.ds(pl.multiple_of(qi * T + qs * TS, TS), TS)
                    for ks in range(T // TS):
                        if diag and ks > qs:
                            continue
                        rk = slice(ks * TS, (ks + 1) * TS)
                        s = _mm_nt(k_ref[rk, sl], qsub)
                        if diag and ks == qs:
                            s = jnp.where(_chunk_mask_t(TS), s, NEG)
                        p = jnp.exp2(s - lse)
                        dv_scr[rk, :] += _mm(p, doj)
                        dp = _mm_nt(v_ref[rk, :], doj)
                        ds = (p * (dp - dlt)).astype(MXU_DTYPE)
                        dk_scr[rk, sl] += _mm(ds, qsub)
                        dqt_ref[sl, cols] += _mm(ktr_ref[sl, rk], ds) * MLA_SCALE

        @pl.when(qi > ki)
        def _():
            step(False)

        @pl.when(qi == ki)
        def _():
            step(True)

        @pl.when(qi == n - 1)
        def _():
            dk_ref[...] = dk_scr[...] * 0.6931471805599453
            dv_ref[...] = dv_scr[...]

    qmap = lambda hp, t, qt, kt: (qt[t], hp)
    kmap = lambda hp, t, qt, kt: (kt[t], hp)
    stat = pl.BlockSpec((1, 2, T), lambda hp, t, qt, kt: (hp, 0, qt[t]))
    hbm = pl.BlockSpec(memory_space=pl.ANY)
    hosted = send is not None
    recv_shape = [jax.ShapeDtypeStruct(send.shape, send.dtype)] if hosted else []
    return pl.pallas_call(
        body, name="flash_bwd_exchange" if hosted else "flash_bwd",
        grid_spec=pltpu.PrefetchScalarGridSpec(
            num_scalar_prefetch=2, grid=(MLA_H // 2, qt.shape[0]),
            in_specs=[pl.BlockSpec((T, 256), qmap), pl.BlockSpec((T, 256), kmap),
                      pl.BlockSpec((256, T), lambda hp, t, qt, kt: (hp, kt[t])), pl.BlockSpec((T, 128), kmap),
                      stat, stat, pl.BlockSpec((T, 128), qmap)] + [hbm] * hosted,
            out_specs=[pl.BlockSpec((256, S), lambda hp, t, qt, kt: (hp, 0)), pl.BlockSpec((T, 256), kmap),
                       pl.BlockSpec((T, 128), kmap)] + [hbm] * hosted,
            scratch_shapes=[pltpu.VMEM((T, 256), F32), pltpu.VMEM((T, 128), F32)] + COMM_SEMS * hosted),
        out_shape=[jax.ShapeDtypeStruct((1024, S), F32), jax.ShapeDtypeStruct((S, 1024), F32),
                   jax.ShapeDtypeStruct((S, BW), F32)] + recv_shape,
        compiler_params=_params(("arbitrary", "arbitrary"), 56),
    )(qt, kt, qh, kh, ktr, vh, lse, delta, do, *([send] * hosted))


def _mla_prep_bwd(proj, dqf, dkf, dvf, gq, gkv, wuq_t, wuk_t, wuv_t, cos_t, sin_a, sin_b):
    S = proj.shape[0]
    tm = min(512, S)

    def body(p_ref, dq_ref, dk_ref, dv_ref, gq_ref, gkv_ref, wq_ref, wk_ref, wv_ref, c_ref, sa_ref, sb_ref,
             db_ref, dwq_ref, dwk_ref, dwv_ref, dgq_ref, dgkv_ref):
        _first_step_zero((dwq_ref, dwk_ref, dwv_ref, dgq_ref, dgkv_ref), pl.program_id(0) == 0)
        c, sa, sb = c_ref[...], sa_ref[...], sb_ref[...]
        nq, rq = _rms(p_ref[:, 0:Q_RANK])
        nkv, rkv = _rms(p_ref[:, Q_RANK:Q_RANK + KV_RANK])
        cq = (nq * gq_ref[...]).astype(MXU_DTYPE)
        ckv = (nkv * gkv_ref[...]).astype(MXU_DTYPE)
        dkr = jnp.zeros((tm, 128), F32)
        dq_pre = []
        for h in range(MLA_H):
            sl = slice(128 * h, 128 * (h + 1))
            dq_pre.append(_rope_t(jnp.transpose(dq_ref[sl, :]), c, sa, sb).astype(MXU_DTYPE))
            dkr = dkr + dk_ref[:, sl]
        dq_pre = jnp.concatenate(dq_pre, axis=1)
        dcq = _mm(dq_pre, wq_ref[...])
        dwq_ref[...] += _mm_tn(dq_pre, cq)
        dgq_ref[...] += jnp.sum(dcq * nq, axis=0, keepdims=True)
        db_ref[:, 0:Q_RANK] = _rms_bwd(dcq * gq_ref[...], nq, rq).astype(db_ref.dtype)
        dk = dk_ref[...].astype(MXU_DTYPE)
        dv = dv_ref[...].astype(MXU_DTYPE)
        dckv = _mm(dk, wk_ref[...]) + _mm(dv, wv_ref[...])
        dwk_ref[...] += _mm_tn(dk, ckv)
        dwv_ref[...] += _mm_tn(dv, ckv)
        dgkv_ref[...] += jnp.sum(dckv * nkv, axis=0, keepdims=True)
        db_ref[:, Q_RANK:Q_RANK + KV_RANK] = _rms_bwd(dckv * gkv_ref[...], nkv, rkv).astype(db_ref.dtype)
        lane = lax.broadcasted_iota(jnp.int32, (1, 128), 1)
        rope_lanes = (lane >= MLA_NOPE) & (lane < MLA_QK)
        db_ref[:, Q_RANK + KV_RANK:BW] = _rope_t(jnp.where(rope_lanes, dkr, 0.0), c, sa, sb).astype(db_ref.dtype)

    full = lambda a: pl.BlockSpec(a.shape, lambda i: (0,) * a.ndim)
    fulls = lambda shape: pl.BlockSpec(shape, lambda i: (0,) * len(shape))
    tab = pl.BlockSpec((tm, 128), lambda i: (i, 0))
    row = lambda w, cb=0: pl.BlockSpec((tm, w), lambda i, cb=cb: (i, cb))
    return pl.pallas_call(
        body, name="mla_prep_bwd", grid=(S // tm,),
        in_specs=[row(BW, QD // BW), pl.BlockSpec((1024, tm), lambda i: (0, i)), row(1024), row(BW),
                  full(gq), full(gkv), full(wuq_t),
                  full(wuk_t), full(wuv_t), tab, tab, tab],
        out_specs=[row(BW), fulls((1024, Q_RANK)), fulls((1024, KV_RANK)), fulls((BW, KV_RANK)),
                   fulls((1, Q_RANK)), fulls((1, KV_RANK))],
        out_shape=[jax.ShapeDtypeStruct((S, BW), MXU_DTYPE), jax.ShapeDtypeStruct((1024, Q_RANK), F32),
                   jax.ShapeDtypeStruct((1024, KV_RANK), F32), jax.ShapeDtypeStruct((BW, KV_RANK), F32),
                   jax.ShapeDtypeStruct((1, Q_RANK), F32), jax.ShapeDtypeStruct((1, KV_RANK), F32)],
        compiler_params=_params(("arbitrary",), 56),
    )(proj, dqf, dkf, dvf, gq, gkv, wuq_t, wuk_t, wuv_t, cos_t, sin_a, sin_b)


def _sgu_bwd(proj, dya, ln_g, ln_b, ws, bfull):
    S = proj.shape[0]
    tm = min(512, S)

    def body(u_ref, v_ref, z_ref, dy_ref, g_ref, b_ref, ws_ref, bf_ref,
             da_ref, dws_ref, dbf_ref, dlg_ref, dlb_ref, dbs_ref):
        i = pl.program_id(0)
        _first_step_zero((dws_ref, dbf_ref, dlg_ref, dlb_ref, dbs_ref), i == 0)
        tri = _sgu_tri()
        ws_m = [jnp.where(tri, ws_ref[g], 0.0).astype(MXU_DTYPE) for g in range(SGU_G)]
        grp = lax.broadcasted_iota(jnp.int32, (1, BW), 1) // (BW // SGU_G)
        for b in range(tm // SGU_T):
            r = slice(b * SGU_T, (b + 1) * SGU_T)
            vln, vhat, rstd = _layernorm(v_ref[r, :], g_ref[...], b_ref[...])
            vb = vln.astype(MXU_DTYPE)
            mixed = _sgu_mix(ws_m, vb, bf_ref[...], grp)
            u, z, dy = u_ref[r, :], z_ref[r, :], dy_ref[r, :]
            sg = _sigmoid(z)
            sz = z * sg
            da_ref[r, 0:BW] = (dy * mixed * sz).astype(da_ref.dtype)
            da_ref[r, 2 * BW:3 * BW] = (dy * u * mixed * (sg * (1.0 + z * (1.0 - sg)))).astype(da_ref.dtype)
            dmix = dy * u * sz
            dbf_ref[...] += dmix
            dvln = jnp.zeros((SGU_T, BW), F32)
            for g in range(SGU_G):
                dmg = jnp.where(grp == g, dmix, 0.0).astype(MXU_DTYPE)
                dvln = dvln + _mm_tn(ws_m[g], dmg)
                dws_ref[g] += jnp.where(tri, _mm_nt(dmg, vb), 0.0)
            dlg_ref[...] += jnp.sum(dvln * vhat, axis=0, keepdims=True)
            dlb_ref[...] += jnp.sum(dvln, axis=0, keepdims=True)
            dvh = dvln * g_ref[...]
            da_ref[r, BW:2 * BW] = (rstd * (dvh - jnp.mean(dvh, axis=-1, keepdims=True)
                                            - vhat * jnp.mean(dvh * vhat, axis=-1, keepdims=True))).astype(da_ref.dtype)

        @pl.when(i == pl.num_programs(0) - 1)
        def _():
            dbf = dbf_ref[...]
            for g in range(SGU_G):
                dbs_ref[:, g:g + 1] = jnp.sum(jnp.where(grp == g, dbf, 0.0), axis=1, keepdims=True)

    blk = lambda cb: pl.BlockSpec((tm, BW), lambda i, cb=cb: (i, cb))
    fulls = lambda shape: pl.BlockSpec(shape, lambda i: (0,) * len(shape))
    return pl.pallas_call(
        body, name="sgu_bwd", grid=(S // tm,),
        in_specs=[blk(UA // BW), blk(VA // BW), blk(ZA // BW), blk(0), fulls((1, BW)), fulls((1, BW)),
                  fulls((SGU_G, SGU_T, SGU_T)), fulls((SGU_T, BW))],
        out_specs=[pl.BlockSpec((tm, 3 * BW), lambda i: (i, 0)), fulls((SGU_G, SGU_T, SGU_T)),
                   fulls((SGU_T, BW)), fulls((1, BW)), fulls((1, BW)), fulls((SGU_T, SGU_G))],
        out_shape=[jax.ShapeDtypeStruct((S, 3 * BW), MXU_DTYPE), jax.ShapeDtypeStruct((SGU_G, SGU_T, SGU_T), F32),
                   jax.ShapeDtypeStruct((SGU_T, BW), F32), jax.ShapeDtypeStruct((1, BW), F32),
                   jax.ShapeDtypeStruct((1, BW), F32), jax.ShapeDtypeStruct((SGU_T, SGU_G), F32)],
        compiler_params=_params(("arbitrary",), 40),
    )(proj, proj, proj, dya, ln_g, ln_b, ws, bfull)


DX_TK = 3 * BW


def _inproj_bwd_dx(dgl, da, small, w_t, x, pre_g, dxo, send=None):
    S = x.shape[0]
    tm = min(512, S)
    nk = DP // DX_TK

    def body(g_ref, a_ref, s0, s1, s2, s3, s4, s5, w_ref, x_ref, pg_ref, dxo_ref, *rest):
        if send is None:
            dx_ref, dg_ref, acc = rest
        else:
            send_ref, dx_ref, dg_ref, recv_ref, acc, *sems = rest
            _hosted_comm(send_ref, recv_ref, sems, same_source=False)
        i, k = pl.program_id(0), pl.program_id(1)
        _first_step_zero((dg_ref,), (i == 0) & (k == 0))

        @pl.when(k == 0)
        def _():
            acc[...] = jnp.zeros(acc.shape, F32)

        @pl.when(k < 2)
        def _():
            acc[...] += _mm(g_ref[...], w_ref[...])

        @pl.when(k == 2)
        def _():
            acc[...] += _mm(a_ref[...], w_ref[...])

        for kk, trio in ((3, (s0, s1, s2)), (4, (s3, s4, s5))):
            @pl.when(k == kk)
            def _(trio=trio):
                t = acc[...]
                for n, ref in enumerate(trio):
                    t = t + _mm(ref[...], w_ref[n * BW:(n + 1) * BW, :])
                acc[...] = t

        @pl.when(k == nk - 1)
        def _():
            n, r = _rms(x_ref[...])
            dxn = acc[...]
            dg_ref[...] += jnp.sum(dxn * n, axis=0, keepdims=True)
            dx_ref[...] = dxo_ref[...] + _rms_bwd(dxn * pg_ref[...], n, r)

    row = pl.BlockSpec((tm, D), lambda i, k: (i, 0))
    vec = pl.BlockSpec((1, D), lambda i, k: (0, 0))
    fixed = lambda w: pl.BlockSpec((tm, w), lambda i, k: (i, 0))
    hbm = pl.BlockSpec(memory_space=pl.ANY)
    hosted = send is not None
    recv_shape = [jax.ShapeDtypeStruct(send.shape, send.dtype)] if hosted else []
    return pl.pallas_call(
        body, name="inproj_bwd_dx_exchange" if hosted else "inproj_bwd_dx", grid=(S // tm, nk),
        in_specs=[pl.BlockSpec((tm, DX_TK), lambda i, k: (i, jnp.minimum(k, 1))), fixed(DX_TK)]
                 + [fixed(BW)] * 6 + [pl.BlockSpec((DX_TK, D), lambda i, k: (k, 0)), row, vec, row] + [hbm] * hosted,
        out_specs=[row, vec] + [hbm] * hosted,
        out_shape=[jax.ShapeDtypeStruct((S, D), F32), jax.ShapeDtypeStruct((1, D), F32)] + recv_shape,
        scratch_shapes=[pltpu.VMEM((tm, D), F32)] + COMM_SEMS * hosted,
        compiler_params=_params(("arbitrary", "arbitrary"), 56),
    )(dgl, da, *small, w_t, x, pre_g, dxo, *([send] * hosted))


def _matmul_tn(segs, b):
    S, K = b.shape
    tk = min(1024, S)
    widths = [a.shape[1] for a in segs]
    offs = [sum(widths[:n]) for n in range(len(segs))]

    def body(*refs):
        seg_refs, b_ref, o_ref = refs[:len(segs)], refs[len(segs)], refs[len(segs) + 1]
        _first_step_zero((o_ref,), pl.program_id(0) == 0)
        bv = b_ref[...]
        for ref, off, w in zip(seg_refs, offs, widths):
            o_ref[off:off + w, :] += _mm_tn(ref[...], bv)

    return pl.pallas_call(
        body, name="matmul_tn", grid=(S // tk,),
        in_specs=[pl.BlockSpec((tk, w), lambda k: (k, 0)) for w in widths] + [pl.BlockSpec((tk, K), lambda k: (k, 0))],
        out_specs=pl.BlockSpec((sum(widths), K), lambda k: (0, 0)),
        out_shape=jax.ShapeDtypeStruct((sum(widths), K), F32),
        compiler_params=_params(("arbitrary",), 56),
    )(*segs, b)


def _pad_rows_w_in(wt):
    z = lambda n: jnp.zeros((n, wt.shape[1]), wt.dtype)
    return jnp.concatenate([wt[4512:], wt[:1920], z(64), wt[1920:1952], z(32), wt[1952:4512]], axis=0)


def _unpad_rows_w_in(dwt):
    a = dwt[3 * D:]
    return jnp.concatenate([a[:1920], a[1984:2016], a[2048:], dwt[:3 * D]], axis=0)


def _pad_head_rows(wt, width):
    k = wt.shape[1]
    return jnp.pad(wt.reshape(MLA_H, width, k), ((0, 0), (0, 128 - width), (0, 0))).reshape(MLA_H * 128, k)


def _unpad_head_rows(wt, width):
    k = wt.shape[1]
    return wt.reshape(MLA_H, 128, k)[:, :width].reshape(MLA_H * width, k)


def _rope_tables(S):
    half = MLA_ROPE // 2
    inv = 10000.0 ** (-jnp.arange(half, dtype=F32) / half)
    ang = jnp.arange(S, dtype=F32)[:, None] * inv[None, :]
    cos, sin = jnp.cos(ang), jnp.sin(ang)
    one = lambda n: jnp.ones((S, n), F32)
    zero = lambda n: jnp.zeros((S, n), F32)
    cos_t = jnp.concatenate([one(MLA_NOPE), cos, cos, one(128 - MLA_QK)], axis=1)
    sin_a = jnp.concatenate([zero(MLA_NOPE), -sin, zero(128 - MLA_NOPE - half)], axis=1)
    sin_b = jnp.concatenate([zero(MLA_NOPE + half), sin, zero(128 - MLA_QK)], axis=1)
    return cos_t, sin_a, sin_b


SHARDED = ("w_in", "mla_w_uq", "mla_w_ukv", "w_branch", "w_out")
SHARD_PACK = SHARDED + ("gate_b",)
REPLICATED = ("pre_g", "post_g", "sgu_ln_g", "sgu_ln_b", "sgu_w", "sgu_b", "mla_q_norm_g", "mla_kv_norm_g",
              "ca_rel_bias")
OUT_ORDER = ("w_in", "pre_g", "post_g", "sgu_ln_g", "sgu_ln_b", "sgu_w", "sgu_b", "mla_q_norm_g", "mla_kv_norm_g",
             "mla_w_uq", "mla_w_ukv", "ca_rel_bias", "w_branch", "gate_b", "w_out")
TRANSPOSED = {"w_in": (0, 2, 1), "mla_w_uq": (0, 2, 1), "mla_w_ukv": (0, 2, 1), "w_branch": (0, 1, 3, 2)}


def _canon(name, a):
    return jnp.transpose(a, TRANSPOSED[name]) if name in TRANSPOSED else a


ROW_MULT = 64
PIECE_MULT = 16


def _piece_rows(shape):
    size = 1
    for s in shape:
        size *= s
    return -(-size // (1024 * PIECE_MULT)) * PIECE_MULT


def _pack_rows(pieces, lead, mult=ROW_MULT):
    rows = []
    for a in pieces:
        flat = a.reshape(a.shape[:lead] + (-1,))
        nr = _piece_rows(a.shape[lead:])
        flat = jnp.pad(flat, [(0, 0)] * lead + [(0, nr * 1024 - flat.shape[-1])])
        rows.append(flat.reshape(a.shape[:lead] + (nr, 1024)))
    total = sum(r.shape[lead] for r in rows)
    tail = (-total) % mult
    if tail:
        rows.append(jnp.zeros(rows[0].shape[:lead] + (tail, 1024), rows[0].dtype))
    return jnp.concatenate(rows, axis=lead)


def _unpack_rows(p, shapes, lead):
    out, r = [], 0
    for s in shapes:
        nr, size = _piece_rows(s), 1
        for d in s:
            size *= d
        piece = lax.slice_in_dim(p, r, r + nr, axis=lead).reshape(p.shape[:lead] + (-1,))
        out.append(lax.slice_in_dim(piece, 0, size, axis=lead).reshape(p.shape[:lead] + tuple(s)))
        r += nr
    return out


def kernel(x, w_in, pre_g, post_g, sgu_ln_g, sgu_ln_b, sgu_w, sgu_b, mla_q_norm_g, mla_kv_norm_g, mla_w_uq, mla_w_ukv, ca_rel_bias, w_branch, gate_b, w_out, loss_target, m_w_in, m_pre_g, m_post_g, m_sgu_ln_g, m_sgu_ln_b, m_sgu_w, m_sgu_b, m_mla_q_norm_g, m_mla_kv_norm_g, m_mla_w_uq, m_mla_w_ukv, m_ca_rel_bias, m_w_branch, m_gate_b, m_w_out, v_w_in, v_pre_g, v_post_g, v_sgu_ln_g, v_sgu_ln_b, v_sgu_w, v_sgu_b, v_mla_q_norm_g, v_mla_kv_norm_g, v_mla_w_uq, v_mla_w_ukv, v_ca_rel_bias, v_w_branch, v_gate_b, v_w_out):
    weights = dict(w_in=w_in, pre_g=pre_g, post_g=post_g, sgu_ln_g=sgu_ln_g, sgu_ln_b=sgu_ln_b, sgu_w=sgu_w,
                   sgu_b=sgu_b, mla_q_norm_g=mla_q_norm_g, mla_kv_norm_g=mla_kv_norm_g, mla_w_uq=mla_w_uq,
                   mla_w_ukv=mla_w_ukv, ca_rel_bias=ca_rel_bias, w_branch=w_branch, gate_b=gate_b, w_out=w_out)
    mom_m = dict(w_in=m_w_in, pre_g=m_pre_g, post_g=m_post_g, sgu_ln_g=m_sgu_ln_g, sgu_ln_b=m_sgu_ln_b,
                 sgu_w=m_sgu_w, sgu_b=m_sgu_b, mla_q_norm_g=m_mla_q_norm_g, mla_kv_norm_g=m_mla_kv_norm_g,
                 mla_w_uq=m_mla_w_uq, mla_w_ukv=m_mla_w_ukv, ca_rel_bias=m_ca_rel_bias, w_branch=m_w_branch,
                 gate_b=m_gate_b, w_out=m_w_out)
    mom_v = dict(w_in=v_w_in, pre_g=v_pre_g, post_g=v_post_g, sgu_ln_g=v_sgu_ln_g, sgu_ln_b=v_sgu_ln_b,
                 sgu_w=v_sgu_w, sgu_b=v_sgu_b, mla_q_norm_g=v_mla_q_norm_g, mla_kv_norm_g=v_mla_kv_norm_g,
                 mla_w_uq=v_mla_w_uq, mla_w_ukv=v_mla_w_ukv, ca_rel_bias=v_ca_rel_bias, w_branch=v_w_branch,
                 gate_b=v_gate_b, w_out=v_w_out)
    depth = w_in.shape[0]
    S = x.shape[1]
    xs = x.reshape(S, D)

    cw = {n: _canon(n, weights[n]) for n in SHARD_PACK}

    def layer_block(l):
        return _pack_rows([cw[n][l].astype(MXU_DTYPE) for n in SHARDED], 0, PIECE_MULT)

    def split_block(got):
        return dict(zip(SHARDED, _unpack_rows(got, [cw[n].shape[1:] for n in SHARDED], 1)))

    gate_all, = _unpack_rows(_all_gather(_pack_rows([gate_b], 0, PIECE_MULT)), [gate_b.shape], 1)
    cos_t, sin_a, sin_b = _rope_tables(S)

    def layer_weights(gl, l):
        w_t = _pad_rows_w_in(gl["w_in"].reshape(D_IN, D))
        wuq_t = _pad_head_rows(gl["mla_w_uq"].reshape(MLA_H * MLA_QK, Q_RANK), MLA_QK)
        ukv_t = gl["mla_w_ukv"]
        wuk_t = jnp.pad(ukv_t[:, :MLA_NOPE], ((0, 0), (0, 128 - MLA_NOPE), (0, 0))).reshape(MLA_H * 128, KV_RANK)
        wuv_t = ukv_t[:, MLA_NOPE:].reshape(BW, KV_RANK)
        wbr_t = jnp.swapaxes(gl["w_branch"], 0, 1).reshape(N_BRANCH, D, BW)
        return dict(
            w_t=w_t, w_pad=w_t.T, wuq_t=wuq_t, wuq=wuq_t.T, wuk_t=wuk_t, wuk=wuk_t.T, wuv_t=wuv_t, wuv=wuv_t.T,
            wbr_t=wbr_t, wbr=jnp.swapaxes(wbr_t, 1, 2), wout=gl["w_out"].reshape(D, D),
            gate_b=jnp.swapaxes(gate_all[:, l], 0, 1).reshape(N_BRANCH, D),
            pre_g=pre_g[l][None], post_g=post_g[l][None], ln_g=sgu_ln_g[l][None], ln_b=sgu_ln_b[l][None],
            ws=sgu_w[l], bfull=jnp.repeat(sgu_b[l].T, BW // SGU_G, axis=1),
            gq=mla_q_norm_g[l][None], gkv=mla_kv_norm_g[l][None], bias=_bias_build(ca_rel_bias[l]))

    layers, saved = [], []
    h_x = xs
    got = _all_gather(layer_block(0))
    for l in range(depth):
        lw = layer_weights(split_block(got), l)
        layers.append(lw)
        proj, xn = _inproj_fwd(h_x, lw["pre_g"], lw["w_pad"])
        ya = _sgu_fwd(proj, lw["ln_g"], lw["ln_b"], lw["ws"], lw["bfull"])
        qh, kh, vh, ktr, vtr = _mla_prep_fwd(proj, lw["gq"], lw["gkv"], lw["wuq"], lw["wuk"], lw["wuv"],
                                             cos_t, sin_a, sin_b)
        if l + 1 < depth:
            ob, lse, got = _flash_fwd(qh, kh, vtr, gather=layer_block(l + 1))
        else:
            ob, lse = _flash_fwd(qh, kh, vtr)
        oc = _band_fwd(proj, lw["bias"])
        x_new, merged, hh = _merge_fwd(h_x, ya, ob, oc, proj, lw["gate_b"], lw["wbr"], lw["wout"], lw["post_g"])
        saved.append(dict(x=h_x, proj=proj, xn=xn, ya=ya, qh=qh, kh=kh, vh=vh, ktr=ktr, ob=ob, lse=lse, oc=oc,
                          merged=merged, h=hh))
        h_x = x_new

    loss_part, dx = _loss_fwd_bwd(h_x, loss_target.reshape(S, D))
    loss = lax.psum(loss_part[0, 0], MESH_AXES)

    rows = {}
    rep = {n: [None] * depth for n in REPLICATED}
    recvs = [None] * depth
    pending = None
    for l in reversed(range(depth)):
        lw, sv = layers[l], saved[l]
        proj = sv["proj"]
        dmerged, dw_out, dg_post = _out_bwd(dx, sv["h"], sv["merged"], lw["post_g"], lw["wout"].T)
        dgl, dzb, dzc, dya, dob, doc, dwbr_t, dgb = _gate_bwd(
            dmerged, sv["ya"], sv["ob"], sv["oc"], proj, lw["gate_b"], lw["wbr"], lw["wbr_t"])
        dqc, dkc, dvc, dbias = _band_bwd(proj, lw["bias"], doc)
        drel = _bias_fold(dbias)
        flash_args = (sv["qh"], sv["kh"], sv["ktr"], sv["vh"], sv["lse"], _attn_delta(sv["ob"], dob), dob)
        if pending is None:
            dqf, dkf, dvf = _flash_bwd(*flash_args)
        else:
            dqf, dkf, dvf, recvs[l + 1] = _flash_bwd(*flash_args, send=pending)
        db, dwuq_t, dwuk_t, dwuv_t, dgq, dgkv = _mla_prep_bwd(
            proj, dqf, dkf, dvf, lw["gq"], lw["gkv"], lw["wuq_t"], lw["wuk_t"], lw["wuv_t"], cos_t, sin_a, sin_b)
        da, dws, _, dlg, dlb, dbs = _sgu_bwd(proj, dya, lw["ln_g"], lw["ln_b"], lw["ws"], lw["bfull"])
        small = (db, dzb, dqc, dkc, dvc, dzc)
        dw_t = jnp.concatenate([_matmul_tn([dgl], sv["xn"]), _matmul_tn([da, db, dzb], sv["xn"]),
                                _matmul_tn([dqc, dkc, dvc, dzc], sv["xn"])], axis=0)

        rows["w_in"] = _unpad_rows_w_in(dw_t).reshape(N_DEV, D_IN // N_DEV, D)
        rows["mla_w_uq"] = _unpad_head_rows(dwuq_t, MLA_QK).reshape(N_DEV, MLA_QK, Q_RANK)
        dk3 = dwuk_t.reshape(MLA_H, 128, KV_RANK)[:, :MLA_NOPE]
        dv3 = dwuv_t.reshape(MLA_H, 64, KV_RANK)
        rows["mla_w_ukv"] = jnp.concatenate([dk3, dv3], axis=1)
        rows["w_branch"] = jnp.swapaxes(dwbr_t.reshape(N_BRANCH, N_DEV, D // N_DEV, BW), 0, 1)
        rows["w_out"] = dw_out.reshape(N_DEV, D // N_DEV, D)
        rows["gate_b"] = jnp.swapaxes(dgb.reshape(N_BRANCH, N_DEV, D // N_DEV), 0, 1)
        pending = _pack_rows([rows[n].astype(MXU_DTYPE) for n in SHARD_PACK], 1)
        dx_args = (dgl, da, small, lw["w_t"], sv["x"], lw["pre_g"], dx)
        if l > 0:
            dx, dg_pre = _inproj_bwd_dx(*dx_args)
        else:
            dx, dg_pre, recvs[0] = _inproj_bwd_dx(*dx_args, send=pending)
        rep["pre_g"][l] = dg_pre[0]
        rep["post_g"][l] = dg_post[0]
        rep["sgu_ln_g"][l] = dlg[0]
        rep["sgu_ln_b"][l] = dlb[0]
        rep["sgu_w"][l] = dws
        rep["sgu_b"][l] = dbs.T
        rep["mla_q_norm_g"][l] = dgq[0]
        rep["mla_kv_norm_g"][l] = dgkv[0]
        rep["ca_rel_bias"][l] = drel
    grad_x = dx.reshape(x.shape)

    cm = {n: _canon(n, mom_m[n]) for n in SHARD_PACK}
    cv = {n: _canon(n, mom_v[n]) for n in SHARD_PACK}
    shard_shapes = [cw[n].shape[1:] for n in SHARD_PACK]
    per_layer = []
    for l in range(depth):
        at = lambda d: _pack_rows([d[n][l] for n in SHARD_PACK], 0)
        outs_l = _reduce_adamw(recvs[l], at(cw), at(cm), at(cv))
        per_layer.append([_unpack_rows(p, shard_shapes, 0) for p in outs_l])
    outs_sh = [{n: _canon(n, jnp.stack([per_layer[l][k][i] for l in range(depth)]))
                for i, n in enumerate(SHARD_PACK)} for k in range(4)]

    rp = lambda d: _pack_rows([d[n] for n in REPLICATED], 0)
    allp = _all_gather(rp({n: jnp.stack(rep[n]) for n in REPLICATED}))
    outs_rep = _reduce_adamw(allp, rp(weights), rp(mom_m), rp(mom_v))
    rep_shapes = [weights[n].shape for n in REPLICATED]
    outs_rep = [dict(zip(REPLICATED, _unpack_rows(p, rep_shapes, 0))) for p in outs_rep]

    outs = [{**a, **b} for a, b in zip(outs_sh, outs_rep)]
    return (loss, grad_x, *[o[n] for o in outs for n in OUT_ORDER])
```

```python
import functools

import jax
import jax.numpy as jnp
from jax import lax
from jax.experimental import pallas as pl
from jax.experimental.pallas import tpu as pltpu

F32 = jnp.float32
MXU_DTYPE = jnp.bfloat16
EPS = 1e-6
NEG = -1e30
MESH_AXES = ("x", "y", "c")
N_DEV = 8

D = 1024
BW = 512
N_BRANCH = 3
CHUNK = 64
SGU_T = 128
SGU_G = 8
MLA_H = 8
MLA_NOPE = 64
MLA_ROPE = 32
MLA_QK = MLA_NOPE + MLA_ROPE
Q_RANK = 256
KV_RANK = 128
CA_H = 8
CA_DH = 64
LEFT_CHUNKS = 8
REL_CLIP = 128
D_IN = 7584

G_OFF, UA, VA, ZA, QD, ZB, QC, KC, VC, ZC, DP = 0, 3072, 3584, 4096, 4608, 5120, 5632, 6144, 6656, 7168, 7680
BAND_TQ = 256
BAND_W = 3 * BAND_TQ

ADAM_LR, ADAM_B1, ADAM_B2, ADAM_EPS, ADAM_WD, ADAM_STEP = 0.001, 0.9, 0.999, 1e-08, 0.01, 10


def _params(sem, mib):
    return pltpu.CompilerParams(dimension_semantics=sem, vmem_limit_bytes=mib << 20)


def _mm(a, b):
    return jnp.dot(a.astype(MXU_DTYPE), b.astype(MXU_DTYPE), preferred_element_type=F32)


def _mm_nt(a, b):
    return lax.dot_general(a.astype(MXU_DTYPE), b.astype(MXU_DTYPE), (((1,), (1,)), ((), ())),
                           preferred_element_type=F32)


def _mm_tn(a, b):
    return lax.dot_general(a.astype(MXU_DTYPE), b.astype(MXU_DTYPE), (((0,), (0,)), ((), ())),
                           preferred_element_type=F32)


def _sigmoid(z):
    return 1.0 / (1.0 + jnp.exp(-z))


def _rms(x):
    r = lax.rsqrt(jnp.mean(x * x, axis=-1, keepdims=True) + EPS)
    return x * r, r


def _rms_bwd(dn, n, r):
    return r * (dn - n * jnp.mean(dn * n, axis=-1, keepdims=True))


def _rope(b, c, sa, sb):
    return b * c + pltpu.roll(b, 112, 1) * sa + pltpu.roll(b, 16, 1) * sb


def _rope_t(d, c, sa, sb):
    return d * c + pltpu.roll(d * sa, 16, 1) + pltpu.roll(d * sb, 112, 1)


def _all_gather(blk):
    R = blk.shape[0]

    def body(x_ref, out_ref, send_sems, recv_sems, local_sem):
        x, y, c = lax.axis_index("x"), lax.axis_index("y"), lax.axis_index("c")
        me, sibling = (x, y, c), (x, y, 1 - c)
        chips = [(1 - x, y), (x, 1 - y), (1 - x, 1 - y)]

        def slot(px, py, pc):
            return out_ref.at[4 * px + 2 * py + pc]

        def copy(k, block, to, src=None):
            return pltpu.make_async_remote_copy(
                src_ref=slot(*block) if src is None else src, dst_ref=slot(*block),
                send_sem=send_sems.at[k], recv_sem=recv_sems.at[k],
                device_id=to, device_id_type=pl.DeviceIdType.MESH)

        mine = pltpu.make_async_copy(x_ref, slot(*me), local_sem)
        mine.start()
        first = [copy(0, me, sibling, src=x_ref)]
        first += [copy(1 + j, me, (*chip, c), src=x_ref) for j, chip in enumerate(chips)]
        for cp in first:
            cp.start()
        passed = [copy(4 + j, (*chip, c), sibling) for j, chip in enumerate(chips)]
        for j, chip in enumerate(chips):
            copy(1 + j, (*chip, c), me).wait_recv()
            passed[j].start()
        copy(0, sibling, me).wait_recv()
        for j, chip in enumerate(chips):
            copy(4 + j, (*chip, 1 - c), me).wait_recv()
        for cp in first + passed:
            cp.wait_send()
        mine.wait()

    return pl.pallas_call(
        body, name="all_gather",
        out_shape=jax.ShapeDtypeStruct((N_DEV,) + blk.shape, blk.dtype),
        in_specs=[pl.BlockSpec(memory_space=pl.ANY)],
        out_specs=pl.BlockSpec(memory_space=pl.ANY),
        scratch_shapes=[pltpu.SemaphoreType.DMA((7,)), pltpu.SemaphoreType.DMA((7,)), pltpu.SemaphoreType.DMA(())],
    )(blk)


def _peer_copies(s_ref, r_ref, send_sems, recv_sems, local_sem, same_source):
    x, y, c = lax.axis_index("x"), lax.axis_index("y"), lax.axis_index("c")
    me = 4 * x + 2 * y + c

    def src(pid):
        return s_ref if same_source else s_ref.at[pid]

    def peer(k):
        px = 1 - x if (k >> 2) & 1 else x
        py = 1 - y if (k >> 1) & 1 else y
        pc = 1 - c if k & 1 else c
        return (px, py, pc), 4 * px + 2 * py + pc

    def remote(k, row):
        pos, pid = peer(k)
        return pltpu.make_async_remote_copy(
            src_ref=src(pid), dst_ref=r_ref.at[me if row is None else pid],
            send_sem=send_sems.at[k], recv_sem=recv_sems.at[k],
            device_id=pos, device_id_type=pl.DeviceIdType.MESH)

    def local():
        return pltpu.make_async_copy(src(me), r_ref.at[me], local_sem)

    def start():
        local().start()
        for k in range(1, N_DEV):
            remote(k, None).start()

    def wait():
        for k in range(1, N_DEV):
            remote(k, "peer").wait_recv()
        for k in range(1, N_DEV):
            remote(k, None).wait_send()
        local().wait()

    return start, wait


COMM_SEMS = [pltpu.SemaphoreType.DMA((N_DEV,)), pltpu.SemaphoreType.DMA((N_DEV,)), pltpu.SemaphoreType.DMA(())]


def _hosted_comm(s_ref, r_ref, sems, same_source):
    start, wait = _peer_copies(s_ref, r_ref, *sems, same_source=same_source)
    i, j = pl.program_id(0), pl.program_id(1)

    @pl.when((i == 0) & (j == 0))
    def _():
        start()

    @pl.when((i == pl.num_programs(0) - 1) & (j == pl.num_programs(1) - 1))
    def _():
        wait()


def _reduce_adamw(recv, w, m, v):
    R = w.shape[0]
    tr = next(t for t in (128, 64, 32, 16) if R % t == 0)
    c1 = 1.0 - ADAM_B1 ** ADAM_STEP
    c2 = 1.0 - ADAM_B2 ** ADAM_STEP

    def body(r_ref, w_ref, m_ref, v_ref, g_ref, d_ref, nm_ref, nv_ref):
        g = r_ref[0].astype(F32)
        for s in range(1, N_DEV):
            g = g + r_ref[s].astype(F32)
        m2 = ADAM_B1 * m_ref[...] + (1.0 - ADAM_B1) * g
        v2 = ADAM_B2 * v_ref[...] + (1.0 - ADAM_B2) * (g * g)
        m_hat = m2 / c1
        v_hat = v2 / c2
        g_ref[...] = g
        d_ref[...] = -ADAM_LR * (m_hat / (jnp.sqrt(v_hat) + ADAM_EPS) + ADAM_WD * w_ref[...])
        nm_ref[...] = m2
        nv_ref[...] = v2

    row = pl.BlockSpec((tr, 1024), lambda i: (i, 0))
    return pl.pallas_call(
        body, name="reduce_adamw", grid=(R // tr,),
        in_specs=[pl.BlockSpec((N_DEV, tr, 1024), lambda i: (0, i, 0)), row, row, row],
        out_specs=[row, row, row, row],
        out_shape=[jax.ShapeDtypeStruct((R, 1024), F32)] * 4,
        compiler_params=_params(("parallel",), 40),
    )(recv, w, m, v)


def _inproj_fwd(x, pre_g, w_pad):
    S = x.shape[0]
    tm, tn = min(1024, S), 1536

    def body(x_ref, g_ref, w_ref, proj_ref, xn_ref):
        @pl.when(pl.program_id(1) == 0)
        def _():
            n, _ = _rms(x_ref[...])
            xn_ref[...] = (n * g_ref[...]).astype(xn_ref.dtype)
        proj_ref[...] = jnp.dot(xn_ref[...], w_ref[...], preferred_element_type=F32)

    return pl.pallas_call(
        body, name="inproj_fwd", grid=(S // tm, DP // tn),
        in_specs=[pl.BlockSpec((tm, D), lambda i, j: (i, 0)), pl.BlockSpec((1, D), lambda i, j: (0, 0)),
                  pl.BlockSpec((D, tn), lambda i, j: (0, j))],
        out_specs=[pl.BlockSpec((tm, tn), lambda i, j: (i, j)), pl.BlockSpec((tm, D), lambda i, j: (i, 0))],
        out_shape=[jax.ShapeDtypeStruct((S, DP), F32), jax.ShapeDtypeStruct((S, D), MXU_DTYPE)],
        compiler_params=_params(("parallel", "arbitrary"), 48),
    )(x, pre_g, w_pad)


def _sgu_tri():
    return lax.broadcasted_iota(jnp.int32, (SGU_T, SGU_T), 0) >= lax.broadcasted_iota(jnp.int32, (SGU_T, SGU_T), 1)


def _sgu_mix(ws_m, vb, bfull, grp):
    mixed = bfull
    for g in range(SGU_G):
        mixed = mixed + jnp.where(grp == g, _mm(ws_m[g], vb), 0.0)
    return mixed


def _layernorm(v, g, b):
    xc = v - jnp.mean(v, axis=-1, keepdims=True)
    rstd = lax.rsqrt(jnp.mean(xc * xc, axis=-1, keepdims=True) + EPS)
    vhat = xc * rstd
    return vhat * g + b, vhat, rstd


def _sgu_fwd(proj, ln_g, ln_b, ws, bfull):
    S = proj.shape[0]
    tm = min(512, S)

    def body(u_ref, v_ref, z_ref, g_ref, b_ref, ws_ref, bf_ref, ya_ref):
        tri = _sgu_tri()
        ws_m = [jnp.where(tri, ws_ref[g], 0.0).astype(MXU_DTYPE) for g in range(SGU_G)]
        grp = lax.broadcasted_iota(jnp.int32, (1, BW), 1) // (BW // SGU_G)
        for b in range(tm // SGU_T):
            r = slice(b * SGU_T, (b + 1) * SGU_T)
            vln, _, _ = _layernorm(v_ref[r, :], g_ref[...], b_ref[...])
            mixed = _sgu_mix(ws_m, vln.astype(MXU_DTYPE), bf_ref[...], grp)
            z = z_ref[r, :]
            ya_ref[r, :] = (u_ref[r, :] * mixed * (z * _sigmoid(z))).astype(ya_ref.dtype)

    blk = lambda cb: pl.BlockSpec((tm, BW), lambda i, cb=cb: (i, cb))
    vec = pl.BlockSpec((1, BW), lambda i: (0, 0))
    return pl.pallas_call(
        body, name="sgu_fwd", grid=(S // tm,),
        in_specs=[blk(UA // BW), blk(VA // BW), blk(ZA // BW), vec, vec,
                  pl.BlockSpec((SGU_G, SGU_T, SGU_T), lambda i: (0, 0, 0)),
                  pl.BlockSpec((SGU_T, BW), lambda i: (0, 0))],
        out_specs=pl.BlockSpec((tm, BW), lambda i: (i, 0)),
        out_shape=jax.ShapeDtypeStruct((S, BW), MXU_DTYPE),
        compiler_params=_params(("parallel",), 32),
    )(proj, proj, proj, ln_g, ln_b, ws, bfull)


def _mla_prep_fwd(proj, gq, gkv, wuq, wuk, wuv, cos_t, sin_a, sin_b):
    S = proj.shape[0]
    tm = min(512, S)

    def body(p_ref, gq_ref, gkv_ref, wuq_ref, wuk_ref, wuv_ref, c_ref, sa_ref, sb_ref,
             q_ref, k_ref, v_ref, kt_ref, vt_ref):
        nq, _ = _rms(p_ref[:, 0:Q_RANK])
        nkv, _ = _rms(p_ref[:, Q_RANK:Q_RANK + KV_RANK])
        cq = (nq * gq_ref[...]).astype(MXU_DTYPE)
        ckv = (nkv * gkv_ref[...]).astype(MXU_DTYPE)
        qf = _mm(cq, wuq_ref[...])
        kf = _mm(ckv, wuk_ref[...])
        v = _mm(ckv, wuv_ref[...])
        v_ref[...] = v.astype(v_ref.dtype)
        vt_ref[...] = jnp.transpose(v).astype(vt_ref.dtype)
        c, sa, sb = c_ref[...], sa_ref[...], sb_ref[...]
        krr = _rope(p_ref[:, Q_RANK + KV_RANK:BW], c, sa, sb)
        for h in range(MLA_H):
            sl = slice(128 * h, 128 * (h + 1))
            q_ref[:, sl] = (_rope(qf[:, sl], c, sa, sb) * MLA_SCALE_LOG2).astype(q_ref.dtype)
            kh = kf[:, sl] + krr
            k_ref[:, sl] = kh.astype(k_ref.dtype)
            kt_ref[sl, :] = jnp.transpose(kh).astype(kt_ref.dtype)

    full = lambda a: pl.BlockSpec(a.shape, lambda i: (0,) * a.ndim)
    tab = pl.BlockSpec((tm, 128), lambda i: (i, 0))
    return pl.pallas_call(
        body, name="mla_prep_fwd", grid=(S // tm,),
        in_specs=[pl.BlockSpec((tm, BW), lambda i: (i, QD // BW)), full(gq), full(gkv), full(wuq), full(wuk),
                  full(wuv), tab, tab, tab],
        out_specs=[pl.BlockSpec((tm, 1024), lambda i: (i, 0)), pl.BlockSpec((tm, 1024), lambda i: (i, 0)),
                   pl.BlockSpec((tm, BW), lambda i: (i, 0)), pl.BlockSpec((1024, tm), lambda i: (0, i)),
                   pl.BlockSpec((BW, tm), lambda i: (0, i))],
        out_shape=[jax.ShapeDtypeStruct((S, 1024), MXU_DTYPE), jax.ShapeDtypeStruct((S, 1024), MXU_DTYPE),
                   jax.ShapeDtypeStruct((S, BW), MXU_DTYPE), jax.ShapeDtypeStruct((1024, S), MXU_DTYPE),
                   jax.ShapeDtypeStruct((BW, S), MXU_DTYPE)],
        compiler_params=_params(("parallel",), 48),
    )(proj, gq, gkv, wuq, wuk, wuv, cos_t, sin_a, sin_b)


MLA_SCALE = MLA_QK ** -0.5
MLA_SCALE_LOG2 = MLA_SCALE * 1.4426950408889634


def _tri_tables(n, q_major):
    if q_major:
        pairs = [(qi, ki) for qi in range(n) for ki in range(qi + 1)]
    else:
        pairs = [(qi, ki) for ki in range(n) for qi in range(ki, n)]
    return (jnp.asarray([p[0] for p in pairs], jnp.int32), jnp.asarray([p[1] for p in pairs], jnp.int32))


def _chunk_mask_t(T):
    kc = lax.broadcasted_iota(jnp.int32, (T, T), 0) >> 6
    qc = lax.broadcasted_iota(jnp.int32, (T, T), 1) >> 6
    return kc <= qc


FLASH_T = 1024
FLASH_SUB = 512


def _col_reduce(op, reduce_fn, x):
    r = x.shape[0]
    while r > 8 and r % 16 == 0:
        r //= 2
        x = op(x[:r], x[r:])
    return reduce_fn(x, axis=0, keepdims=True)


def _flash_fwd(qh, kh, vt, gather=None):
    S = qh.shape[0]
    T = min(FLASH_T, S)
    TS = min(FLASH_SUB, T)
    n = S // T
    qt, kt = _tri_tables(n, q_major=True)

    def body(qt_ref, kt_ref, q_ref, k_ref, vt_ref, *rest):
        if gather is None:
            o_ref, lse_ref, m_scr, acc_scr = rest
        else:
            g_ref, o_ref, lse_ref, got_ref, m_scr, acc_scr, *sems = rest
            _hosted_comm(g_ref, got_ref, sems, same_source=True)
        t = pl.program_id(1)
        qi, ki = qt_ref[t], kt_ref[t]
        row = lax.broadcasted_iota(jnp.int32, (128, 1), 0)
        ones_row = (64, 0)

        @pl.when(ki == 0)
        def _():
            m_scr[...] = jnp.full(m_scr.shape, NEG, F32)
            acc_scr[...] = jnp.zeros(acc_scr.shape, F32)

        def step(diag):
            for j in range(2):
                sl = slice(128 * j, 128 * (j + 1))
                for qs in range(T // TS):
                    cq = slice(qs * TS, (qs + 1) * TS)
                    qsub = q_ref[cq, sl]
                    for ks in range(T // TS):
                        if diag and ks > qs:
                            continue
                        rk = slice(ks * TS, (ks + 1) * TS)
                        s = _mm_nt(k_ref[rk, sl], qsub)
                        if diag and ks == qs:
                            s = jnp.where(_chunk_mask_t(TS), s, NEG)
                        m_old = m_scr[j, :, cq]
                        m_new = jnp.maximum(m_old, _col_reduce(jnp.maximum, jnp.max, s))
                        alpha = jnp.exp2(m_old - m_new)
                        p = jnp.exp2(s - m_new)
                        vt1 = jnp.where(row == ones_row[j], 1.0, vt_ref[:, rk]).astype(MXU_DTYPE)
                        acc_scr[j, :, cq] = alpha * acc_scr[j, :, cq] + _mm(vt1, p)
                        m_scr[j, :, cq] = m_new

        @pl.when(ki < qi)
        def _():
            step(False)

        @pl.when(ki == qi)
        def _():
            step(True)
            l = [acc_scr[j, ones_row[j]:ones_row[j] + 1, :] for j in range(2)]
            o_ref[...] = jnp.transpose(jnp.where(row < 64, acc_scr[0] / l[0], acc_scr[1] / l[1]))
            for j in range(2):
                lse_ref[0, j:j + 1, :] = m_scr[j] + jnp.log2(l[j])

    qmap = lambda hp, t, qt, kt: (qt[t], hp)
    kmap = lambda hp, t, qt, kt: (kt[t], hp)
    hbm = pl.BlockSpec(memory_space=pl.ANY)
    hosted = gather is not None
    got_shape = [jax.ShapeDtypeStruct((N_DEV,) + gather.shape, gather.dtype)] if hosted else []
    return pl.pallas_call(
        body, name="flash_fwd_gather" if hosted else "flash_fwd",
        grid_spec=pltpu.PrefetchScalarGridSpec(
            num_scalar_prefetch=2, grid=(MLA_H // 2, qt.shape[0]),
            in_specs=[pl.BlockSpec((T, 256), qmap), pl.BlockSpec((T, 256), kmap),
                      pl.BlockSpec((128, T), lambda hp, t, qt, kt: (hp, kt[t]))] + [hbm] * hosted,
            out_specs=[pl.BlockSpec((T, 128), qmap),
                       pl.BlockSpec((1, 2, T), lambda hp, t, qt, kt: (hp, 0, qt[t]))] + [hbm] * hosted,
            scratch_shapes=[pltpu.VMEM((2, 1, T), F32), pltpu.VMEM((2, 128, T), F32)] + COMM_SEMS * hosted),
        out_shape=[jax.ShapeDtypeStruct((S, BW), F32), jax.ShapeDtypeStruct((MLA_H // 2, 2, S), F32)] + got_shape,
        compiler_params=_params(("arbitrary", "arbitrary"), 40),
    )(qt, kt, qh, kh, vt, *([gather] * hosted))


def _attn_delta(o, do):
    S = o.shape[0]
    T = min(2048, S)

    def body(o_ref, do_ref, d_ref):
        head = lax.broadcasted_iota(jnp.int32, (1, 128), 1) // 64
        prod = o_ref[...] * do_ref[...]
        for j in range(2):
            d_ref[0, j:j + 1, :] = jnp.sum(jnp.transpose(jnp.where(head == j, prod, 0.0)), axis=0, keepdims=True)

    blk = pl.BlockSpec((T, 128), lambda hp, i: (i, hp))
    return pl.pallas_call(
        body, name="attn_delta", grid=(MLA_H // 2, S // T),
        in_specs=[blk, blk],
        out_specs=pl.BlockSpec((1, 2, T), lambda hp, i: (hp, 0, i)),
        out_shape=jax.ShapeDtypeStruct((MLA_H // 2, 2, S), F32),
        compiler_params=_params(("parallel", "parallel"), 32),
    )(o, do)


def _band_specs(S):
    q = pl.BlockSpec((BAND_TQ, 128), lambda hp, qi: (qi, QC // 128 + hp))
    ks = [pl.BlockSpec((BAND_TQ, 128), lambda hp, qi, t=t: (jnp.maximum(qi - 2 + t, 0), KC // 128 + hp))
          for t in range(3)]
    vs = [pl.BlockSpec((BAND_TQ, 128), lambda hp, qi, t=t: (jnp.maximum(qi - 2 + t, 0), VC // 128 + hp))
          for t in range(3)]
    bias = pl.BlockSpec((2, BAND_TQ, BAND_W), lambda hp, qi: (hp, 0, 0))
    return q, ks, vs, bias


LOG2E = 1.4426950408889634
LN2 = 0.6931471805599453
CA_SCALE = CA_DH ** -0.5


def _band_probs(q2j, kcat, bias2_j, valid):
    s = _mm_nt(q2j, kcat) + bias2_j
    s = jnp.where(valid, s, NEG)
    p = jnp.exp2(s - jnp.max(s, axis=1, keepdims=True))
    return p * (1.0 / jnp.sum(p, axis=1, keepdims=True))


def _band_valid(qi):
    tile = lax.broadcasted_iota(jnp.int32, (1, BAND_W), 1) // BAND_TQ
    return tile + qi >= 2


def _band_fwd(proj, bias):
    S = proj.shape[0]

    def body(q_ref, k0, k1, k2, v0, v1, v2, b_ref, o_ref):
        qi = pl.program_id(1)
        head = lax.broadcasted_iota(jnp.int32, (1, 128), 1) // 64
        kcat = jnp.concatenate([k0[...], k1[...], k2[...]], axis=0).astype(MXU_DTYPE)
        vcat = jnp.concatenate([v0[...], v1[...], v2[...]], axis=0).astype(MXU_DTYPE)
        valid = _band_valid(qi)
        q2 = q_ref[...] * (CA_SCALE * LOG2E)
        o = jnp.zeros((BAND_TQ, 128), F32)
        for j in range(2):
            pn = _band_probs(jnp.where(head == j, q2, 0.0), kcat, b_ref[j], valid)
            o = o + _mm(pn, jnp.where(head == j, vcat, 0))
        o_ref[...] = o

    q, ks, vs, bspec = _band_specs(S)
    return pl.pallas_call(
        body, name="band_fwd", grid=(CA_H // 2, S // BAND_TQ),
        in_specs=[q, *ks, *vs, bspec],
        out_specs=pl.BlockSpec((BAND_TQ, 128), lambda hp, qi: (qi, hp)),
        out_shape=jax.ShapeDtypeStruct((S, BW), F32),
        compiler_params=_params(("parallel", "arbitrary"), 40),
    )(proj, proj, proj, proj, proj, proj, proj, bias)


def _merge_fwd(x, ya, ob, oc, proj, gate_b, wbr, wout, post_g):
    S = x.shape[0]
    tm = min(256, S)

    def body(x_ref, ya_ref, ob_ref, oc_ref, zb_ref, zc_ref, g0, g1, g2, gb_ref, wbr_ref, wo_ref, pg_ref,
             xo_ref, mg_ref, h_ref):
        zb, zc = zb_ref[...], zc_ref[...]
        ys = [ya_ref[...], ob_ref[...] * (zb * _sigmoid(zb)), oc_ref[...] * (zc * _sigmoid(zc))]
        merged = jnp.zeros((tm, D), F32)
        for i, g_ref in enumerate((g0, g1, g2)):
            merged = merged + _sigmoid(g_ref[...] + gb_ref[i:i + 1, :]) * _mm(ys[i], wbr_ref[i])
        mg_ref[...] = merged.astype(mg_ref.dtype)
        h = _mm(merged, wo_ref[...])
        h_ref[...] = h
        n, _ = _rms(h)
        xo_ref[...] = x_ref[...] + n * pg_ref[...]

    row = lambda w, cb=0: pl.BlockSpec((tm, w), lambda i, cb=cb: (i, cb))
    full = lambda a: pl.BlockSpec(a.shape, lambda i: (0,) * a.ndim)
    return pl.pallas_call(
        body, name="merge_fwd", grid=(S // tm,),
        in_specs=[row(D), row(BW), row(BW), row(BW), row(BW, ZB // BW), row(BW, ZC // BW),
                  row(D, 0), row(D, 1), row(D, 2), full(gate_b), full(wbr), full(wout), full(post_g)],
        out_specs=[row(D), row(D), row(D)],
        out_shape=[jax.ShapeDtypeStruct((S, D), F32), jax.ShapeDtypeStruct((S, D), MXU_DTYPE),
                   jax.ShapeDtypeStruct((S, D), F32)],
        compiler_params=_params(("parallel",), 56),
    )(x, ya, ob, oc, proj, proj, proj, proj, proj, gate_b, wbr, wout, post_g)


def _loss_fwd_bwd(y, target):
    S = y.shape[0]
    tm = min(512, S)

    def body(y_ref, t_ref, loss_ref, dy_ref):
        @pl.when(pl.program_id(0) == 0)
        def _():
            loss_ref[...] = jnp.zeros((1, 1), F32)
        err = y_ref[...] - t_ref[...]
        loss_ref[...] += 0.5 * jnp.sum(jnp.mean(err * err, axis=-1, keepdims=True), axis=0, keepdims=True)
        dy_ref[...] = err * (1.0 / D)

    row = pl.BlockSpec((tm, D), lambda i: (i, 0))
    return pl.pallas_call(
        body, name="loss", grid=(S // tm,),
        in_specs=[row, row],
        out_specs=[pl.BlockSpec((1, 1), lambda i: (0, 0)), row],
        out_shape=[jax.ShapeDtypeStruct((1, 1), F32), jax.ShapeDtypeStruct((S, D), F32)],
        compiler_params=_params(("arbitrary",), 32),
    )(y, target)


def _first_step_zero(refs, first):
    @pl.when(first)
    def _():
        for r in refs:
            r[...] = jnp.zeros(r.shape, r.dtype)


def _out_bwd(dxo, h, merged, post_g, wout_t):
    S = dxo.shape[0]
    tm = min(256, S)

    def body(d_ref, h_ref, mg_ref, pg_ref, wt_ref, dm_ref, dw_ref, dg_ref):
        _first_step_zero((dw_ref, dg_ref), pl.program_id(0) == 0)
        d = d_ref[...]
        hn, r = _rms(h_ref[...])
        dg_ref[...] += jnp.sum(d * hn, axis=0, keepdims=True)
        dh = _rms_bwd(d * pg_ref[...], hn, r)
        dm_ref[...] = _mm(dh, wt_ref[...])
        dw_ref[...] += _mm_tn(mg_ref[...], dh)

    row = pl.BlockSpec((tm, D), lambda i: (i, 0))
    full = lambda shape: pl.BlockSpec(shape, lambda i: (0,) * len(shape))
    return pl.pallas_call(
        body, name="out_bwd", grid=(S // tm,),
        in_specs=[row, row, row, full((1, D)), full((D, D))],
        out_specs=[row, full((D, D)), full((1, D))],
        out_shape=[jax.ShapeDtypeStruct((S, D), F32), jax.ShapeDtypeStruct((D, D), F32),
                   jax.ShapeDtypeStruct((1, D), F32)],
        compiler_params=_params(("arbitrary",), 40),
    )(dxo, h, merged, post_g, wout_t)


def _gate_bwd(dm, ya, ob, oc, proj, gate_b, wbr, wbr_t):
    S = dm.shape[0]
    tm = min(128, S)

    def body(dm_ref, ya_ref, ob_ref, oc_ref, zb_ref, zc_ref, g0, g1, g2, gb_ref, wbr_ref, wbt_ref,
             dg_ref, dzb_ref, dzc_ref, dya_ref, dob_ref, doc_ref, dwbr_ref, dgb_ref):
        _first_step_zero((dwbr_ref, dgb_ref), pl.program_id(0) == 0)
        dmv = dm_ref[...]
        zb, zc = zb_ref[...], zc_ref[...]
        sgb, sgc = _sigmoid(zb), _sigmoid(zc)
        ob, oc = ob_ref[...], oc_ref[...]
        ys = [ya_ref[...], (ob * (zb * sgb)).astype(MXU_DTYPE), (oc * (zc * sgc)).astype(MXU_DTYPE)]
        dys = []
        for i, g_ref in enumerate((g0, g1, g2)):
            br = _mm(ys[i], wbr_ref[i])
            gate = _sigmoid(g_ref[...] + gb_ref[i:i + 1, :])
            dgl = dmv * br * (gate * (1.0 - gate))
            dg_ref[:, D * i:D * (i + 1)] = dgl.astype(dg_ref.dtype)
            dgb_ref[i:i + 1, :] += jnp.sum(dgl, axis=0, keepdims=True)
            dbr = dmv * gate
            dys.append(_mm(dbr, wbt_ref[i]))
            dwbr_ref[i] += _mm_tn(dbr, ys[i])
        dya_ref[...] = dys[0]
        dob_ref[...] = dys[1] * (zb * sgb)
        dzb_ref[...] = (dys[1] * ob * (sgb * (1.0 + zb * (1.0 - sgb)))).astype(dzb_ref.dtype)
        doc_ref[...] = dys[2] * (zc * sgc)
        dzc_ref[...] = (dys[2] * oc * (sgc * (1.0 + zc * (1.0 - sgc)))).astype(dzc_ref.dtype)

    row = lambda w, cb=0: pl.BlockSpec((tm, w), lambda i, cb=cb: (i, cb))
    full = lambda a: pl.BlockSpec(a.shape, lambda i: (0,) * a.ndim)
    sds = lambda w, dt=F32: jax.ShapeDtypeStruct((S, w), dt)
    return pl.pallas_call(
        body, name="gate_bwd", grid=(S // tm,),
        in_specs=[row(D), row(BW), row(BW), row(BW), row(BW, ZB // BW), row(BW, ZC // BW),
                  row(D, 0), row(D, 1), row(D, 2), full(gate_b), full(wbr), full(wbr_t)],
        out_specs=[row(3 * D), row(BW), row(BW), row(BW), row(BW), row(BW),
                   pl.BlockSpec((N_BRANCH, D, BW), lambda i: (0, 0, 0)), pl.BlockSpec((N_BRANCH, D), lambda i: (0, 0))],
        out_shape=[sds(3 * D, MXU_DTYPE), sds(BW, MXU_DTYPE), sds(BW, MXU_DTYPE), sds(BW), sds(BW), sds(BW),
                   jax.ShapeDtypeStruct((N_BRANCH, D, BW), F32), jax.ShapeDtypeStruct((N_BRANCH, D), F32)],
        compiler_params=_params(("arbitrary",), 56),
    )(dm, ya, ob, oc, proj, proj, proj, proj, proj, gate_b, wbr, wbr_t)


def _band_bwd(proj, bias, do):
    S = proj.shape[0]

    def body(q_ref, k0, k1, k2, v0, v1, v2, b_ref, do_ref, dq_ref, dk_ref, dv_ref, db_ref):
        qi = pl.program_id(1)
        _first_step_zero((dk_ref, dv_ref, db_ref), qi == 0)
        head = lax.broadcasted_iota(jnp.int32, (1, 128), 1) // 64
        kcat = jnp.concatenate([k0[...], k1[...], k2[...]], axis=0).astype(MXU_DTYPE)
        vcat = jnp.concatenate([v0[...], v1[...], v2[...]], axis=0).astype(MXU_DTYPE)
        valid = _band_valid(qi)
        q2, dov = q_ref[...] * (CA_SCALE * LOG2E), do_ref[...]
        dq = jnp.zeros((BAND_TQ, 128), F32)
        dk = jnp.zeros((BAND_W, 128), F32)
        dv = jnp.zeros((BAND_W, 128), F32)
        for j in range(2):
            q2j = jnp.where(head == j, q2, 0.0).astype(MXU_DTYPE)
            doj = jnp.where(head == j, dov, 0.0).astype(MXU_DTYPE)
            pn = _band_probs(q2j, kcat, b_ref[j], valid)
            dv = dv + _mm_tn(pn, doj)
            dp = _mm_nt(doj, vcat)
            ds = pn * (dp - jnp.sum(pn * dp, axis=1, keepdims=True))
            db_ref[j] += ds
            dsb = ds.astype(MXU_DTYPE)
            dq = dq + _mm(dsb, jnp.where(head == j, kcat, 0))
            dk = dk + _mm_tn(dsb, q2j)
        dq_ref[...] = (dq * CA_SCALE).astype(dq_ref.dtype)
        for t in range(3):
            @pl.when(qi - 2 + t >= 0)
            def _(t=t):
                rows = pl.ds(pl.multiple_of((qi - 2 + t) * BAND_TQ, BAND_TQ), BAND_TQ)
                dk_ref[rows, :] += dk[t * BAND_TQ:(t + 1) * BAND_TQ] * LN2
                dv_ref[rows, :] += dv[t * BAND_TQ:(t + 1) * BAND_TQ]

    q, ks, vs, bspec = _band_specs(S)
    col = pl.BlockSpec((S, 128), lambda hp, qi: (0, hp))
    return pl.pallas_call(
        body, name="band_bwd", grid=(CA_H // 2, S // BAND_TQ),
        in_specs=[q, *ks, *vs, bspec, pl.BlockSpec((BAND_TQ, 128), lambda hp, qi: (qi, hp))],
        out_specs=[pl.BlockSpec((BAND_TQ, 128), lambda hp, qi: (qi, hp)), col, col, bspec],
        out_shape=[jax.ShapeDtypeStruct((S, BW), MXU_DTYPE), jax.ShapeDtypeStruct((S, BW), F32),
                   jax.ShapeDtypeStruct((S, BW), F32), jax.ShapeDtypeStruct((CA_H, BAND_TQ, BAND_W), F32)],
        compiler_params=_params(("parallel", "arbitrary"), 56),
    )(proj, proj, proj, proj, proj, proj, proj, bias, do)


BIAS_LO = REL_CLIP - (CHUNK - 1)
BIAS_FAR = 2 * REL_CLIP
BIAS_NEAR0 = BAND_W // 2


def _band_index(col0, ncol):
    i = lax.broadcasted_iota(jnp.int32, (BAND_TQ, ncol), 0)
    j = lax.broadcasted_iota(jnp.int32, (BAND_TQ, ncol), 1) + col0
    idx = jnp.clip(i + 2 * BAND_TQ - j, -REL_CLIP, REL_CLIP) + REL_CLIP
    ci, cj = i // CHUNK, j // CHUNK
    return jnp.where((ci <= cj) & (cj <= ci + LEFT_CHUNKS), idx, -1)


def _skew(x, right):
    row = lax.broadcasted_iota(jnp.int32, (BAND_TQ, 1), 0)
    for b in range(BAND_TQ.bit_length() - 1):
        shift = (1 << b) if right else BAND_W - (1 << b)
        x = jnp.where(((row >> b) & 1) == 1, pltpu.roll(x, shift, 1), x)
    return x


SKEW_NEAR0 = 2 * BAND_TQ - REL_CLIP + 1
SKEW_NEAR1 = 2 * BAND_TQ + CHUNK
SKEW_WRAP0 = BAND_W - (CHUNK - 1)


def _bias_build(rel_table):
    far = rel_table[:, BIAS_FAR:]
    base = jnp.concatenate([jnp.broadcast_to(far, (CA_H, SKEW_NEAR0)), rel_table[:, BIAS_LO:BIAS_FAR][:, ::-1],
                            jnp.broadcast_to(far, (CA_H, BAND_W - SKEW_NEAR1))], axis=1)[:, None, :]

    def body(base_ref, out_ref):
        valid = _band_index(0, BAND_W) >= 0
        for h in range(CA_H):
            tile = _skew(jnp.broadcast_to(base_ref[h] * LOG2E, (BAND_TQ, BAND_W)), right=True)
            out_ref[h] = jnp.where(valid, tile, NEG)

    return pl.pallas_call(
        body, name="bias_build",
        out_shape=jax.ShapeDtypeStruct((CA_H, BAND_TQ, BAND_W), F32),
        in_specs=[pl.BlockSpec(memory_space=pltpu.VMEM)],
        out_specs=pl.BlockSpec(memory_space=pltpu.VMEM),
        compiler_params=pltpu.CompilerParams(vmem_limit_bytes=40 << 20),
    )(base)


def _bias_fold(db):
    def body(db_ref, sums_ref, far_ref):
        col = lax.broadcasted_iota(jnp.int32, (1, BAND_W), 1)
        is_far = (col < SKEW_NEAR0) | (col >= SKEW_WRAP0)
        for h in range(CA_H):
            sums = jnp.sum(_skew(db_ref[h], right=False), axis=0, keepdims=True)
            sums_ref[h] = sums
            far_ref[h] = jnp.broadcast_to(jnp.sum(jnp.where(is_far, sums, 0.0), axis=1, keepdims=True), (1, 128))

    sums, far = pl.pallas_call(
        body, name="bias_fold",
        out_shape=[jax.ShapeDtypeStruct((CA_H, 1, BAND_W), F32), jax.ShapeDtypeStruct((CA_H, 1, 128), F32)],
        in_specs=[pl.BlockSpec(memory_space=pltpu.VMEM)],
        out_specs=[pl.BlockSpec(memory_space=pltpu.VMEM), pl.BlockSpec(memory_space=pltpu.VMEM)],
        compiler_params=pltpu.CompilerParams(vmem_limit_bytes=40 << 20),
    )(db)
    near = sums[:, 0, SKEW_NEAR0:SKEW_NEAR1][:, ::-1]
    return jnp.concatenate([jnp.zeros((CA_H, BIAS_LO), F32), near, far[:, 0, :1]], axis=1)


def _flash_bwd(qh, kh, ktr, vh, lse, delta, do, send=None):
    S = qh.shape[0]
    T = min(FLASH_T, S)
    TS = min(FLASH_SUB, T)
    n = S // T
    qt, kt = _tri_tables(n, q_major=False)

    def body(qt_ref, kt_ref, q_ref, k_ref, ktr_ref, v_ref, lse_ref, dl_ref, do_ref, *rest):
        if send is None:
            dqt_ref, dk_ref, dv_ref, dk_scr, dv_scr = rest
        else:
            send_ref, dqt_ref, dk_ref, dv_ref, recv_ref, dk_scr, dv_scr, *sems = rest
            _hosted_comm(send_ref, recv_ref, sems, same_source=False)
        t = pl.program_id(1)
        qi, ki = qt_ref[t], kt_ref[t]

        @pl.when(t == 0)
        def _():
            dqt_ref[...] = jnp.zeros(dqt_ref.shape, F32)

        @pl.when(qi == ki)
        def _():
            dk_scr[...] = jnp.zeros(dk_scr.shape, F32)
            dv_scr[...] = jnp.zeros(dv_scr.shape, F32)

        def step(diag):
            head = lax.broadcasted_iota(jnp.int32, (1, 128), 1) // 64
            for j in range(2):
                sl = slice(128 * j, 128 * (j + 1))
                for qs in range(T // TS):
                    cq = slice(qs * TS, (qs + 1) * TS)
                    qsub = q_ref[cq, sl]
                    doj = jnp.where(head == j, do_ref[cq, :], 0.0).astype(MXU_DTYPE)
                    lse, dlt = lse_ref[0, j:j + 1, cq], dl_ref[0, j:j + 1, cq]
                    cols = pl.ds(pl.multiple_of(qi * T + qs * TS, TS), TS)
                    for ks in range(T // TS):
                        if diag and ks > qs:
                            continue
                        rk = slice(ks * TS, (ks + 1) * TS)
                        s = _mm_nt(k_ref[rk, sl], qsub)
                        if diag and ks == qs:
                            s = jnp.where(_chunk_mask_t(TS), s, NEG)
                        p = jnp.exp2(s - lse)
                        dv_scr[rk, :] += _mm(p, doj)
                        dp = _mm_nt(v_ref[rk, :], doj)
                        ds = (p * (dp - dlt)).astype(MXU_DTYPE)
                        dk_scr[rk, sl] += _mm(ds, qsub)
                        dqt_ref[sl, cols] += _mm(ktr_ref[sl, rk], ds) * MLA_SCALE

        @pl.when(qi > ki)
        def _():
            step(False)

        @pl.when(qi == ki)
        def _():
            step(True)

        @pl.when(qi == n - 1)
        def _():
            dk_ref[...] = dk_scr[...] * 0.6931471805599453
            dv_ref[...] = dv_scr[...]

    qmap = lambda hp, t, qt, kt: (qt[t], hp)
    kmap = lambda hp, t, qt, kt: (kt[t], hp)
    stat = pl.BlockSpec((1, 2, T), lambda hp, t, qt, kt: (hp, 0, qt[t]))
    hbm = pl.BlockSpec(memory_space=pl.ANY)
    hosted = send is not None
    recv_shape = [jax.ShapeDtypeStruct(send.shape, send.dtype)] if hosted else []
    return pl.pallas_call(
        body, name="flash_bwd_exchange" if hosted else "flash_bwd",
        grid_spec=pltpu.PrefetchScalarGridSpec(
            num_scalar_prefetch=2, grid=(MLA_H // 2, qt.shape[0]),
            in_specs=[pl.BlockSpec((T, 256), qmap), pl.BlockSpec((T, 256), kmap),
                      pl.BlockSpec((256, T), lambda hp, t, qt, kt: (hp, kt[t])), pl.BlockSpec((T, 128), kmap),
                      stat, stat, pl.BlockSpec((T, 128), qmap)] + [hbm] * hosted,
            out_specs=[pl.BlockSpec((256, S), lambda hp, t, qt, kt: (hp, 0)), pl.BlockSpec((T, 256), kmap),
                       pl.BlockSpec((T, 128), kmap)] + [hbm] * hosted,
            scratch_shapes=[pltpu.VMEM((T, 256), F32), pltpu.VMEM((T, 128), F32)] + COMM_SEMS * hosted),
        out_shape=[jax.ShapeDtypeStruct((1024, S), F32), jax.ShapeDtypeStruct((S, 1024), F32),
                   jax.ShapeDtypeStruct((S, BW), F32)] + recv_shape,
        compiler_params=_params(("arbitrary", "arbitrary"), 56),
    )(qt, kt, qh, kh, ktr, vh, lse, delta, do, *([send] * hosted))


def _mla_prep_bwd(proj, dqf, dkf, dvf, gq, gkv, wuq_t, wuk_t, wuv_t, cos_t, sin_a, sin_b):
    S = proj.shape[0]
    tm = min(512, S)

    def body(p_ref, dq_ref, dk_ref, dv_ref, gq_ref, gkv_ref, wq_ref, wk_ref, wv_ref, c_ref, sa_ref, sb_ref,
             db_ref, dwq_ref, dwk_ref, dwv_ref, dgq_ref, dgkv_ref):
        _first_step_zero((dwq_ref, dwk_ref, dwv_ref, dgq_ref, dgkv_ref), pl.program_id(0) == 0)
        c, sa, sb = c_ref[...], sa_ref[...], sb_ref[...]
        nq, rq = _rms(p_ref[:, 0:Q_RANK])
        nkv, rkv = _rms(p_ref[:, Q_RANK:Q_RANK + KV_RANK])
        cq = (nq * gq_ref[...]).astype(MXU_DTYPE)
        ckv = (nkv * gkv_ref[...]).astype(MXU_DTYPE)
        dkr = jnp.zeros((tm, 128), F32)
        dq_pre = []
        for h in range(MLA_H):
            sl = slice(128 * h, 128 * (h + 1))
            dq_pre.append(_rope_t(jnp.transpose(dq_ref[sl, :]), c, sa, sb).astype(MXU_DTYPE))
            dkr = dkr + dk_ref[:, sl]
        dq_pre = jnp.concatenate(dq_pre, axis=1)
        dcq = _mm(dq_pre, wq_ref[...])
        dwq_ref[...] += _mm_tn(dq_pre, cq)
        dgq_ref[...] += jnp.sum(dcq * nq, axis=0, keepdims=True)
        db_ref[:, 0:Q_RANK] = _rms_bwd(dcq * gq_ref[...], nq, rq).astype(db_ref.dtype)
        dk = dk_ref[...].astype(MXU_DTYPE)
        dv = dv_ref[...].astype(MXU_DTYPE)
        dckv = _mm(dk, wk_ref[...]) + _mm(dv, wv_ref[...])
        dwk_ref[...] += _mm_tn(dk, ckv)
        dwv_ref[...] += _mm_tn(dv, ckv)
        dgkv_ref[...] += jnp.sum(dckv * nkv, axis=0, keepdims=True)
        db_ref[:, Q_RANK:Q_RANK + KV_RANK] = _rms_bwd(dckv * gkv_ref[...], nkv, rkv).astype(db_ref.dtype)
        lane = lax.broadcasted_iota(jnp.int32, (1, 128), 1)
        rope_lanes = (lane >= MLA_NOPE) & (lane < MLA_QK)
        db_ref[:, Q_RANK + KV_RANK:BW] = _rope_t(jnp.where(rope_lanes, dkr, 0.0), c, sa, sb).astype(db_ref.dtype)

    full = lambda a: pl.BlockSpec(a.shape, lambda i: (0,) * a.ndim)
    fulls = lambda shape: pl.BlockSpec(shape, lambda i: (0,) * len(shape))
    tab = pl.BlockSpec((tm, 128), lambda i: (i, 0))
    row = lambda w, cb=0: pl.BlockSpec((tm, w), lambda i, cb=cb: (i, cb))
    return pl.pallas_call(
        body, name="mla_prep_bwd", grid=(S // tm,),
        in_specs=[row(BW, QD // BW), pl.BlockSpec((1024, tm), lambda i: (0, i)), row(1024), row(BW),
                  full(gq), full(gkv), full(wuq_t),
                  full(wuk_t), full(wuv_t), tab, tab, tab],
        out_specs=[row(BW), fulls((1024, Q_RANK)), fulls((1024, KV_RANK)), fulls((BW, KV_RANK)),
                   fulls((1, Q_RANK)), fulls((1, KV_RANK))],
        out_shape=[jax.ShapeDtypeStruct((S, BW), MXU_DTYPE), jax.ShapeDtypeStruct((1024, Q_RANK), F32),
                   jax.ShapeDtypeStruct((1024, KV_RANK), F32), jax.ShapeDtypeStruct((BW, KV_RANK), F32),
                   jax.ShapeDtypeStruct((1, Q_RANK), F32), jax.ShapeDtypeStruct((1, KV_RANK), F32)],
        compiler_params=_params(("arbitrary",), 56),
    )(proj, dqf, dkf, dvf, gq, gkv, wuq_t, wuk_t, wuv_t, cos_t, sin_a, sin_b)


def _sgu_bwd(proj, dya, ln_g, ln_b, ws, bfull):
    S = proj.shape[0]
    tm = min(512, S)

    def body(u_ref, v_ref, z_ref, dy_ref, g_ref, b_ref, ws_ref, bf_ref,
             da_ref, dws_ref, dbf_ref, dlg_ref, dlb_ref, dbs_ref):
        i = pl.program_id(0)
        _first_step_zero((dws_ref, dbf_ref, dlg_ref, dlb_ref, dbs_ref), i == 0)
        tri = _sgu_tri()
        ws_m = [jnp.where(tri, ws_ref[g], 0.0).astype(MXU_DTYPE) for g in range(SGU_G)]
        grp = lax.broadcasted_iota(jnp.int32, (1, BW), 1) // (BW // SGU_G)
        for b in range(tm // SGU_T):
            r = slice(b * SGU_T, (b + 1) * SGU_T)
            vln, vhat, rstd = _layernorm(v_ref[r, :], g_ref[...], b_ref[...])
            vb = vln.astype(MXU_DTYPE)
            mixed = _sgu_mix(ws_m, vb, bf_ref[...], grp)
            u, z, dy = u_ref[r, :], z_ref[r, :], dy_ref[r, :]
            sg = _sigmoid(z)
            sz = z * sg
            da_ref[r, 0:BW] = (dy * mixed * sz).astype(da_ref.dtype)
            da_ref[r, 2 * BW:3 * BW] = (dy * u * mixed * (sg * (1.0 + z * (1.0 - sg)))).astype(da_ref.dtype)
            dmix = dy * u * sz
            dbf_ref[...] += dmix
            dvln = jnp.zeros((SGU_T, BW), F32)
            for g in range(SGU_G):
                dmg = jnp.where(grp == g, dmix, 0.0).astype(MXU_DTYPE)
                dvln = dvln + _mm_tn(ws_m[g], dmg)
                dws_ref[g] += jnp.where(tri, _mm_nt(dmg, vb), 0.0)
            dlg_ref[...] += jnp.sum(dvln * vhat, axis=0, keepdims=True)
            dlb_ref[...] += jnp.sum(dvln, axis=0, keepdims=True)
            dvh = dvln * g_ref[...]
            da_ref[r, BW:2 * BW] = (rstd * (dvh - jnp.mean(dvh, axis=-1, keepdims=True)
                                            - vhat * jnp.mean(dvh * vhat, axis=-1, keepdims=True))).astype(da_ref.dtype)

        @pl.when(i == pl.num_programs(0) - 1)
        def _():
            dbf = dbf_ref[...]
            for g in range(SGU_G):
                dbs_ref[:, g:g + 1] = jnp.sum(jnp.where(grp == g, dbf, 0.0), axis=1, keepdims=True)

    blk = lambda cb: pl.BlockSpec((tm, BW), lambda i, cb=cb: (i, cb))
    fulls = lambda shape: pl.BlockSpec(shape, lambda i: (0,) * len(shape))
    return pl.pallas_call(
        body, name="sgu_bwd", grid=(S // tm,),
        in_specs=[blk(UA // BW), blk(VA // BW), blk(ZA // BW), blk(0), fulls((1, BW)), fulls((1, BW)),
                  fulls((SGU_G, SGU_T, SGU_T)), fulls((SGU_T, BW))],
        out_specs=[pl.BlockSpec((tm, 3 * BW), lambda i: (i, 0)), fulls((SGU_G, SGU_T, SGU_T)),
                   fulls((SGU_T, BW)), fulls((1, BW)), fulls((1, BW)), fulls((SGU_T, SGU_G))],
        out_shape=[jax.ShapeDtypeStruct((S, 3 * BW), MXU_DTYPE), jax.ShapeDtypeStruct((SGU_G, SGU_T, SGU_T), F32),
                   jax.ShapeDtypeStruct((SGU_T, BW), F32), jax.ShapeDtypeStruct((1, BW), F32),
                   jax.ShapeDtypeStruct((1, BW), F32), jax.ShapeDtypeStruct((SGU_T, SGU_G), F32)],
        compiler_params=_params(("arbitrary",), 40),
    )(proj, proj, proj, dya, ln_g, ln_b, ws, bfull)


DX_TK = 3 * BW


def _inproj_bwd_dx(dgl, da, small, w_t, x, pre_g, dxo, send=None):
    S = x.shape[0]
    tm = min(512, S)
    nk = DP // DX_TK

    def body(g_ref, a_ref, s0, s1, s2, s3, s4, s5, w_ref, x_ref, pg_ref, dxo_ref, *rest):
        if send is None:
            dx_ref, dg_ref, acc = rest
        else:
            send_ref, dx_ref, dg_ref, recv_ref, acc, *sems = rest
            _hosted_comm(send_ref, recv_ref, sems, same_source=False)
        i, k = pl.program_id(0), pl.program_id(1)
        _first_step_zero((dg_ref,), (i == 0) & (k == 0))

        @pl.when(k == 0)
        def _():
            acc[...] = jnp.zeros(acc.shape, F32)

        @pl.when(k < 2)
        def _():
            acc[...] += _mm(g_ref[...], w_ref[...])

        @pl.when(k == 2)
        def _():
            acc[...] += _mm(a_ref[...], w_ref[...])

        for kk, trio in ((3, (s0, s1, s2)), (4, (s3, s4, s5))):
            @pl.when(k == kk)
            def _(trio=trio):
                t = acc[...]
                for n, ref in enumerate(trio):
                    t = t + _mm(ref[...], w_ref[n * BW:(n + 1) * BW, :])
                acc[...] = t

        @pl.when(k == nk - 1)
        def _():
            n, r = _rms(x_ref[...])
            dxn = acc[...]
            dg_ref[...] += jnp.sum(dxn * n, axis=0, keepdims=True)
            dx_ref[...] = dxo_ref[...] + _rms_bwd(dxn * pg_ref[...], n, r)

    row = pl.BlockSpec((tm, D), lambda i, k: (i, 0))
    vec = pl.BlockSpec((1, D), lambda i, k: (0, 0))
    fixed = lambda w: pl.BlockSpec((tm, w), lambda i, k: (i, 0))
    hbm = pl.BlockSpec(memory_space=pl.ANY)
    hosted = send is not None
    recv_shape = [jax.ShapeDtypeStruct(send.shape, send.dtype)] if hosted else []
    return pl.pallas_call(
        body, name="inproj_bwd_dx_exchange" if hosted else "inproj_bwd_dx", grid=(S // tm, nk),
        in_specs=[pl.BlockSpec((tm, DX_TK), lambda i, k: (i, jnp.minimum(k, 1))), fixed(DX_TK)]
                 + [fixed(BW)] * 6 + [pl.BlockSpec((DX_TK, D), lambda i, k: (k, 0)), row, vec, row] + [hbm] * hosted,
        out_specs=[row, vec] + [hbm] * hosted,
        out_shape=[jax.ShapeDtypeStruct((S, D), F32), jax.ShapeDtypeStruct((1, D), F32)] + recv_shape,
        scratch_shapes=[pltpu.VMEM((tm, D), F32)] + COMM_SEMS * hosted,
        compiler_params=_params(("arbitrary", "arbitrary"), 56),
    )(dgl, da, *small, w_t, x, pre_g, dxo, *([send] * hosted))


def _matmul_tn(segs, b):
    S, K = b.shape
    tk = min(1024, S)
    widths = [a.shape[1] for a in segs]
    offs = [sum(widths[:n]) for n in range(len(segs))]

    def body(*refs):
        seg_refs, b_ref, o_ref = refs[:len(segs)], refs[len(segs)], refs[len(segs) + 1]
        _first_step_zero((o_ref,), pl.program_id(0) == 0)
        bv = b_ref[...]
        for ref, off, w in zip(seg_refs, offs, widths):
            o_ref[off:off + w, :] += _mm_tn(ref[...], bv)

    return pl.pallas_call(
        body, name="matmul_tn", grid=(S // tk,),
        in_specs=[pl.BlockSpec((tk, w), lambda k: (k, 0)) for w in widths] + [pl.BlockSpec((tk, K), lambda k: (k, 0))],
        out_specs=pl.BlockSpec((sum(widths), K), lambda k: (0, 0)),
        out_shape=jax.ShapeDtypeStruct((sum(widths), K), F32),
        compiler_params=_params(("arbitrary",), 56),
    )(*segs, b)


def _pad_rows_w_in(wt):
    z = lambda n: jnp.zeros((n, wt.shape[1]), wt.dtype)
    return jnp.concatenate([wt[4512:], wt[:1920], z(64), wt[1920:1952], z(32), wt[1952:4512]], axis=0)


def _unpad_rows_w_in(dwt):
    a = dwt[3 * D:]
    return jnp.concatenate([a[:1920], a[1984:2016], a[2048:], dwt[:3 * D]], axis=0)


def _pad_head_rows(wt, width):
    k = wt.shape[1]
    return jnp.pad(wt.reshape(MLA_H, width, k), ((0, 0), (0, 128 - width), (0, 0))).reshape(MLA_H * 128, k)


def _unpad_head_rows(wt, width):
    k = wt.shape[1]
    return wt.reshape(MLA_H, 128, k)[:, :width].reshape(MLA_H * width, k)


def _rope_tables(S):
    half = MLA_ROPE // 2
    inv = 10000.0 ** (-jnp.arange(half, dtype=F32) / half)
    ang = jnp.arange(S, dtype=F32)[:, None] * inv[None, :]
    cos, sin = jnp.cos(ang), jnp.sin(ang)
    one = lambda n: jnp.ones((S, n), F32)
    zero = lambda n: jnp.zeros((S, n), F32)
    cos_t = jnp.concatenate([one(MLA_NOPE), cos, cos, one(128 - MLA_QK)], axis=1)
    sin_a = jnp.concatenate([zero(MLA_NOPE), -sin, zero(128 - MLA_NOPE - half)], axis=1)
    sin_b = jnp.concatenate([zero(MLA_NOPE + half), sin, zero(128 - MLA_QK)], axis=1)
    return cos_t, sin_a, sin_b


SHARDED = ("w_in", "mla_w_uq", "mla_w_ukv", "w_branch", "w_out")
SHARD_PACK = SHARDED + ("gate_b",)
REPLICATED = ("pre_g", "post_g", "sgu_ln_g", "sgu_ln_b", "sgu_w", "sgu_b", "mla_q_norm_g", "mla_kv_norm_g",
              "ca_rel_bias")
OUT_ORDER = ("w_in", "pre_g", "post_g", "sgu_ln_g", "sgu_ln_b", "sgu_w", "sgu_b", "mla_q_norm_g", "mla_kv_norm_g",
             "mla_w_uq", "mla_w_ukv", "ca_rel_bias", "w_branch", "gate_b", "w_out")
TRANSPOSED = {"w_in": (0, 2, 1), "mla_w_uq": (0, 2, 1), "mla_w_ukv": (0, 2, 1), "w_branch": (0, 1, 3, 2)}


def _canon(name, a):
    return jnp.transpose(a, TRANSPOSED[name]) if name in TRANSPOSED else a


ROW_MULT = 64
PIECE_MULT = 16


def _piece_rows(shape):
    size = 1
    for s in shape:
        size *= s
    return -(-size // (1024 * PIECE_MULT)) * PIECE_MULT


def _pack_rows(pieces, lead, mult=ROW_MULT):
    rows = []
    for a in pieces:
        flat = a.reshape(a.shape[:lead] + (-1,))
        nr = _piece_rows(a.shape[lead:])
        flat = jnp.pad(flat, [(0, 0)] * lead + [(0, nr * 1024 - flat.shape[-1])])
        rows.append(flat.reshape(a.shape[:lead] + (nr, 1024)))
    total = sum(r.shape[lead] for r in rows)
    tail = (-total) % mult
    if tail:
        rows.append(jnp.zeros(rows[0].shape[:lead] + (tail, 1024), rows[0].dtype))
    return jnp.concatenate(rows, axis=lead)


def _unpack_rows(p, shapes, lead):
    out, r = [], 0
    for s in shapes:
        nr, size = _piece_rows(s), 1
        for d in s:
            size *= d
        piece = lax.slice_in_dim(p, r, r + nr, axis=lead).reshape(p.shape[:lead] + (-1,))
        out.append(lax.slice_in_dim(piece, 0, size, axis=lead).reshape(p.shape[:lead] + tuple(s)))
        r += nr
    return out


def kernel(x, w_in, pre_g, post_g, sgu_ln_g, sgu_ln_b, sgu_w, sgu_b, mla_q_norm_g, mla_kv_norm_g, mla_w_uq, mla_w_ukv, ca_rel_bias, w_branch, gate_b, w_out, loss_target, m_w_in, m_pre_g, m_post_g, m_sgu_ln_g, m_sgu_ln_b, m_sgu_w, m_sgu_b, m_mla_q_norm_g, m_mla_kv_norm_g, m_mla_w_uq, m_mla_w_ukv, m_ca_rel_bias, m_w_branch, m_gate_b, m_w_out, v_w_in, v_pre_g, v_post_g, v_sgu_ln_g, v_sgu_ln_b, v_sgu_w, v_sgu_b, v_mla_q_norm_g, v_mla_kv_norm_g, v_mla_w_uq, v_mla_w_ukv, v_ca_rel_bias, v_w_branch, v_gate_b, v_w_out):
    weights = dict(w_in=w_in, pre_g=pre_g, post_g=post_g, sgu_ln_g=sgu_ln_g, sgu_ln_b=sgu_ln_b, sgu_w=sgu_w,
                   sgu_b=sgu_b, mla_q_norm_g=mla_q_norm_g, mla_kv_norm_g=mla_kv_norm_g, mla_w_uq=mla_w_uq,
                   mla_w_ukv=mla_w_ukv, ca_rel_bias=ca_rel_bias, w_branch=w_branch, gate_b=gate_b, w_out=w_out)
    mom_m = dict(w_in=m_w_in, pre_g=m_pre_g, post_g=m_post_g, sgu_ln_g=m_sgu_ln_g, sgu_ln_b=m_sgu_ln_b,
                 sgu_w=m_sgu_w, sgu_b=m_sgu_b, mla_q_norm_g=m_mla_q_norm_g, mla_kv_norm_g=m_mla_kv_norm_g,
                 mla_w_uq=m_mla_w_uq, mla_w_ukv=m_mla_w_ukv, ca_rel_bias=m_ca_rel_bias, w_branch=m_w_branch,
                 gate_b=m_gate_b, w_out=m_w_out)
    mom_v = dict(w_in=v_w_in, pre_g=v_pre_g, post_g=v_post_g, sgu_ln_g=v_sgu_ln_g, sgu_ln_b=v_sgu_ln_b,
                 sgu_w=v_sgu_w, sgu_b=v_sgu_b, mla_q_norm_g=v_mla_q_norm_g, mla_kv_norm_g=v_mla_kv_norm_g,
                 mla_w_uq=v_mla_w_uq, mla_w_ukv=v_mla_w_ukv, ca_rel_bias=v_ca_rel_bias, w_branch=v_w_branch,
                 gate_b=v_gate_b, w_out=v_w_out)
    depth = w_in.shape[0]
    S = x.shape[1]
    xs = x.reshape(S, D)

    cw = {n: _canon(n, weights[n]) for n in SHARD_PACK}

    def layer_block(l):
        return _pack_rows([cw[n][l].astype(MXU_DTYPE) for n in SHARDED], 0, PIECE_MULT)

    def split_block(got):
        return dict(zip(SHARDED, _unpack_rows(got, [cw[n].shape[1:] for n in SHARDED], 1)))

    gate_all, = _unpack_rows(_all_gather(_pack_rows([gate_b], 0, PIECE_MULT)), [gate_b.shape], 1)
    cos_t, sin_a, sin_b = _rope_tables(S)

    def layer_weights(gl, l):
        w_t = _pad_rows_w_in(gl["w_in"].reshape(D_IN, D))
        wuq_t = _pad_head_rows(gl["mla_w_uq"].reshape(MLA_H * MLA_QK, Q_RANK), MLA_QK)
        ukv_t = gl["mla_w_ukv"]
        wuk_t = jnp.pad(ukv_t[:, :MLA_NOPE], ((0, 0), (0, 128 - MLA_NOPE), (0, 0))).reshape(MLA_H * 128, KV_RANK)
        wuv_t = ukv_t[:, MLA_NOPE:].reshape(BW, KV_RANK)
        wbr_t = jnp.swapaxes(gl["w_branch"], 0, 1).reshape(N_BRANCH, D, BW)
        return dict(
            w_t=w_t, w_pad=w_t.T, wuq_t=wuq_t, wuq=wuq_t.T, wuk_t=wuk_t, wuk=wuk_t.T, wuv_t=wuv_t, wuv=wuv_t.T,
            wbr_t=wbr_t, wbr=jnp.swapaxes(wbr_t, 1, 2), wout=gl["w_out"].reshape(D, D),
            gate_b=jnp.swapaxes(gate_all[:, l], 0, 1).reshape(N_BRANCH, D),
            pre_g=pre_g[l][None], post_g=post_g[l][None], ln_g=sgu_ln_g[l][None], ln_b=sgu_ln_b[l][None],
            ws=sgu_w[l], bfull=jnp.repeat(sgu_b[l].T, BW // SGU_G, axis=1),
            gq=mla_q_norm_g[l][None], gkv=mla_kv_norm_g[l][None], bias=_bias_build(ca_rel_bias[l]))

    layers, saved = [], []
    h_x = xs
    got = _all_gather(layer_block(0))
    for l in range(depth):
        lw = layer_weights(split_block(got), l)
        layers.append(lw)
        proj, xn = _inproj_fwd(h_x, lw["pre_g"], lw["w_pad"])
        ya = _sgu_fwd(proj, lw["ln_g"], lw["ln_b"], lw["ws"], lw["bfull"])
        qh, kh, vh, ktr, vtr = _mla_prep_fwd(proj, lw["gq"], lw["gkv"], lw["wuq"], lw["wuk"], lw["wuv"],
                                             cos_t, sin_a, sin_b)
        if l + 1 < depth:
            ob, lse, got = _flash_fwd(qh, kh, vtr, gather=layer_block(l + 1))
        else:
            ob, lse = _flash_fwd(qh, kh, vtr)
        oc = _band_fwd(proj, lw["bias"])
        x_new, merged, hh = _merge_fwd(h_x, ya, ob, oc, proj, lw["gate_b"], lw["wbr"], lw["wout"], lw["post_g"])
        saved.append(dict(x=h_x, proj=proj, xn=xn, ya=ya, qh=qh, kh=kh, vh=vh, ktr=ktr, ob=ob, lse=lse, oc=oc,
                          merged=merged, h=hh))
        h_x = x_new

    loss_part, dx = _loss_fwd_bwd(h_x, loss_target.reshape(S, D))
    loss = lax.psum(loss_part[0, 0], MESH_AXES)

    rows = {}
    rep = {n: [None] * depth for n in REPLICATED}
    recvs = [None] * depth
    pending = None
    for l in reversed(range(depth)):
        lw, sv = layers[l], saved[l]
        proj = sv["proj"]
        dmerged, dw_out, dg_post = _out_bwd(dx, sv["h"], sv["merged"], lw["post_g"], lw["wout"].T)
        dgl, dzb, dzc, dya, dob, doc, dwbr_t, dgb = _gate_bwd(
            dmerged, sv["ya"], sv["ob"], sv["oc"], proj, lw["gate_b"], lw["wbr"], lw["wbr_t"])
        dqc, dkc, dvc, dbias = _band_bwd(proj, lw["bias"], doc)
        drel = _bias_fold(dbias)
        flash_args = (sv["qh"], sv["kh"], sv["ktr"], sv["vh"], sv["lse"], _attn_delta(sv["ob"], dob), dob)
        if pending is None:
            dqf, dkf, dvf = _flash_bwd(*flash_args)
        else:
            dqf, dkf, dvf, recvs[l + 1] = _flash_bwd(*flash_args, send=pending)
        db, dwuq_t, dwuk_t, dwuv_t, dgq, dgkv = _mla_prep_bwd(
            proj, dqf, dkf, dvf, lw["gq"], lw["gkv"], lw["wuq_t"], lw["wuk_t"], lw["wuv_t"], cos_t, sin_a, sin_b)
        da, dws, _, dlg, dlb, dbs = _sgu_bwd(proj, dya, lw["ln_g"], lw["ln_b"], lw["ws"], lw["bfull"])
        small = (db, dzb, dqc, dkc, dvc, dzc)
        dw_t = jnp.concatenate([_matmul_tn([dgl], sv["xn"]), _matmul_tn([da, db, dzb], sv["xn"]),
                                _matmul_tn([dqc, dkc, dvc, dzc], sv["xn"])], axis=0)

        rows["w_in"] = _unpad_rows_w_in(dw_t).reshape(N_DEV, D_IN // N_DEV, D)
        rows["mla_w_uq"] = _unpad_head_rows(dwuq_t, MLA_QK).reshape(N_DEV, MLA_QK, Q_RANK)
        dk3 = dwuk_t.reshape(MLA_H, 128, KV_RANK)[:, :MLA_NOPE]
        dv3 = dwuv_t.reshape(MLA_H, 64, KV_RANK)
        rows["mla_w_ukv"] = jnp.concatenate([dk3, dv3], axis=1)
        rows["w_branch"] = jnp.swapaxes(dwbr_t.reshape(N_BRANCH, N_DEV, D // N_DEV, BW), 0, 1)
        rows["w_out"] = dw_out.reshape(N_DEV, D // N_DEV, D)
        rows["gate_b"] = jnp.swapaxes(dgb.reshape(N_BRANCH, N_DEV, D // N_DEV), 0, 1)
        pending = _pack_rows([rows[n].astype(MXU_DTYPE) for n in SHARD_PACK], 1)
        dx_args = (dgl, da, small, lw["w_t"], sv["x"], lw["pre_g"], dx)
        if l > 0:
            dx, dg_pre = _inproj_bwd_dx(*dx_args)
        else:
            dx, dg_pre, recvs[0] = _inproj_bwd_dx(*dx_args, send=pending)
        rep["pre_g"][l] = dg_pre[0]
        rep["post_g"][l] = dg_post[0]
        rep["sgu_ln_g"][l] = dlg[0]
        rep["sgu_ln_b"][l] = dlb[0]
        rep["sgu_w"][l] = dws
        rep["sgu_b"][l] = dbs.T
        rep["mla_q_norm_g"][l] = dgq[0]
        rep["mla_kv_norm_g"][l] = dgkv[0]
        rep["ca_rel_bias"][l] = drel
    grad_x = dx.reshape(x.shape)

    cm = {n: _canon(n, mom_m[n]) for n in SHARD_PACK}
    cv = {n: _canon(n, mom_v[n]) for n in SHARD_PACK}
    shard_shapes = [cw[n].shape[1:] for n in SHARD_PACK]
    per_layer = []
    for l in range(depth):
        at = lambda d: _pack_rows([d[n][l] for n in SHARD_PACK], 0)
        outs_l = _reduce_adamw(recvs[l], at(cw), at(cm), at(cv))
        per_layer.append([_unpack_rows(p, shard_shapes, 0) for p in outs_l])
    outs_sh = [{n: _canon(n, jnp.stack([per_layer[l][k][i] for l in range(depth)]))
                for i, n in enumerate(SHARD_PACK)} for k in range(4)]

    rp = lambda d: _pack_rows([d[n] for n in REPLICATED], 0)
    allp = _all_gather(rp({n: jnp.stack(rep[n]) for n in REPLICATED}))
    outs_rep = _reduce_adamw(allp, rp(weights), rp(mom_m), rp(mom_v))
    rep_shapes = [weights[n].shape for n in REPLICATED]
    outs_rep = [dict(zip(REPLICATED, _unpack_rows(p, rep_shapes, 0))) for p in outs_rep]

    outs = [{**a, **b} for a, b in zip(outs_sh, outs_rep)]
    return (loss, grad_x, *[o[n] for o in outs for n in OUT_ORDER])
```

```python
import functools

import jax
import jax.numpy as jnp
from jax import lax
from jax.experimental import pallas as pl
from jax.experimental.pallas import tpu as pltpu

F32 = jnp.float32
MXU_DTYPE = jnp.bfloat16
EPS = 1e-6
NEG = -1e30
MESH_AXES = ("x", "y", "c")
N_DEV = 8

D = 1024
BW = 512
N_BRANCH = 3
CHUNK = 64
SGU_T = 128
SGU_G = 8
MLA_H = 8
MLA_NOPE = 64
MLA_ROPE = 32
MLA_QK = MLA_NOPE + MLA_ROPE
Q_RANK = 256
KV_RANK = 128
CA_H = 8
CA_DH = 64
LEFT_CHUNKS = 8
REL_CLIP = 128
D_IN = 7584

G_OFF, UA, VA, ZA, QD, ZB, QC, KC, VC, ZC, DP = 0, 3072, 3584, 4096, 4608, 5120, 5632, 6144, 6656, 7168, 7680
BAND_TQ = 256
BAND_W = 3 * BAND_TQ

ADAM_LR, ADAM_B1, ADAM_B2, ADAM_EPS, ADAM_WD, ADAM_STEP = 0.001, 0.9, 0.999, 1e-08, 0.01, 10


def _params(sem, mib):
    return pltpu.CompilerParams(dimension_semantics=sem, vmem_limit_bytes=mib << 20)


def _mm(a, b):
    return jnp.dot(a.astype(MXU_DTYPE), b.astype(MXU_DTYPE), preferred_element_type=F32)


def _mm_nt(a, b):
    return lax.dot_general(a.astype(MXU_DTYPE), b.astype(MXU_DTYPE), (((1,), (1,)), ((), ())),
                           preferred_element_type=F32)


def _mm_tn(a, b):
    return lax.dot_general(a.astype(MXU_DTYPE), b.astype(MXU_DTYPE), (((0,), (0,)), ((), ())),
                           preferred_element_type=F32)


def _sigmoid(z):
    return 1.0 / (1.0 + jnp.exp(-z))


def _rms(x):
    r = lax.rsqrt(jnp.mean(x * x, axis=-1, keepdims=True) + EPS)
    return x * r, r


def _rms_bwd(dn, n, r):
    return r * (dn - n * jnp.mean(dn * n, axis=-1, keepdims=True))


def _rope(b, c, sa, sb):
    return b * c + pltpu.roll(b, 112, 1) * sa + pltpu.roll(b, 16, 1) * sb


def _rope_t(d, c, sa, sb):
    return d * c + pltpu.roll(d * sa, 16, 1) + pltpu.roll(d * sb, 112, 1)


def _all_gather(blk):
    R = blk.shape[0]

    def body(x_ref, out_ref, send_sems, recv_sems, local_sem):
        x, y, c = lax.axis_index("x"), lax.axis_index("y"), lax.axis_index("c")
        me, sibling = (x, y, c), (x, y, 1 - c)
        chips = [(1 - x, y), (x, 1 - y), (1 - x, 1 - y)]

        def slot(px, py, pc):
            return out_ref.at[4 * px + 2 * py + pc]

        def copy(k, block, to, src=None):
            return pltpu.make_async_remote_copy(
                src_ref=slot(*block) if src is None else src, dst_ref=slot(*block),
                send_sem=send_sems.at[k], recv_sem=recv_sems.at[k],
                device_id=to, device_id_type=pl.DeviceIdType.MESH)

        mine = pltpu.make_async_copy(x_ref, slot(*me), local_sem)
        mine.start()
        first = [copy(0, me, sibling, src=x_ref)]
        first += [copy(1 + j, me, (*chip, c), src=x_ref) for j, chip in enumerate(chips)]
        for cp in first:
            cp.start()
        passed = [copy(4 + j, (*chip, c), sibling) for j, chip in enumerate(chips)]
        for j, chip in enumerate(chips):
            copy(1 + j, (*chip, c), me).wait_recv()
            passed[j].start()
        copy(0, sibling, me).wait_recv()
        for j, chip in enumerate(chips):
            copy(4 + j, (*chip, 1 - c), me).wait_recv()
        for cp in first + passed:
            cp.wait_send()
        mine.wait()

    return pl.pallas_call(
        body, name="all_gather",
        out_shape=jax.ShapeDtypeStruct((N_DEV,) + blk.shape, blk.dtype),
        in_specs=[pl.BlockSpec(memory_space=pl.ANY)],
        out_specs=pl.BlockSpec(memory_space=pl.ANY),
        scratch_shapes=[pltpu.SemaphoreType.DMA((7,)), pltpu.SemaphoreType.DMA((7,)), pltpu.SemaphoreType.DMA(())],
    )(blk)


def _peer_copies(s_ref, r_ref, send_sems, recv_sems, local_sem, same_source):
    x, y, c = lax.axis_index("x"), lax.axis_index("y"), lax.axis_index("c")
    me = 4 * x + 2 * y + c

    def src(pid):
        return s_ref if same_source else s_ref.at[pid]

    def peer(k):
        px = 1 - x if (k >> 2) & 1 else x
        py = 1 - y if (k >> 1) & 1 else y
        pc = 1 - c if k & 1 else c
        return (px, py, pc), 4 * px + 2 * py + pc

    def remote(k, row):
        pos, pid = peer(k)
        return pltpu.make_async_remote_copy(
            src_ref=src(pid), dst_ref=r_ref.at[me if row is None else pid],
            send_sem=send_sems.at[k], recv_sem=recv_sems.at[k],
            device_id=pos, device_id_type=pl.DeviceIdType.MESH)

    def local():
        return pltpu.make_async_copy(src(me), r_ref.at[me], local_sem)

    def start():
        local().start()
        for k in range(1, N_DEV):
            remote(k, None).start()

    def wait():
        for k in range(1, N_DEV):
            remote(k, "peer").wait_recv()
        for k in range(1, N_DEV):
            remote(k, None).wait_send()
        local().wait()

    return start, wait


COMM_SEMS = [pltpu.SemaphoreType.DMA((N_DEV,)), pltpu.SemaphoreType.DMA((N_DEV,)), pltpu.SemaphoreType.DMA(())]


def _hosted_comm(s_ref, r_ref, sems, same_source):
    start, wait = _peer_copies(s_ref, r_ref, *sems, same_source=same_source)
    i, j = pl.program_id(0), pl.program_id(1)

    @pl.when((i == 0) & (j == 0))
    def _():
        start()

    @pl.when((i == pl.num_programs(0) - 1) & (j == pl.num_programs(1) - 1))
    def _():
        wait()


def _reduce_adamw(recv, w, m, v):
    R = w.shape[0]
    tr = next(t for t in (128, 64, 32, 16) if R % t == 0)
    c1 = 1.0 - ADAM_B1 ** ADAM_STEP
    c2 = 1.0 - ADAM_B2 ** ADAM_STEP

    def body(r_ref, w_ref, m_ref, v_ref, g_ref, d_ref, nm_ref, nv_ref):
        g = r_ref[0].astype(F32)
        for s in range(1, N_DEV):
            g = g + r_ref[s].astype(F32)
        m2 = ADAM_B1 * m_ref[...] + (1.0 - ADAM_B1) * g
        v2 = ADAM_B2 * v_ref[...] + (1.0 - ADAM_B2) * (g * g)
        m_hat = m2 / c1
        v_hat = v2 / c2
        g_ref[...] = g
        d_ref[...] = -ADAM_LR * (m_hat / (jnp.sqrt(v_hat) + ADAM_EPS) + ADAM_WD * w_ref[...])
        nm_ref[...] = m2
        nv_ref[...] = v2

    row = pl.BlockSpec((tr, 1024), lambda i: (i, 0))
    return pl.pallas_call(
        body, name="reduce_adamw", grid=(R // tr,),
        in_specs=[pl.BlockSpec((N_DEV, tr, 1024), lambda i: (0, i, 0)), row, row, row],
        out_specs=[row, row, row, row],
        out_shape=[jax.ShapeDtypeStruct((R, 1024), F32)] * 4,
        compiler_params=_params(("parallel",), 40),
    )(recv, w, m, v)


def _inproj_fwd(x, pre_g, w_pad):
    S = x.shape[0]
    tm, tn = min(1024, S), 1536

    def body(x_ref, g_ref, w_ref, proj_ref, xn_ref):
        @pl.when(pl.program_id(1) == 0)
        def _():
            n, _ = _rms(x_ref[...])
            xn_ref[...] = (n * g_ref[...]).astype(xn_ref.dtype)
        proj_ref[...] = jnp.dot(xn_ref[...], w_ref[...], preferred_element_type=F32)

    return pl.pallas_call(
        body, name="inproj_fwd", grid=(S // tm, DP // tn),
        in_specs=[pl.BlockSpec((tm, D), lambda i, j: (i, 0)), pl.BlockSpec((1, D), lambda i, j: (0, 0)),
                  pl.BlockSpec((D, tn), lambda i, j: (0, j))],
        out_specs=[pl.BlockSpec((tm, tn), lambda i, j: (i, j)), pl.BlockSpec((tm, D), lambda i, j: (i, 0))],
        out_shape=[jax.ShapeDtypeStruct((S, DP), F32), jax.ShapeDtypeStruct((S, D), MXU_DTYPE)],
        compiler_params=_params(("parallel", "arbitrary"), 48),
    )(x, pre_g, w_pad)


def _sgu_tri():
    return lax.broadcasted_iota(jnp.int32, (SGU_T, SGU_T), 0) >= lax.broadcasted_iota(jnp.int32, (SGU_T, SGU_T), 1)


def _sgu_mix(ws_m, vb, bfull, grp):
    mixed = bfull
    for g in range(SGU_G):
        mixed = mixed + jnp.where(grp == g, _mm(ws_m[g], vb), 0.0)
    return mixed


def _layernorm(v, g, b):
    xc = v - jnp.mean(v, axis=-1, keepdims=True)
    rstd = lax.rsqrt(jnp.mean(xc * xc, axis=-1, keepdims=True) + EPS)
    vhat = xc * rstd
    return vhat * g + b, vhat, rstd


def _sgu_fwd(proj, ln_g, ln_b, ws, bfull):
    S = proj.shape[0]
    tm = min(512, S)

    def body(u_ref, v_ref, z_ref, g_ref, b_ref, ws_ref, bf_ref, ya_ref):
        tri = _sgu_tri()
        ws_m = [jnp.where(tri, ws_ref[g], 0.0).astype(MXU_DTYPE) for g in range(SGU_G)]
        grp = lax.broadcasted_iota(jnp.int32, (1, BW), 1) // (BW // SGU_G)
        for b in range(tm // SGU_T):
            r = slice(b * SGU_T, (b + 1) * SGU_T)
            vln, _, _ = _layernorm(v_ref[r, :], g_ref[...], b_ref[...])
            mixed = _sgu_mix(ws_m, vln.astype(MXU_DTYPE), bf_ref[...], grp)
            z = z_ref[r, :]
            ya_ref[r, :] = (u_ref[r, :] * mixed * (z * _sigmoid(z))).astype(ya_ref.dtype)

    blk = lambda cb: pl.BlockSpec((tm, BW), lambda i, cb=cb: (i, cb))
    vec = pl.BlockSpec((1, BW), lambda i: (0, 0))
    return pl.pallas_call(
        body, name="sgu_fwd", grid=(S // tm,),
        in_specs=[blk(UA // BW), blk(VA // BW), blk(ZA // BW), vec, vec,
                  pl.BlockSpec((SGU_G, SGU_T, SGU_T), lambda i: (0, 0, 0)),
                  pl.BlockSpec((SGU_T, BW), lambda i: (0, 0))],
        out_specs=pl.BlockSpec((tm, BW), lambda i: (i, 0)),
        out_shape=jax.ShapeDtypeStruct((S, BW), MXU_DTYPE),
        compiler_params=_params(("parallel",), 32),
    )(proj, proj, proj, ln_g, ln_b, ws, bfull)


def _mla_prep_fwd(proj, gq, gkv, wuq, wuk, wuv, cos_t, sin_a, sin_b):
    S = proj.shape[0]
    tm = min(512, S)

    def body(p_ref, gq_ref, gkv_ref, wuq_ref, wuk_ref, wuv_ref, c_ref, sa_ref, sb_ref,
             q_ref, k_ref, v_ref, kt_ref, vt_ref):
        nq, _ = _rms(p_ref[:, 0:Q_RANK])
        nkv, _ = _rms(p_ref[:, Q_RANK:Q_RANK + KV_RANK])
        cq = (nq * gq_ref[...]).astype(MXU_DTYPE)
        ckv = (nkv * gkv_ref[...]).astype(MXU_DTYPE)
        qf = _mm(cq, wuq_ref[...])
        kf = _mm(ckv, wuk_ref[...])
        v = _mm(ckv, wuv_ref[...])
        v_ref[...] = v.astype(v_ref.dtype)
        vt_ref[...] = jnp.transpose(v).astype(vt_ref.dtype)
        c, sa, sb = c_ref[...], sa_ref[...], sb_ref[...]
        krr = _rope(p_ref[:, Q_RANK + KV_RANK:BW], c, sa, sb)
        for h in range(MLA_H):
            sl = slice(128 * h, 128 * (h + 1))
            q_ref[:, sl] = (_rope(qf[:, sl], c, sa, sb) * MLA_SCALE_LOG2).astype(q_ref.dtype)
            kh = kf[:, sl] + krr
            k_ref[:, sl] = kh.astype(k_ref.dtype)
            kt_ref[sl, :] = jnp.transpose(kh).astype(kt_ref.dtype)

    full = lambda a: pl.BlockSpec(a.shape, lambda i: (0,) * a.ndim)
    tab = pl.BlockSpec((tm, 128), lambda i: (i, 0))
    return pl.pallas_call(
        body, name="mla_prep_fwd", grid=(S // tm,),
        in_specs=[pl.BlockSpec((tm, BW), lambda i: (i, QD // BW)), full(gq), full(gkv), full(wuq), full(wuk),
                  full(wuv), tab, tab, tab],
        out_specs=[pl.BlockSpec((tm, 1024), lambda i: (i, 0)), pl.BlockSpec((tm, 1024), lambda i: (i, 0)),
                   pl.BlockSpec((tm, BW), lambda i: (i, 0)), pl.BlockSpec((1024, tm), lambda i: (0, i)),
                   pl.BlockSpec((BW, tm), lambda i: (0, i))],
        out_shape=[jax.ShapeDtypeStruct((S, 1024), MXU_DTYPE), jax.ShapeDtypeStruct((S, 1024), MXU_DTYPE),
                   jax.ShapeDtypeStruct((S, BW), MXU_DTYPE), jax.ShapeDtypeStruct((1024, S), MXU_DTYPE),
                   jax.ShapeDtypeStruct((BW, S), MXU_DTYPE)],
        compiler_params=_params(("parallel",), 48),
    )(proj, gq, gkv, wuq, wuk, wuv, cos_t, sin_a, sin_b)


MLA_SCALE = MLA_QK ** -0.5
MLA_SCALE_LOG2 = MLA_SCALE * 1.4426950408889634


def _tri_tables(n, q_major):
    if q_major:
        pairs = [(qi, ki) for qi in range(n) for ki in range(qi + 1)]
    else:
        pairs = [(qi, ki) for ki in range(n) for qi in range(ki, n)]
    return (jnp.asarray([p[0] for p in pairs], jnp.int32), jnp.asarray([p[1] for p in pairs], jnp.int32))


def _chunk_mask_t(T):
    kc = lax.broadcasted_iota(jnp.int32, (T, T), 0) >> 6
    qc = lax.broadcasted_iota(jnp.int32, (T, T), 1) >> 6
    return kc <= qc


FLASH_T = 2048
FLASH_SUB = 512


def _col_reduce(op, reduce_fn, x):
    r = x.shape[0]
    while r > 8 and r % 16 == 0:
        r //= 2
        x = op(x[:r], x[r:])
    return reduce_fn(x, axis=0, keepdims=True)


def _flash_fwd(qh, kh, vt, gather=None):
    S = qh.shape[0]
    T = min(FLASH_T, S)
    TS = min(FLASH_SUB, T)
    n = S // T
    qt, kt = _tri_tables(n, q_major=True)

    def body(qt_ref, kt_ref, q_ref, k_ref, vt_ref, *rest):
        if gather is None:
            o_ref, lse_ref, m_scr, acc_scr = rest
        else:
            g_ref, o_ref, lse_ref, got_ref, m_scr, acc_scr, *sems = rest
            _hosted_comm(g_ref, got_ref, sems, same_source=True)
        t = pl.program_id(1)
        qi, ki = qt_ref[t], kt_ref[t]
        row = lax.broadcasted_iota(jnp.int32, (128, 1), 0)
        ones_row = (64, 0)

        @pl.when(ki == 0)
        def _():
            m_scr[...] = jnp.full(m_scr.shape, NEG, F32)
            acc_scr[...] = jnp.zeros(acc_scr.shape, F32)

        def step(diag):
            for j in range(2):
                sl = slice(128 * j, 128 * (j + 1))
                for qs in range(T // TS):
                    cq = slice(qs * TS, (qs + 1) * TS)
                    qsub = q_ref[cq, sl]
                    for ks in range(T // TS):
                        if diag and ks > qs:
                            continue
                        rk = slice(ks * TS, (ks + 1) * TS)
                        s = _mm_nt(k_ref[rk, sl], qsub)
                        if diag and ks == qs:
                            s = jnp.where(_chunk_mask_t(TS), s, NEG)
                        m_old = m_scr[j, :, cq]
                        m_new = jnp.maximum(m_old, _col_reduce(jnp.maximum, jnp.max, s))
                        alpha = jnp.exp2(m_old - m_new)
                        p = jnp.exp2(s - m_new)
                        vt1 = jnp.where(row == ones_row[j], 1.0, vt_ref[:, rk]).astype(MXU_DTYPE)
                        acc_scr[j, :, cq] = alpha * acc_scr[j, :, cq] + _mm(vt1, p)
                        m_scr[j, :, cq] = m_new

        @pl.when(ki < qi)
        def _():
            step(False)

        @pl.when(ki == qi)
        def _():
            step(True)
            l = [acc_scr[j, ones_row[j]:ones_row[j] + 1, :] for j in range(2)]
            o_ref[...] = jnp.transpose(jnp.where(row < 64, acc_scr[0] / l[0], acc_scr[1] / l[1]))
            for j in range(2):
                lse_ref[0, j:j + 1, :] = m_scr[j] + jnp.log2(l[j])

    qmap = lambda hp, t, qt, kt: (qt[t], hp)
    kmap = lambda hp, t, qt, kt: (kt[t], hp)
    hbm = pl.BlockSpec(memory_space=pl.ANY)
    hosted = gather is not None
    got_shape = [jax.ShapeDtypeStruct((N_DEV,) + gather.shape, gather.dtype)] if hosted else []
    return pl.pallas_call(
        body, name="flash_fwd_gather" if hosted else "flash_fwd",
        grid_spec=pltpu.PrefetchScalarGridSpec(
            num_scalar_prefetch=2, grid=(MLA_H // 2, qt.shape[0]),
            in_specs=[pl.BlockSpec((T, 256), qmap), pl.BlockSpec((T, 256), kmap),
                      pl.BlockSpec((128, T), lambda hp, t, qt, kt: (hp, kt[t]))] + [hbm] * hosted,
            out_specs=[pl.BlockSpec((T, 128), qmap),
                       pl.BlockSpec((1, 2, T), lambda hp, t, qt, kt: (hp, 0, qt[t]))] + [hbm] * hosted,
            scratch_shapes=[pltpu.VMEM((2, 1, T), F32), pltpu.VMEM((2, 128, T), F32)] + COMM_SEMS * hosted),
        out_shape=[jax.ShapeDtypeStruct((S, BW), F32), jax.ShapeDtypeStruct((MLA_H // 2, 2, S), F32)] + got_shape,
        compiler_params=_params(("arbitrary", "arbitrary"), 40),
    )(qt, kt, qh, kh, vt, *([gather] * hosted))


def _attn_delta(o, do):
    S = o.shape[0]
    T = min(2048, S)

    def body(o_ref, do_ref, d_ref):
        head = lax.broadcasted_iota(jnp.int32, (1, 128), 1) // 64
        prod = o_ref[...] * do_ref[...]
        for j in range(2):
            d_ref[0, j:j + 1, :] = jnp.sum(jnp.transpose(jnp.where(head == j, prod, 0.0)), axis=0, keepdims=True)

    blk = pl.BlockSpec((T, 128), lambda hp, i: (i, hp))
    return pl.pallas_call(
        body, name="attn_delta", grid=(MLA_H // 2, S // T),
        in_specs=[blk, blk],
        out_specs=pl.BlockSpec((1, 2, T), lambda hp, i: (hp, 0, i)),
        out_shape=jax.ShapeDtypeStruct((MLA_H // 2, 2, S), F32),
        compiler_params=_params(("parallel", "parallel"), 32),
    )(o, do)


def _band_specs(S):
    q = pl.BlockSpec((BAND_TQ, 128), lambda hp, qi: (qi, QC // 128 + hp))
    ks = [pl.BlockSpec((BAND_TQ, 128), lambda hp, qi, t=t: (jnp.maximum(qi - 2 + t, 0), KC // 128 + hp))
          for t in range(3)]
    vs = [pl.BlockSpec((BAND_TQ, 128), lambda hp, qi, t=t: (jnp.maximum(qi - 2 + t, 0), VC // 128 + hp))
          for t in range(3)]
    bias = pl.BlockSpec((2, BAND_TQ, BAND_W), lambda hp, qi: (hp, 0, 0))
    return q, ks, vs, bias


LOG2E = 1.4426950408889634
LN2 = 0.6931471805599453
CA_SCALE = CA_DH ** -0.5


def _band_probs(q2j, kcat, bias2_j, valid):
    s = _mm_nt(q2j, kcat) + bias2_j
    s = jnp.where(valid, s, NEG)
    p = jnp.exp2(s - jnp.max(s, axis=1, keepdims=True))
    return p * (1.0 / jnp.sum(p, axis=1, keepdims=True))


def _band_valid(qi):
    tile = lax.broadcasted_iota(jnp.int32, (1, BAND_W), 1) // BAND_TQ
    return tile + qi >= 2


def _band_fwd(proj, bias):
    S = proj.shape[0]

    def body(q_ref, k0, k1, k2, v0, v1, v2, b_ref, o_ref):
        qi = pl.program_id(1)
        head = lax.broadcasted_iota(jnp.int32, (1, 128), 1) // 64
        kcat = jnp.concatenate([k0[...], k1[...], k2[...]], axis=0).astype(MXU_DTYPE)
        vcat = jnp.concatenate([v0[...], v1[...], v2[...]], axis=0).astype(MXU_DTYPE)
        valid = _band_valid(qi)
        q2 = q_ref[...] * (CA_SCALE * LOG2E)
        o = jnp.zeros((BAND_TQ, 128), F32)
        for j in range(2):
            pn = _band_probs(jnp.where(head == j, q2, 0.0), kcat, b_ref[j], valid)
            o = o + _mm(pn, jnp.where(head == j, vcat, 0))
        o_ref[...] = o

    q, ks, vs, bspec = _band_specs(S)
    return pl.pallas_call(
        body, name="band_fwd", grid=(CA_H // 2, S // BAND_TQ),
        in_specs=[q, *ks, *vs, bspec],
        out_specs=pl.BlockSpec((BAND_TQ, 128), lambda hp, qi: (qi, hp)),
        out_shape=jax.ShapeDtypeStruct((S, BW), F32),
        compiler_params=_params(("parallel", "arbitrary"), 40),
    )(proj, proj, proj, proj, proj, proj, proj, bias)


def _merge_fwd(x, ya, ob, oc, proj, gate_b, wbr, wout, post_g):
    S = x.shape[0]
    tm = min(256, S)

    def body(x_ref, ya_ref, ob_ref, oc_ref, zb_ref, zc_ref, g0, g1, g2, gb_ref, wbr_ref, wo_ref, pg_ref,
             xo_ref, mg_ref, h_ref):
        zb, zc = zb_ref[...], zc_ref[...]
        ys = [ya_ref[...], ob_ref[...] * (zb * _sigmoid(zb)), oc_ref[...] * (zc * _sigmoid(zc))]
        merged = jnp.zeros((tm, D), F32)
        for i, g_ref in enumerate((g0, g1, g2)):
            merged = merged + _sigmoid(g_ref[...] + gb_ref[i:i + 1, :]) * _mm(ys[i], wbr_ref[i])
        mg_ref[...] = merged.astype(mg_ref.dtype)
        h = _mm(merged, wo_ref[...])
        h_ref[...] = h
        n, _ = _rms(h)
        xo_ref[...] = x_ref[...] + n * pg_ref[...]

    row = lambda w, cb=0: pl.BlockSpec((tm, w), lambda i, cb=cb: (i, cb))
    full = lambda a: pl.BlockSpec(a.shape, lambda i: (0,) * a.ndim)
    return pl.pallas_call(
        body, name="merge_fwd", grid=(S // tm,),
        in_specs=[row(D), row(BW), row(BW), row(BW), row(BW, ZB // BW), row(BW, ZC // BW),
                  row(D, 0), row(D, 1), row(D, 2), full(gate_b), full(wbr), full(wout), full(post_g)],
        out_specs=[row(D), row(D), row(D)],
        out_shape=[jax.ShapeDtypeStruct((S, D), F32), jax.ShapeDtypeStruct((S, D), MXU_DTYPE),
                   jax.ShapeDtypeStruct((S, D), F32)],
        compiler_params=_params(("parallel",), 56),
    )(x, ya, ob, oc, proj, proj, proj, proj, proj, gate_b, wbr, wout, post_g)


def _loss_fwd_bwd(y, target):
    S = y.shape[0]
    tm = min(512, S)

    def body(y_ref, t_ref, loss_ref, dy_ref):
        @pl.when(pl.program_id(0) == 0)
        def _():
            loss_ref[...] = jnp.zeros((1, 1), F32)
        err = y_ref[...] - t_ref[...]
        loss_ref[...] += 0.5 * jnp.sum(jnp.mean(err * err, axis=-1, keepdims=True), axis=0, keepdims=True)
        dy_ref[...] = err * (1.0 / D)

    row = pl.BlockSpec((tm, D), lambda i: (i, 0))
    return pl.pallas_call(
        body, name="loss", grid=(S // tm,),
        in_specs=[row, row],
        out_specs=[pl.BlockSpec((1, 1), lambda i: (0, 0)), row],
        out_shape=[jax.ShapeDtypeStruct((1, 1), F32), jax.ShapeDtypeStruct((S, D), F32)],
        compiler_params=_params(("arbitrary",), 32),
    )(y, target)


def _first_step_zero(refs, first):
    @pl.when(first)
    def _():
        for r in refs:
            r[...] = jnp.zeros(r.shape, r.dtype)


def _out_bwd(dxo, h, merged, post_g, wout_t):
    S = dxo.shape[0]
    tm = min(256, S)

    def body(d_ref, h_ref, mg_ref, pg_ref, wt_ref, dm_ref, dw_ref, dg_ref):
        _first_step_zero((dw_ref, dg_ref), pl.program_id(0) == 0)
        d = d_ref[...]
        hn, r = _rms(h_ref[...])
        dg_ref[...] += jnp.sum(d * hn, axis=0, keepdims=True)
        dh = _rms_bwd(d * pg_ref[...], hn, r)
        dm_ref[...] = _mm(dh, wt_ref[...])
        dw_ref[...] += _mm_tn(mg_ref[...], dh)

    row = pl.BlockSpec((tm, D), lambda i: (i, 0))
    full = lambda shape: pl.BlockSpec(shape, lambda i: (0,) * len(shape))
    return pl.pallas_call(
        body, name="out_bwd", grid=(S // tm,),
        in_specs=[row, row, row, full((1, D)), full((D, D))],
        out_specs=[row, full((D, D)), full((1, D))],
        out_shape=[jax.ShapeDtypeStruct((S, D), F32), jax.ShapeDtypeStruct((D, D), F32),
                   jax.ShapeDtypeStruct((1, D), F32)],
        compiler_params=_params(("arbitrary",), 40),
    )(dxo, h, merged, post_g, wout_t)


def _gate_bwd(dm, ya, ob, oc, proj, gate_b, wbr, wbr_t):
    S = dm.shape[0]
    tm = min(128, S)

    def body(dm_ref, ya_ref, ob_ref, oc_ref, zb_ref, zc_ref, g0, g1, g2, gb_ref, wbr_ref, wbt_ref,
             dg_ref, dzb_ref, dzc_ref, dya_ref, dob_ref, doc_ref, dwbr_ref, dgb_ref):
        _first_step_zero((dwbr_ref, dgb_ref), pl.program_id(0) == 0)
        dmv = dm_ref[...]
        zb, zc = zb_ref[...], zc_ref[...]
        sgb, sgc = _sigmoid(zb), _sigmoid(zc)
        ob, oc = ob_ref[...], oc_ref[...]
        ys = [ya_ref[...], (ob * (zb * sgb)).astype(MXU_DTYPE), (oc * (zc * sgc)).astype(MXU_DTYPE)]
        dys = []
        for i, g_ref in enumerate((g0, g1, g2)):
            br = _mm(ys[i], wbr_ref[i])
            gate = _sigmoid(g_ref[...] + gb_ref[i:i + 1, :])
            dgl = dmv * br * (gate * (1.0 - gate))
            dg_ref[:, D * i:D * (i + 1)] = dgl.astype(dg_ref.dtype)
            dgb_ref[i:i + 1, :] += jnp.sum(dgl, axis=0, keepdims=True)
            dbr = dmv * gate
            dys.append(_mm(dbr, wbt_ref[i]))
            dwbr_ref[i] += _mm_tn(dbr, ys[i])
        dya_ref[...] = dys[0]
        dob_ref[...] = dys[1] * (zb * sgb)
        dzb_ref[...] = (dys[1] * ob * (sgb * (1.0 + zb * (1.0 - sgb)))).astype(dzb_ref.dtype)
        doc_ref[...] = dys[2] * (zc * sgc)
        dzc_ref[...] = (dys[2] * oc * (sgc * (1.0 + zc * (1.0 - sgc)))).astype(dzc_ref.dtype)

    row = lambda w, cb=0: pl.BlockSpec((tm, w), lambda i, cb=cb: (i, cb))
    full = lambda a: pl.BlockSpec(a.shape, lambda i: (0,) * a.ndim)
    sds = lambda w, dt=F32: jax.ShapeDtypeStruct((S, w), dt)
    return pl.pallas_call(
        body, name="gate_bwd", grid=(S // tm,),
        in_specs=[row(D), row(BW), row(BW), row(BW), row(BW, ZB // BW), row(BW, ZC // BW),
                  row(D, 0), row(D, 1), row(D, 2), full(gate_b), full(wbr), full(wbr_t)],
        out_specs=[row(3 * D), row(BW), row(BW), row(BW), row(BW), row(BW),
                   pl.BlockSpec((N_BRANCH, D, BW), lambda i: (0, 0, 0)), pl.BlockSpec((N_BRANCH, D), lambda i: (0, 0))],
        out_shape=[sds(3 * D, MXU_DTYPE), sds(BW, MXU_DTYPE), sds(BW, MXU_DTYPE), sds(BW), sds(BW), sds(BW),
                   jax.ShapeDtypeStruct((N_BRANCH, D, BW), F32), jax.ShapeDtypeStruct((N_BRANCH, D), F32)],
        compiler_params=_params(("arbitrary",), 56),
    )(dm, ya, ob, oc, proj, proj, proj, proj, proj, gate_b, wbr, wbr_t)


def _band_bwd(proj, bias, do):
    S = proj.shape[0]

    def body(q_ref, k0, k1, k2, v0, v1, v2, b_ref, do_ref, dq_ref, dk_ref, dv_ref, db_ref):
        qi = pl.program_id(1)
        _first_step_zero((dk_ref, dv_ref, db_ref), qi == 0)
        head = lax.broadcasted_iota(jnp.int32, (1, 128), 1) // 64
        kcat = jnp.concatenate([k0[...], k1[...], k2[...]], axis=0).astype(MXU_DTYPE)
        vcat = jnp.concatenate([v0[...], v1[...], v2[...]], axis=0).astype(MXU_DTYPE)
        valid = _band_valid(qi)
        q2, dov = q_ref[...] * (CA_SCALE * LOG2E), do_ref[...]
        dq = jnp.zeros((BAND_TQ, 128), F32)
        dk = jnp.zeros((BAND_W, 128), F32)
        dv = jnp.zeros((BAND_W, 128), F32)
        for j in range(2):
            q2j = jnp.where(head == j, q2, 0.0).astype(MXU_DTYPE)
            doj = jnp.where(head == j, dov, 0.0).astype(MXU_DTYPE)
            pn = _band_probs(q2j, kcat, b_ref[j], valid)
            dv = dv + _mm_tn(pn, doj)
            dp = _mm_nt(doj, vcat)
            ds = pn * (dp - jnp.sum(pn * dp, axis=1, keepdims=True))
            db_ref[j] += ds
            dsb = ds.astype(MXU_DTYPE)
            dq = dq + _mm(dsb, jnp.where(head == j, kcat, 0))
            dk = dk + _mm_tn(dsb, q2j)
        dq_ref[...] = (dq * CA_SCALE).astype(dq_ref.dtype)
        for t in range(3):
            @pl.when(qi - 2 + t >= 0)
            def _(t=t):
                rows = pl.ds(pl.multiple_of((qi - 2 + t) * BAND_TQ, BAND_TQ), BAND_TQ)
                dk_ref[rows, :] += dk[t * BAND_TQ:(t + 1) * BAND_TQ] * LN2
                dv_ref[rows, :] += dv[t * BAND_TQ:(t + 1) * BAND_TQ]

    q, ks, vs, bspec = _band_specs(S)
    col = pl.BlockSpec((S, 128), lambda hp, qi: (0, hp))
    return pl.pallas_call(
        body, name="band_bwd", grid=(CA_H // 2, S // BAND_TQ),
        in_specs=[q, *ks, *vs, bspec, pl.BlockSpec((BAND_TQ, 128), lambda hp, qi: (qi, hp))],
        out_specs=[pl.BlockSpec((BAND_TQ, 128), lambda hp, qi: (qi, hp)), col, col, bspec],
        out_shape=[jax.ShapeDtypeStruct((S, BW), MXU_DTYPE), jax.ShapeDtypeStruct((S, BW), F32),
                   jax.ShapeDtypeStruct((S, BW), F32), jax.ShapeDtypeStruct((CA_H, BAND_TQ, BAND_W), F32)],
        compiler_params=_params(("parallel", "arbitrary"), 56),
    )(proj, proj, proj, proj, proj, proj, proj, bias, do)


BIAS_LO = REL_CLIP - (CHUNK - 1)
BIAS_FAR = 2 * REL_CLIP
BIAS_NEAR0 = BAND_W // 2


def _band_index(col0, ncol):
    i = lax.broadcasted_iota(jnp.int32, (BAND_TQ, ncol), 0)
    j = lax.broadcasted_iota(jnp.int32, (BAND_TQ, ncol), 1) + col0
    idx = jnp.clip(i + 2 * BAND_TQ - j, -REL_CLIP, REL_CLIP) + REL_CLIP
    ci, cj = i // CHUNK, j // CHUNK
    return jnp.where((ci <= cj) & (cj <= ci + LEFT_CHUNKS), idx, -1)


def _skew(x, right):
    row = lax.broadcasted_iota(jnp.int32, (BAND_TQ, 1), 0)
    for b in range(BAND_TQ.bit_length() - 1):
        shift = (1 << b) if right else BAND_W - (1 << b)
        x = jnp.where(((row >> b) & 1) == 1, pltpu.roll(x, shift, 1), x)
    return x


SKEW_NEAR0 = 2 * BAND_TQ - REL_CLIP + 1
SKEW_NEAR1 = 2 * BAND_TQ + CHUNK
SKEW_WRAP0 = BAND_W - (CHUNK - 1)


def _bias_build(rel_table):
    far = rel_table[:, BIAS_FAR:]
    base = jnp.concatenate([jnp.broadcast_to(far, (CA_H, SKEW_NEAR0)), rel_table[:, BIAS_LO:BIAS_FAR][:, ::-1],
                            jnp.broadcast_to(far, (CA_H, BAND_W - SKEW_NEAR1))], axis=1)[:, None, :]

    def body(base_ref, out_ref):
        valid = _band_index(0, BAND_W) >= 0
        for h in range(CA_H):
            tile = _skew(jnp.broadcast_to(base_ref[h] * LOG2E, (BAND_TQ, BAND_W)), right=True)
            out_ref[h] = jnp.where(valid, tile, NEG)

    return pl.pallas_call(
        body, name="bias_build",
        out_shape=jax.ShapeDtypeStruct((CA_H, BAND_TQ, BAND_W), F32),
        in_specs=[pl.BlockSpec(memory_space=pltpu.VMEM)],
        out_specs=pl.BlockSpec(memory_space=pltpu.VMEM),
        compiler_params=pltpu.CompilerParams(vmem_limit_bytes=40 << 20),
    )(base)


def _bias_fold(db):
    def body(db_ref, sums_ref, far_ref):
        col = lax.broadcasted_iota(jnp.int32, (1, BAND_W), 1)
        is_far = (col < SKEW_NEAR0) | (col >= SKEW_WRAP0)
        for h in range(CA_H):
            sums = jnp.sum(_skew(db_ref[h], right=False), axis=0, keepdims=True)
            sums_ref[h] = sums
            far_ref[h] = jnp.broadcast_to(jnp.sum(jnp.where(is_far, sums, 0.0), axis=1, keepdims=True), (1, 128))

    sums, far = pl.pallas_call(
        body, name="bias_fold",
        out_shape=[jax.ShapeDtypeStruct((CA_H, 1, BAND_W), F32), jax.ShapeDtypeStruct((CA_H, 1, 128), F32)],
        in_specs=[pl.BlockSpec(memory_space=pltpu.VMEM)],
        out_specs=[pl.BlockSpec(memory_space=pltpu.VMEM), pl.BlockSpec(memory_space=pltpu.VMEM)],
        compiler_params=pltpu.CompilerParams(vmem_limit_bytes=40 << 20),
    )(db)
    near = sums[:, 0, SKEW_NEAR0:SKEW_NEAR1][:, ::-1]
    return jnp.concatenate([jnp.zeros((CA_H, BIAS_LO), F32), near, far[:, 0, :1]], axis=1)


def _flash_bwd(qh, kh, ktr, vh, lse, delta, do, send=None):
    S = qh.shape[0]
    T = min(FLASH_T, S)
    TS = min(FLASH_SUB, T)
    n = S // T
    qt, kt = _tri_tables(n, q_major=False)

    def body(qt_ref, kt_ref, q_ref, k_ref, ktr_ref, v_ref, lse_ref, dl_ref, do_ref, *rest):
        if send is None:
            dqt_ref, dk_ref, dv_ref, dk_scr, dv_scr = rest
        else:
            send_ref, dqt_ref, dk_ref, dv_ref, recv_ref, dk_scr, dv_scr, *sems = rest
            _hosted_comm(send_ref, recv_ref, sems, same_source=False)
        t = pl.program_id(1)
        qi, ki = qt_ref[t], kt_ref[t]

        @pl.when(t == 0)
        def _():
            dqt_ref[...] = jnp.zeros(dqt_ref.shape, F32)

        @pl.when(qi == ki)
        def _():
            dk_scr[...] = jnp.zeros(dk_scr.shape, F32)
            dv_scr[...] = jnp.zeros(dv_scr.shape, F32)

        def step(diag):
            head = lax.broadcasted_iota(jnp.int32, (1, 128), 1) // 64
            for j in range(2):
                sl = slice(128 * j, 128 * (j + 1))
                for qs in range(T // TS):
                    cq = slice(qs * TS, (qs + 1) * TS)
                    qsub = q_ref[cq, sl]
                    doj = jnp.where(head == j, do_ref[cq, :], 0.0).astype(MXU_DTYPE)
                    lse, dlt = lse_ref[0, j:j + 1, cq], dl_ref[0, j:j + 1, cq]
                    cols = pl.ds(pl.multiple_of(qi * T + qs * TS, TS), TS)
                    for ks in range(T // TS):
                        if diag and ks > qs:
                            continue
                        rk = slice(ks * TS, (ks + 1) * TS)
                        s = _mm_nt(k_ref[rk, sl], qsub)
                        if diag and ks == qs:
                            s = jnp.where(_chunk_mask_t(TS), s, NEG)
                        p = jnp.exp2(s - lse)
                        dv_scr[rk, :] += _mm(p, doj)
                        dp = _mm_nt(v_ref[rk, :], doj)
                        ds = (p * (dp - dlt)).astype(MXU_DTYPE)
                        dk_scr[rk, sl] += _mm(ds, qsub)
                        dqt_ref[sl, cols] += _mm(ktr_ref[sl, rk], ds) * MLA_SCALE

        @pl.when(qi > ki)
        def _():
            step(False)

        @pl.when(qi == ki)
        def _():
            step(True)

        @pl.when(qi == n - 1)
        def _():
            dk_ref[...] = dk_scr[...] * 0.6931471805599453
            dv_ref[...] = dv_scr[...]

    qmap = lambda hp, t, qt, kt: (qt[t], hp)
    kmap = lambda hp, t, qt, kt: (kt[t], hp)
    stat = pl.BlockSpec((1, 2, T), lambda hp, t, qt, kt: (hp, 0, qt[t]))
    hbm = pl.BlockSpec(memory_space=pl.ANY)
    hosted = send is not None
    recv_shape = [jax.ShapeDtypeStruct(send.shape, send.dtype)] if hosted else []
    return pl.pallas_call(
        body, name="flash_bwd_exchange" if hosted else "flash_bwd",
        grid_spec=pltpu.PrefetchScalarGridSpec(
            num_scalar_prefetch=2, grid=(MLA_H // 2, qt.shape[0]),
            in_specs=[pl.BlockSpec((T, 256), qmap), pl.BlockSpec((T, 256), kmap),
                      pl.BlockSpec((256, T), lambda hp, t, qt, kt: (hp, kt[t])), pl.BlockSpec((T, 128), kmap),
                      stat, stat, pl.BlockSpec((T, 128), qmap)] + [hbm] * hosted,
            out_specs=[pl.BlockSpec((256, S), lambda hp, t, qt, kt: (hp, 0)), pl.BlockSpec((T, 256), kmap),
                       pl.BlockSpec((T, 128), kmap)] + [hbm] * hosted,
            scratch_shapes=[pltpu.VMEM((T, 256), F32), pltpu.VMEM((T, 128), F32)] + COMM_SEMS * hosted),
        out_shape=[jax.ShapeDtypeStruct((1024, S), F32), jax.ShapeDtypeStruct((S, 1024), F32),
                   jax.ShapeDtypeStruct((S, BW), F32)] + recv_shape,
        compiler_params=_params(("arbitrary", "arbitrary"), 56),
    )(qt, kt, qh, kh, ktr, vh, lse, delta, do, *([send] * hosted))


def _mla_prep_bwd(proj, dqf, dkf, dvf, gq, gkv, wuq_t, wuk_t, wuv_t, cos_t, sin_a, sin_b):
    S = proj.shape[0]
    tm = min(512, S)

    def body(p_ref, dq_ref, dk_ref, dv_ref, gq_ref, gkv_ref, wq_ref, wk_ref, wv_ref, c_ref, sa_ref, sb_ref,
             db_ref, dwq_ref, dwk_ref, dwv_ref, dgq_ref, dgkv_ref):
        _first_step_zero((dwq_ref, dwk_ref, dwv_ref, dgq_ref, dgkv_ref), pl.program_id(0) == 0)
        c, sa, sb = c_ref[...], sa_ref[...], sb_ref[...]
        nq, rq = _rms(p_ref[:, 0:Q_RANK])
        nkv, rkv = _rms(p_ref[:, Q_RANK:Q_RANK + KV_RANK])
        cq = (nq * gq_ref[...]).astype(MXU_DTYPE)
        ckv = (nkv * gkv_ref[...]).astype(MXU_DTYPE)
        dkr = jnp.zeros((tm, 128), F32)
        dq_pre = []
        for h in range(MLA_H):
            sl = slice(128 * h, 128 * (h + 1))
            dq_pre.append(_rope_t(jnp.transpose(dq_ref[sl, :]), c, sa, sb).astype(MXU_DTYPE))
            dkr = dkr + dk_ref[:, sl]
        dq_pre = jnp.concatenate(dq_pre, axis=1)
        dcq = _mm(dq_pre, wq_ref[...])
        dwq_ref[...] += _mm_tn(dq_pre, cq)
        dgq_ref[...] += jnp.sum(dcq * nq, axis=0, keepdims=True)
        db_ref[:, 0:Q_RANK] = _rms_bwd(dcq * gq_ref[...], nq, rq).astype(db_ref.dtype)
        dk = dk_ref[...].astype(MXU_DTYPE)
        dv = dv_ref[...].astype(MXU_DTYPE)
        dckv = _mm(dk, wk_ref[...]) + _mm(dv, wv_ref[...])
        dwk_ref[...] += _mm_tn(dk, ckv)
        dwv_ref[...] += _mm_tn(dv, ckv)
        dgkv_ref[...] += jnp.sum(dckv * nkv, axis=0, keepdims=True)
        db_ref[:, Q_RANK:Q_RANK + KV_RANK] = _rms_bwd(dckv * gkv_ref[...], nkv, rkv).astype(db_ref.dtype)
        lane = lax.broadcasted_iota(jnp.int32, (1, 128), 1)
        rope_lanes = (lane >= MLA_NOPE) & (lane < MLA_QK)
        db_ref[:, Q_RANK + KV_RANK:BW] = _rope_t(jnp.where(rope_lanes, dkr, 0.0), c, sa, sb).astype(db_ref.dtype)

    full = lambda a: pl.BlockSpec(a.shape, lambda i: (0,) * a.ndim)
    fulls = lambda shape: pl.BlockSpec(shape, lambda i: (0,) * len(shape))
    tab = pl.BlockSpec((tm, 128), lambda i: (i, 0))
    row = lambda w, cb=0: pl.BlockSpec((tm, w), lambda i, cb=cb: (i, cb))
    return pl.pallas_call(
        body, name="mla_prep_bwd", grid=(S // tm,),
        in_specs=[row(BW, QD // BW), pl.BlockSpec((1024, tm), lambda i: (0, i)), row(1024), row(BW),
                  full(gq), full(gkv), full(wuq_t),
                  full(wuk_t), full(wuv_t), tab, tab, tab],
        out_specs=[row(BW), fulls((1024, Q_RANK)), fulls((1024, KV_RANK)), fulls((BW, KV_RANK)),
                   fulls((1, Q_RANK)), fulls((1, KV_RANK))],
        out_shape=[jax.ShapeDtypeStruct((S, BW), MXU_DTYPE), jax.ShapeDtypeStruct((1024, Q_RANK), F32),
                   jax.ShapeDtypeStruct((1024, KV_RANK), F32), jax.ShapeDtypeStruct((BW, KV_RANK), F32),
                   jax.ShapeDtypeStruct((1, Q_RANK), F32), jax.ShapeDtypeStruct((1, KV_RANK), F32)],
        compiler_params=_params(("arbitrary",), 56),
    )(proj, dqf, dkf, dvf, gq, gkv, wuq_t, wuk_t, wuv_t, cos_t, sin_a, sin_b)


def _sgu_bwd(proj, dya, ln_g, ln_b, ws, bfull):
    S = proj.shape[0]
    tm = min(512, S)

    def body(u_ref, v_ref, z_ref, dy_ref, g_ref, b_ref, ws_ref, bf_ref,
             da_ref, dws_ref, dbf_ref, dlg_ref, dlb_ref, dbs_ref):
        i = pl.program_id(0)
        _first_step_zero((dws_ref, dbf_ref, dlg_ref, dlb_ref, dbs_ref), i == 0)
        tri = _sgu_tri()
        ws_m = [jnp.where(tri, ws_ref[g], 0.0).astype(MXU_DTYPE) for g in range(SGU_G)]
        grp = lax.broadcasted_iota(jnp.int32, (1, BW), 1) // (BW // SGU_G)
        for b in range(tm // SGU_T):
            r = slice(b * SGU_T, (b + 1) * SGU_T)
            vln, vhat, rstd = _layernorm(v_ref[r, :], g_ref[...], b_ref[...])
            vb = vln.astype(MXU_DTYPE)
            mixed = _sgu_mix(ws_m, vb, bf_ref[...], grp)
            u, z, dy = u_ref[r, :], z_ref[r, :], dy_ref[r, :]
            sg = _sigmoid(z)
            sz = z * sg
            da_ref[r, 0:BW] = (dy * mixed * sz).astype(da_ref.dtype)
            da_ref[r, 2 * BW:3 * BW] = (dy * u * mixed * (sg * (1.0 + z * (1.0 - sg)))).astype(da_ref.dtype)
            dmix = dy * u * sz
            dbf_ref[...] += dmix
            dvln = jnp.zeros((SGU_T, BW), F32)
            for g in range(SGU_G):
                dmg = jnp.where(grp == g, dmix, 0.0).astype(MXU_DTYPE)
                dvln = dvln + _mm_tn(ws_m[g], dmg)
                dws_ref[g] += jnp.where(tri, _mm_nt(dmg, vb), 0.0)
            dlg_ref[...] += jnp.sum(dvln * vhat, axis=0, keepdims=True)
            dlb_ref[...] += jnp.sum(dvln, axis=0, keepdims=True)
            dvh = dvln * g_ref[...]
            da_ref[r, BW:2 * BW] = (rstd * (dvh - jnp.mean(dvh, axis=-1, keepdims=True)
                                            - vhat * jnp.mean(dvh * vhat, axis=-1, keepdims=True))).astype(da_ref.dtype)

        @pl.when(i == pl.num_programs(0) - 1)
        def _():
            dbf = dbf_ref[...]
            for g in range(SGU_G):
                dbs_ref[:, g:g + 1] = jnp.sum(jnp.where(grp == g, dbf, 0.0), axis=1, keepdims=True)

    blk = lambda cb: pl.BlockSpec((tm, BW), lambda i, cb=cb: (i, cb))
    fulls = lambda shape: pl.BlockSpec(shape, lambda i: (0,) * len(shape))
    return pl.pallas_call(
        body, name="sgu_bwd", grid=(S // tm,),
        in_specs=[blk(UA // BW), blk(VA // BW), blk(ZA // BW), blk(0), fulls((1, BW)), fulls((1, BW)),
                  fulls((SGU_G, SGU_T, SGU_T)), fulls((SGU_T, BW))],
        out_specs=[pl.BlockSpec((tm, 3 * BW), lambda i: (i, 0)), fulls((SGU_G, SGU_T, SGU_T)),
                   fulls((SGU_T, BW)), fulls((1, BW)), fulls((1, BW)), fulls((SGU_T, SGU_G))],
        out_shape=[jax.ShapeDtypeStruct((S, 3 * BW), MXU_DTYPE), jax.ShapeDtypeStruct((SGU_G, SGU_T, SGU_T), F32),
                   jax.ShapeDtypeStruct((SGU_T, BW), F32), jax.ShapeDtypeStruct((1, BW), F32),
                   jax.ShapeDtypeStruct((1, BW), F32), jax.ShapeDtypeStruct((SGU_T, SGU_G), F32)],
        compiler_params=_params(("arbitrary",), 40),
    )(proj, proj, proj, dya, ln_g, ln_b, ws, bfull)


DX_TK = 3 * BW


def _inproj_bwd_dx(dgl, da, small, w_t, x, pre_g, dxo, send=None):
    S = x.shape[0]
    tm = min(512, S)
    nk = DP // DX_TK

    def body(g_ref, a_ref, s0, s1, s2, s3, s4, s5, w_ref, x_ref, pg_ref, dxo_ref, *rest):
        if send is None:
            dx_ref, dg_ref, acc = rest
        else:
            send_ref, dx_ref, dg_ref, recv_ref, acc, *sems = rest
            _hosted_comm(send_ref, recv_ref, sems, same_source=False)
        i, k = pl.program_id(0), pl.program_id(1)
        _first_step_zero((dg_ref,), (i == 0) & (k == 0))

        @pl.when(k == 0)
        def _():
            acc[...] = jnp.zeros(acc.shape, F32)

        @pl.when(k < 2)
        def _():
            acc[...] += _mm(g_ref[...], w_ref[...])

        @pl.when(k == 2)
        def _():
            acc[...] += _mm(a_ref[...], w_ref[...])

        for kk, trio in ((3, (s0, s1, s2)), (4, (s3, s4, s5))):
            @pl.when(k == kk)
            def _(trio=trio):
                t = acc[...]
                for n, ref in enumerate(trio):
                    t = t + _mm(ref[...], w_ref[n * BW:(n + 1) * BW, :])
                acc[...] = t

        @pl.when(k == nk - 1)
        def _():
            n, r = _rms(x_ref[...])
            dxn = acc[...]
            dg_ref[...] += jnp.sum(dxn * n, axis=0, keepdims=True)
            dx_ref[...] = dxo_ref[...] + _rms_bwd(dxn * pg_ref[...], n, r)

    row = pl.BlockSpec((tm, D), lambda i, k: (i, 0))
    vec = pl.BlockSpec((1, D), lambda i, k: (0, 0))
    fixed = lambda w: pl.BlockSpec((tm, w), lambda i, k: (i, 0))
    hbm = pl.BlockSpec(memory_space=pl.ANY)
    hosted = send is not None
    recv_shape = [jax.ShapeDtypeStruct(send.shape, send.dtype)] if hosted else []
    return pl.pallas_call(
        body, name="inproj_bwd_dx_exchange" if hosted else "inproj_bwd_dx", grid=(S // tm, nk),
        in_specs=[pl.BlockSpec((tm, DX_TK), lambda i, k: (i, jnp.minimum(k, 1))), fixed(DX_TK)]
                 + [fixed(BW)] * 6 + [pl.BlockSpec((DX_TK, D), lambda i, k: (k, 0)), row, vec, row] + [hbm] * hosted,
        out_specs=[row, vec] + [hbm] * hosted,
        out_shape=[jax.ShapeDtypeStruct((S, D), F32), jax.ShapeDtypeStruct((1, D), F32)] + recv_shape,
        scratch_shapes=[pltpu.VMEM((tm, D), F32)] + COMM_SEMS * hosted,
        compiler_params=_params(("arbitrary", "arbitrary"), 56),
    )(dgl, da, *small, w_t, x, pre_g, dxo, *([send] * hosted))


def _matmul_tn(segs, b):
    S, K = b.shape
    tk = min(1024, S)
    widths = [a.shape[1] for a in segs]
    offs = [sum(widths[:n]) for n in range(len(segs))]

    def body(*refs):
        seg_refs, b_ref, o_ref = refs[:len(segs)], refs[len(segs)], refs[len(segs) + 1]
        _first_step_zero((o_ref,), pl.program_id(0) == 0)
        bv = b_ref[...]
        for ref, off, w in zip(seg_refs, offs, widths):
            o_ref[off:off + w, :] += _mm_tn(ref[...], bv)

    return pl.pallas_call(
        body, name="matmul_tn", grid=(S // tk,),
        in_specs=[pl.BlockSpec((tk, w), lambda k: (k, 0)) for w in widths] + [pl.BlockSpec((tk, K), lambda k: (k, 0))],
        out_specs=pl.BlockSpec((sum(widths), K), lambda k: (0, 0)),
        out_shape=jax.ShapeDtypeStruct((sum(widths), K), F32),
        compiler_params=_params(("arbitrary",), 56),
    )(*segs, b)


def _pad_rows_w_in(wt):
    z = lambda n: jnp.zeros((n, wt.shape[1]), wt.dtype)
    return jnp.concatenate([wt[4512:], wt[:1920], z(64), wt[1920:1952], z(32), wt[1952:4512]], axis=0)


def _unpad_rows_w_in(dwt):
    a = dwt[3 * D:]
    return jnp.concatenate([a[:1920], a[1984:2016], a[2048:], dwt[:3 * D]], axis=0)


def _pad_head_rows(wt, width):
    k = wt.shape[1]
    return jnp.pad(wt.reshape(MLA_H, width, k), ((0, 0), (0, 128 - width), (0, 0))).reshape(MLA_H * 128, k)


def _unpad_head_rows(wt, width):
    k = wt.shape[1]
    return wt.reshape(MLA_H, 128, k)[:, :width].reshape(MLA_H * width, k)


def _rope_tables(S):
    half = MLA_ROPE // 2
    inv = 10000.0 ** (-jnp.arange(half, dtype=F32) / half)
    ang = jnp.arange(S, dtype=F32)[:, None] * inv[None, :]
    cos, sin = jnp.cos(ang), jnp.sin(ang)
    one = lambda n: jnp.ones((S, n), F32)
    zero = lambda n: jnp.zeros((S, n), F32)
    cos_t = jnp.concatenate([one(MLA_NOPE), cos, cos, one(128 - MLA_QK)], axis=1)
    sin_a = jnp.concatenate([zero(MLA_NOPE), -sin, zero(128 - MLA_NOPE - half)], axis=1)
    sin_b = jnp.concatenate([zero(MLA_NOPE + half), sin, zero(128 - MLA_QK)], axis=1)
    return cos_t, sin_a, sin_b


SHARDED = ("w_in", "mla_w_uq", "mla_w_ukv", "w_branch", "w_out")
SHARD_PACK = SHARDED + ("gate_b",)
REPLICATED = ("pre_g", "post_g", "sgu_ln_g", "sgu_ln_b", "sgu_w", "sgu_b", "mla_q_norm_g", "mla_kv_norm_g",
              "ca_rel_bias")
OUT_ORDER = ("w_in", "pre_g", "post_g", "sgu_ln_g", "sgu_ln_b", "sgu_w", "sgu_b", "mla_q_norm_g", "mla_kv_norm_g",
             "mla_w_uq", "mla_w_ukv", "ca_rel_bias", "w_branch", "gate_b", "w_out")
TRANSPOSED = {"w_in": (0, 2, 1), "mla_w_uq": (0, 2, 1), "mla_w_ukv": (0, 2, 1), "w_branch": (0, 1, 3, 2)}


def _canon(name, a):
    return jnp.transpose(a, TRANSPOSED[name]) if name in TRANSPOSED else a


ROW_MULT = 64
PIECE_MULT = 16


def _piece_rows(shape):
    size = 1
    for s in shape:
        size *= s
    return -(-size // (1024 * PIECE_MULT)) * PIECE_MULT


def _pack_rows(pieces, lead, mult=ROW_MULT):
    rows = []
    for a in pieces:
        flat = a.reshape(a.shape[:lead] + (-1,))
        nr = _piece_rows(a.shape[lead:])
        flat = jnp.pad(flat, [(0, 0)] * lead + [(0, nr * 1024 - flat.shape[-1])])
        rows.append(flat.reshape(a.shape[:lead] + (nr, 1024)))
    total = sum(r.shape[lead] for r in rows)
    tail = (-total) % mult
    if tail:
        rows.append(jnp.zeros(rows[0].shape[:lead] + (tail, 1024), rows[0].dtype))
    return jnp.concatenate(rows, axis=lead)


def _unpack_rows(p, shapes, lead):
    out, r = [], 0
    for s in shapes:
        nr, size = _piece_rows(s), 1
        for d in s:
            size *= d
        piece = lax.slice_in_dim(p, r, r + nr, axis=lead).reshape(p.shape[:lead] + (-1,))
        out.append(lax.slice_in_dim(piece, 0, size, axis=lead).reshape(p.shape[:lead] + tuple(s)))
        r += nr
    return out


def kernel(x, w_in, pre_g, post_g, sgu_ln_g, sgu_ln_b, sgu_w, sgu_b, mla_q_norm_g, mla_kv_norm_g, mla_w_uq, mla_w_ukv, ca_rel_bias, w_branch, gate_b, w_out, loss_target, m_w_in, m_pre_g, m_post_g, m_sgu_ln_g, m_sgu_ln_b, m_sgu_w, m_sgu_b, m_mla_q_norm_g, m_mla_kv_norm_g, m_mla_w_uq, m_mla_w_ukv, m_ca_rel_bias, m_w_branch, m_gate_b, m_w_out, v_w_in, v_pre_g, v_post_g, v_sgu_ln_g, v_sgu_ln_b, v_sgu_w, v_sgu_b, v_mla_q_norm_g, v_mla_kv_norm_g, v_mla_w_uq, v_mla_w_ukv, v_ca_rel_bias, v_w_branch, v_gate_b, v_w_out):
    weights = dict(w_in=w_in, pre_g=pre_g, post_g=post_g, sgu_ln_g=sgu_ln_g, sgu_ln_b=sgu_ln_b, sgu_w=sgu_w,
                   sgu_b=sgu_b, mla_q_norm_g=mla_q_norm_g, mla_kv_norm_g=mla_kv_norm_g, mla_w_uq=mla_w_uq,
                   mla_w_ukv=mla_w_ukv, ca_rel_bias=ca_rel_bias, w_branch=w_branch, gate_b=gate_b, w_out=w_out)
    mom_m = dict(w_in=m_w_in, pre_g=m_pre_g, post_g=m_post_g, sgu_ln_g=m_sgu_ln_g, sgu_ln_b=m_sgu_ln_b,
                 sgu_w=m_sgu_w, sgu_b=m_sgu_b, mla_q_norm_g=m_mla_q_norm_g, mla_kv_norm_g=m_mla_kv_norm_g,
                 mla_w_uq=m_mla_w_uq, mla_w_ukv=m_mla_w_ukv, ca_rel_bias=m_ca_rel_bias, w_branch=m_w_branch,
                 gate_b=m_gate_b, w_out=m_w_out)
    mom_v = dict(w_in=v_w_in, pre_g=v_pre_g, post_g=v_post_g, sgu_ln_g=v_sgu_ln_g, sgu_ln_b=v_sgu_ln_b,
                 sgu_w=v_sgu_w, sgu_b=v_sgu_b, mla_q_norm_g=v_mla_q_norm_g, mla_kv_norm_g=v_mla_kv_norm_g,
                 mla_w_uq=v_mla_w_uq, mla_w_ukv=v_mla_w_ukv, ca_rel_bias=v_ca_rel_bias, w_branch=v_w_branch,
                 gate_b=v_gate_b, w_out=v_w_out)
    depth = w_in.shape[0]
    S = x.shape[1]
    xs = x.reshape(S, D)

    cw = {n: _canon(n, weights[n]) for n in SHARD_PACK}

    def layer_block(l):
        return _pack_rows([cw[n][l].astype(MXU_DTYPE) for n in SHARDED], 0, PIECE_MULT)

    def split_block(got):
        return dict(zip(SHARDED, _unpack_rows(got, [cw[n].shape[1:] for n in SHARDED], 1)))

    gate_all, = _unpack_rows(_all_gather(_pack_rows([gate_b], 0, PIECE_MULT)), [gate_b.shape], 1)
    cos_t, sin_a, sin_b = _rope_tables(S)

    def layer_weights(gl, l):
        w_t = _pad_rows_w_in(gl["w_in"].reshape(D_IN, D))
        wuq_t = _pad_head_rows(gl["mla_w_uq"].reshape(MLA_H * MLA_QK, Q_RANK), MLA_QK)
        ukv_t = gl["mla_w_ukv"]
        wuk_t = jnp.pad(ukv_t[:, :MLA_NOPE], ((0, 0), (0, 128 - MLA_NOPE), (0, 0))).reshape(MLA_H * 128, KV_RANK)
        wuv_t = ukv_t[:, MLA_NOPE:].reshape(BW, KV_RANK)
        wbr_t = jnp.swapaxes(gl["w_branch"], 0, 1).reshape(N_BRANCH, D, BW)
        return dict(
            w_t=w_t, w_pad=w_t.T, wuq_t=wuq_t, wuq=wuq_t.T, wuk_t=wuk_t, wuk=wuk_t.T, wuv_t=wuv_t, wuv=wuv_t.T,
            wbr_t=wbr_t, wbr=jnp.swapaxes(wbr_t, 1, 2), wout=gl["w_out"].reshape(D, D),
            gate_b=jnp.swapaxes(gate_all[:, l], 0, 1).reshape(N_BRANCH, D),
            pre_g=pre_g[l][None], post_g=post_g[l][None], ln_g=sgu_ln_g[l][None], ln_b=sgu_ln_b[l][None],
            ws=sgu_w[l], bfull=jnp.repeat(sgu_b[l].T, BW // SGU_G, axis=1),
            gq=mla_q_norm_g[l][None], gkv=mla_kv_norm_g[l][None], bias=_bias_build(ca_rel_bias[l]))

    layers, saved = [], []
    h_x = xs
    got = _all_gather(layer_block(0))
    for l in range(depth):
        lw = layer_weights(split_block(got), l)
        layers.append(lw)
        proj, xn = _inproj_fwd(h_x, lw["pre_g"], lw["w_pad"])
        ya = _sgu_fwd(proj, lw["ln_g"], lw["ln_b"], lw["ws"], lw["bfull"])
        qh, kh, vh, ktr, vtr = _mla_prep_fwd(proj, lw["gq"], lw["gkv"], lw["wuq"], lw["wuk"], lw["wuv"],
                                             cos_t, sin_a, sin_b)
        if l + 1 < depth:
            ob, lse, got = _flash_fwd(qh, kh, vtr, gather=layer_block(l + 1))
        else:
            ob, lse = _flash_fwd(qh, kh, vtr)
        oc = _band_fwd(proj, lw["bias"])
        x_new, merged, hh = _merge_fwd(h_x, ya, ob, oc, proj, lw["gate_b"], lw["wbr"], lw["wout"], lw["post_g"])
        saved.append(dict(x=h_x, proj=proj, xn=xn, ya=ya, qh=qh, kh=kh, vh=vh, ktr=ktr, ob=ob, lse=lse, oc=oc,
                          merged=merged, h=hh))
        h_x = x_new

    loss_part, dx = _loss_fwd_bwd(h_x, loss_target.reshape(S, D))
    loss = lax.psum(loss_part[0, 0], MESH_AXES)

    rows = {}
    rep = {n: [None] * depth for n in REPLICATED}
    recvs = [None] * depth
    pending = None
    for l in reversed(range(depth)):
        lw, sv = layers[l], saved[l]
        proj = sv["proj"]
        dmerged, dw_out, dg_post = _out_bwd(dx, sv["h"], sv["merged"], lw["post_g"], lw["wout"].T)
        dgl, dzb, dzc, dya, dob, doc, dwbr_t, dgb = _gate_bwd(
            dmerged, sv["ya"], sv["ob"], sv["oc"], proj, lw["gate_b"], lw["wbr"], lw["wbr_t"])
        dqc, dkc, dvc, dbias = _band_bwd(proj, lw["bias"], doc)
        drel = _bias_fold(dbias)
        flash_args = (sv["qh"], sv["kh"], sv["ktr"], sv["vh"], sv["lse"], _attn_delta(sv["ob"], dob), dob)
        if pending is None:
            dqf, dkf, dvf = _flash_bwd(*flash_args)
        else:
            dqf, dkf, dvf, recvs[l + 1] = _flash_bwd(*flash_args, send=pending)
        db, dwuq_t, dwuk_t, dwuv_t, dgq, dgkv = _mla_prep_bwd(
            proj, dqf, dkf, dvf, lw["gq"], lw["gkv"], lw["wuq_t"], lw["wuk_t"], lw["wuv_t"], cos_t, sin_a, sin_b)
        da, dws, _, dlg, dlb, dbs = _sgu_bwd(proj, dya, lw["ln_g"], lw["ln_b"], lw["ws"], lw["bfull"])
        small = (db, dzb, dqc, dkc, dvc, dzc)
        dw_t = jnp.concatenate([_matmul_tn([dgl], sv["xn"]), _matmul_tn([da, db, dzb], sv["xn"]),
                                _matmul_tn([dqc, dkc, dvc, dzc], sv["xn"])], axis=0)

        rows["w_in"] = _unpad_rows_w_in(dw_t).reshape(N_DEV, D_IN // N_DEV, D)
        rows["mla_w_uq"] = _unpad_head_rows(dwuq_t, MLA_QK).reshape(N_DEV, MLA_QK, Q_RANK)
        dk3 = dwuk_t.reshape(MLA_H, 128, KV_RANK)[:, :MLA_NOPE]
        dv3 = dwuv_t.reshape(MLA_H, 64, KV_RANK)
        rows["mla_w_ukv"] = jnp.concatenate([dk3, dv3], axis=1)
        rows["w_branch"] = jnp.swapaxes(dwbr_t.reshape(N_BRANCH, N_DEV, D // N_DEV, BW), 0, 1)
        rows["w_out"] = dw_out.reshape(N_DEV, D // N_DEV, D)
        rows["gate_b"] = jnp.swapaxes(dgb.reshape(N_BRANCH, N_DEV, D // N_DEV), 0, 1)
        pending = _pack_rows([rows[n].astype(MXU_DTYPE) for n in SHARD_PACK], 1)
        dx_args = (dgl, da, small, lw["w_t"], sv["x"], lw["pre_g"], dx)
        if l > 0:
            dx, dg_pre = _inproj_bwd_dx(*dx_args)
        else:
            dx, dg_pre, recvs[0] = _inproj_bwd_dx(*dx_args, send=pending)
        rep["pre_g"][l] = dg_pre[0]
        rep["post_g"][l] = dg_post[0]
        rep["sgu_ln_g"][l] = dlg[0]
        rep["sgu_ln_b"][l] = dlb[0]
        rep["sgu_w"][l] = dws
        rep["sgu_b"][l] = dbs.T
        rep["mla_q_norm_g"][l] = dgq[0]
        rep["mla_kv_norm_g"][l] = dgkv[0]
        rep["ca_rel_bias"][l] = drel
    grad_x = dx.reshape(x.shape)

    cm = {n: _canon(n, mom_m[n]) for n in SHARD_PACK}
    cv = {n: _canon(n, mom_v[n]) for n in SHARD_PACK}
    shard_shapes = [cw[n].shape[1:] for n in SHARD_PACK]
    per_layer = []
    for l in range(depth):
        at = lambda d: _pack_rows([d[n][l] for n in SHARD_PACK], 0)
        outs_l = _reduce_adamw(recvs[l], at(cw), at(cm), at(cv))
        per_layer.append([_unpack_rows(p, shard_shapes, 0) for p in outs_l])
    outs_sh = [{n: _canon(n, jnp.stack([per_layer[l][k][i] for l in range(depth)]))
                for i, n in enumerate(SHARD_PACK)} for k in range(4)]

    rp = lambda d: _pack_rows([d[n] for n in REPLICATED], 0)
    allp = _all_gather(rp({n: jnp.stack(rep[n]) for n in REPLICATED}))
    outs_rep = _reduce_adamw(allp, rp(weights), rp(mom_m), rp(mom_v))
    rep_shapes = [weights[n].shape for n in REPLICATED]
    outs_rep = [dict(zip(REPLICATED, _unpack_rows(p, rep_shapes, 0))) for p in outs_rep]

    outs = [{**a, **b} for a, b in zip(outs_sh, outs_rep)]
    return (loss, grad_x, *[o[n] for o in outs for n in OUT_ORDER])
```

```python
import functools

import jax
import jax.numpy as jnp
from jax import lax
from jax.experimental import pallas as pl
from jax.experimental.pallas import tpu as pltpu

F32 = jnp.float32
MXU_DTYPE = jnp.bfloat16
EPS = 1e-6
NEG = -1e30
MESH_AXES = ("x", "y", "c")
N_DEV = 8

D = 1024
BW = 512
N_BRANCH = 3
CHUNK = 64
SGU_T = 128
SGU_G = 8
MLA_H = 8
MLA_NOPE = 64
MLA_ROPE = 32
MLA_QK = MLA_NOPE + MLA_ROPE
Q_RANK = 256
KV_RANK = 128
CA_H = 8
CA_DH = 64
LEFT_CHUNKS = 8
REL_CLIP = 128
D_IN = 7584

G_OFF, UA, VA, ZA, QD, ZB, QC, KC, VC, ZC, DP = 0, 3072, 3584, 4096, 4608, 5120, 5632, 6144, 6656, 7168, 7680
BAND_TQ = 256
BAND_W = 3 * BAND_TQ

ADAM_LR, ADAM_B1, ADAM_B2, ADAM_EPS, ADAM_WD, ADAM_STEP = 0.001, 0.9, 0.999, 1e-08, 0.01, 10


def _params(sem, mib):
    return pltpu.CompilerParams(dimension_semantics=sem, vmem_limit_bytes=mib << 20)


def _mm(a, b):
    return jnp.dot(a.astype(MXU_DTYPE), b.astype(MXU_DTYPE), preferred_element_type=F32)


def _mm_nt(a, b):
    return lax.dot_general(a.astype(MXU_DTYPE), b.astype(MXU_DTYPE), (((1,), (1,)), ((), ())),
                           preferred_element_type=F32)


def _mm_tn(a, b):
    return lax.dot_general(a.astype(MXU_DTYPE), b.astype(MXU_DTYPE), (((0,), (0,)), ((), ())),
                           preferred_element_type=F32)


def _sigmoid(z):
    return 1.0 / (1.0 + jnp.exp(-z))


def _rms(x):
    r = lax.rsqrt(jnp.mean(x * x, axis=-1, keepdims=True) + EPS)
    return x * r, r


def _rms_bwd(dn, n, r):
    return r * (dn - n * jnp.mean(dn * n, axis=-1, keepdims=True))


def _rope(b, c, sa, sb):
    return b * c + pltpu.roll(b, 112, 1) * sa + pltpu.roll(b, 16, 1) * sb


def _rope_t(d, c, sa, sb):
    return d * c + pltpu.roll(d * sa, 16, 1) + pltpu.roll(d * sb, 112, 1)


def _all_gather(blk):
    R = blk.shape[0]

    def body(x_ref, out_ref, send_sems, recv_sems, local_sem):
        x, y, c = lax.axis_index("x"), lax.axis_index("y"), lax.axis_index("c")
        me, sibling = (x, y, c), (x, y, 1 - c)
        chips = [(1 - x, y), (x, 1 - y), (1 - x, 1 - y)]

        def slot(px, py, pc):
            return out_ref.at[4 * px + 2 * py + pc]

        def copy(k, block, to, src=None):
            return pltpu.make_async_remote_copy(
                src_ref=slot(*block) if src is None else src, dst_ref=slot(*block),
                send_sem=send_sems.at[k], recv_sem=recv_sems.at[k],
                device_id=to, device_id_type=pl.DeviceIdType.MESH)

        mine = pltpu.make_async_copy(x_ref, slot(*me), local_sem)
        mine.start()
        first = [copy(0, me, sibling, src=x_ref)]
        first += [copy(1 + j, me, (*chip, c), src=x_ref) for j, chip in enumerate(chips)]
        for cp in first:
            cp.start()
        passed = [copy(4 + j, (*chip, c), sibling) for j, chip in enumerate(chips)]
        for j, chip in enumerate(chips):
            copy(1 + j, (*chip, c), me).wait_recv()
            passed[j].start()
        copy(0, sibling, me).wait_recv()
        for j, chip in enumerate(chips):
            copy(4 + j, (*chip, 1 - c), me).wait_recv()
        for cp in first + passed:
            cp.wait_send()
        mine.wait()

    return pl.pallas_call(
        body, name="all_gather",
        out_shape=jax.ShapeDtypeStruct((N_DEV,) + blk.shape, blk.dtype),
        in_specs=[pl.BlockSpec(memory_space=pl.ANY)],
        out_specs=pl.BlockSpec(memory_space=pl.ANY),
        scratch_shapes=[pltpu.SemaphoreType.DMA((7,)), pltpu.SemaphoreType.DMA((7,)), pltpu.SemaphoreType.DMA(())],
    )(blk)


def _peer_copies(s_ref, r_ref, send_sems, recv_sems, local_sem, same_source):
    x, y, c = lax.axis_index("x"), lax.axis_index("y"), lax.axis_index("c")
    me = 4 * x + 2 * y + c

    def src(pid):
        return s_ref if same_source else s_ref.at[pid]

    def peer(k):
        px = 1 - x if (k >> 2) & 1 else x
        py = 1 - y if (k >> 1) & 1 else y
        pc = 1 - c if k & 1 else c
        return (px, py, pc), 4 * px + 2 * py + pc

    def remote(k, row):
        pos, pid = peer(k)
        return pltpu.make_async_remote_copy(
            src_ref=src(pid), dst_ref=r_ref.at[me if row is None else pid],
            send_sem=send_sems.at[k], recv_sem=recv_sems.at[k],
            device_id=pos, device_id_type=pl.DeviceIdType.MESH)

    def local():
        return pltpu.make_async_copy(src(me), r_ref.at[me], local_sem)

    def start():
        local().start()
        for k in range(1, N_DEV):
            remote(k, None).start()

    def wait():
        for k in range(1, N_DEV):
            remote(k, "peer").wait_recv()
        for k in range(1, N_DEV):
            remote(k, None).wait_send()
        local().wait()

    return start, wait


COMM_SEMS = [pltpu.SemaphoreType.DMA((N_DEV,)), pltpu.SemaphoreType.DMA((N_DEV,)), pltpu.SemaphoreType.DMA(())]


def _hosted_comm(s_ref, r_ref, sems, same_source):
    start, wait = _peer_copies(s_ref, r_ref, *sems, same_source=same_source)
    i, j = pl.program_id(0), pl.program_id(1)

    @pl.when((i == 0) & (j == 0))
    def _():
        start()

    @pl.when((i == pl.num_programs(0) - 1) & (j == pl.num_programs(1) - 1))
    def _():
        wait()


def _reduce_adamw(recv, w, m, v):
    R = w.shape[0]
    tr = next(t for t in (128, 64, 32, 16) if R % t == 0)
    c1 = 1.0 - ADAM_B1 ** ADAM_STEP
    c2 = 1.0 - ADAM_B2 ** ADAM_STEP

    def body(r_ref, w_ref, m_ref, v_ref, g_ref, d_ref, nm_ref, nv_ref):
        g = r_ref[0].astype(F32)
        for s in range(1, N_DEV):
            g = g + r_ref[s].astype(F32)
        m2 = ADAM_B1 * m_ref[...] + (1.0 - ADAM_B1) * g
        v2 = ADAM_B2 * v_ref[...] + (1.0 - ADAM_B2) * (g * g)
        m_hat = m2 / c1
        v_hat = v2 / c2
        g_ref[...] = g
        d_ref[...] = -ADAM_LR * (m_hat / (jnp.sqrt(v_hat) + ADAM_EPS) + ADAM_WD * w_ref[...])
        nm_ref[...] = m2
        nv_ref[...] = v2

    row = pl.BlockSpec((tr, 1024), lambda i: (i, 0))
    return pl.pallas_call(
        body, name="reduce_adamw", grid=(R // tr,),
        in_specs=[pl.BlockSpec((N_DEV, tr, 1024), lambda i: (0, i, 0)), row, row, row],
        out_specs=[row, row, row, row],
        out_shape=[jax.ShapeDtypeStruct((R, 1024), F32)] * 4,
        compiler_params=_params(("parallel",), 40),
    )(recv, w, m, v)


def _inproj_fwd(x, pre_g, w_pad):
    S = x.shape[0]
    tm, tn = min(1024, S), 1536

    def body(x_ref, g_ref, w_ref, proj_ref, xn_ref):
        @pl.when(pl.program_id(1) == 0)
        def _():
            n, _ = _rms(x_ref[...])
            xn_ref[...] = (n * g_ref[...]).astype(xn_ref.dtype)
        proj_ref[...] = jnp.dot(xn_ref[...], w_ref[...], preferred_element_type=F32)

    return pl.pallas_call(
        body, name="inproj_fwd", grid=(S // tm, DP // tn),
        in_specs=[pl.BlockSpec((tm, D), lambda i, j: (i, 0)), pl.BlockSpec((1, D), lambda i, j: (0, 0)),
                  pl.BlockSpec((D, tn), lambda i, j: (0, j))],
        out_specs=[pl.BlockSpec((tm, tn), lambda i, j: (i, j)), pl.BlockSpec((tm, D), lambda i, j: (i, 0))],
        out_shape=[jax.ShapeDtypeStruct((S, DP), F32), jax.ShapeDtypeStruct((S, D), MXU_DTYPE)],
        compiler_params=_params(("parallel", "arbitrary"), 48),
    )(x, pre_g, w_pad)


def _sgu_tri():
    return lax.broadcasted_iota(jnp.int32, (SGU_T, SGU_T), 0) >= lax.broadcasted_iota(jnp.int32, (SGU_T, SGU_T), 1)


def _sgu_mix(ws_m, vb, bfull, grp):
    mixed = bfull
    for g in range(SGU_G):
        mixed = mixed + jnp.where(grp == g, _mm(ws_m[g], vb), 0.0)
    return mixed


def _layernorm(v, g, b):
    xc = v - jnp.mean(v, axis=-1, keepdims=True)
    rstd = lax.rsqrt(jnp.mean(xc * xc, axis=-1, keepdims=True) + EPS)
    vhat = xc * rstd
    return vhat * g + b, vhat, rstd


def _sgu_fwd(proj, ln_g, ln_b, ws, bfull):
    S = proj.shape[0]
    tm = min(512, S)

    def body(u_ref, v_ref, z_ref, g_ref, b_ref, ws_ref, bf_ref, ya_ref):
        tri = _sgu_tri()
        ws_m = [jnp.where(tri, ws_ref[g], 0.0).astype(MXU_DTYPE) for g in range(SGU_G)]
        grp = lax.broadcasted_iota(jnp.int32, (1, BW), 1) // (BW // SGU_G)
        for b in range(tm // SGU_T):
            r = slice(b * SGU_T, (b + 1) * SGU_T)
            vln, _, _ = _layernorm(v_ref[r, :], g_ref[...], b_ref[...])
            mixed = _sgu_mix(ws_m, vln.astype(MXU_DTYPE), bf_ref[...], grp)
            z = z_ref[r, :]
            ya_ref[r, :] = (u_ref[r, :] * mixed * (z * _sigmoid(z))).astype(ya_ref.dtype)

    blk = lambda cb: pl.BlockSpec((tm, BW), lambda i, cb=cb: (i, cb))
    vec = pl.BlockSpec((1, BW), lambda i: (0, 0))
    return pl.pallas_call(
        body, name="sgu_fwd", grid=(S // tm,),
        in_specs=[blk(UA // BW), blk(VA // BW), blk(ZA // BW), vec, vec,
                  pl.BlockSpec((SGU_G, SGU_T, SGU_T), lambda i: (0, 0, 0)),
                  pl.BlockSpec((SGU_T, BW), lambda i: (0, 0))],
        out_specs=pl.BlockSpec((tm, BW), lambda i: (i, 0)),
        out_shape=jax.ShapeDtypeStruct((S, BW), MXU_DTYPE),
        compiler_params=_params(("parallel",), 32),
    )(proj, proj, proj, ln_g, ln_b, ws, bfull)


def _mla_prep_fwd(proj, gq, gkv, wuq, wuk, wuv, cos_t, sin_a, sin_b):
    S = proj.shape[0]
    tm = min(512, S)

    def body(p_ref, gq_ref, gkv_ref, wuq_ref, wuk_ref, wuv_ref, c_ref, sa_ref, sb_ref,
             q_ref, k_ref, v_ref, kt_ref, vt_ref):
        nq, _ = _rms(p_ref[:, 0:Q_RANK])
        nkv, _ = _rms(p_ref[:, Q_RANK:Q_RANK + KV_RANK])
        cq = (nq * gq_ref[...]).astype(MXU_DTYPE)
        ckv = (nkv * gkv_ref[...]).astype(MXU_DTYPE)
        qf = _mm(cq, wuq_ref[...])
        kf = _mm(ckv, wuk_ref[...])
        v = _mm(ckv, wuv_ref[...])
        v_ref[...] = v.astype(v_ref.dtype)
        vt_ref[...] = jnp.transpose(v).astype(vt_ref.dtype)
        c, sa, sb = c_ref[...], sa_ref[...], sb_ref[...]
        krr = _rope(p_ref[:, Q_RANK + KV_RANK:BW], c, sa, sb)
        for h in range(MLA_H):
            sl = slice(128 * h, 128 * (h + 1))
            q_ref[:, sl] = (_rope(qf[:, sl], c, sa, sb) * MLA_SCALE_LOG2).astype(q_ref.dtype)
            kh = kf[:, sl] + krr
            k_ref[:, sl] = kh.astype(k_ref.dtype)
            kt_ref[sl, :] = jnp.transpose(kh).astype(kt_ref.dtype)

    full = lambda a: pl.BlockSpec(a.shape, lambda i: (0,) * a.ndim)
    tab = pl.BlockSpec((tm, 128), lambda i: (i, 0))
    return pl.pallas_call(
        body, name="mla_prep_fwd", grid=(S // tm,),
        in_specs=[pl.BlockSpec((tm, BW), lambda i: (i, QD // BW)), full(gq), full(gkv), full(wuq), full(wuk),
                  full(wuv), tab, tab, tab],
        out_specs=[pl.BlockSpec((tm, 1024), lambda i: (i, 0)), pl.BlockSpec((tm, 1024), lambda i: (i, 0)),
                   pl.BlockSpec((tm, BW), lambda i: (i, 0)), pl.BlockSpec((1024, tm), lambda i: (0, i)),
                   pl.BlockSpec((BW, tm), lambda i: (0, i))],
        out_shape=[jax.ShapeDtypeStruct((S, 1024), MXU_DTYPE), jax.ShapeDtypeStruct((S, 1024), MXU_DTYPE),
                   jax.ShapeDtypeStruct((S, BW), MXU_DTYPE), jax.ShapeDtypeStruct((1024, S), MXU_DTYPE),
                   jax.ShapeDtypeStruct((BW, S), MXU_DTYPE)],
        compiler_params=_params(("parallel",), 48),
    )(proj, gq, gkv, wuq, wuk, wuv, cos_t, sin_a, sin_b)


MLA_SCALE = MLA_QK ** -0.5
MLA_SCALE_LOG2 = MLA_SCALE * 1.4426950408889634


def _tri_tables(n, q_major):
    if q_major:
        pairs = [(qi, ki) for qi in range(n) for ki in range(qi + 1)]
    else:
        pairs = [(qi, ki) for ki in range(n) for qi in range(ki, n)]
    return (jnp.asarray([p[0] for p in pairs], jnp.int32), jnp.asarray([p[1] for p in pairs], jnp.int32))


def _chunk_mask_t(T):
    kc = lax.broadcasted_iota(jnp.int32, (T, T), 0) >> 6
    qc = lax.broadcasted_iota(jnp.int32, (T, T), 1) >> 6
    return kc <= qc


FLASH_T = 2048
FLASH_SUB = 512


def _col_reduce(op, reduce_fn, x):
    r = x.shape[0]
    while r > 8 and r % 16 == 0:
        r //= 2
        x = op(x[:r], x[r:])
    return reduce_fn(x, axis=0, keepdims=True)


def _flash_fwd(qh, kh, vt, gather=None):
    S = qh.shape[0]
    T = min(FLASH_T, S)
    TS = min(FLASH_SUB, T)
    n = S // T
    qt, kt = _tri_tables(n, q_major=True)

    def body(qt_ref, kt_ref, q_ref, k_ref, vt_ref, *rest):
        if gather is None:
            o_ref, lse_ref, m_scr, acc_scr = rest
        else:
            g_ref, o_ref, lse_ref, got_ref, m_scr, acc_scr, *sems = rest
            _hosted_comm(g_ref, got_ref, sems, same_source=True)
        t = pl.program_id(1)
        qi, ki = qt_ref[t], kt_ref[t]
        row = lax.broadcasted_iota(jnp.int32, (128, 1), 0)
        ones_row = (64, 0)

        @pl.when(ki == 0)
        def _():
            m_scr[...] = jnp.full(m_scr.shape, NEG, F32)
            acc_scr[...] = jnp.zeros(acc_scr.shape, F32)

        def step(diag):
            for j in range(2):
                sl = slice(128 * j, 128 * (j + 1))
                for qs in range(T // TS):
                    cq = slice(qs * TS, (qs + 1) * TS)
                    qsub = q_ref[cq, sl]
                    for ks in range(T // TS):
                        if diag and ks > qs:
                            continue
                        rk = slice(ks * TS, (ks + 1) * TS)
                        s = _mm_nt(k_ref[rk, sl], qsub)
                        if diag and ks == qs:
                            s = jnp.where(_chunk_mask_t(TS), s, NEG)
                        m_old = m_scr[j, :, cq]
                        m_new = jnp.maximum(m_old, _col_reduce(jnp.maximum, jnp.max, s))
                        alpha = jnp.exp2(m_old - m_new)
                        p = jnp.exp2(s - m_new)
                        vt1 = jnp.where(row == ones_row[j], 1.0, vt_ref[:, rk]).astype(MXU_DTYPE)
                        acc_scr[j, :, cq] = alpha * acc_scr[j, :, cq] + _mm(vt1, p)
                        m_scr[j, :, cq] = m_new

        @pl.when(ki < qi)
        def _():
            step(False)

        @pl.when(ki == qi)
        def _():
            step(True)
            l = [acc_scr[j, ones_row[j]:ones_row[j] + 1, :] for j in range(2)]
            o_ref[...] = jnp.transpose(jnp.where(row < 64, acc_scr[0] / l[0], acc_scr[1] / l[1]))
            for j in range(2):
                lse_ref[0, j:j + 1, :] = m_scr[j] + jnp.log2(l[j])

    qmap = lambda hp, t, qt, kt: (qt[t], hp)
    kmap = lambda hp, t, qt, kt: (kt[t], hp)
    hbm = pl.BlockSpec(memory_space=pl.ANY)
    hosted = gather is not None
    got_shape = [jax.ShapeDtypeStruct((N_DEV,) + gather.shape, gather.dtype)] if hosted else []
    return pl.pallas_call(
        body, name="flash_fwd_gather" if hosted else "flash_fwd",
        grid_spec=pltpu.PrefetchScalarGridSpec(
            num_scalar_prefetch=2, grid=(MLA_H // 2, qt.shape[0]),
            in_specs=[pl.BlockSpec((T, 256), qmap), pl.BlockSpec((T, 256), kmap),
                      pl.BlockSpec((128, T), lambda hp, t, qt, kt: (hp, kt[t]))] + [hbm] * hosted,
            out_specs=[pl.BlockSpec((T, 128), qmap),
                       pl.BlockSpec((1, 2, T), lambda hp, t, qt, kt: (hp, 0, qt[t]))] + [hbm] * hosted,
            scratch_shapes=[pltpu.VMEM((2, 1, T), F32), pltpu.VMEM((2, 128, T), F32)] + COMM_SEMS * hosted),
        out_shape=[jax.ShapeDtypeStruct((S, BW), F32), jax.ShapeDtypeStruct((MLA_H // 2, 2, S), F32)] + got_shape,
        compiler_params=_params(("arbitrary", "arbitrary"), 40),
    )(qt, kt, qh, kh, vt, *([gather] * hosted))


def _attn_delta(o, do):
    S = o.shape[0]
    T = min(2048, S)

    def body(o_ref, do_ref, d_ref):
        head = lax.broadcasted_iota(jnp.int32, (1, 128), 1) // 64
        prod = o_ref[...] * do_ref[...]
        for j in range(2):
            d_ref[0, j:j + 1, :] = jnp.sum(jnp.transpose(jnp.where(head == j, prod, 0.0)), axis=0, keepdims=True)

    blk = pl.BlockSpec((T, 128), lambda hp, i: (i, hp))
    return pl.pallas_call(
        body, name="attn_delta", grid=(MLA_H // 2, S // T),
        in_specs=[blk, blk],
        out_specs=pl.BlockSpec((1, 2, T), lambda hp, i: (hp, 0, i)),
        out_shape=jax.ShapeDtypeStruct((MLA_H // 2, 2, S), F32),
        compiler_params=_params(("parallel", "parallel"), 32),
    )(o, do)


def _band_specs(S):
    q = pl.BlockSpec((BAND_TQ, 128), lambda hp, qi: (qi, QC // 128 + hp))
    ks = [pl.BlockSpec((BAND_TQ, 128), lambda hp, qi, t=t: (jnp.maximum(qi - 2 + t, 0), KC // 128 + hp))
          for t in range(3)]
    vs = [pl.BlockSpec((BAND_TQ, 128), lambda hp, qi, t=t: (jnp.maximum(qi - 2 + t, 0), VC // 128 + hp))
          for t in range(3)]
    bias = pl.BlockSpec((2, BAND_TQ, BAND_W), lambda hp, qi: (hp, 0, 0))
    return q, ks, vs, bias


LOG2E = 1.4426950408889634
LN2 = 0.6931471805599453
CA_SCALE = CA_DH ** -0.5


def _band_probs(q2j, kcat, bias2_j, valid):
    s = _mm_nt(q2j, kcat) + bias2_j
    s = jnp.where(valid, s, NEG)
    p = jnp.exp2(s - jnp.max(s, axis=1, keepdims=True))
    return p * (1.0 / jnp.sum(p, axis=1, keepdims=True))


def _band_valid(qi):
    tile = lax.broadcasted_iota(jnp.int32, (1, BAND_W), 1) // BAND_TQ
    return tile + qi >= 2


def _band_fwd(proj, bias):
    S = proj.shape[0]

    def body(q_ref, k0, k1, k2, k3, v0, v1, v2, v3, b_ref, o_ref):
        qi = pl.program_id(1)
        head = lax.broadcasted_iota(jnp.int32, (1, 128), 1) // 64
        kb = [r[...].astype(MXU_DTYPE) for r in (k0, k1, k2, k3)]
        vb = [r[...].astype(MXU_DTYPE) for r in (v0, v1, v2, v3)]
        for a in range(2):
            rows = slice(a * BAND_TQ, (a + 1) * BAND_TQ)
            kcat = jnp.concatenate(kb[a:a + 3], axis=0)
            vcat = jnp.concatenate(vb[a:a + 3], axis=0)
            valid = _band_valid(2 * qi + a)
            q2 = q_ref[rows, :] * (CA_SCALE * LOG2E)
            o = jnp.zeros((BAND_TQ, 128), F32)
            for j in range(2):
                pn = _band_probs(jnp.where(head == j, q2, 0.0), kcat, b_ref[j], valid)
                o = o + _mm(pn, jnp.where(head == j, vcat, 0))
            o_ref[rows, :] = o

    blk = lambda col0: [pl.BlockSpec((BAND_TQ, 128),
                                     lambda hp, qi, t=t: (jnp.maximum(2 * qi - 2 + t, 0), col0 // 128 + hp))
                        for t in range(4)]
    pair = lambda col0: pl.BlockSpec((2 * BAND_TQ, 128), lambda hp, qi: (qi, col0 // 128 + hp))
    return pl.pallas_call(
        body, name="band_fwd", grid=(CA_H // 2, S // (2 * BAND_TQ)),
        in_specs=[pair(QC), *blk(KC), *blk(VC), pl.BlockSpec((2, BAND_TQ, BAND_W), lambda hp, qi: (hp, 0, 0))],
        out_specs=pair(0),
        out_shape=jax.ShapeDtypeStruct((S, BW), F32),
        compiler_params=_params(("parallel", "arbitrary"), 40),
    )(proj, proj, proj, proj, proj, proj, proj, proj, proj, bias)


def _merge_fwd(x, ya, ob, oc, proj, gate_b, wbr, wout, post_g):
    S = x.shape[0]
    tm = min(256, S)

    def body(x_ref, ya_ref, ob_ref, oc_ref, zb_ref, zc_ref, g0, g1, g2, gb_ref, wbr_ref, wo_ref, pg_ref,
             xo_ref, mg_ref, h_ref):
        zb, zc = zb_ref[...], zc_ref[...]
        ys = [ya_ref[...], ob_ref[...] * (zb * _sigmoid(zb)), oc_ref[...] * (zc * _sigmoid(zc))]
        merged = jnp.zeros((tm, D), F32)
        for i, g_ref in enumerate((g0, g1, g2)):
            merged = merged + _sigmoid(g_ref[...] + gb_ref[i:i + 1, :]) * _mm(ys[i], wbr_ref[i])
        mg_ref[...] = merged.astype(mg_ref.dtype)
        h = _mm(merged, wo_ref[...])
        h_ref[...] = h
        n, _ = _rms(h)
        xo_ref[...] = x_ref[...] + n * pg_ref[...]

    row = lambda w, cb=0: pl.BlockSpec((tm, w), lambda i, cb=cb: (i, cb))
    full = lambda a: pl.BlockSpec(a.shape, lambda i: (0,) * a.ndim)
    return pl.pallas_call(
        body, name="merge_fwd", grid=(S // tm,),
        in_specs=[row(D), row(BW), row(BW), row(BW), row(BW, ZB // BW), row(BW, ZC // BW),
                  row(D, 0), row(D, 1), row(D, 2), full(gate_b), full(wbr), full(wout), full(post_g)],
        out_specs=[row(D), row(D), row(D)],
        out_shape=[jax.ShapeDtypeStruct((S, D), F32), jax.ShapeDtypeStruct((S, D), MXU_DTYPE),
                   jax.ShapeDtypeStruct((S, D), F32)],
        compiler_params=_params(("parallel",), 56),
    )(x, ya, ob, oc, proj, proj, proj, proj, proj, gate_b, wbr, wout, post_g)


def _loss_fwd_bwd(y, target):
    S = y.shape[0]
    tm = min(512, S)

    def body(y_ref, t_ref, loss_ref, dy_ref):
        @pl.when(pl.program_id(0) == 0)
        def _():
            loss_ref[...] = jnp.zeros((1, 1), F32)
        err = y_ref[...] - t_ref[...]
        loss_ref[...] += 0.5 * jnp.sum(jnp.mean(err * err, axis=-1, keepdims=True), axis=0, keepdims=True)
        dy_ref[...] = err * (1.0 / D)

    row = pl.BlockSpec((tm, D), lambda i: (i, 0))
    return pl.pallas_call(
        body, name="loss", grid=(S // tm,),
        in_specs=[row, row],
        out_specs=[pl.BlockSpec((1, 1), lambda i: (0, 0)), row],
        out_shape=[jax.ShapeDtypeStruct((1, 1), F32), jax.ShapeDtypeStruct((S, D), F32)],
        compiler_params=_params(("arbitrary",), 32),
    )(y, target)


def _first_step_zero(refs, first):
    @pl.when(first)
    def _():
        for r in refs:
            r[...] = jnp.zeros(r.shape, r.dtype)


def _out_bwd(dxo, h, merged, post_g, wout_t):
    S = dxo.shape[0]
    tm = min(256, S)

    def body(d_ref, h_ref, mg_ref, pg_ref, wt_ref, dm_ref, dw_ref, dg_ref):
        _first_step_zero((dw_ref, dg_ref), pl.program_id(0) == 0)
        d = d_ref[...]
        hn, r = _rms(h_ref[...])
        dg_ref[...] += jnp.sum(d * hn, axis=0, keepdims=True)
        dh = _rms_bwd(d * pg_ref[...], hn, r)
        dm_ref[...] = _mm(dh, wt_ref[...])
        dw_ref[...] += _mm_tn(mg_ref[...], dh)

    row = pl.BlockSpec((tm, D), lambda i: (i, 0))
    full = lambda shape: pl.BlockSpec(shape, lambda i: (0,) * len(shape))
    return pl.pallas_call(
        body, name="out_bwd", grid=(S // tm,),
        in_specs=[row, row, row, full((1, D)), full((D, D))],
        out_specs=[row, full((D, D)), full((1, D))],
        out_shape=[jax.ShapeDtypeStruct((S, D), F32), jax.ShapeDtypeStruct((D, D), F32),
                   jax.ShapeDtypeStruct((1, D), F32)],
        compiler_params=_params(("arbitrary",), 40),
    )(dxo, h, merged, post_g, wout_t)


def _gate_bwd(dm, ya, ob, oc, proj, gate_b, wbr, wbr_t):
    S = dm.shape[0]
    tm = min(128, S)

    def body(dm_ref, ya_ref, ob_ref, oc_ref, zb_ref, zc_ref, g0, g1, g2, gb_ref, wbr_ref, wbt_ref,
             dg_ref, dzb_ref, dzc_ref, dya_ref, dob_ref, doc_ref, dwbr_ref, dgb_ref):
        _first_step_zero((dwbr_ref, dgb_ref), pl.program_id(0) == 0)
        dmv = dm_ref[...]
        zb, zc = zb_ref[...], zc_ref[...]
        sgb, sgc = _sigmoid(zb), _sigmoid(zc)
        ob, oc = ob_ref[...], oc_ref[...]
        ys = [ya_ref[...], (ob * (zb * sgb)).astype(MXU_DTYPE), (oc * (zc * sgc)).astype(MXU_DTYPE)]
        dys = []
        for i, g_ref in enumerate((g0, g1, g2)):
            br = _mm(ys[i], wbr_ref[i])
            gate = _sigmoid(g_ref[...] + gb_ref[i:i + 1, :])
            dgl = dmv * br * (gate * (1.0 - gate))
            dg_ref[:, D * i:D * (i + 1)] = dgl.astype(dg_ref.dtype)
            dgb_ref[i:i + 1, :] += jnp.sum(dgl, axis=0, keepdims=True)
            dbr = dmv * gate
            dys.append(_mm(dbr, wbt_ref[i]))
            dwbr_ref[i] += _mm_tn(dbr, ys[i])
        dya_ref[...] = dys[0]
        dob_ref[...] = dys[1] * (zb * sgb)
        dzb_ref[...] = (dys[1] * ob * (sgb * (1.0 + zb * (1.0 - sgb)))).astype(dzb_ref.dtype)
        doc_ref[...] = dys[2] * (zc * sgc)
        dzc_ref[...] = (dys[2] * oc * (sgc * (1.0 + zc * (1.0 - sgc)))).astype(dzc_ref.dtype)

    row = lambda w, cb=0: pl.BlockSpec((tm, w), lambda i, cb=cb: (i, cb))
    full = lambda a: pl.BlockSpec(a.shape, lambda i: (0,) * a.ndim)
    sds = lambda w, dt=F32: jax.ShapeDtypeStruct((S, w), dt)
    return pl.pallas_call(
        body, name="gate_bwd", grid=(S // tm,),
        in_specs=[row(D), row(BW), row(BW), row(BW), row(BW, ZB // BW), row(BW, ZC // BW),
                  row(D, 0), row(D, 1), row(D, 2), full(gate_b), full(wbr), full(wbr_t)],
        out_specs=[row(3 * D), row(BW), row(BW), row(BW), row(BW), row(BW),
                   pl.BlockSpec((N_BRANCH, D, BW), lambda i: (0, 0, 0)), pl.BlockSpec((N_BRANCH, D), lambda i: (0, 0))],
        out_shape=[sds(3 * D, MXU_DTYPE), sds(BW, MXU_DTYPE), sds(BW, MXU_DTYPE), sds(BW), sds(BW), sds(BW),
                   jax.ShapeDtypeStruct((N_BRANCH, D, BW), F32), jax.ShapeDtypeStruct((N_BRANCH, D), F32)],
        compiler_params=_params(("arbitrary",), 56),
    )(dm, ya, ob, oc, proj, proj, proj, proj, proj, gate_b, wbr, wbr_t)


def _band_bwd(proj, bias, do):
    S = proj.shape[0]

    def body(q_ref, k0, k1, k2, v0, v1, v2, b_ref, do_ref, dq_ref, dk_ref, dv_ref, db_ref):
        qi = pl.program_id(1)
        _first_step_zero((dk_ref, dv_ref, db_ref), qi == 0)
        head = lax.broadcasted_iota(jnp.int32, (1, 128), 1) // 64
        kcat = jnp.concatenate([k0[...], k1[...], k2[...]], axis=0).astype(MXU_DTYPE)
        vcat = jnp.concatenate([v0[...], v1[...], v2[...]], axis=0).astype(MXU_DTYPE)
        valid = _band_valid(qi)
        q2, dov = q_ref[...] * (CA_SCALE * LOG2E), do_ref[...]
        dq = jnp.zeros((BAND_TQ, 128), F32)
        dk = jnp.zeros((BAND_W, 128), F32)
        dv = jnp.zeros((BAND_W, 128), F32)
        for j in range(2):
            q2j = jnp.where(head == j, q2, 0.0).astype(MXU_DTYPE)
            doj = jnp.where(head == j, dov, 0.0).astype(MXU_DTYPE)
            pn = _band_probs(q2j, kcat, b_ref[j], valid)
            dv = dv + _mm_tn(pn, doj)
            dp = _mm_nt(doj, vcat)
            ds = pn * (dp - jnp.sum(pn * dp, axis=1, keepdims=True))
            db_ref[j] += ds
            dsb = ds.astype(MXU_DTYPE)
            dq = dq + _mm(dsb, jnp.where(head == j, kcat, 0))
            dk = dk + _mm_tn(dsb, q2j)
        dq_ref[...] = (dq * CA_SCALE).astype(dq_ref.dtype)
        for t in range(3):
            @pl.when(qi - 2 + t >= 0)
            def _(t=t):
                rows = pl.ds(pl.multiple_of((qi - 2 + t) * BAND_TQ, BAND_TQ), BAND_TQ)
                dk_ref[rows, :] += dk[t * BAND_TQ:(t + 1) * BAND_TQ] * LN2
                dv_ref[rows, :] += dv[t * BAND_TQ:(t + 1) * BAND_TQ]

    q, ks, vs, bspec = _band_specs(S)
    col = pl.BlockSpec((S, 128), lambda hp, qi: (0, hp))
    return pl.pallas_call(
        body, name="band_bwd", grid=(CA_H // 2, S // BAND_TQ),
        in_specs=[q, *ks, *vs, bspec, pl.BlockSpec((BAND_TQ, 128), lambda hp, qi: (qi, hp))],
        out_specs=[pl.BlockSpec((BAND_TQ, 128), lambda hp, qi: (qi, hp)), col, col, bspec],
        out_shape=[jax.ShapeDtypeStruct((S, BW), MXU_DTYPE), jax.ShapeDtypeStruct((S, BW), F32),
                   jax.ShapeDtypeStruct((S, BW), F32), jax.ShapeDtypeStruct((CA_H, BAND_TQ, BAND_W), F32)],
        compiler_params=_params(("parallel", "arbitrary"), 56),
    )(proj, proj, proj, proj, proj, proj, proj, bias, do)


BIAS_LO = REL_CLIP - (CHUNK - 1)
BIAS_FAR = 2 * REL_CLIP
BIAS_NEAR0 = BAND_W // 2


def _band_index(col0, ncol):
    i = lax.broadcasted_iota(jnp.int32, (BAND_TQ, ncol), 0)
    j = lax.broadcasted_iota(jnp.int32, (BAND_TQ, ncol), 1) + col0
    idx = jnp.clip(i + 2 * BAND_TQ - j, -REL_CLIP, REL_CLIP) + REL_CLIP
    ci, cj = i // CHUNK, j // CHUNK
    return jnp.where((ci <= cj) & (cj <= ci + LEFT_CHUNKS), idx, -1)


def _skew(x, right):
    row = lax.broadcasted_iota(jnp.int32, (BAND_TQ, 1), 0)
    for b in range(BAND_TQ.bit_length() - 1):
        shift = (1 << b) if right else BAND_W - (1 << b)
        x = jnp.where(((row >> b) & 1) == 1, pltpu.roll(x, shift, 1), x)
    return x


SKEW_NEAR0 = 2 * BAND_TQ - REL_CLIP + 1
SKEW_NEAR1 = 2 * BAND_TQ + CHUNK
SKEW_WRAP0 = BAND_W - (CHUNK - 1)


def _bias_build(rel_table):
    far = rel_table[:, BIAS_FAR:]
    base = jnp.concatenate([jnp.broadcast_to(far, (CA_H, SKEW_NEAR0)), rel_table[:, BIAS_LO:BIAS_FAR][:, ::-1],
                            jnp.broadcast_to(far, (CA_H, BAND_W - SKEW_NEAR1))], axis=1)[:, None, :]

    def body(base_ref, out_ref):
        valid = _band_index(0, BAND_W) >= 0
        for h in range(CA_H):
            tile = _skew(jnp.broadcast_to(base_ref[h] * LOG2E, (BAND_TQ, BAND_W)), right=True)
            out_ref[h] = jnp.where(valid, tile, NEG)

    return pl.pallas_call(
        body, name="bias_build",
        out_shape=jax.ShapeDtypeStruct((CA_H, BAND_TQ, BAND_W), F32),
        in_specs=[pl.BlockSpec(memory_space=pltpu.VMEM)],
        out_specs=pl.BlockSpec(memory_space=pltpu.VMEM),
        compiler_params=pltpu.CompilerParams(vmem_limit_bytes=40 << 20),
    )(base)


def _bias_fold(db):
    def body(db_ref, sums_ref, far_ref):
        col = lax.broadcasted_iota(jnp.int32, (1, BAND_W), 1)
        is_far = (col < SKEW_NEAR0) | (col >= SKEW_WRAP0)
        for h in range(CA_H):
            sums = jnp.sum(_skew(db_ref[h], right=False), axis=0, keepdims=True)
            sums_ref[h] = sums
            far_ref[h] = jnp.broadcast_to(jnp.sum(jnp.where(is_far, sums, 0.0), axis=1, keepdims=True), (1, 128))

    sums, far = pl.pallas_call(
        body, name="bias_fold",
        out_shape=[jax.ShapeDtypeStruct((CA_H, 1, BAND_W), F32), jax.ShapeDtypeStruct((CA_H, 1, 128), F32)],
        in_specs=[pl.BlockSpec(memory_space=pltpu.VMEM)],
        out_specs=[pl.BlockSpec(memory_space=pltpu.VMEM), pl.BlockSpec(memory_space=pltpu.VMEM)],
        compiler_params=pltpu.CompilerParams(vmem_limit_bytes=40 << 20),
    )(db)
    near = sums[:, 0, SKEW_NEAR0:SKEW_NEAR1][:, ::-1]
    return jnp.concatenate([jnp.zeros((CA_H, BIAS_LO), F32), near, far[:, 0, :1]], axis=1)


def _flash_bwd(qh, kh, ktr, vh, lse, delta, do, send=None):
    S = qh.shape[0]
    T = min(FLASH_T, S)
    TS = min(FLASH_SUB, T)
    n = S // T
    qt, kt = _tri_tables(n, q_major=False)

    def body(qt_ref, kt_ref, q_ref, k_ref, ktr_ref, v_ref, lse_ref, dl_ref, do_ref, *rest):
        if send is None:
            dqt_ref, dk_ref, dv_ref, dk_scr, dv_scr = rest
        else:
            send_ref, dqt_ref, dk_ref, dv_ref, recv_ref, dk_scr, dv_scr, *sems = rest
            _hosted_comm(send_ref, recv_ref, sems, same_source=False)
        t = pl.program_id(1)
        qi, ki = qt_ref[t], kt_ref[t]

        @pl.when(t == 0)
        def _():
            dqt_ref[...] = jnp.zeros(dqt_ref.shape, F32)

        @pl.when(qi == ki)
        def _():
            dk_scr[...] = jnp.zeros(dk_scr.shape, F32)
            dv_scr[...] = jnp.zeros(dv_scr.shape, F32)

        def step(diag):
            head = lax.broadcasted_iota(jnp.int32, (1, 128), 1) // 64
            for j in range(2):
                sl = slice(128 * j, 128 * (j + 1))
                for qs in range(T // TS):
                    cq = slice(qs * TS, (qs + 1) * TS)
                    qsub = q_ref[cq, sl]
                    doj = jnp.where(head == j, do_ref[cq, :], 0.0).astype(MXU_DTYPE)
                    lse, dlt = lse_ref[0, j:j + 1, cq], dl_ref[0, j:j + 1, cq]
                    cols = pl.ds(pl.multiple_of(qi * T + qs * TS, TS), TS)
                    for ks in range(T // TS):
                        if diag and ks > qs:
                            continue
                        rk = slice(ks * TS, (ks + 1) * TS)
                        s = _mm_nt(k_ref[rk, sl], qsub)
                        if diag and ks == qs:
                            s = jnp.where(_chunk_mask_t(TS), s, NEG)
                        p = jnp.exp2(s - lse)
                        dv_scr[rk, :] += _mm(p, doj)
                        dp = _mm_nt(v_ref[rk, :], doj)
                        ds = (p * (dp - dlt)).astype(MXU_DTYPE)
                        dk_scr[rk, sl] += _mm(ds, qsub)
                        dqt_ref[sl, cols] += _mm(ktr_ref[sl, rk], ds) * MLA_SCALE

        @pl.when(qi > ki)
        def _():
            step(False)

        @pl.when(qi == ki)
        def _():
            step(True)

        @pl.when(qi == n - 1)
        def _():
            dk_ref[...] = dk_scr[...] * 0.6931471805599453
            dv_ref[...] = dv_scr[...]

    qmap = lambda hp, t, qt, kt: (qt[t], hp)
    kmap = lambda hp, t, qt, kt: (kt[t], hp)
    stat = pl.BlockSpec((1, 2, T), lambda hp, t, qt, kt: (hp, 0, qt[t]))
    hbm = pl.BlockSpec(memory_space=pl.ANY)
    hosted = send is not None
    recv_shape = [jax.ShapeDtypeStruct(send.shape, send.dtype)] if hosted else []
    return pl.pallas_call(
        body, name="flash_bwd_exchange" if hosted else "flash_bwd",
        grid_spec=pltpu.PrefetchScalarGridSpec(
            num_scalar_prefetch=2, grid=(MLA_H // 2, qt.shape[0]),
            in_specs=[pl.BlockSpec((T, 256), qmap), pl.BlockSpec((T, 256), kmap),
                      pl.BlockSpec((256, T), lambda hp, t, qt, kt: (hp, kt[t])), pl.BlockSpec((T, 128), kmap),
                      stat, stat, pl.BlockSpec((T, 128), qmap)] + [hbm] * hosted,
            out_specs=[pl.BlockSpec((256, S), lambda hp, t, qt, kt: (hp, 0)), pl.BlockSpec((T, 256), kmap),
                       pl.BlockSpec((T, 128), kmap)] + [hbm] * hosted,
            scratch_shapes=[pltpu.VMEM((T, 256), F32), pltpu.VMEM((T, 128), F32)] + COMM_SEMS * hosted),
        out_shape=[jax.ShapeDtypeStruct((1024, S), F32), jax.ShapeDtypeStruct((S, 1024), F32),
                   jax.ShapeDtypeStruct((S, BW), F32)] + recv_shape,
        compiler_params=_params(("arbitrary", "arbitrary"), 56),
    )(qt, kt, qh, kh, ktr, vh, lse, delta, do, *([send] * hosted))


def _mla_prep_bwd(proj, dqf, dkf, dvf, gq, gkv, wuq_t, wuk_t, wuv_t, cos_t, sin_a, sin_b):
    S = proj.shape[0]
    tm = min(512, S)

    def body(p_ref, dq_ref, dk_ref, dv_ref, gq_ref, gkv_ref, wq_ref, wk_ref, wv_ref, c_ref, sa_ref, sb_ref,
             db_ref, dwq_ref, dwk_ref, dwv_ref, dgq_ref, dgkv_ref):
        _first_step_zero((dwq_ref, dwk_ref, dwv_ref, dgq_ref, dgkv_ref), pl.program_id(0) == 0)
        c, sa, sb = c_ref[...], sa_ref[...], sb_ref[...]
        nq, rq = _rms(p_ref[:, 0:Q_RANK])
        nkv, rkv = _rms(p_ref[:, Q_RANK:Q_RANK + KV_RANK])
        cq = (nq * gq_ref[...]).astype(MXU_DTYPE)
        ckv = (nkv * gkv_ref[...]).astype(MXU_DTYPE)
        dkr = jnp.zeros((tm, 128), F32)
        dq_pre = []
        for h in range(MLA_H):
            sl = slice(128 * h, 128 * (h + 1))
            dq_pre.append(_rope_t(jnp.transpose(dq_ref[sl, :]), c, sa, sb).astype(MXU_DTYPE))
            dkr = dkr + dk_ref[:, sl]
        dq_pre = jnp.concatenate(dq_pre, axis=1)
        dcq = _mm(dq_pre, wq_ref[...])
        dwq_ref[...] += _mm_tn(dq_pre, cq)
        dgq_ref[...] += jnp.sum(dcq * nq, axis=0, keepdims=True)
        db_ref[:, 0:Q_RANK] = _rms_bwd(dcq * gq_ref[...], nq, rq).astype(db_ref.dtype)
        dk = dk_ref[...].astype(MXU_DTYPE)
        dv = dv_ref[...].astype(MXU_DTYPE)
        dckv = _mm(dk, wk_ref[...]) + _mm(dv, wv_ref[...])
        dwk_ref[...] += _mm_tn(dk, ckv)
        dwv_ref[...] += _mm_tn(dv, ckv)
        dgkv_ref[...] += jnp.sum(dckv * nkv, axis=0, keepdims=True)
        db_ref[:, Q_RANK:Q_RANK + KV_RANK] = _rms_bwd(dckv * gkv_ref[...], nkv, rkv).astype(db_ref.dtype)
        lane = lax.broadcasted_iota(jnp.int32, (1, 128), 1)
        rope_lanes = (lane >= MLA_NOPE) & (lane < MLA_QK)
        db_ref[:, Q_RANK + KV_RANK:BW] = _rope_t(jnp.where(rope_lanes, dkr, 0.0), c, sa, sb).astype(db_ref.dtype)

    full = lambda a: pl.BlockSpec(a.shape, lambda i: (0,) * a.ndim)
    fulls = lambda shape: pl.BlockSpec(shape, lambda i: (0,) * len(shape))
    tab = pl.BlockSpec((tm, 128), lambda i: (i, 0))
    row = lambda w, cb=0: pl.BlockSpec((tm, w), lambda i, cb=cb: (i, cb))
    return pl.pallas_call(
        body, name="mla_prep_bwd", grid=(S // tm,),
        in_specs=[row(BW, QD // BW), pl.BlockSpec((1024, tm), lambda i: (0, i)), row(1024), row(BW),
                  full(gq), full(gkv), full(wuq_t),
                  full(wuk_t), full(wuv_t), tab, tab, tab],
        out_specs=[row(BW), fulls((1024, Q_RANK)), fulls((1024, KV_RANK)), fulls((BW, KV_RANK)),
                   fulls((1, Q_RANK)), fulls((1, KV_RANK))],
        out_shape=[jax.ShapeDtypeStruct((S, BW), MXU_DTYPE), jax.ShapeDtypeStruct((1024, Q_RANK), F32),
                   jax.ShapeDtypeStruct((1024, KV_RANK), F32), jax.ShapeDtypeStruct((BW, KV_RANK), F32),
                   jax.ShapeDtypeStruct((1, Q_RANK), F32), jax.ShapeDtypeStruct((1, KV_RANK), F32)],
        compiler_params=_params(("arbitrary",), 56),
    )(proj, dqf, dkf, dvf, gq, gkv, wuq_t, wuk_t, wuv_t, cos_t, sin_a, sin_b)


def _sgu_bwd(proj, dya, ln_g, ln_b, ws, bfull):
    S = proj.shape[0]
    tm = min(512, S)

    def body(u_ref, v_ref, z_ref, dy_ref, g_ref, b_ref, ws_ref, bf_ref,
             da_ref, dws_ref, dbf_ref, dlg_ref, dlb_ref, dbs_ref):
        i = pl.program_id(0)
        _first_step_zero((dws_ref, dbf_ref, dlg_ref, dlb_ref, dbs_ref), i == 0)
        tri = _sgu_tri()
        ws_m = [jnp.where(tri, ws_ref[g], 0.0).astype(MXU_DTYPE) for g in range(SGU_G)]
        grp = lax.broadcasted_iota(jnp.int32, (1, BW), 1) // (BW // SGU_G)
        for b in range(tm // SGU_T):
            r = slice(b * SGU_T, (b + 1) * SGU_T)
            vln, vhat, rstd = _layernorm(v_ref[r, :], g_ref[...], b_ref[...])
            vb = vln.astype(MXU_DTYPE)
            mixed = _sgu_mix(ws_m, vb, bf_ref[...], grp)
            u, z, dy = u_ref[r, :], z_ref[r, :], dy_ref[r, :]
            sg = _sigmoid(z)
            sz = z * sg
            da_ref[r, 0:BW] = (dy * mixed * sz).astype(da_ref.dtype)
            da_ref[r, 2 * BW:3 * BW] = (dy * u * mixed * (sg * (1.0 + z * (1.0 - sg)))).astype(da_ref.dtype)
            dmix = dy * u * sz
            dbf_ref[...] += dmix
            dvln = jnp.zeros((SGU_T, BW), F32)
            for g in range(SGU_G):
                dmg = jnp.where(grp == g, dmix, 0.0).astype(MXU_DTYPE)
                dvln = dvln + _mm_tn(ws_m[g], dmg)
                dws_ref[g] += jnp.where(tri, _mm_nt(dmg, vb), 0.0)
            dlg_ref[...] += jnp.sum(dvln * vhat, axis=0, keepdims=True)
            dlb_ref[...] += jnp.sum(dvln, axis=0, keepdims=True)
            dvh = dvln * g_ref[...]
            da_ref[r, BW:2 * BW] = (rstd * (dvh - jnp.mean(dvh, axis=-1, keepdims=True)
                                            - vhat * jnp.mean(dvh * vhat, axis=-1, keepdims=True))).astype(da_ref.dtype)

        @pl.when(i == pl.num_programs(0) - 1)
        def _():
            dbf = dbf_ref[...]
            for g in range(SGU_G):
                dbs_ref[:, g:g + 1] = jnp.sum(jnp.where(grp == g, dbf, 0.0), axis=1, keepdims=True)

    blk = lambda cb: pl.BlockSpec((tm, BW), lambda i, cb=cb: (i, cb))
    fulls = lambda shape: pl.BlockSpec(shape, lambda i: (0,) * len(shape))
    return pl.pallas_call(
        body, name="sgu_bwd", grid=(S // tm,),
        in_specs=[blk(UA // BW), blk(VA // BW), blk(ZA // BW), blk(0), fulls((1, BW)), fulls((1, BW)),
                  fulls((SGU_G, SGU_T, SGU_T)), fulls((SGU_T, BW))],
        out_specs=[pl.BlockSpec((tm, 3 * BW), lambda i: (i, 0)), fulls((SGU_G, SGU_T, SGU_T)),
                   fulls((SGU_T, BW)), fulls((1, BW)), fulls((1, BW)), fulls((SGU_T, SGU_G))],
        out_shape=[jax.ShapeDtypeStruct((S, 3 * BW), MXU_DTYPE), jax.ShapeDtypeStruct((SGU_G, SGU_T, SGU_T), F32),
                   jax.ShapeDtypeStruct((SGU_T, BW), F32), jax.ShapeDtypeStruct((1, BW), F32),
                   jax.ShapeDtypeStruct((1, BW), F32), jax.ShapeDtypeStruct((SGU_T, SGU_G), F32)],
        compiler_params=_params(("arbitrary",), 40),
    )(proj, proj, proj, dya, ln_g, ln_b, ws, bfull)


DX_TK = 3 * BW


def _inproj_bwd_dx(dgl, da, small, w_t, x, pre_g, dxo, send=None):
    S = x.shape[0]
    tm = min(512, S)
    nk = DP // DX_TK

    def body(g_ref, a_ref, s0, s1, s2, s3, s4, s5, w_ref, x_ref, pg_ref, dxo_ref, *rest):
        if send is None:
            dx_ref, dg_ref, acc = rest
        else:
            send_ref, dx_ref, dg_ref, recv_ref, acc, *sems = rest
            _hosted_comm(send_ref, recv_ref, sems, same_source=False)
        i, k = pl.program_id(0), pl.program_id(1)
        _first_step_zero((dg_ref,), (i == 0) & (k == 0))

        @pl.when(k == 0)
        def _():
            acc[...] = jnp.zeros(acc.shape, F32)

        @pl.when(k < 2)
        def _():
            acc[...] += _mm(g_ref[...], w_ref[...])

        @pl.when(k == 2)
        def _():
            acc[...] += _mm(a_ref[...], w_ref[...])

        for kk, trio in ((3, (s0, s1, s2)), (4, (s3, s4, s5))):
            @pl.when(k == kk)
            def _(trio=trio):
                t = acc[...]
                for n, ref in enumerate(trio):
                    t = t + _mm(ref[...], w_ref[n * BW:(n + 1) * BW, :])
                acc[...] = t

        @pl.when(k == nk - 1)
        def _():
            n, r = _rms(x_ref[...])
            dxn = acc[...]
            dg_ref[...] += jnp.sum(dxn * n, axis=0, keepdims=True)
            dx_ref[...] = dxo_ref[...] + _rms_bwd(dxn * pg_ref[...], n, r)

    row = pl.BlockSpec((tm, D), lambda i, k: (i, 0))
    vec = pl.BlockSpec((1, D), lambda i, k: (0, 0))
    fixed = lambda w: pl.BlockSpec((tm, w), lambda i, k: (i, 0))
    hbm = pl.BlockSpec(memory_space=pl.ANY)
    hosted = send is not None
    recv_shape = [jax.ShapeDtypeStruct(send.shape, send.dtype)] if hosted else []
    return pl.pallas_call(
        body, name="inproj_bwd_dx_exchange" if hosted else "inproj_bwd_dx", grid=(S // tm, nk),
        in_specs=[pl.BlockSpec((tm, DX_TK), lambda i, k: (i, jnp.minimum(k, 1))), fixed(DX_TK)]
                 + [fixed(BW)] * 6 + [pl.BlockSpec((DX_TK, D), lambda i, k: (k, 0)), row, vec, row] + [hbm] * hosted,
        out_specs=[row, vec] + [hbm] * hosted,
        out_shape=[jax.ShapeDtypeStruct((S, D), F32), jax.ShapeDtypeStruct((1, D), F32)] + recv_shape,
        scratch_shapes=[pltpu.VMEM((tm, D), F32)] + COMM_SEMS * hosted,
        compiler_params=_params(("arbitrary", "arbitrary"), 56),
    )(dgl, da, *small, w_t, x, pre_g, dxo, *([send] * hosted))


def _matmul_tn(segs, b):
    S, K = b.shape
    tk = min(1024, S)
    widths = [a.shape[1] for a in segs]
    offs = [sum(widths[:n]) for n in range(len(segs))]

    def body(*refs):
        seg_refs, b_ref, o_ref = refs[:len(segs)], refs[len(segs)], refs[len(segs) + 1]
        _first_step_zero((o_ref,), pl.program_id(0) == 0)
        bv = b_ref[...]
        for ref, off, w in zip(seg_refs, offs, widths):
            o_ref[off:off + w, :] += _mm_tn(ref[...], bv)

    return pl.pallas_call(
        body, name="matmul_tn", grid=(S // tk,),
        in_specs=[pl.BlockSpec((tk, w), lambda k: (k, 0)) for w in widths] + [pl.BlockSpec((tk, K), lambda k: (k, 0))],
        out_specs=pl.BlockSpec((sum(widths), K), lambda k: (0, 0)),
        out_shape=jax.ShapeDtypeStruct((sum(widths), K), F32),
        compiler_params=_params(("arbitrary",), 56),
    )(*segs, b)


def _pad_rows_w_in(wt):
    z = lambda n: jnp.zeros((n, wt.shape[1]), wt.dtype)
    return jnp.concatenate([wt[4512:], wt[:1920], z(64), wt[1920:1952], z(32), wt[1952:4512]], axis=0)


def _unpad_rows_w_in(dwt):
    a = dwt[3 * D:]
    return jnp.concatenate([a[:1920], a[1984:2016], a[2048:], dwt[:3 * D]], axis=0)


def _pad_head_rows(wt, width):
    k = wt.shape[1]
    return jnp.pad(wt.reshape(MLA_H, width, k), ((0, 0), (0, 128 - width), (0, 0))).reshape(MLA_H * 128, k)


def _unpad_head_rows(wt, width):
    k = wt.shape[1]
    return wt.reshape(MLA_H, 128, k)[:, :width].reshape(MLA_H * width, k)


def _rope_tables(S):
    half = MLA_ROPE // 2
    inv = 10000.0 ** (-jnp.arange(half, dtype=F32) / half)
    ang = jnp.arange(S, dtype=F32)[:, None] * inv[None, :]
    cos, sin = jnp.cos(ang), jnp.sin(ang)
    one = lambda n: jnp.ones((S, n), F32)
    zero = lambda n: jnp.zeros((S, n), F32)
    cos_t = jnp.concatenate([one(MLA_NOPE), cos, cos, one(128 - MLA_QK)], axis=1)
    sin_a = jnp.concatenate([zero(MLA_NOPE), -sin, zero(128 - MLA_NOPE - half)], axis=1)
    sin_b = jnp.concatenate([zero(MLA_NOPE + half), sin, zero(128 - MLA_QK)], axis=1)
    return cos_t, sin_a, sin_b


SHARDED = ("w_in", "mla_w_uq", "mla_w_ukv", "w_branch", "w_out")
SHARD_PACK = SHARDED + ("gate_b",)
REPLICATED = ("pre_g", "post_g", "sgu_ln_g", "sgu_ln_b", "sgu_w", "sgu_b", "mla_q_norm_g", "mla_kv_norm_g",
              "ca_rel_bias")
OUT_ORDER = ("w_in", "pre_g", "post_g", "sgu_ln_g", "sgu_ln_b", "sgu_w", "sgu_b", "mla_q_norm_g", "mla_kv_norm_g",
             "mla_w_uq", "mla_w_ukv", "ca_rel_bias", "w_branch", "gate_b", "w_out")
TRANSPOSED = {"w_in": (0, 2, 1), "mla_w_uq": (0, 2, 1), "mla_w_ukv": (0, 2, 1), "w_branch": (0, 1, 3, 2)}


def _canon(name, a):
    return jnp.transpose(a, TRANSPOSED[name]) if name in TRANSPOSED else a


ROW_MULT = 64
PIECE_MULT = 16


def _piece_rows(shape):
    size = 1
    for s in shape:
        size *= s
    return -(-size // (1024 * PIECE_MULT)) * PIECE_MULT


def _pack_rows(pieces, lead, mult=ROW_MULT):
    rows = []
    for a in pieces:
        flat = a.reshape(a.shape[:lead] + (-1,))
        nr = _piece_rows(a.shape[lead:])
        flat = jnp.pad(flat, [(0, 0)] * lead + [(0, nr * 1024 - flat.shape[-1])])
        rows.append(flat.reshape(a.shape[:lead] + (nr, 1024)))
    total = sum(r.shape[lead] for r in rows)
    tail = (-total) % mult
    if tail:
        rows.append(jnp.zeros(rows[0].shape[:lead] + (tail, 1024), rows[0].dtype))
    return jnp.concatenate(rows, axis=lead)


def _unpack_rows(p, shapes, lead):
    out, r = [], 0
    for s in shapes:
        nr, size = _piece_rows(s), 1
        for d in s:
            size *= d
        piece = lax.slice_in_dim(p, r, r + nr, axis=lead).reshape(p.shape[:lead] + (-1,))
        out.append(lax.slice_in_dim(piece, 0, size, axis=lead).reshape(p.shape[:lead] + tuple(s)))
        r += nr
    return out


def kernel(x, w_in, pre_g, post_g, sgu_ln_g, sgu_ln_b, sgu_w, sgu_b, mla_q_norm_g, mla_kv_norm_g, mla_w_uq, mla_w_ukv, ca_rel_bias, w_branch, gate_b, w_out, loss_target, m_w_in, m_pre_g, m_post_g, m_sgu_ln_g, m_sgu_ln_b, m_sgu_w, m_sgu_b, m_mla_q_norm_g, m_mla_kv_norm_g, m_mla_w_uq, m_mla_w_ukv, m_ca_rel_bias, m_w_branch, m_gate_b, m_w_out, v_w_in, v_pre_g, v_post_g, v_sgu_ln_g, v_sgu_ln_b, v_sgu_w, v_sgu_b, v_mla_q_norm_g, v_mla_kv_norm_g, v_mla_w_uq, v_mla_w_ukv, v_ca_rel_bias, v_w_branch, v_gate_b, v_w_out):
    weights = dict(w_in=w_in, pre_g=pre_g, post_g=post_g, sgu_ln_g=sgu_ln_g, sgu_ln_b=sgu_ln_b, sgu_w=sgu_w,
                   sgu_b=sgu_b, mla_q_norm_g=mla_q_norm_g, mla_kv_norm_g=mla_kv_norm_g, mla_w_uq=mla_w_uq,
                   mla_w_ukv=mla_w_ukv, ca_rel_bias=ca_rel_bias, w_branch=w_branch, gate_b=gate_b, w_out=w_out)
    mom_m = dict(w_in=m_w_in, pre_g=m_pre_g, post_g=m_post_g, sgu_ln_g=m_sgu_ln_g, sgu_ln_b=m_sgu_ln_b,
                 sgu_w=m_sgu_w, sgu_b=m_sgu_b, mla_q_norm_g=m_mla_q_norm_g, mla_kv_norm_g=m_mla_kv_norm_g,
                 mla_w_uq=m_mla_w_uq, mla_w_ukv=m_mla_w_ukv, ca_rel_bias=m_ca_rel_bias, w_branch=m_w_branch,
                 gate_b=m_gate_b, w_out=m_w_out)
    mom_v = dict(w_in=v_w_in, pre_g=v_pre_g, post_g=v_post_g, sgu_ln_g=v_sgu_ln_g, sgu_ln_b=v_sgu_ln_b,
                 sgu_w=v_sgu_w, sgu_b=v_sgu_b, mla_q_norm_g=v_mla_q_norm_g, mla_kv_norm_g=v_mla_kv_norm_g,
                 mla_w_uq=v_mla_w_uq, mla_w_ukv=v_mla_w_ukv, ca_rel_bias=v_ca_rel_bias, w_branch=v_w_branch,
                 gate_b=v_gate_b, w_out=v_w_out)
    depth = w_in.shape[0]
    S = x.shape[1]
    xs = x.reshape(S, D)

    cw = {n: _canon(n, weights[n]) for n in SHARD_PACK}

    def layer_block(l):
        return _pack_rows([cw[n][l].astype(MXU_DTYPE) for n in SHARDED], 0, PIECE_MULT)

    def split_block(got):
        return dict(zip(SHARDED, _unpack_rows(got, [cw[n].shape[1:] for n in SHARDED], 1)))

    gate_all, = _unpack_rows(_all_gather(_pack_rows([gate_b], 0, PIECE_MULT)), [gate_b.shape], 1)
    cos_t, sin_a, sin_b = _rope_tables(S)

    def layer_weights(gl, l):
        w_t = _pad_rows_w_in(gl["w_in"].reshape(D_IN, D))
        wuq_t = _pad_head_rows(gl["mla_w_uq"].reshape(MLA_H * MLA_QK, Q_RANK), MLA_QK)
        ukv_t = gl["mla_w_ukv"]
        wuk_t = jnp.pad(ukv_t[:, :MLA_NOPE], ((0, 0), (0, 128 - MLA_NOPE), (0, 0))).reshape(MLA_H * 128, KV_RANK)
        wuv_t = ukv_t[:, MLA_NOPE:].reshape(BW, KV_RANK)
        wbr_t = jnp.swapaxes(gl["w_branch"], 0, 1).reshape(N_BRANCH, D, BW)
        return dict(
            w_t=w_t, w_pad=w_t.T, wuq_t=wuq_t, wuq=wuq_t.T, wuk_t=wuk_t, wuk=wuk_t.T, wuv_t=wuv_t, wuv=wuv_t.T,
            wbr_t=wbr_t, wbr=jnp.swapaxes(wbr_t, 1, 2), wout=gl["w_out"].reshape(D, D),
            gate_b=jnp.swapaxes(gate_all[:, l], 0, 1).reshape(N_BRANCH, D),
            pre_g=pre_g[l][None], post_g=post_g[l][None], ln_g=sgu_ln_g[l][None], ln_b=sgu_ln_b[l][None],
            ws=sgu_w[l], bfull=jnp.repeat(sgu_b[l].T, BW // SGU_G, axis=1),
            gq=mla_q_norm_g[l][None], gkv=mla_kv_norm_g[l][None], bias=_bias_build(ca_rel_bias[l]))

    layers, saved = [], []
    h_x = xs
    got = _all_gather(layer_block(0))
    for l in range(depth):
        lw = layer_weights(split_block(got), l)
        layers.append(lw)
        proj, xn = _inproj_fwd(h_x, lw["pre_g"], lw["w_pad"])
        ya = _sgu_fwd(proj, lw["ln_g"], lw["ln_b"], lw["ws"], lw["bfull"])
        qh, kh, vh, ktr, vtr = _mla_prep_fwd(proj, lw["gq"], lw["gkv"], lw["wuq"], lw["wuk"], lw["wuv"],
                                             cos_t, sin_a, sin_b)
        if l + 1 < depth:
            ob, lse, got = _flash_fwd(qh, kh, vtr, gather=layer_block(l + 1))
        else:
            ob, lse = _flash_fwd(qh, kh, vtr)
        oc = _band_fwd(proj, lw["bias"])
        x_new, merged, hh = _merge_fwd(h_x, ya, ob, oc, proj, lw["gate_b"], lw["wbr"], lw["wout"], lw["post_g"])
        saved.append(dict(x=h_x, proj=proj, xn=xn, ya=ya, qh=qh, kh=kh, vh=vh, ktr=ktr, ob=ob, lse=lse, oc=oc,
                          merged=merged, h=hh))
        h_x = x_new

    loss_part, dx = _loss_fwd_bwd(h_x, loss_target.reshape(S, D))
    loss = lax.psum(loss_part[0, 0], MESH_AXES)

    rows = {}
    rep = {n: [None] * depth for n in REPLICATED}
    recvs = [None] * depth
    pending = None
    for l in reversed(range(depth)):
        lw, sv = layers[l], saved[l]
        proj = sv["proj"]
        dmerged, dw_out, dg_post = _out_bwd(dx, sv["h"], sv["merged"], lw["post_g"], lw["wout"].T)
        dgl, dzb, dzc, dya, dob, doc, dwbr_t, dgb = _gate_bwd(
            dmerged, sv["ya"], sv["ob"], sv["oc"], proj, lw["gate_b"], lw["wbr"], lw["wbr_t"])
        dqc, dkc, dvc, dbias = _band_bwd(proj, lw["bias"], doc)
        drel = _bias_fold(dbias)
        flash_args = (sv["qh"], sv["kh"], sv["ktr"], sv["vh"], sv["lse"], _attn_delta(sv["ob"], dob), dob)
        if pending is None:
            dqf, dkf, dvf = _flash_bwd(*flash_args)
        else:
            dqf, dkf, dvf, recvs[l + 1] = _flash_bwd(*flash_args, send=pending)
        db, dwuq_t, dwuk_t, dwuv_t, dgq, dgkv = _mla_prep_bwd(
            proj, dqf, dkf, dvf, lw["gq"], lw["gkv"], lw["wuq_t"], lw["wuk_t"], lw["wuv_t"], cos_t, sin_a, sin_b)
        da, dws, _, dlg, dlb, dbs = _sgu_bwd(proj, dya, lw["ln_g"], lw["ln_b"], lw["ws"], lw["bfull"])
        small = (db, dzb, dqc, dkc, dvc, dzc)
        dw_t = jnp.concatenate([_matmul_tn([dgl], sv["xn"]), _matmul_tn([da, db, dzb], sv["xn"]),
                                _matmul_tn([dqc, dkc, dvc, dzc], sv["xn"])], axis=0)

        rows["w_in"] = _unpad_rows_w_in(dw_t).reshape(N_DEV, D_IN // N_DEV, D)
        rows["mla_w_uq"] = _unpad_head_rows(dwuq_t, MLA_QK).reshape(N_DEV, MLA_QK, Q_RANK)
        dk3 = dwuk_t.reshape(MLA_H, 128, KV_RANK)[:, :MLA_NOPE]
        dv3 = dwuv_t.reshape(MLA_H, 64, KV_RANK)
        rows["mla_w_ukv"] = jnp.concatenate([dk3, dv3], axis=1)
        rows["w_branch"] = jnp.swapaxes(dwbr_t.reshape(N_BRANCH, N_DEV, D // N_DEV, BW), 0, 1)
        rows["w_out"] = dw_out.reshape(N_DEV, D // N_DEV, D)
        rows["gate_b"] = jnp.swapaxes(dgb.reshape(N_BRANCH, N_DEV, D // N_DEV), 0, 1)
        pending = _pack_rows([rows[n].astype(MXU_DTYPE) for n in SHARD_PACK], 1)
        dx_args = (dgl, da, small, lw["w_t"], sv["x"], lw["pre_g"], dx)
        if l > 0:
            dx, dg_pre = _inproj_bwd_dx(*dx_args)
        else:
            dx, dg_pre, recvs[0] = _inproj_bwd_dx(*dx_args, send=pending)
        rep["pre_g"][l] = dg_pre[0]
        rep["post_g"][l] = dg_post[0]
        rep["sgu_ln_g"][l] = dlg[0]
        rep["sgu_ln_b"][l] = dlb[0]
        rep["sgu_w"][l] = dws
        rep["sgu_b"][l] = dbs.T
        rep["mla_q_norm_g"][l] = dgq[0]
        rep["mla_kv_norm_g"][l] = dgkv[0]
        rep["ca_rel_bias"][l] = drel
    grad_x = dx.reshape(x.shape)

    cm = {n: _canon(n, mom_m[n]) for n in SHARD_PACK}
    cv = {n: _canon(n, mom_v[n]) for n in SHARD_PACK}
    shard_shapes = [cw[n].shape[1:] for n in SHARD_PACK]
    per_layer = []
    for l in range(depth):
        at = lambda d: _pack_rows([d[n][l] for n in SHARD_PACK], 0)
        outs_l = _reduce_adamw(recvs[l], at(cw), at(cm), at(cv))
        per_layer.append([_unpack_rows(p, shard_shapes, 0) for p in outs_l])
    outs_sh = [{n: _canon(n, jnp.stack([per_layer[l][k][i] for l in range(depth)]))
                for i, n in enumerate(SHARD_PACK)} for k in range(4)]

    rp = lambda d: _pack_rows([d[n] for n in REPLICATED], 0)
    allp = _all_gather(rp({n: jnp.stack(rep[n]) for n in REPLICATED}))
    outs_rep = _reduce_adamw(allp, rp(weights), rp(mom_m), rp(mom_v))
    rep_shapes = [weights[n].shape for n in REPLICATED]
    outs_rep = [dict(zip(REPLICATED, _unpack_rows(p, rep_shapes, 0))) for p in outs_rep]

    outs = [{**a, **b} for a, b in zip(outs_sh, outs_rep)]
    return (loss, grad_x, *[o[n] for o in outs for n in OUT_ORDER])
```
